```python
import math
import numpy as np
import jax
import jax.numpy as jnp
from jax import lax

D_MODEL = 1024
BATCH = 2
SEQ = 16384
DEPTH = 4

N_EVEN = (DEPTH + 1) // 2
N_ODD = DEPTH // 2
DEEPNORM_ALPHA = (2.0 * DEPTH) ** 0.25
DEEPNORM_BETA = (8.0 * DEPTH) ** -0.25
LN_EPS = 1e-5
MIX_WIDTH = D_MODEL

GLA_HEADS = 4
GLA_DV = MIX_WIDTH // 2 // GLA_HEADS
GLA_DK = GLA_DV // 2
GLA_GATE_RANK = 16
GLA_GATE_TAU = 16.0
GLA_CHUNK = 64

S5_WIDTH = MIX_WIDTH // 2
S5_GROUP = 16
S5_GROUPS = S5_WIDTH // S5_GROUP
S5_STATE = 64
S5_DT_MIN = 1e-3
S5_DT_MAX = 1e-1
S5_MAX_RE = -1e-4

EV_SIZES = (GLA_HEADS * GLA_DK, GLA_HEADS * GLA_DK, GLA_HEADS * GLA_DV, GLA_HEADS * GLA_DV, GLA_GATE_RANK, S5_WIDTH)
EV_COLS = sum(EV_SIZES)

DSA_HEADS = 8
DSA_HEAD_DIM = 64
DSA_WIDTH = DSA_HEADS * DSA_HEAD_DIM
IDX_HEADS = 4
IDX_DIM = 64
DSA_TOPK_MAX = 256

DIL_PATTERNS = ((128, 1), (512, 4), (2048, 16))
DIL_GROUPS = len(DIL_PATTERNS)
DIL_HEADS = 8
DIL_HEAD_DIM = 64
DIL_WIDTH = DIL_HEADS * DIL_HEAD_DIM

OD_SIZES = (DSA_WIDTH, DSA_WIDTH, DSA_WIDTH, IDX_HEADS * IDX_DIM, IDX_DIM, IDX_HEADS, 3 * DIL_GROUPS * DIL_WIDTH)
OD_COLS = sum(OD_SIZES)
OD_WIDTH = DSA_WIDTH + DIL_WIDTH
Q_BLOCK = 128

N_EXPERTS = 32
TOP_K = 4
D_FF = D_MODEL
SWIGLU_ALPHA = 1.702
SWIGLU_LIMIT = 7.0
MOE_BLOCK = 512

kernel_name = 'hybrid_gla_s5_dsa_dilated_moe_deepnorm'


def _split(h, sizes):
    return jnp.split(h, [int(i) for i in np.cumsum(sizes)[:-1]], axis=-1)


def layer_norm(x, g, b):
    xf = x.astype(jnp.float32)
    mu = jnp.mean(xf, axis=-1, keepdims=True)
    var = jnp.mean(jnp.square(xf - mu), axis=-1, keepdims=True)
    return ((xf - mu) * lax.rsqrt(var + LN_EPS) * g.astype(jnp.float32) + b.astype(jnp.float32)).astype(x.dtype)


def gla_mix(q, k, v, r, gate_lr, w_gate2, b_gate2, norm_g):
    f32 = jnp.float32
    bsz, L, H, DK = q.shape
    DV = v.shape[-1]
    C = GLA_CHUNK
    NC = L // C
    logit = (gate_lr @ w_gate2 + b_gate2).astype(f32).reshape(bsz, L, H, DK)
    log_a = jax.nn.log_sigmoid(logit) / GLA_GATE_TAU
    chunk = lambda t: t.astype(f32).reshape(bsz, NC, C, H, t.shape[-1])
    qc = chunk(q) * DK ** -0.5
    kc, vc = chunk(k), chunk(v)
    b = jnp.cumsum(chunk(log_a), axis=2)
    b_last = b[:, :, -1:]
    q_t = qc * jnp.exp(b)
    k_t = kc * jnp.exp(-b)
    k_end = kc * jnp.exp(b_last - b)
    causal = jnp.tril(jnp.ones((C, C), dtype=bool))
    att = jnp.where(causal, jnp.einsum('bnihd,bnjhd->bnhij', q_t, k_t), 0.0)
    o_intra = jnp.einsum('bnhij,bnjhv->bnihv', att, vc)
    kv_chunk = jnp.einsum('bnjhd,bnjhv->bnhdv', k_end, vc)
    decay = jnp.exp(b_last[:, :, 0])

    def step(S, inp):
        dec, kv = inp
        return dec[..., None] * S + kv, S

    S0 = jnp.zeros((bsz, H, DK, DV), f32)
    _, S_start = lax.scan(step, S0, (jnp.moveaxis(decay, 1, 0), jnp.moveaxis(kv_chunk, 1, 0)))
    S_start = jnp.moveaxis(S_start, 0, 1)
    o = o_intra + jnp.einsum('bnihd,bnhdv->bnihv', q_t, S_start)
    o = o.reshape(bsz, L, H, DV)
    o = o * lax.rsqrt(jnp.mean(o * o, axis=-1, keepdims=True) + LN_EPS) * norm_g.astype(f32)
    o = o * jax.nn.silu(r.astype(f32))
    return o.reshape(bsz, L, H * DV).astype(q.dtype)


def _complex_affine_combine(e1, e2):
    a1r, a1i, b1r, b1i = e1
    a2r, a2i, b2r, b2i = e2
    return (a2r * a1r - a2i * a1i,
            a2r * a1i + a2i * a1r,
            a2r * b1r - a2i * b1i + b2r,
            a2r * b1i + a2i * b1r + b2i)


def s5_mix(u, a_re, a_im, log_dt, b_re, b_im, c_re, c_im, d_skip, w_glu, b_glu):
    f32 = jnp.float32
    bsz, L, _ = u.shape
    ug = u.astype(f32).reshape(bsz, L, S5_GROUPS, S5_GROUP)
    lam_re = jnp.minimum(a_re.astype(f32), S5_MAX_RE)
    lam_im = a_im.astype(f32)
    dt = jnp.exp(log_dt.astype(f32))[:, None]
    mag = jnp.exp(lam_re * dt)
    ab_re = mag * jnp.cos(lam_im * dt)
    ab_im = mag * jnp.sin(lam_im * dt)
    inv = 1.0 / (lam_re * lam_re + lam_im * lam_im)
    z_re = ((ab_re - 1.0) * lam_re + ab_im * lam_im) * inv
    z_im = (ab_im * lam_re - (ab_re - 1.0) * lam_im) * inv
    br, bi = b_re.astype(f32), b_im.astype(f32)
    bb_re = z_re[..., None] * br - z_im[..., None] * bi
    bb_im = z_re[..., None] * bi + z_im[..., None] * br
    bu_re = jnp.einsum('blgn,gpn->lbgp', ug, bb_re)
    bu_im = jnp.einsum('blgn,gpn->lbgp', ug, bb_im)
    a_seq_re = jnp.broadcast_to(ab_re, (L, 1) + ab_re.shape)
    a_seq_im = jnp.broadcast_to(ab_im, (L, 1) + ab_im.shape)
    _, _, xr, xi = lax.associative_scan(_complex_affine_combine, (a_seq_re, a_seq_im, bu_re, bu_im), axis=0)
    y = (jnp.einsum('lbgp,gnp->blgn', xr, c_re.astype(f32))
         - jnp.einsum('lbgp,gnp->blgn', xi, c_im.astype(f32)))
    y = y + d_skip.astype(f32).reshape(S5_GROUPS, S5_GROUP) * ug
    y = jax.nn.gelu(y.reshape(bsz, L, S5_WIDTH)).astype(u.dtype)
    return y * jax.nn.sigmoid(y @ w_glu + b_glu)


def dsa_mix(q, k, v, qi, ki, wi):
    f32 = jnp.float32
    bsz, L, H, Dh = q.shape
    topk = min(DSA_TOPK_MAX, L // 4)
    n_blocks = L // Q_BLOCK
    kpos = jnp.arange(L)
    ki32 = ki.astype(f32)
    gather = jax.vmap(lambda t, idx: t[idx])

    def block(i):
        s0 = i * Q_BLOCK
        qpos = s0 + jnp.arange(Q_BLOCK)
        qb = lax.dynamic_slice_in_dim(q, s0, Q_BLOCK, 1).astype(f32)
        qib = lax.dynamic_slice_in_dim(qi, s0, Q_BLOCK, 1).astype(f32)
        wib = lax.dynamic_slice_in_dim(wi, s0, Q_BLOCK, 1).astype(f32)
        dots = jnp.einsum('bqhd,bsd->bqhs', qib, ki32) * IDX_DIM ** -0.5
        score = jnp.einsum('bqh,bqhs->bqs', wib, jax.nn.relu(dots))
        score = jnp.where(kpos[None, :] <= qpos[:, None], score, -jnp.inf)
        _, idx = lax.top_k(score, topk)
        valid = idx <= qpos[None, :, None]
        kg = gather(k, idx).astype(f32)
        vg = gather(v, idx).astype(f32)
        s = jnp.einsum('bqhd,bqkhd->bqhk', qb, kg) * Dh ** -0.5
        s = jnp.where(valid[:, :, None, :], s, -jnp.inf)
        p = jax.nn.softmax(s, axis=-1)
        return jnp.einsum('bqhk,bqkhd->bqhd', p, vg).astype(q.dtype)

    out = lax.map(block, jnp.arange(n_blocks))
    return jnp.moveaxis(out, 0, 1).reshape(bsz, L, H * Dh)


def dilated_mix(q, k, v):
    f32 = jnp.float32
    bsz, L, NG, H, Dh = q.shape
    n_blocks = L // Q_BLOCK

    def block(i):
        s0 = i * Q_BLOCK
        qpos = s0 + jnp.arange(Q_BLOCK)
        qb = lax.dynamic_slice_in_dim(q, s0, Q_BLOCK, 1).astype(f32)
        outs, lses = [], []
        for g, (window, dil) in enumerate(DIL_PATTERNS):
            offs = dil * jnp.arange(window // dil + 1)
            src = qpos[:, None] - offs[None, :]
            valid = src >= 0
            src = jnp.maximum(src, 0)
            kg = k[:, :, g][:, src].astype(f32)
            vg = v[:, :, g][:, src].astype(f32)
            s = jnp.einsum('bqhd,bqwhd->bqhw', qb[:, :, g], kg) * Dh ** -0.5
            s = jnp.where(valid[None, :, None, :], s, -jnp.inf)
            lse = jax.nn.logsumexp(s, axis=-1)
            outs.append(jnp.einsum('bqhw,bqwhd->bqhd', jnp.exp(s - lse[..., None]), vg))
            lses.append(lse)
        wts = jax.nn.softmax(jnp.stack(lses, 0), axis=0)
        o = jnp.einsum('gbqh,gbqhd->bqhd', wts, jnp.stack(outs, 0))
        return o.astype(q.dtype)

    out = lax.map(block, jnp.arange(n_blocks))
    return jnp.moveaxis(out, 0, 1).reshape(bsz, L, H * Dh)


def moe_ffn(x, router_w, router_b, w1, b1, w2, b2):
    f32 = jnp.float32
    bsz, L, D = x.shape
    T = bsz * L
    A = T * TOP_K
    n_blocks = -(-(A + N_EXPERTS * (MOE_BLOCK - 1)) // MOE_BLOCK)
    P = n_blocks * MOE_BLOCK
    xt = x.reshape(T, D)
    logits = (xt @ router_w + router_b).astype(f32)
    top_logit, top_e = lax.top_k(logits, TOP_K)
    gate = jax.nn.softmax(top_logit, axis=-1)
    e_flat = top_e.reshape(A)
    order = jnp.argsort(e_flat)
    e_sorted = e_flat[order]
    counts = jnp.bincount(e_flat, length=N_EXPERTS)
    padded = (counts + MOE_BLOCK - 1) // MOE_BLOCK * MOE_BLOCK
    start = jnp.cumsum(counts) - counts
    pend = jnp.cumsum(padded)
    pstart = pend - padded
    dest = pstart[e_sorted] + jnp.arange(A) - start[e_sorted]
    tok_buf = jnp.full((P,), T, jnp.int32).at[dest].set((order // TOP_K).astype(jnp.int32))
    gate_buf = jnp.zeros((P,), f32).at[dest].set(gate.reshape(A)[order])
    x_buf = xt.at[tok_buf].get(mode='fill', fill_value=0)
    block_e = jnp.minimum(jnp.searchsorted(pend, jnp.arange(n_blocks) * MOE_BLOCK, side='right'), N_EXPERTS - 1)

    def expert_block(args):
        xb, e = args
        h = xb @ w1[e] + b1[e]
        glu = jnp.minimum(h[:, :D_FF], SWIGLU_LIMIT)
        lin = jnp.clip(h[:, D_FF:], -SWIGLU_LIMIT, SWIGLU_LIMIT)
        act = glu * jax.nn.sigmoid(SWIGLU_ALPHA * glu) * (lin + 1.0)
        return act @ w2[e] + b2[e]

    y_buf = lax.map(expert_block, (x_buf.reshape(n_blocks, MOE_BLOCK, D), block_e)).reshape(P, D)
    y = jax.ops.segment_sum(y_buf.astype(f32) * gate_buf[:, None], tok_buf, num_segments=T)
    return y.reshape(bsz, L, D).astype(x.dtype)


def even_mixer(x, w_in, w_gate2, b_gate2, norm_g, a_re, a_im, log_dt, b_re, b_im, c_re, c_im, d_skip, w_glu, b_glu, w_out):
    bsz, L, _ = x.shape
    q, k, v, r, g_lr, u = _split(x @ w_in, EV_SIZES)
    heads = lambda t, d: t.reshape(bsz, L, GLA_HEADS, d)
    o_a = gla_mix(heads(q, GLA_DK), heads(k, GLA_DK), heads(v, GLA_DV), heads(r, GLA_DV), g_lr, w_gate2, b_gate2, norm_g)
    o_b = s5_mix(u, a_re, a_im, log_dt, b_re, b_im, c_re, c_im, d_skip, w_glu, b_glu)
    return jnp.concatenate([o_a, o_b], axis=-1) @ w_out


def odd_mixer(x, w_in, w_out):
    bsz, L, _ = x.shape
    q, k, v, qi, ki, wi, dil = _split(x @ w_in, OD_SIZES)
    hd = lambda t: t.reshape(bsz, L, DSA_HEADS, DSA_HEAD_DIM)
    o_c = dsa_mix(hd(q), hd(k), hd(v), qi.reshape(bsz, L, IDX_HEADS, IDX_DIM), ki, wi * IDX_HEADS ** -0.5)
    dil = dil.reshape(bsz, L, 3, DIL_GROUPS, DIL_HEADS, DIL_HEAD_DIM)
    o_d = dilated_mix(dil[:, :, 0], dil[:, :, 1], dil[:, :, 2])
    return jnp.concatenate([o_c, o_d], axis=-1) @ w_out


def setup_inputs(seed: int = 0) -> dict:
    key = jax.random.key(seed)
    ks = iter(jax.random.split(key, 40))
    nrm = lambda shape, scale: scale * jax.random.normal(next(ks), shape, jnp.float32)
    NE, NO = N_EVEN, N_ODD
    G, P, N = S5_GROUPS, S5_STATE, S5_GROUP
    a_im0 = math.pi * jnp.arange(P, dtype=jnp.float32)
    return {
        'x': nrm((BATCH, SEQ, D_MODEL), 1.0),
        'ev_w_in': nrm((NE, D_MODEL, EV_COLS), D_MODEL ** -0.5),
        'gla_w_gate2': nrm((NE, GLA_GATE_RANK, GLA_HEADS * GLA_DK), GLA_GATE_RANK ** -0.5),
        'gla_b_gate2': nrm((NE, GLA_HEADS * GLA_DK), 0.1),
        'gla_norm_g': 1.0 + nrm((NE, GLA_DV), 0.02),
        's5_a_re': -0.5 + nrm((NE, G, P), 0.01),
        's5_a_im': a_im0 + nrm((NE, G, P), 0.01),
        's5_log_dt': jax.random.uniform(next(ks), (NE, G), jnp.float32, math.log(S5_DT_MIN), math.log(S5_DT_MAX)),
        's5_b_re': nrm((NE, G, P, N), (2.0 * N) ** -0.5),
        's5_b_im': nrm((NE, G, P, N), (2.0 * N) ** -0.5),
        's5_c_re': nrm((NE, G, N, P), P ** -0.5),
        's5_c_im': nrm((NE, G, N, P), P ** -0.5),
        's5_d': nrm((NE, S5_WIDTH), 1.0),
        's5_w_glu': nrm((NE, S5_WIDTH, S5_WIDTH), S5_WIDTH ** -0.5),
        's5_b_glu': nrm((NE, S5_WIDTH), 0.02),
        'ev_w_out': nrm((NE, MIX_WIDTH, D_MODEL), DEEPNORM_BETA * MIX_WIDTH ** -0.5),
        'od_w_in': nrm((NO, D_MODEL, OD_COLS), D_MODEL ** -0.5),
        'od_w_out': nrm((NO, OD_WIDTH, D_MODEL), DEEPNORM_BETA * OD_WIDTH ** -0.5),
        'ln1_g': 1.0 + nrm((DEPTH, D_MODEL), 0.02),
        'ln1_b': nrm((DEPTH, D_MODEL), 0.02),
        'ln2_g': 1.0 + nrm((DEPTH, D_MODEL), 0.02),
        'ln2_b': nrm((DEPTH, D_MODEL), 0.02),
        'router_w': nrm((DEPTH, D_MODEL, N_EXPERTS), D_MODEL ** -0.5),
        'router_b': nrm((DEPTH, N_EXPERTS), 0.01),
        'moe_w1': nrm((DEPTH, N_EXPERTS, D_MODEL, 2 * D_FF), DEEPNORM_BETA * D_MODEL ** -0.5),
        'moe_b1': nrm((DEPTH, N_EXPERTS, 2 * D_FF), 0.01),
        'moe_w2': nrm((DEPTH, N_EXPERTS, D_FF, D_MODEL), DEEPNORM_BETA * D_FF ** -0.5),
        'moe_b2': nrm((DEPTH, N_EXPERTS, D_MODEL), 0.01),
    }


def reference(x, ev_w_in, gla_w_gate2, gla_b_gate2, gla_norm_g, s5_a_re, s5_a_im, s5_log_dt, s5_b_re, s5_b_im,
              s5_c_re, s5_c_im, s5_d, s5_w_glu, s5_b_glu, ev_w_out, od_w_in, od_w_out, ln1_g, ln1_b, ln2_g, ln2_b,
              router_w, router_b, moe_w1, moe_b1, moe_w2, moe_b2):
    for layer in range(DEPTH):
        j = layer // 2
        if layer % 2 == 0:
            mix = even_mixer(x, ev_w_in[j], gla_w_gate2[j], gla_b_gate2[j], gla_norm_g[j], s5_a_re[j], s5_a_im[j],
                             s5_log_dt[j], s5_b_re[j], s5_b_im[j], s5_c_re[j], s5_c_im[j], s5_d[j], s5_w_glu[j],
                             s5_b_glu[j], ev_w_out[j])
        else:
            mix = odd_mixer(x, od_w_in[j], od_w_out[j])
        x = layer_norm(DEEPNORM_ALPHA * x + mix, ln1_g[layer], ln1_b[layer])
        ffn = moe_ffn(x, router_w[layer], router_b[layer], moe_w1[layer], moe_b1[layer], moe_w2[layer], moe_b2[layer])
        x = layer_norm(DEEPNORM_ALPHA * x + ffn, ln2_g[layer], ln2_b[layer])
    return x
```

```python
import math
from functools import partial

import numpy as np
import jax
import jax.numpy as jnp
from jax import lax
from jax.experimental import pallas as pl
from jax.experimental.pallas import tpu as pltpu

D_MODEL = 1024
DEPTH = 4
DEEPNORM_ALPHA = (2.0 * DEPTH) ** 0.25
LN_EPS = 1e-5
MIX_WIDTH = D_MODEL

GLA_HEADS = 4
GLA_DV = MIX_WIDTH // 2 // GLA_HEADS
GLA_DK = GLA_DV // 2
GLA_GATE_RANK = 16
GLA_GATE_TAU = 16.0
GLA_CHUNK = 64

S5_WIDTH = MIX_WIDTH // 2
S5_GROUP = 16
S5_GROUPS = S5_WIDTH // S5_GROUP
S5_STATE = 64
S5_MAX_RE = -1e-4

EV_SIZES = (GLA_HEADS * GLA_DK, GLA_HEADS * GLA_DK, GLA_HEADS * GLA_DV, GLA_HEADS * GLA_DV, GLA_GATE_RANK, S5_WIDTH)

DSA_HEADS = 8
DSA_HEAD_DIM = 64
DSA_WIDTH = DSA_HEADS * DSA_HEAD_DIM
IDX_HEADS = 4
IDX_DIM = 64
DSA_TOPK_MAX = 256

DIL_PATTERNS = ((128, 1), (512, 4), (2048, 16))
DIL_GROUPS = len(DIL_PATTERNS)
DIL_HEADS = 8
DIL_HEAD_DIM = 64
DIL_WIDTH = DIL_HEADS * DIL_HEAD_DIM

OD_SIZES = (DSA_WIDTH, DSA_WIDTH, DSA_WIDTH, IDX_HEADS * IDX_DIM, IDX_DIM, IDX_HEADS, 3 * DIL_GROUPS * DIL_WIDTH)
Q_BLOCK = 128

N_EXPERTS = 32
TOP_K = 4
D_FF = D_MODEL
SWIGLU_ALPHA = 1.702
SWIGLU_LIMIT = 7.0
MOE_BLOCK = 512

VMEM_LIMIT_BYTES = 48 * 1024 * 1024


def _split(h, sizes):
    return jnp.split(h, [int(i) for i in np.cumsum(sizes)[:-1]], axis=-1)


def _mm_kernel(x_ref, w_ref, o_ref):
    o_ref[...] = jnp.dot(x_ref[...].astype(jnp.bfloat16), w_ref[...].astype(jnp.bfloat16),
                         preferred_element_type=jnp.float32)


def _matmul(x, w, tm=512, tn=512):
    T, K = x.shape
    N = w.shape[1]
    n_pad = -(-N // tn) * tn
    if n_pad != N:
        w = jnp.pad(w, ((0, 0), (0, n_pad - N)))
    out = pl.pallas_call(
        _mm_kernel,
        grid=(T // tm, n_pad // tn),
        in_specs=[pl.BlockSpec((tm, K), lambda i, j: (i, 0)),
                  pl.BlockSpec((K, tn), lambda i, j: (0, j))],
        out_specs=pl.BlockSpec((tm, tn), lambda i, j: (i, j)),
        out_shape=jax.ShapeDtypeStruct((T, n_pad), jnp.float32),
        compiler_params=pltpu.CompilerParams(
            dimension_semantics=("parallel", "parallel"), vmem_limit_bytes=VMEM_LIMIT_BYTES),
    )(x, w)
    return out[:, :N] if n_pad != N else out


def _layer_norm_rows(z, g, b):
    mu = jnp.mean(z, axis=-1, keepdims=True)
    zc = z - mu
    var = jnp.mean(zc * zc, axis=-1, keepdims=True)
    return zc * lax.rsqrt(var + LN_EPS) * g + b


def _mm_res_ln_kernel(a_ref, w_ref, x_ref, g_ref, b_ref, o_ref):
    mix = jnp.dot(a_ref[...].astype(jnp.bfloat16), w_ref[...].astype(jnp.bfloat16),
                  preferred_element_type=jnp.float32)
    o_ref[...] = _layer_norm_rows(DEEPNORM_ALPHA * x_ref[...] + mix, g_ref[...], b_ref[...])


def _matmul_res_ln(a, w, x, g, b, tm=512):
    T, K = a.shape
    D = w.shape[1]
    return pl.pallas_call(
        _mm_res_ln_kernel,
        grid=(T // tm,),
        in_specs=[pl.BlockSpec((tm, K), lambda i: (i, 0)),
                  pl.BlockSpec((K, D), lambda i: (0, 0)),
                  pl.BlockSpec((tm, D), lambda i: (i, 0)),
                  pl.BlockSpec((1, D), lambda i: (0, 0)),
                  pl.BlockSpec((1, D), lambda i: (0, 0))],
        out_specs=pl.BlockSpec((tm, D), lambda i: (i, 0)),
        out_shape=jax.ShapeDtypeStruct((T, D), jnp.float32),
        compiler_params=pltpu.CompilerParams(
            dimension_semantics=("parallel",), vmem_limit_bytes=VMEM_LIMIT_BYTES),
    )(a, w, x, g.reshape(1, D), b.reshape(1, D))


def _res_ln_kernel(y_ref, x_ref, g_ref, b_ref, o_ref):
    o_ref[...] = _layer_norm_rows(DEEPNORM_ALPHA * x_ref[...] + y_ref[...], g_ref[...], b_ref[...])


def _res_ln(y, x, g, b, tm=512):
    T, D = x.shape
    return pl.pallas_call(
        _res_ln_kernel,
        grid=(T // tm,),
        in_specs=[pl.BlockSpec((tm, D), lambda i: (i, 0)),
                  pl.BlockSpec((tm, D), lambda i: (i, 0)),
                  pl.BlockSpec((1, D), lambda i: (0, 0)),
                  pl.BlockSpec((1, D), lambda i: (0, 0))],
        out_specs=pl.BlockSpec((tm, D), lambda i: (i, 0)),
        out_shape=jax.ShapeDtypeStruct((T, D), jnp.float32),
        compiler_params=pltpu.CompilerParams(
            dimension_semantics=("parallel",), vmem_limit_bytes=VMEM_LIMIT_BYTES),
    )(y, x, g.reshape(1, D), b.reshape(1, D))


def gla_mix(q, k, v, r, gate_lr, w_gate2, b_gate2, norm_g):
    f32 = jnp.float32
    bsz, L, H, DK = q.shape
    DV = v.shape[-1]
    C = GLA_CHUNK
    NC = L // C
    logit = (gate_lr @ w_gate2 + b_gate2).astype(f32).reshape(bsz, L, H, DK)
    log_a = jax.nn.log_sigmoid(logit) / GLA_GATE_TAU
    chunk = lambda t: t.astype(f32).reshape(bsz, NC, C, H, t.shape[-1])
    qc = chunk(q) * DK ** -0.5
    kc, vc = chunk(k), chunk(v)
    b = jnp.cumsum(chunk(log_a), axis=2)
    b_last = b[:, :, -1:]
    q_t = qc * jnp.exp(b)
    k_t = kc * jnp.exp(-b)
    k_end = kc * jnp.exp(b_last - b)
    causal = jnp.tril(jnp.ones((C, C), dtype=bool))
    att = jnp.where(causal, jnp.einsum('bnihd,bnjhd->bnhij', q_t, k_t), 0.0)
    o_intra = jnp.einsum('bnhij,bnjhv->bnihv', att, vc)
    kv_chunk = jnp.einsum('bnjhd,bnjhv->bnhdv', k_end, vc)
    decay = jnp.exp(b_last[:, :, 0])

    def step(S, inp):
        dec, kv = inp
        return dec[..., None] * S + kv, S

    S0 = jnp.zeros((bsz, H, DK, DV), f32)
    _, S_start = lax.scan(step, S0, (jnp.moveaxis(decay, 1, 0), jnp.moveaxis(kv_chunk, 1, 0)))
    S_start = jnp.moveaxis(S_start, 0, 1)
    o = o_intra + jnp.einsum('bnihd,bnhdv->bnihv', q_t, S_start)
    o = o.reshape(bsz, L, H, DV)
    o = o * lax.rsqrt(jnp.mean(o * o, axis=-1, keepdims=True) + LN_EPS) * norm_g.astype(f32)
    o = o * jax.nn.silu(r.astype(f32))
    return o.reshape(bsz, L, H * DV).astype(q.dtype)


def _complex_affine_combine(e1, e2):
    a1r, a1i, b1r, b1i = e1
    a2r, a2i, b2r, b2i = e2
    return (a2r * a1r - a2i * a1i,
            a2r * a1i + a2i * a1r,
            a2r * b1r - a2i * b1i + b2r,
            a2r * b1i + a2i * b1r + b2i)


def s5_mix(u, a_re, a_im, log_dt, b_re, b_im, c_re, c_im, d_skip, w_glu, b_glu):
    f32 = jnp.float32
    bsz, L, _ = u.shape
    ug = u.astype(f32).reshape(bsz, L, S5_GROUPS, S5_GROUP)
    lam_re = jnp.minimum(a_re.astype(f32), S5_MAX_RE)
    lam_im = a_im.astype(f32)
    dt = jnp.exp(log_dt.astype(f32))[:, None]
    mag = jnp.exp(lam_re * dt)
    ab_re = mag * jnp.cos(lam_im * dt)
    ab_im = mag * jnp.sin(lam_im * dt)
    inv = 1.0 / (lam_re * lam_re + lam_im * lam_im)
    z_re = ((ab_re - 1.0) * lam_re + ab_im * lam_im) * inv
    z_im = (ab_im * lam_re - (ab_re - 1.0) * lam_im) * inv
    br, bi = b_re.astype(f32), b_im.astype(f32)
    bb_re = z_re[..., None] * br - z_im[..., None] * bi
    bb_im = z_re[..., None] * bi + z_im[..., None] * br
    bu_re = jnp.einsum('blgn,gpn->lbgp', ug, bb_re)
    bu_im = jnp.einsum('blgn,gpn->lbgp', ug, bb_im)
    a_seq_re = jnp.broadcast_to(ab_re, (L, 1) + ab_re.shape)
    a_seq_im = jnp.broadcast_to(ab_im, (L, 1) + ab_im.shape)
    _, _, xr, xi = lax.associative_scan(_complex_affine_combine, (a_seq_re, a_seq_im, bu_re, bu_im), axis=0)
    y = (jnp.einsum('lbgp,gnp->blgn', xr, c_re.astype(f32))
         - jnp.einsum('lbgp,gnp->blgn', xi, c_im.astype(f32)))
    y = y + d_skip.astype(f32).reshape(S5_GROUPS, S5_GROUP) * ug
    y = jax.nn.gelu(y.reshape(bsz, L, S5_WIDTH)).astype(u.dtype)
    return y * jax.nn.sigmoid(y @ w_glu + b_glu)


def dsa_mix(q, k, v, qi, ki, wi):
    f32 = jnp.float32
    bsz, L, H, Dh = q.shape
    topk = min(DSA_TOPK_MAX, L // 4)
    n_blocks = L // Q_BLOCK
    kpos = jnp.arange(L)
    ki32 = ki.astype(f32)
    gather = jax.vmap(lambda t, idx: t[idx])

    def block(i):
        s0 = i * Q_BLOCK
        qpos = s0 + jnp.arange(Q_BLOCK)
        qb = lax.dynamic_slice_in_dim(q, s0, Q_BLOCK, 1).astype(f32)
        qib = lax.dynamic_slice_in_dim(qi, s0, Q_BLOCK, 1).astype(f32)
        wib = lax.dynamic_slice_in_dim(wi, s0, Q_BLOCK, 1).astype(f32)
        dots = jnp.einsum('bqhd,bsd->bqhs', qib, ki32) * IDX_DIM ** -0.5
        score = jnp.einsum('bqh,bqhs->bqs', wib, jax.nn.relu(dots))
        score = jnp.where(kpos[None, :] <= qpos[:, None], score, -jnp.inf)
        _, idx = lax.top_k(score, topk)
        valid = idx <= qpos[None, :, None]
        kg = gather(k, idx).astype(f32)
        vg = gather(v, idx).astype(f32)
        s = jnp.einsum('bqhd,bqkhd->bqhk', qb, kg) * Dh ** -0.5
        s = jnp.where(valid[:, :, None, :], s, -jnp.inf)
        p = jax.nn.softmax(s, axis=-1)
        return jnp.einsum('bqhk,bqkhd->bqhd', p, vg).astype(q.dtype)

    out = lax.map(block, jnp.arange(n_blocks))
    return jnp.moveaxis(out, 0, 1).reshape(bsz, L, H * Dh)


def dilated_mix(q, k, v):
    f32 = jnp.float32
    bsz, L, NG, H, Dh = q.shape
    n_blocks = L // Q_BLOCK

    def block(i):
        s0 = i * Q_BLOCK
        qpos = s0 + jnp.arange(Q_BLOCK)
        qb = lax.dynamic_slice_in_dim(q, s0, Q_BLOCK, 1).astype(f32)
        outs, lses = [], []
        for g, (window, dil) in enumerate(DIL_PATTERNS):
            offs = dil * jnp.arange(window // dil + 1)
            src = qpos[:, None] - offs[None, :]
            valid = src >= 0
            src = jnp.maximum(src, 0)
            kg = k[:, :, g][:, src].astype(f32)
            vg = v[:, :, g][:, src].astype(f32)
            s = jnp.einsum('bqhd,bqwhd->bqhw', qb[:, :, g], kg) * Dh ** -0.5
            s = jnp.where(valid[None, :, None, :], s, -jnp.inf)
            lse = jax.nn.logsumexp(s, axis=-1)
            outs.append(jnp.einsum('bqhw,bqwhd->bqhd', jnp.exp(s - lse[..., None]), vg))
            lses.append(lse)
        wts = jax.nn.softmax(jnp.stack(lses, 0), axis=0)
        o = jnp.einsum('gbqh,gbqhd->bqhd', wts, jnp.stack(outs, 0))
        return o.astype(q.dtype)

    out = lax.map(block, jnp.arange(n_blocks))
    return jnp.moveaxis(out, 0, 1).reshape(bsz, L, H * Dh)


def moe_ffn(x, router_w, router_b, w1, b1, w2, b2):
    f32 = jnp.float32
    bsz, L, D = x.shape
    T = bsz * L
    A = T * TOP_K
    n_blocks = -(-(A + N_EXPERTS * (MOE_BLOCK - 1)) // MOE_BLOCK)
    P = n_blocks * MOE_BLOCK
    xt = x.reshape(T, D)
    logits = (xt @ router_w + router_b).astype(f32)
    top_logit, top_e = lax.top_k(logits, TOP_K)
    gate = jax.nn.softmax(top_logit, axis=-1)
    e_flat = top_e.reshape(A)
    order = jnp.argsort(e_flat)
    e_sorted = e_flat[order]
    counts = jnp.bincount(e_flat, length=N_EXPERTS)
    padded = (counts + MOE_BLOCK - 1) // MOE_BLOCK * MOE_BLOCK
    start = jnp.cumsum(counts) - counts
    pend = jnp.cumsum(padded)
    pstart = pend - padded
    dest = pstart[e_sorted] + jnp.arange(A) - start[e_sorted]
    tok_buf = jnp.full((P,), T, jnp.int32).at[dest].set((order // TOP_K).astype(jnp.int32))
    gate_buf = jnp.zeros((P,), f32).at[dest].set(gate.reshape(A)[order])
    x_buf = xt.at[tok_buf].get(mode='fill', fill_value=0)
    block_e = jnp.minimum(jnp.searchsorted(pend, jnp.arange(n_blocks) * MOE_BLOCK, side='right'), N_EXPERTS - 1)

    def expert_block(args):
        xb, e = args
        h = xb @ w1[e] + b1[e]
        glu = jnp.minimum(h[:, :D_FF], SWIGLU_LIMIT)
        lin = jnp.clip(h[:, D_FF:], -SWIGLU_LIMIT, SWIGLU_LIMIT)
        act = glu * jax.nn.sigmoid(SWIGLU_ALPHA * glu) * (lin + 1.0)
        return act @ w2[e] + b2[e]

    y_buf = lax.map(expert_block, (x_buf.reshape(n_blocks, MOE_BLOCK, D), block_e)).reshape(P, D)
    y = jax.ops.segment_sum(y_buf.astype(f32) * gate_buf[:, None], tok_buf, num_segments=T)
    return y.reshape(bsz, L, D).astype(x.dtype)


def even_mixer(x, w_in, w_gate2, b_gate2, norm_g, a_re, a_im, log_dt, b_re, b_im, c_re, c_im, d_skip, w_glu, b_glu):
    bsz, L, D = x.shape
    h = _matmul(x.reshape(bsz * L, D), w_in).reshape(bsz, L, -1)
    q, k, v, r, g_lr, u = _split(h, EV_SIZES)
    heads = lambda t, d: t.reshape(bsz, L, GLA_HEADS, d)
    o_a = gla_mix(heads(q, GLA_DK), heads(k, GLA_DK), heads(v, GLA_DV), heads(r, GLA_DV), g_lr, w_gate2, b_gate2, norm_g)
    o_b = s5_mix(u, a_re, a_im, log_dt, b_re, b_im, c_re, c_im, d_skip, w_glu, b_glu)
    return jnp.concatenate([o_a, o_b], axis=-1)


def odd_mixer(x, w_in):
    bsz, L, D = x.shape
    h = _matmul(x.reshape(bsz * L, D), w_in).reshape(bsz, L, -1)
    q, k, v, qi, ki, wi, dil = _split(h, OD_SIZES)
    hd = lambda t: t.reshape(bsz, L, DSA_HEADS, DSA_HEAD_DIM)
    o_c = dsa_mix(hd(q), hd(k), hd(v), qi.reshape(bsz, L, IDX_HEADS, IDX_DIM), ki, wi * IDX_HEADS ** -0.5)
    dil = dil.reshape(bsz, L, 3, DIL_GROUPS, DIL_HEADS, DIL_HEAD_DIM)
    o_d = dilated_mix(dil[:, :, 0], dil[:, :, 1], dil[:, :, 2])
    return jnp.concatenate([o_c, o_d], axis=-1)


def kernel(x, ev_w_in, gla_w_gate2, gla_b_gate2, gla_norm_g, s5_a_re, s5_a_im, s5_log_dt, s5_b_re, s5_b_im,
           s5_c_re, s5_c_im, s5_d, s5_w_glu, s5_b_glu, ev_w_out, od_w_in, od_w_out, ln1_g, ln1_b, ln2_g, ln2_b,
           router_w, router_b, moe_w1, moe_b1, moe_w2, moe_b2):
    bsz, L, D = x.shape
    T = bsz * L
    for layer in range(DEPTH):
        j = layer // 2
        if layer % 2 == 0:
            mix = even_mixer(x, ev_w_in[j], gla_w_gate2[j], gla_b_gate2[j], gla_norm_g[j], s5_a_re[j], s5_a_im[j],
                             s5_log_dt[j], s5_b_re[j], s5_b_im[j], s5_c_re[j], s5_c_im[j], s5_d[j], s5_w_glu[j],
                             s5_b_glu[j])
            w_out = ev_w_out[j]
        else:
            mix = odd_mixer(x, od_w_in[j])
            w_out = od_w_out[j]
        x = _matmul_res_ln(mix.reshape(T, -1), w_out, x.reshape(T, D), ln1_g[layer], ln1_b[layer]).reshape(bsz, L, D)
        ffn = moe_ffn(x, router_w[layer], router_b[layer], moe_w1[layer], moe_b1[layer], moe_w2[layer], moe_b2[layer])
        x = _res_ln(ffn.reshape(T, D), x.reshape(T, D), ln2_g[layer], ln2_b[layer]).reshape(bsz, L, D)
    return x
```

```python
import math
from functools import partial

import numpy as np
import jax
import jax.numpy as jnp
from jax import lax
from jax.experimental import pallas as pl
from jax.experimental.pallas import tpu as pltpu

D_MODEL = 1024
DEPTH = 4
DEEPNORM_ALPHA = (2.0 * DEPTH) ** 0.25
LN_EPS = 1e-5
MIX_WIDTH = D_MODEL

GLA_HEADS = 4
GLA_DV = MIX_WIDTH // 2 // GLA_HEADS
GLA_DK = GLA_DV // 2
GLA_GATE_RANK = 16
GLA_GATE_TAU = 16.0
GLA_CHUNK = 64

S5_WIDTH = MIX_WIDTH // 2
S5_GROUP = 16
S5_GROUPS = S5_WIDTH // S5_GROUP
S5_STATE = 64
S5_MAX_RE = -1e-4

EV_SIZES = (GLA_HEADS * GLA_DK, GLA_HEADS * GLA_DK, GLA_HEADS * GLA_DV, GLA_HEADS * GLA_DV, GLA_GATE_RANK, S5_WIDTH)

DSA_HEADS = 8
DSA_HEAD_DIM = 64
DSA_WIDTH = DSA_HEADS * DSA_HEAD_DIM
IDX_HEADS = 4
IDX_DIM = 64
DSA_TOPK_MAX = 256

DIL_PATTERNS = ((128, 1), (512, 4), (2048, 16))
DIL_GROUPS = len(DIL_PATTERNS)
DIL_HEADS = 8
DIL_HEAD_DIM = 64
DIL_WIDTH = DIL_HEADS * DIL_HEAD_DIM

OD_SIZES = (DSA_WIDTH, DSA_WIDTH, DSA_WIDTH, IDX_HEADS * IDX_DIM, IDX_DIM, IDX_HEADS, 3 * DIL_GROUPS * DIL_WIDTH)

N_EXPERTS = 32
TOP_K = 4
D_FF = D_MODEL
SWIGLU_ALPHA = 1.702
SWIGLU_LIMIT = 7.0
MOE_BLOCK = 512

LANES = 128
VMEM_LIMIT_BYTES = 48 * 1024 * 1024
NEG_BIG = -1e30
INT_MIN = -2 ** 31
INT_MAX = 2 ** 31 - 1
KEY_NEG_INF = -0x7F800000

OD_Q, OD_K, OD_V, OD_QI, OD_DIL, OD_KI = 0, 512, 1024, 1536, 2048, 2048 + 9 * 512
OD_COLS_PADDED = 7168


def _split(h, sizes):
    return jnp.split(h, [int(i) for i in np.cumsum(sizes)[:-1]], axis=-1)


def _params(*sem):
    return pltpu.CompilerParams(dimension_semantics=sem, vmem_limit_bytes=VMEM_LIMIT_BYTES)


def _mm_kernel(x_ref, w_ref, o_ref):
    o_ref[...] = jnp.dot(x_ref[...].astype(jnp.bfloat16), w_ref[...].astype(jnp.bfloat16),
                         preferred_element_type=jnp.float32).astype(o_ref.dtype)


def _matmul(x, w, out_dtype=jnp.float32, tm=512, tn=512):
    T, K = x.shape
    N = w.shape[1]
    tm, tn = min(tm, T), min(tn, N)
    assert T % tm == 0 and N % tn == 0
    return pl.pallas_call(
        _mm_kernel,
        grid=(T // tm, N // tn),
        in_specs=[pl.BlockSpec((tm, K), lambda i, j: (i, 0)),
                  pl.BlockSpec((K, tn), lambda i, j: (0, j))],
        out_specs=pl.BlockSpec((tm, tn), lambda i, j: (i, j)),
        out_shape=jax.ShapeDtypeStruct((T, N), out_dtype),
        compiler_params=_params("parallel", "arbitrary"),
    )(x, w)


def _layer_norm_rows(z, g, b):
    mu = jnp.mean(z, axis=-1, keepdims=True)
    zc = z - mu
    var = jnp.mean(zc * zc, axis=-1, keepdims=True)
    return zc * lax.rsqrt(var + LN_EPS) * g + b


def _mm_res_ln_kernel(a_ref, w_ref, x_ref, g_ref, b_ref, o_ref):
    mix = jnp.dot(a_ref[...].astype(jnp.bfloat16), w_ref[...].astype(jnp.bfloat16),
                  preferred_element_type=jnp.float32)
    o_ref[...] = _layer_norm_rows(DEEPNORM_ALPHA * x_ref[...] + mix, g_ref[...], b_ref[...])


def _matmul_res_ln(a, w, x, g, b, tm=512):
    T, K = a.shape
    D = w.shape[1]
    tm = min(tm, T)
    return pl.pallas_call(
        _mm_res_ln_kernel,
        grid=(T // tm,),
        in_specs=[pl.BlockSpec((tm, K), lambda i: (i, 0)),
                  pl.BlockSpec((K, D), lambda i: (0, 0)),
                  pl.BlockSpec((tm, D), lambda i: (i, 0)),
                  pl.BlockSpec((1, D), lambda i: (0, 0)),
                  pl.BlockSpec((1, D), lambda i: (0, 0))],
        out_specs=pl.BlockSpec((tm, D), lambda i: (i, 0)),
        out_shape=jax.ShapeDtypeStruct((T, D), jnp.float32),
        compiler_params=_params("parallel"),
    )(a, w, x, g.reshape(1, D), b.reshape(1, D))


def _mm2_res_ln_kernel(a1_ref, a2_ref, w1_ref, w2_ref, x_ref, g_ref, b_ref, o_ref):
    mix = jnp.dot(a1_ref[...], w1_ref[...], preferred_element_type=jnp.float32)
    mix += jnp.dot(a2_ref[...], w2_ref[...], preferred_element_type=jnp.float32)
    o_ref[...] = _layer_norm_rows(DEEPNORM_ALPHA * x_ref[...] + mix, g_ref[...], b_ref[...])


def _matmul2_res_ln(a1, a2, w, x, g, b, tm=512):
    T, K1 = a1.shape
    D = w.shape[1]
    tm = min(tm, T)
    wb = w.astype(jnp.bfloat16)
    return pl.pallas_call(
        _mm2_res_ln_kernel,
        grid=(T // tm,),
        in_specs=[pl.BlockSpec((tm, K1), lambda i: (i, 0)),
                  pl.BlockSpec((tm, a2.shape[1]), lambda i: (i, 0)),
                  pl.BlockSpec((K1, D), lambda i: (0, 0)),
                  pl.BlockSpec((a2.shape[1], D), lambda i: (1, 0)),
                  pl.BlockSpec((tm, D), lambda i: (i, 0)),
                  pl.BlockSpec((1, D), lambda i: (0, 0)),
                  pl.BlockSpec((1, D), lambda i: (0, 0))],
        out_specs=pl.BlockSpec((tm, D), lambda i: (i, 0)),
        out_shape=jax.ShapeDtypeStruct((T, D), jnp.float32),
        compiler_params=_params("parallel"),
    )(a1, a2, wb, wb, x, g.reshape(1, D), b.reshape(1, D))


def _res_ln_kernel(y_ref, x_ref, g_ref, b_ref, o_ref):
    o_ref[...] = _layer_norm_rows(DEEPNORM_ALPHA * x_ref[...] + y_ref[...], g_ref[...], b_ref[...])


def _res_ln(y, x, g, b, tm=512):
    T, D = x.shape
    tm = min(tm, T)
    return pl.pallas_call(
        _res_ln_kernel,
        grid=(T // tm,),
        in_specs=[pl.BlockSpec((tm, D), lambda i: (i, 0)),
                  pl.BlockSpec((tm, D), lambda i: (i, 0)),
                  pl.BlockSpec((1, D), lambda i: (0, 0)),
                  pl.BlockSpec((1, D), lambda i: (0, 0))],
        out_specs=pl.BlockSpec((tm, D), lambda i: (i, 0)),
        out_shape=jax.ShapeDtypeStruct((T, D), jnp.float32),
        compiler_params=_params("parallel"),
    )(y, x, g.reshape(1, D), b.reshape(1, D))


def _index_keys(qi_blk, ki_blk, wi_blk, q0, k0):
    tq, tk = qi_blk.shape[0], ki_blk.shape[0]
    lhs = jnp.concatenate([qi_blk[:, LANES * h:LANES * (h + 1)] for h in range(IDX_HEADS)], axis=0)
    d = lax.dot_general(lhs, ki_blk, (((1,), (1,)), ((), ())), preferred_element_type=jnp.float32)
    sc = wi_blk[:, 0:1] * jnp.maximum(d[0:tq], 0.0)
    for h in range(1, IDX_HEADS):
        sc = sc + wi_blk[:, h:h + 1] * jnp.maximum(d[h * tq:(h + 1) * tq], 0.0)
    qpos = q0 + lax.broadcasted_iota(jnp.int32, (tq, 1), 0)
    kpos = k0 + lax.broadcasted_iota(jnp.int32, (1, tk), 1)
    sc = jnp.where(kpos <= qpos, sc, -jnp.inf)
    bits = lax.bitcast_convert_type(sc, jnp.int32)
    return jnp.where(bits < 0, INT_MIN - bits, bits)


def _dsa_select_kernel(qi_ref, wi_ref, ki_ref, thr_ref, cut_ref, key_ref, *, tq, topk, row_group):
    i = pl.program_id(1)
    nblk = i + 1
    q0 = i * tq
    qi_blk = qi_ref[...]
    wi_blk = wi_ref[...]

    def fill(j, carry):
        k0 = pl.multiple_of(j * tq, tq)
        key_ref[j] = _index_keys(qi_blk, ki_ref[pl.ds(k0, tq), :], wi_blk, q0, k0)
        return carry

    lax.fori_loop(0, nblk, fill, 0)

    nslab = tq // LANES
    lane = lax.broadcasted_iota(jnp.int32, (row_group, LANES), 1)

    def count(r0, pred):
        def body(j, acc):
            for s in range(nslab):
                kk = key_ref[j, r0:r0 + row_group, s * LANES:(s + 1) * LANES]
                acc = acc + jnp.where(pred(kk, j * tq + s * LANES), 1, 0)
            return acc
        acc = lax.fori_loop(0, nblk, body, jnp.zeros((row_group, LANES), jnp.int32))
        return jnp.sum(acc.astype(jnp.float32), axis=1, keepdims=True).astype(jnp.int32)

    for r0 in range(0, tq, row_group):
        def bit_step(p, thr):
            cand = thr ^ lax.shift_left(jnp.int32(1), 31 - p)
            candb = jnp.broadcast_to(cand, (row_group, LANES))
            cnt = count(r0, lambda kk, base: kk >= candb)
            return jnp.where(cnt >= topk, cand, thr)

        thr = lax.fori_loop(0, 32, bit_step, jnp.full((row_group, 1), INT_MIN, jnp.int32))
        thrb = jnp.broadcast_to(thr, (row_group, LANES))
        n_gt = count(r0, lambda kk, base: kk > thrb)
        n_ge = count(r0, lambda kk, base: kk >= thrb)
        need = topk - n_gt
        tie = jnp.logical_and(n_ge > topk, thr > KEY_NEG_INF)
        any_tie = jnp.max(jnp.where(tie, 1.0, 0.0)) > 0.5

        def resolve_ties():
            def col_step(p, c):
                cand = c | lax.shift_left(jnp.int32(1), 14 - p)
                cnt = count(r0, lambda kk, base: jnp.logical_and(kk == thrb, lane < cand - base))
                return jnp.where(cnt < need, cand, c)
            c = lax.fori_loop(0, 15, col_step, jnp.zeros((row_group, 1), jnp.int32))
            return jnp.where(tie, c + 1, INT_MAX)

        cut = lax.cond(any_tie, resolve_ties, lambda: jnp.full((row_group, 1), INT_MAX, jnp.int32))
        thr_ref[r0:r0 + row_group, :] = jnp.maximum(thr, KEY_NEG_INF + 1)
        cut_ref[r0:r0 + row_group, :] = cut


def _dsa_select(hb, wi, bsz, L, tq):
    T = bsz * L
    nq = L // tq
    topk = min(DSA_TOPK_MAX, L // 4)
    kern = partial(_dsa_select_kernel, tq=tq, topk=topk, row_group=min(64, tq))
    return pl.pallas_call(
        kern,
        grid=(bsz, nq),
        in_specs=[pl.BlockSpec((tq, 512), lambda b, i: (b * nq + i, OD_QI // 512)),
                  pl.BlockSpec((tq, LANES), lambda b, i: (b * nq + i, 0)),
                  pl.BlockSpec((L, LANES), lambda b, i: (b, OD_KI // LANES))],
        out_specs=[pl.BlockSpec((tq, 1), lambda b, i: (b * nq + i, 0)),
                   pl.BlockSpec((tq, 1), lambda b, i: (b * nq + i, 0))],
        out_shape=[jax.ShapeDtypeStruct((T, 1), jnp.int32), jax.ShapeDtypeStruct((T, 1), jnp.int32)],
        scratch_shapes=[pltpu.VMEM((nq, tq, tq), jnp.int32)],
        compiler_params=_params("parallel", "arbitrary"),
    )(hb, wi, hb)


def _split_head_pair(x_pair):
    lane = lax.broadcasted_iota(jnp.int32, x_pair.shape, 1)
    zero = jnp.zeros_like(x_pair)
    return jnp.where(lane < 64, x_pair, zero), jnp.where(lane >= 64, x_pair, zero)


def _dsa_attn_kernel(q_ref, qi_ref, wi_ref, thr_ref, cut_ref, k_ref, v_ref, ki_ref, o_ref,
                     qm_ref, m_ref, l_ref, acc_ref, *, tq, tk, nk):
    i = pl.program_id(1)
    j = pl.program_id(2)
    npair = DSA_HEADS // 2

    @pl.when(j == 0)
    def _():
        for p in range(npair):
            qa, qb = _split_head_pair(q_ref[:, LANES * p:LANES * (p + 1)])
            qm_ref[2 * p] = qa
            qm_ref[2 * p + 1] = qb
        m_ref[...] = jnp.full(m_ref.shape, NEG_BIG, jnp.float32)
        l_ref[...] = jnp.zeros(l_ref.shape, jnp.float32)
        acc_ref[...] = jnp.zeros(acc_ref.shape, jnp.float32)

    @pl.when(j * tk < (i + 1) * tq)
    def _():
        key = _index_keys(qi_ref[...], ki_ref[...], wi_ref[...], i * tq, j * tk)
        col = j * tk + lax.broadcasted_iota(jnp.int32, (1, tk), 1)
        thr = thr_ref[...]
        sel = jnp.logical_or(key > thr, jnp.logical_and(key == thr, col < cut_ref[...]))
        for h in range(DSA_HEADS):
            p = h // 2
            s = lax.dot_general(qm_ref[h], k_ref[:, LANES * p:LANES * (p + 1)], (((1,), (1,)), ((), ())),
                                preferred_element_type=jnp.float32)
            s = jnp.where(sel, s, NEG_BIG)
            m_prev = m_ref[h]
            m_new = jnp.maximum(m_prev, jnp.max(s, axis=1, keepdims=True))
            alpha = jnp.exp(m_prev - m_new)
            e = jnp.exp(s - m_new)
            l_ref[h] = alpha * l_ref[h] + jnp.sum(e, axis=1, keepdims=True)
            pv = jnp.dot(e.astype(jnp.bfloat16), v_ref[:, LANES * p:LANES * (p + 1)],
                         preferred_element_type=jnp.float32)
            acc_ref[h] = alpha * acc_ref[h] + pv
            m_ref[h] = m_new

    @pl.when(j == nk - 1)
    def _():
        lane = lax.broadcasted_iota(jnp.int32, (tq, LANES), 1)
        for p in range(npair):
            oa = acc_ref[2 * p] / l_ref[2 * p]
            ob = acc_ref[2 * p + 1] / l_ref[2 * p + 1]
            o_ref[:, LANES * p:LANES * (p + 1)] = jnp.where(lane < 64, oa, ob).astype(o_ref.dtype)


def _dsa_attention(hb, wi, thr, cut, bsz, L, tq, tk):
    T = bsz * L
    nq, nk = L // tq, L // tk
    kern = partial(_dsa_attn_kernel, tq=tq, tk=tk, nk=nk)

    def qmap(col):
        return lambda b, i, j: (b * nq + i, col)

    def kmap(col):
        return lambda b, i, j: (b * nk + jnp.minimum(j, ((i + 1) * tq - 1) // tk), col)

    return pl.pallas_call(
        kern,
        grid=(bsz, nq, nk),
        in_specs=[pl.BlockSpec((tq, 512), qmap(OD_Q // 512)),
                  pl.BlockSpec((tq, 512), qmap(OD_QI // 512)),
                  pl.BlockSpec((tq, LANES), qmap(0)),
                  pl.BlockSpec((tq, 1), qmap(0)),
                  pl.BlockSpec((tq, 1), qmap(0)),
                  pl.BlockSpec((tk, 512), kmap(OD_K // 512)),
                  pl.BlockSpec((tk, 512), kmap(OD_V // 512)),
                  pl.BlockSpec((tk, LANES), kmap(OD_KI // LANES))],
        out_specs=pl.BlockSpec((tq, DSA_WIDTH), qmap(0)),
        out_shape=jax.ShapeDtypeStruct((T, DSA_WIDTH), jnp.bfloat16),
        scratch_shapes=[pltpu.VMEM((DSA_HEADS, tq, LANES), jnp.bfloat16),
                        pltpu.VMEM((DSA_HEADS, tq, 1), jnp.float32),
                        pltpu.VMEM((DSA_HEADS, tq, 1), jnp.float32),
                        pltpu.VMEM((DSA_HEADS, tq, LANES), jnp.float32)],
        compiler_params=_params("parallel", "parallel", "arbitrary"),
    )(hb, hb, wi, thr, cut, hb, hb, hb)


def _dilated_kernel(q_ref, kp_ref, kc_ref, vp_ref, vc_ref, o_ref, lse_ref, *, tq):
    a = pl.program_id(2)
    row = lax.broadcasted_iota(jnp.int32, (tq, 2 * tq), 0)
    c = lax.broadcasted_iota(jnp.int32, (tq, 2 * tq), 1)
    first_col = jnp.where(a == 0, tq, 0)
    valid = jnp.logical_and(jnp.logical_and(c >= row, c <= row + tq), c >= first_col)
    lane = lax.broadcasted_iota(jnp.int32, (tq, LANES), 1)
    for p in range(DIL_HEADS // 2):
        sl = slice(LANES * p, LANES * (p + 1))
        kk = jnp.concatenate([kp_ref[:, sl], kc_ref[:, sl]], axis=0)
        vv = jnp.concatenate([vp_ref[:, sl], vc_ref[:, sl]], axis=0)
        outs, lses = [], []
        for qh in _split_head_pair(q_ref[:, sl]):
            s = lax.dot_general(qh, kk, (((1,), (1,)), ((), ())), preferred_element_type=jnp.float32)
            s = jnp.where(valid, s, NEG_BIG)
            m = jnp.max(s, axis=1, keepdims=True)
            e = jnp.exp(s - m)
            l = jnp.sum(e, axis=1, keepdims=True)
            outs.append(jnp.dot(e.astype(jnp.bfloat16), vv, preferred_element_type=jnp.float32) / l)
            lses.append(m + jnp.log(l))
        o_ref[:, sl] = jnp.where(lane < 64, outs[0], outs[1])
        lse_ref[:, sl] = jnp.where(lane < 64, lses[0], lses[1])


def _dilated_group(hb, bsz, L, g, tq):
    window, dil = DIL_PATTERNS[g]
    assert window // dil == tq
    T = bsz * L
    M = L // dil
    nb = M // tq
    ncol = hb.shape[1] // 512
    hv = hb.reshape(bsz * M, dil * hb.shape[1])
    qcol, kcol, vcol = (OD_DIL // 512 + s * DIL_GROUPS + g for s in range(3))

    def cur(col):
        return lambda b, r, a: (b * nb + a, r * ncol + col)

    def prev(col):
        return lambda b, r, a: (b * nb + jnp.maximum(a - 1, 0), r * ncol + col)

    blk = (tq, 512)
    out_map = lambda b, r, a: (b * nb + a, r)
    o, lse = pl.pallas_call(
        partial(_dilated_kernel, tq=tq),
        grid=(bsz, dil, nb),
        in_specs=[pl.BlockSpec(blk, cur(qcol)), pl.BlockSpec(blk, prev(kcol)), pl.BlockSpec(blk, cur(kcol)),
                  pl.BlockSpec(blk, prev(vcol)), pl.BlockSpec(blk, cur(vcol))],
        out_specs=[pl.BlockSpec(blk, out_map), pl.BlockSpec(blk, out_map)],
        out_shape=[jax.ShapeDtypeStruct((bsz * M, dil * 512), jnp.float32)] * 2,
        compiler_params=_params("parallel", "parallel", "arbitrary"),
    )(hv, hv, hv, hv, hv)
    return o.reshape(T, 512), lse.reshape(T, 512)


def _dilated_combine_kernel(o0, o1, o2, l0, l1, l2, out_ref):
    a, b, c = l0[...], l1[...], l2[...]
    m = jnp.maximum(jnp.maximum(a, b), c)
    ea, eb, ec = jnp.exp(a - m), jnp.exp(b - m), jnp.exp(c - m)
    num = ea * o0[...] + eb * o1[...] + ec * o2[...]
    out_ref[...] = (num / (ea + eb + ec)).astype(out_ref.dtype)


def _dilated_combine(outs, lses, tm=512):
    T, W = outs[0].shape
    tm = min(tm, T)
    spec = pl.BlockSpec((tm, W), lambda i: (i, 0))
    return pl.pallas_call(
        _dilated_combine_kernel,
        grid=(T // tm,),
        in_specs=[spec] * 6,
        out_specs=spec,
        out_shape=jax.ShapeDtypeStruct((T, W), jnp.bfloat16),
        compiler_params=_params("parallel"),
    )(*outs, *lses)


def _odd_weights(w_in):
    D = w_in.shape[0]
    q, k, v, qi, ki, wi, dil = _split(w_in, OD_SIZES)
    zeros = lambda n: jnp.zeros((D, n), w_in.dtype)
    qi_exp = jnp.concatenate(
        [jnp.concatenate([qi[:, IDX_DIM * h:IDX_DIM * (h + 1)] * IDX_DIM ** -0.5, zeros(LANES - IDX_DIM)], axis=1)
         for h in range(IDX_HEADS)], axis=1)
    dil = dil.reshape(D, 3, DIL_GROUPS * DIL_WIDTH)
    dil = jnp.concatenate([dil[:, 0] * DIL_HEAD_DIM ** -0.5, dil[:, 1], dil[:, 2]], axis=1)
    wb = jnp.concatenate([q * DSA_HEAD_DIM ** -0.5, k, v, qi_exp, dil, ki, zeros(LANES - IDX_DIM),
                          zeros(OD_COLS_PADDED - OD_KI - LANES)], axis=1).astype(jnp.bfloat16)
    wwi = jnp.concatenate([wi * IDX_HEADS ** -0.5, zeros(LANES - IDX_HEADS)], axis=1).astype(jnp.bfloat16)
    return wb, wwi


def _odd_layer_mix(x2, w_in, bsz, L):
    wb, wwi = _odd_weights(w_in)
    hb = _matmul(x2, wb, jnp.bfloat16, tm=1024, tn=1024)
    wi = _matmul(x2, wwi, jnp.float32, tm=1024, tn=LANES)
    tq_sel = min(256, L)
    thr, cut = _dsa_select(hb, wi, bsz, L, tq_sel)
    o_c = _dsa_attention(hb, wi, thr, cut, bsz, L, min(256, L), min(512, L))
    groups = [_dilated_group(hb, bsz, L, g, 128) for g in range(DIL_GROUPS)]
    o_d = _dilated_combine([o for o, _ in groups], [l for _, l in groups])
    return o_c, o_d


def gla_mix(q, k, v, r, gate_lr, w_gate2, b_gate2, norm_g):
    f32 = jnp.float32
    bsz, L, H, DK = q.shape
    DV = v.shape[-1]
    C = GLA_CHUNK
    NC = L // C
    logit = (gate_lr @ w_gate2 + b_gate2).astype(f32).reshape(bsz, L, H, DK)
    log_a = jax.nn.log_sigmoid(logit) / GLA_GATE_TAU
    chunk = lambda t: t.astype(f32).reshape(bsz, NC, C, H, t.shape[-1])
    qc = chunk(q) * DK ** -0.5
    kc, vc = chunk(k), chunk(v)
    b = jnp.cumsum(chunk(log_a), axis=2)
    b_last = b[:, :, -1:]
    q_t = qc * jnp.exp(b)
    k_t = kc * jnp.exp(-b)
    k_end = kc * jnp.exp(b_last - b)
    causal = jnp.tril(jnp.ones((C, C), dtype=bool))
    att = jnp.where(causal, jnp.einsum('bnihd,bnjhd->bnhij', q_t, k_t), 0.0)
    o_intra = jnp.einsum('bnhij,bnjhv->bnihv', att, vc)
    kv_chunk = jnp.einsum('bnjhd,bnjhv->bnhdv', k_end, vc)
    decay = jnp.exp(b_last[:, :, 0])

    def step(S, inp):
        dec, kv = inp
        return dec[..., None] * S + kv, S

    S0 = jnp.zeros((bsz, H, DK, DV), f32)
    _, S_start = lax.scan(step, S0, (jnp.moveaxis(decay, 1, 0), jnp.moveaxis(kv_chunk, 1, 0)))
    S_start = jnp.moveaxis(S_start, 0, 1)
    o = o_intra + jnp.einsum('bnihd,bnhdv->bnihv', q_t, S_start)
    o = o.reshape(bsz, L, H, DV)
    o = o * lax.rsqrt(jnp.mean(o * o, axis=-1, keepdims=True) + LN_EPS) * norm_g.astype(f32)
    o = o * jax.nn.silu(r.astype(f32))
    return o.reshape(bsz, L, H * DV).astype(q.dtype)


def _complex_affine_combine(e1, e2):
    a1r, a1i, b1r, b1i = e1
    a2r, a2i, b2r, b2i = e2
    return (a2r * a1r - a2i * a1i,
            a2r * a1i + a2i * a1r,
            a2r * b1r - a2i * b1i + b2r,
            a2r * b1i + a2i * b1r + b2i)


def s5_mix(u, a_re, a_im, log_dt, b_re, b_im, c_re, c_im, d_skip, w_glu, b_glu):
    f32 = jnp.float32
    bsz, L, _ = u.shape
    ug = u.astype(f32).reshape(bsz, L, S5_GROUPS, S5_GROUP)
    lam_re = jnp.minimum(a_re.astype(f32), S5_MAX_RE)
    lam_im = a_im.astype(f32)
    dt = jnp.exp(log_dt.astype(f32))[:, None]
    mag = jnp.exp(lam_re * dt)
    ab_re = mag * jnp.cos(lam_im * dt)
    ab_im = mag * jnp.sin(lam_im * dt)
    inv = 1.0 / (lam_re * lam_re + lam_im * lam_im)
    z_re = ((ab_re - 1.0) * lam_re + ab_im * lam_im) * inv
    z_im = (ab_im * lam_re - (ab_re - 1.0) * lam_im) * inv
    br, bi = b_re.astype(f32), b_im.astype(f32)
    bb_re = z_re[..., None] * br - z_im[..., None] * bi
    bb_im = z_re[..., None] * bi + z_im[..., None] * br
    bu_re = jnp.einsum('blgn,gpn->lbgp', ug, bb_re)
    bu_im = jnp.einsum('blgn,gpn->lbgp', ug, bb_im)
    a_seq_re = jnp.broadcast_to(ab_re, (L, 1) + ab_re.shape)
    a_seq_im = jnp.broadcast_to(ab_im, (L, 1) + ab_im.shape)
    _, _, xr, xi = lax.associative_scan(_complex_affine_combine, (a_seq_re, a_seq_im, bu_re, bu_im), axis=0)
    y = (jnp.einsum('lbgp,gnp->blgn', xr, c_re.astype(f32))
         - jnp.einsum('lbgp,gnp->blgn', xi, c_im.astype(f32)))
    y = y + d_skip.astype(f32).reshape(S5_GROUPS, S5_GROUP) * ug
    y = jax.nn.gelu(y.reshape(bsz, L, S5_WIDTH)).astype(u.dtype)
    return y * jax.nn.sigmoid(y @ w_glu + b_glu)


def moe_ffn(x, router_w, router_b, w1, b1, w2, b2):
    f32 = jnp.float32
    bsz, L, D = x.shape
    T = bsz * L
    A = T * TOP_K
    n_blocks = -(-(A + N_EXPERTS * (MOE_BLOCK - 1)) // MOE_BLOCK)
    P = n_blocks * MOE_BLOCK
    xt = x.reshape(T, D)
    logits = (xt @ router_w + router_b).astype(f32)
    top_logit, top_e = lax.top_k(logits, TOP_K)
    gate = jax.nn.softmax(top_logit, axis=-1)
    e_flat = top_e.reshape(A)
    order = jnp.argsort(e_flat)
    e_sorted = e_flat[order]
    counts = jnp.bincount(e_flat, length=N_EXPERTS)
    padded = (counts + MOE_BLOCK - 1) // MOE_BLOCK * MOE_BLOCK
    start = jnp.cumsum(counts) - counts
    pend = jnp.cumsum(padded)
    pstart = pend - padded
    dest = pstart[e_sorted] + jnp.arange(A) - start[e_sorted]
    tok_buf = jnp.full((P,), T, jnp.int32).at[dest].set((order // TOP_K).astype(jnp.int32))
    gate_buf = jnp.zeros((P,), f32).at[dest].set(gate.reshape(A)[order])
    x_buf = xt.at[tok_buf].get(mode='fill', fill_value=0)
    block_e = jnp.minimum(jnp.searchsorted(pend, jnp.arange(n_blocks) * MOE_BLOCK, side='right'), N_EXPERTS - 1)

    def expert_block(args):
        xb, e = args
        h = xb @ w1[e] + b1[e]
        glu = jnp.minimum(h[:, :D_FF], SWIGLU_LIMIT)
        lin = jnp.clip(h[:, D_FF:], -SWIGLU_LIMIT, SWIGLU_LIMIT)
        act = glu * jax.nn.sigmoid(SWIGLU_ALPHA * glu) * (lin + 1.0)
        return act @ w2[e] + b2[e]

    y_buf = lax.map(expert_block, (x_buf.reshape(n_blocks, MOE_BLOCK, D), block_e)).reshape(P, D)
    y = jax.ops.segment_sum(y_buf.astype(f32) * gate_buf[:, None], tok_buf, num_segments=T)
    return y.reshape(bsz, L, D).astype(x.dtype)


def even_mixer(x, w_in, w_gate2, b_gate2, norm_g, a_re, a_im, log_dt, b_re, b_im, c_re, c_im, d_skip, w_glu, b_glu):
    bsz, L, D = x.shape
    n_pad = -(-w_in.shape[1] // 512) * 512
    w_pad = jnp.pad(w_in, ((0, 0), (0, n_pad - w_in.shape[1])))
    h = _matmul(x.reshape(bsz * L, D), w_pad)[:, :w_in.shape[1]].reshape(bsz, L, -1)
    q, k, v, r, g_lr, u = _split(h, EV_SIZES)
    heads = lambda t, d: t.reshape(bsz, L, GLA_HEADS, d)
    o_a = gla_mix(heads(q, GLA_DK), heads(k, GLA_DK), heads(v, GLA_DV), heads(r, GLA_DV), g_lr, w_gate2, b_gate2, norm_g)
    o_b = s5_mix(u, a_re, a_im, log_dt, b_re, b_im, c_re, c_im, d_skip, w_glu, b_glu)
    return jnp.concatenate([o_a, o_b], axis=-1)


def kernel(x, ev_w_in, gla_w_gate2, gla_b_gate2, gla_norm_g, s5_a_re, s5_a_im, s5_log_dt, s5_b_re, s5_b_im,
           s5_c_re, s5_c_im, s5_d, s5_w_glu, s5_b_glu, ev_w_out, od_w_in, od_w_out, ln1_g, ln1_b, ln2_g, ln2_b,
           router_w, router_b, moe_w1, moe_b1, moe_w2, moe_b2):
    bsz, L, D = x.shape
    T = bsz * L
    x = x.reshape(T, D)
    for layer in range(DEPTH):
        j = layer // 2
        if layer % 2 == 0:
            mix = even_mixer(x.reshape(bsz, L, D), ev_w_in[j], gla_w_gate2[j], gla_b_gate2[j], gla_norm_g[j],
                             s5_a_re[j], s5_a_im[j], s5_log_dt[j], s5_b_re[j], s5_b_im[j], s5_c_re[j], s5_c_im[j],
                             s5_d[j], s5_w_glu[j], s5_b_glu[j])
            x = _matmul_res_ln(mix.reshape(T, -1), ev_w_out[j], x, ln1_g[layer], ln1_b[layer])
        else:
            o_c, o_d = _odd_layer_mix(x, od_w_in[j], bsz, L)
            x = _matmul2_res_ln(o_c, o_d, od_w_out[j], x, ln1_g[layer], ln1_b[layer])
        ffn = moe_ffn(x.reshape(bsz, L, D), router_w[layer], router_b[layer], moe_w1[layer], moe_b1[layer],
                      moe_w2[layer], moe_b2[layer])
        x = _res_ln(ffn.reshape(T, D), x, ln2_g[layer], ln2_b[layer])
    return x.reshape(bsz, L, D)
```

```python
import math
from functools import partial

import numpy as np
import jax
import jax.numpy as jnp
from jax import lax
from jax.experimental import pallas as pl
from jax.experimental.pallas import tpu as pltpu

D_MODEL = 1024
DEPTH = 4
DEEPNORM_ALPHA = (2.0 * DEPTH) ** 0.25
LN_EPS = 1e-5
MIX_WIDTH = D_MODEL

GLA_HEADS = 4
GLA_DV = MIX_WIDTH // 2 // GLA_HEADS
GLA_DK = GLA_DV // 2
GLA_GATE_RANK = 16
GLA_GATE_TAU = 16.0
GLA_CHUNK = 64

S5_WIDTH = MIX_WIDTH // 2
S5_GROUP = 16
S5_GROUPS = S5_WIDTH // S5_GROUP
S5_STATE = 64
S5_MAX_RE = -1e-4

EV_SIZES = (GLA_HEADS * GLA_DK, GLA_HEADS * GLA_DK, GLA_HEADS * GLA_DV, GLA_HEADS * GLA_DV, GLA_GATE_RANK, S5_WIDTH)

DSA_HEADS = 8
DSA_HEAD_DIM = 64
DSA_WIDTH = DSA_HEADS * DSA_HEAD_DIM
IDX_HEADS = 4
IDX_DIM = 64
DSA_TOPK_MAX = 256

DIL_PATTERNS = ((128, 1), (512, 4), (2048, 16))
DIL_GROUPS = len(DIL_PATTERNS)
DIL_HEADS = 8
DIL_HEAD_DIM = 64
DIL_WIDTH = DIL_HEADS * DIL_HEAD_DIM

OD_SIZES = (DSA_WIDTH, DSA_WIDTH, DSA_WIDTH, IDX_HEADS * IDX_DIM, IDX_DIM, IDX_HEADS, 3 * DIL_GROUPS * DIL_WIDTH)

N_EXPERTS = 32
TOP_K = 4
D_FF = D_MODEL
SWIGLU_ALPHA = 1.702
SWIGLU_LIMIT = 7.0
MOE_BLOCK = 512

LANES = 128
VMEM_LIMIT_BYTES = 48 * 1024 * 1024
NEG_BIG = -1e30
INT_MIN = -2 ** 31
INT_MAX = 2 ** 31 - 1
KEY_NEG_INF = -0x7F800000

OD_Q, OD_K, OD_V, OD_QI, OD_DIL, OD_KI = 0, 512, 1024, 1536, 2048, 2048 + 9 * 512
OD_COLS_PADDED = 7168


def _split(h, sizes):
    return jnp.split(h, [int(i) for i in np.cumsum(sizes)[:-1]], axis=-1)


def _params(*sem):
    return pltpu.CompilerParams(dimension_semantics=sem, vmem_limit_bytes=VMEM_LIMIT_BYTES)


def _mm_kernel(x_ref, w_ref, o_ref):
    o_ref[...] = jnp.dot(x_ref[...].astype(jnp.bfloat16), w_ref[...].astype(jnp.bfloat16),
                         preferred_element_type=jnp.float32).astype(o_ref.dtype)


def _matmul(x, w, out_dtype=jnp.float32, tm=512, tn=512):
    T, K = x.shape
    N = w.shape[1]
    tm, tn = min(tm, T), min(tn, N)
    assert T % tm == 0 and N % tn == 0
    return pl.pallas_call(
        _mm_kernel,
        grid=(T // tm, N // tn),
        in_specs=[pl.BlockSpec((tm, K), lambda i, j: (i, 0)),
                  pl.BlockSpec((K, tn), lambda i, j: (0, j))],
        out_specs=pl.BlockSpec((tm, tn), lambda i, j: (i, j)),
        out_shape=jax.ShapeDtypeStruct((T, N), out_dtype),
        compiler_params=_params("parallel", "arbitrary"),
    )(x, w)


def _layer_norm_rows(z, g, b):
    mu = jnp.mean(z, axis=-1, keepdims=True)
    zc = z - mu
    var = jnp.mean(zc * zc, axis=-1, keepdims=True)
    return zc * lax.rsqrt(var + LN_EPS) * g + b


def _mm_res_ln_kernel(a_ref, w_ref, x_ref, g_ref, b_ref, o_ref):
    mix = jnp.dot(a_ref[...].astype(jnp.bfloat16), w_ref[...].astype(jnp.bfloat16),
                  preferred_element_type=jnp.float32)
    o_ref[...] = _layer_norm_rows(DEEPNORM_ALPHA * x_ref[...] + mix, g_ref[...], b_ref[...])


def _matmul_res_ln(a, w, x, g, b, tm=512):
    T, K = a.shape
    D = w.shape[1]
    tm = min(tm, T)
    return pl.pallas_call(
        _mm_res_ln_kernel,
        grid=(T // tm,),
        in_specs=[pl.BlockSpec((tm, K), lambda i: (i, 0)),
                  pl.BlockSpec((K, D), lambda i: (0, 0)),
                  pl.BlockSpec((tm, D), lambda i: (i, 0)),
                  pl.BlockSpec((1, D), lambda i: (0, 0)),
                  pl.BlockSpec((1, D), lambda i: (0, 0))],
        out_specs=pl.BlockSpec((tm, D), lambda i: (i, 0)),
        out_shape=jax.ShapeDtypeStruct((T, D), jnp.float32),
        compiler_params=_params("parallel"),
    )(a, w, x, g.reshape(1, D), b.reshape(1, D))


def _mm2_res_ln_kernel(a1_ref, a2_ref, w1_ref, w2_ref, x_ref, g_ref, b_ref, o_ref):
    mix = jnp.dot(a1_ref[...], w1_ref[...], preferred_element_type=jnp.float32)
    mix += jnp.dot(a2_ref[...], w2_ref[...], preferred_element_type=jnp.float32)
    o_ref[...] = _layer_norm_rows(DEEPNORM_ALPHA * x_ref[...] + mix, g_ref[...], b_ref[...])


def _matmul2_res_ln(a1, a2, w, x, g, b, tm=512):
    T, K1 = a1.shape
    D = w.shape[1]
    tm = min(tm, T)
    wb = w.astype(jnp.bfloat16)
    return pl.pallas_call(
        _mm2_res_ln_kernel,
        grid=(T // tm,),
        in_specs=[pl.BlockSpec((tm, K1), lambda i: (i, 0)),
                  pl.BlockSpec((tm, a2.shape[1]), lambda i: (i, 0)),
                  pl.BlockSpec((K1, D), lambda i: (0, 0)),
                  pl.BlockSpec((a2.shape[1], D), lambda i: (1, 0)),
                  pl.BlockSpec((tm, D), lambda i: (i, 0)),
                  pl.BlockSpec((1, D), lambda i: (0, 0)),
                  pl.BlockSpec((1, D), lambda i: (0, 0))],
        out_specs=pl.BlockSpec((tm, D), lambda i: (i, 0)),
        out_shape=jax.ShapeDtypeStruct((T, D), jnp.float32),
        compiler_params=_params("parallel"),
    )(a1, a2, wb, wb, x, g.reshape(1, D), b.reshape(1, D))


def _res_ln_kernel(y_ref, x_ref, g_ref, b_ref, o_ref):
    o_ref[...] = _layer_norm_rows(DEEPNORM_ALPHA * x_ref[...] + y_ref[...], g_ref[...], b_ref[...])


def _res_ln(y, x, g, b, tm=512):
    T, D = x.shape
    tm = min(tm, T)
    return pl.pallas_call(
        _res_ln_kernel,
        grid=(T // tm,),
        in_specs=[pl.BlockSpec((tm, D), lambda i: (i, 0)),
                  pl.BlockSpec((tm, D), lambda i: (i, 0)),
                  pl.BlockSpec((1, D), lambda i: (0, 0)),
                  pl.BlockSpec((1, D), lambda i: (0, 0))],
        out_specs=pl.BlockSpec((tm, D), lambda i: (i, 0)),
        out_shape=jax.ShapeDtypeStruct((T, D), jnp.float32),
        compiler_params=_params("parallel"),
    )(y, x, g.reshape(1, D), b.reshape(1, D))


def _index_keys(qi_blk, ki_blk, wi_blk, q0, k0):
    tq, tk = qi_blk.shape[0], ki_blk.shape[0]
    lhs = jnp.concatenate([qi_blk[:, LANES * h:LANES * (h + 1)] for h in range(IDX_HEADS)], axis=0)
    d = lax.dot_general(lhs, ki_blk, (((1,), (1,)), ((), ())), preferred_element_type=jnp.float32)
    sc = wi_blk[:, 0:1] * jnp.maximum(d[0:tq], 0.0)
    for h in range(1, IDX_HEADS):
        sc = sc + wi_blk[:, h:h + 1] * jnp.maximum(d[h * tq:(h + 1) * tq], 0.0)
    qpos = q0 + lax.broadcasted_iota(jnp.int32, (tq, 1), 0)
    kpos = k0 + lax.broadcasted_iota(jnp.int32, (1, tk), 1)
    sc = jnp.where(kpos <= qpos, sc, -jnp.inf)
    bits = lax.bitcast_convert_type(sc, jnp.int32)
    return jnp.where(bits < 0, INT_MIN - bits, bits)


def _dsa_select_kernel(qi_ref, wi_ref, ki_ref, thr_ref, cut_ref, key_ref, *, tq, topk, row_group):
    i = pl.program_id(1)
    nblk = i + 1
    q0 = i * tq
    qi_blk = qi_ref[...]
    wi_blk = wi_ref[...]

    def fill(j, carry):
        k0 = pl.multiple_of(j * tq, tq)
        key_ref[j] = _index_keys(qi_blk, ki_ref[pl.ds(k0, tq), :], wi_blk, q0, k0)
        return carry

    lax.fori_loop(0, nblk, fill, 0)

    nslab = tq // LANES
    lane = lax.broadcasted_iota(jnp.int32, (row_group, LANES), 1)

    def count(r0, pred):
        def body(j, acc):
            for s in range(nslab):
                kk = key_ref[j, r0:r0 + row_group, s * LANES:(s + 1) * LANES]
                acc = acc + jnp.where(pred(kk, j * tq + s * LANES), 1, 0)
            return acc
        acc = lax.fori_loop(0, nblk, body, jnp.zeros((row_group, LANES), jnp.int32))
        return jnp.sum(acc.astype(jnp.float32), axis=1, keepdims=True).astype(jnp.int32)

    for r0 in range(0, tq, row_group):
        def bit_step(p, thr):
            cand = thr ^ lax.shift_left(jnp.int32(1), 31 - p)
            candb = jnp.broadcast_to(cand, (row_group, LANES))
            cnt = count(r0, lambda kk, base: kk >= candb)
            return jnp.where(cnt >= topk, cand, thr)

        thr = lax.fori_loop(0, 32, bit_step, jnp.full((row_group, 1), INT_MIN, jnp.int32))
        thrb = jnp.broadcast_to(thr, (row_group, LANES))
        n_gt = count(r0, lambda kk, base: kk > thrb)
        n_ge = count(r0, lambda kk, base: kk >= thrb)
        need = topk - n_gt
        tie = jnp.logical_and(n_ge > topk, thr > KEY_NEG_INF)
        any_tie = jnp.max(jnp.where(tie, 1.0, 0.0)) > 0.5

        def resolve_ties():
            def col_step(p, c):
                cand = c | lax.shift_left(jnp.int32(1), 14 - p)
                cnt = count(r0, lambda kk, base: jnp.logical_and(kk == thrb, lane < cand - base))
                return jnp.where(cnt < need, cand, c)
            c = lax.fori_loop(0, 15, col_step, jnp.zeros((row_group, 1), jnp.int32))
            return jnp.where(tie, c + 1, INT_MAX)

        cut = lax.cond(any_tie, resolve_ties, lambda: jnp.full((row_group, 1), INT_MAX, jnp.int32))
        thr_ref[r0:r0 + row_group, :] = jnp.maximum(thr, KEY_NEG_INF + 1)
        cut_ref[r0:r0 + row_group, :] = cut


def _dsa_select(hb, wi, bsz, L, tq):
    T = bsz * L
    nq = L // tq
    topk = min(DSA_TOPK_MAX, L // 4)
    kern = partial(_dsa_select_kernel, tq=tq, topk=topk, row_group=min(64, tq))
    return pl.pallas_call(
        kern,
        grid=(bsz, nq),
        in_specs=[pl.BlockSpec((tq, 512), lambda b, i: (b * nq + i, OD_QI // 512)),
                  pl.BlockSpec((tq, LANES), lambda b, i: (b * nq + i, 0)),
                  pl.BlockSpec((L, LANES), lambda b, i: (b, OD_KI // LANES))],
        out_specs=[pl.BlockSpec((tq, 1), lambda b, i: (b * nq + i, 0)),
                   pl.BlockSpec((tq, 1), lambda b, i: (b * nq + i, 0))],
        out_shape=[jax.ShapeDtypeStruct((T, 1), jnp.int32), jax.ShapeDtypeStruct((T, 1), jnp.int32)],
        scratch_shapes=[pltpu.VMEM((nq, tq, tq), jnp.int32)],
        compiler_params=_params("parallel", "arbitrary"),
    )(hb, wi, hb)


def _split_head_pair(x_pair):
    lane = lax.broadcasted_iota(jnp.int32, x_pair.shape, 1)
    zero = jnp.zeros_like(x_pair)
    return jnp.where(lane < 64, x_pair, zero), jnp.where(lane >= 64, x_pair, zero)


def _dsa_attn_kernel(q_ref, qi_ref, wi_ref, thr_ref, cut_ref, k_ref, v_ref, ki_ref, o_ref,
                     qm_ref, m_ref, l_ref, acc_ref, *, tq, tk, nk):
    i = pl.program_id(1)
    j = pl.program_id(2)
    npair = DSA_HEADS // 2

    @pl.when(j == 0)
    def _():
        for p in range(npair):
            qa, qb = _split_head_pair(q_ref[:, LANES * p:LANES * (p + 1)])
            qm_ref[2 * p] = qa
            qm_ref[2 * p + 1] = qb
        m_ref[...] = jnp.full(m_ref.shape, NEG_BIG, jnp.float32)
        l_ref[...] = jnp.zeros(l_ref.shape, jnp.float32)
        acc_ref[...] = jnp.zeros(acc_ref.shape, jnp.float32)

    @pl.when(j * tk < (i + 1) * tq)
    def _():
        key = _index_keys(qi_ref[...], ki_ref[...], wi_ref[...], i * tq, j * tk)
        col = j * tk + lax.broadcasted_iota(jnp.int32, (1, tk), 1)
        thr = thr_ref[...]
        sel = jnp.logical_or(key > thr, jnp.logical_and(key == thr, col < cut_ref[...]))
        for h in range(DSA_HEADS):
            p = h // 2
            s = lax.dot_general(qm_ref[h], k_ref[:, LANES * p:LANES * (p + 1)], (((1,), (1,)), ((), ())),
                                preferred_element_type=jnp.float32)
            s = jnp.where(sel, s, NEG_BIG)
            m_prev = m_ref[h]
            m_new = jnp.maximum(m_prev, jnp.max(s, axis=1, keepdims=True))
            alpha = jnp.exp(m_prev - m_new)
            e = jnp.exp(s - m_new)
            l_ref[h] = alpha * l_ref[h] + jnp.sum(e, axis=1, keepdims=True)
            pv = jnp.dot(e.astype(jnp.bfloat16), v_ref[:, LANES * p:LANES * (p + 1)],
                         preferred_element_type=jnp.float32)
            acc_ref[h] = alpha * acc_ref[h] + pv
            m_ref[h] = m_new

    @pl.when(j == nk - 1)
    def _():
        lane = lax.broadcasted_iota(jnp.int32, (tq, LANES), 1)
        for p in range(npair):
            oa = acc_ref[2 * p] / l_ref[2 * p]
            ob = acc_ref[2 * p + 1] / l_ref[2 * p + 1]
            o_ref[:, LANES * p:LANES * (p + 1)] = jnp.where(lane < 64, oa, ob).astype(o_ref.dtype)


def _dsa_attention(hb, wi, thr, cut, bsz, L, tq, tk):
    T = bsz * L
    nq, nk = L // tq, L // tk
    kern = partial(_dsa_attn_kernel, tq=tq, tk=tk, nk=nk)

    def qmap(col):
        return lambda b, i, j: (b * nq + i, col)

    def kmap(col):
        return lambda b, i, j: (b * nk + jnp.minimum(j, ((i + 1) * tq - 1) // tk), col)

    return pl.pallas_call(
        kern,
        grid=(bsz, nq, nk),
        in_specs=[pl.BlockSpec((tq, 512), qmap(OD_Q // 512)),
                  pl.BlockSpec((tq, 512), qmap(OD_QI // 512)),
                  pl.BlockSpec((tq, LANES), qmap(0)),
                  pl.BlockSpec((tq, 1), qmap(0)),
                  pl.BlockSpec((tq, 1), qmap(0)),
                  pl.BlockSpec((tk, 512), kmap(OD_K // 512)),
                  pl.BlockSpec((tk, 512), kmap(OD_V // 512)),
                  pl.BlockSpec((tk, LANES), kmap(OD_KI // LANES))],
        out_specs=pl.BlockSpec((tq, DSA_WIDTH), qmap(0)),
        out_shape=jax.ShapeDtypeStruct((T, DSA_WIDTH), jnp.bfloat16),
        scratch_shapes=[pltpu.VMEM((DSA_HEADS, tq, LANES), jnp.bfloat16),
                        pltpu.VMEM((DSA_HEADS, tq, 1), jnp.float32),
                        pltpu.VMEM((DSA_HEADS, tq, 1), jnp.float32),
                        pltpu.VMEM((DSA_HEADS, tq, LANES), jnp.float32)],
        compiler_params=_params("parallel", "parallel", "arbitrary"),
    )(hb, hb, wi, thr, cut, hb, hb, hb)


def _dilated_kernel(q_ref, kp_ref, kc_ref, vp_ref, vc_ref, o_ref, lse_ref, *, tq):
    a = pl.program_id(2)
    row = lax.broadcasted_iota(jnp.int32, (tq, 2 * tq), 0)
    c = lax.broadcasted_iota(jnp.int32, (tq, 2 * tq), 1)
    first_col = jnp.where(a == 0, tq, 0)
    valid = jnp.logical_and(jnp.logical_and(c >= row, c <= row + tq), c >= first_col)
    lane = lax.broadcasted_iota(jnp.int32, (tq, LANES), 1)
    for p in range(DIL_HEADS // 2):
        sl = slice(LANES * p, LANES * (p + 1))
        kk = jnp.concatenate([kp_ref[:, sl], kc_ref[:, sl]], axis=0)
        vv = jnp.concatenate([vp_ref[:, sl], vc_ref[:, sl]], axis=0)
        outs, lses = [], []
        for qh in _split_head_pair(q_ref[:, sl]):
            s = lax.dot_general(qh, kk, (((1,), (1,)), ((), ())), preferred_element_type=jnp.float32)
            s = jnp.where(valid, s, NEG_BIG)
            m = jnp.max(s, axis=1, keepdims=True)
            e = jnp.exp(s - m)
            l = jnp.sum(e, axis=1, keepdims=True)
            outs.append(jnp.dot(e.astype(jnp.bfloat16), vv, preferred_element_type=jnp.float32) / l)
            lses.append(m + jnp.log(l))
        o_ref[:, sl] = jnp.where(lane < 64, outs[0], outs[1])
        lse_ref[:, sl] = jnp.where(lane < 64, lses[0], lses[1])


def _dilated_group(hb, bsz, L, g, tq):
    window, dil = DIL_PATTERNS[g]
    assert window // dil == tq
    T = bsz * L
    M = L // dil
    nb = M // tq
    ncol = hb.shape[1] // 512
    hv = hb.reshape(bsz * M, dil * hb.shape[1])
    qcol, kcol, vcol = (OD_DIL // 512 + s * DIL_GROUPS + g for s in range(3))

    def cur(col):
        return lambda b, r, a: (b * nb + a, r * ncol + col)

    def prev(col):
        return lambda b, r, a: (b * nb + jnp.maximum(a - 1, 0), r * ncol + col)

    blk = (tq, 512)
    out_map = lambda b, r, a: (b * nb + a, r)
    o, lse = pl.pallas_call(
        partial(_dilated_kernel, tq=tq),
        grid=(bsz, dil, nb),
        in_specs=[pl.BlockSpec(blk, cur(qcol)), pl.BlockSpec(blk, prev(kcol)), pl.BlockSpec(blk, cur(kcol)),
                  pl.BlockSpec(blk, prev(vcol)), pl.BlockSpec(blk, cur(vcol))],
        out_specs=[pl.BlockSpec(blk, out_map), pl.BlockSpec(blk, out_map)],
        out_shape=[jax.ShapeDtypeStruct((bsz * M, dil * 512), jnp.float32)] * 2,
        compiler_params=_params("parallel", "parallel", "arbitrary"),
    )(hv, hv, hv, hv, hv)
    return o.reshape(T, 512), lse.reshape(T, 512)


def _dilated_combine_kernel(o0, o1, o2, l0, l1, l2, out_ref):
    a, b, c = l0[...], l1[...], l2[...]
    m = jnp.maximum(jnp.maximum(a, b), c)
    ea, eb, ec = jnp.exp(a - m), jnp.exp(b - m), jnp.exp(c - m)
    num = ea * o0[...] + eb * o1[...] + ec * o2[...]
    out_ref[...] = (num / (ea + eb + ec)).astype(out_ref.dtype)


def _dilated_combine(outs, lses, tm=512):
    T, W = outs[0].shape
    tm = min(tm, T)
    spec = pl.BlockSpec((tm, W), lambda i: (i, 0))
    return pl.pallas_call(
        _dilated_combine_kernel,
        grid=(T // tm,),
        in_specs=[spec] * 6,
        out_specs=spec,
        out_shape=jax.ShapeDtypeStruct((T, W), jnp.bfloat16),
        compiler_params=_params("parallel"),
    )(*outs, *lses)


def _odd_weights(w_in):
    D = w_in.shape[0]
    q, k, v, qi, ki, wi, dil = _split(w_in, OD_SIZES)
    zeros = lambda n: jnp.zeros((D, n), w_in.dtype)
    qi_exp = jnp.concatenate(
        [jnp.concatenate([qi[:, IDX_DIM * h:IDX_DIM * (h + 1)] * IDX_DIM ** -0.5, zeros(LANES - IDX_DIM)], axis=1)
         for h in range(IDX_HEADS)], axis=1)
    dil = dil.reshape(D, 3, DIL_GROUPS * DIL_WIDTH)
    dil = jnp.concatenate([dil[:, 0] * DIL_HEAD_DIM ** -0.5, dil[:, 1], dil[:, 2]], axis=1)
    wb = jnp.concatenate([q * DSA_HEAD_DIM ** -0.5, k, v, qi_exp, dil, ki, zeros(LANES - IDX_DIM),
                          zeros(OD_COLS_PADDED - OD_KI - LANES)], axis=1).astype(jnp.bfloat16)
    wwi = jnp.concatenate([wi * IDX_HEADS ** -0.5, zeros(LANES - IDX_HEADS)], axis=1).astype(jnp.bfloat16)
    return wb, wwi


def _odd_layer_mix(x2, w_in, bsz, L):
    wb, wwi = _odd_weights(w_in)
    hb = _matmul(x2, wb, jnp.bfloat16, tm=1024, tn=1024)
    wi = _matmul(x2, wwi, jnp.float32, tm=1024, tn=LANES)
    tq_sel = min(256, L)
    thr, cut = _dsa_select(hb, wi, bsz, L, tq_sel)
    o_c = _dsa_attention(hb, wi, thr, cut, bsz, L, min(256, L), min(512, L))
    groups = [_dilated_group(hb, bsz, L, g, 128) for g in range(DIL_GROUPS)]
    o_d = _dilated_combine([o for o, _ in groups], [l for _, l in groups])
    return o_c, o_d


EV_Q, EV_K, EV_V, EV_R, EV_U, EV_G = 0, 256, 512, 1024, 1536, 2048
EV_COLS_PADDED = 2560
GLA_ROWS = 512
S5_CHUNK = 64


def _gla_kernel(q_ref, k_ref, v_ref, r_ref, g_ref, wg_ref, bg_ref, ng_ref, o_ref, st_ref, *, rows):
    C = GLA_CHUNK

    @pl.when(pl.program_id(1) == 0)
    def _():
        st_ref[...] = jnp.zeros(st_ref.shape, jnp.float32)

    ri = lax.broadcasted_iota(jnp.int32, (C, C), 0)
    ci = lax.broadcasted_iota(jnp.int32, (C, C), 1)
    causal = ci <= ri
    tril = jnp.where(causal, 1.0, 0.0).astype(jnp.float32)
    wg = wg_ref[...].astype(jnp.bfloat16)
    bg = bg_ref[...]
    ng = ng_ref[...]

    def chunk(c, carry):
        r0 = pl.multiple_of(c * C, C)
        rs = pl.ds(r0, C)
        logit = jnp.dot(g_ref[rs, :].astype(jnp.bfloat16), wg, preferred_element_type=jnp.float32) + bg
        log_a = jax.nn.log_sigmoid(logit) / GLA_GATE_TAU
        bcum = jnp.dot(tril, log_a, precision=lax.Precision.HIGHEST, preferred_element_type=jnp.float32)
        b_last = bcum[C - 1:C, :]
        q_t = (q_ref[rs, :] * jnp.exp(bcum)).astype(jnp.bfloat16)
        k_t = (k_ref[rs, :] * jnp.exp(-bcum)).astype(jnp.bfloat16)
        k_end = (k_ref[rs, :] * jnp.exp(b_last - bcum)).astype(jnp.bfloat16)
        dec = jnp.exp(b_last)
        for p in range(GLA_HEADS // 2):
            sl = slice(LANES * p, LANES * (p + 1))
            q_halves = _split_head_pair(q_t[:, sl])
            ke_halves = _split_head_pair(k_end[:, sl])
            for half in range(2):
                h = 2 * p + half
                hs = slice(GLA_DV * h, GLA_DV * (h + 1))
                qm = q_halves[half]
                att = lax.dot_general(qm, k_t[:, sl], (((1,), (1,)), ((), ())), preferred_element_type=jnp.float32)
                att = jnp.where(causal, att, 0.0).astype(jnp.bfloat16)
                v_h = v_ref[rs, hs].astype(jnp.bfloat16)
                st = st_ref[h]
                o = jnp.dot(att, v_h, preferred_element_type=jnp.float32)
                o = o + lax.dot_general(qm, st.astype(jnp.bfloat16), (((1,), (1,)), ((), ())),
                                        preferred_element_type=jnp.float32)
                kv_t = lax.dot_general(v_h, ke_halves[half], (((0,), (0,)), ((), ())),
                                       preferred_element_type=jnp.float32)
                st_ref[h] = st * dec[:, sl] + kv_t
                o = o * lax.rsqrt(jnp.mean(o * o, axis=-1, keepdims=True) + LN_EPS) * ng
                o = o * jax.nn.silu(r_ref[rs, hs])
                o_ref[rs, hs] = o.astype(o_ref.dtype)
        return carry

    lax.fori_loop(0, rows // C, chunk, 0)


def _gla(hf, w_gate2, b_gate2, norm_g, bsz, L):
    T = bsz * L
    rows = min(GLA_ROWS, L)
    nb = L // rows
    dkw = GLA_HEADS * GLA_DK
    dvw = GLA_HEADS * GLA_DV
    wg = jnp.pad(w_gate2, ((0, LANES - GLA_GATE_RANK), (0, 0)))

    def rmap(col):
        return lambda b, i: (b * nb + i, col)

    const = lambda b, i: (0, 0)
    return pl.pallas_call(
        partial(_gla_kernel, rows=rows),
        grid=(bsz, nb),
        in_specs=[pl.BlockSpec((rows, dkw), rmap(EV_Q // dkw)),
                  pl.BlockSpec((rows, dkw), rmap(EV_K // dkw)),
                  pl.BlockSpec((rows, dvw), rmap(EV_V // dvw)),
                  pl.BlockSpec((rows, dvw), rmap(EV_R // dvw)),
                  pl.BlockSpec((rows, LANES), rmap(EV_G // LANES)),
                  pl.BlockSpec((LANES, dkw), const),
                  pl.BlockSpec((1, dkw), const),
                  pl.BlockSpec((1, GLA_DV), const)],
        out_specs=pl.BlockSpec((rows, dvw), rmap(0)),
        out_shape=jax.ShapeDtypeStruct((T, dvw), jnp.bfloat16),
        scratch_shapes=[pltpu.VMEM((GLA_HEADS, GLA_DV, LANES), jnp.float32)],
        compiler_params=_params("parallel", "arbitrary"),
    )(hf, hf, hf, hf, hf, wg, b_gate2.reshape(1, dkw), norm_g.reshape(1, GLA_DV))


def _s5_tables(a_re, a_im, log_dt, b_re, b_im, c_re, c_im, d_skip):
    f32 = jnp.float32
    Cs, G, P, N = S5_CHUNK, S5_GROUPS, S5_STATE, S5_GROUP
    lam_re = jnp.minimum(a_re.astype(f32), S5_MAX_RE)
    lam_im = a_im.astype(f32)
    dt = jnp.exp(log_dt.astype(f32))[:, None]
    mag = jnp.exp(lam_re * dt)
    ab_re = mag * jnp.cos(lam_im * dt)
    ab_im = mag * jnp.sin(lam_im * dt)
    inv = 1.0 / (lam_re * lam_re + lam_im * lam_im)
    z_re = ((ab_re - 1.0) * lam_re + ab_im * lam_im) * inv
    z_im = (ab_im * lam_re - (ab_re - 1.0) * lam_im) * inv
    br, bi = b_re.astype(f32), b_im.astype(f32)
    bb_re = z_re[..., None] * br - z_im[..., None] * bi
    bb_im = z_re[..., None] * bi + z_im[..., None] * br
    kk = jnp.arange(Cs + 1, dtype=f32)[:, None, None]
    pmag = jnp.exp(kk * (lam_re * dt))
    pw_re = pmag * jnp.cos(kk * (lam_im * dt))
    pw_im = pmag * jnp.sin(kk * (lam_im * dt))
    cr, ci = c_re.astype(f32), c_im.astype(f32)
    ca_re = cr[None] * pw_re[:, :, None, :] - ci[None] * pw_im[:, :, None, :]
    ca_im = cr[None] * pw_im[:, :, None, :] + ci[None] * pw_re[:, :, None, :]
    hi = lax.Precision.HIGHEST
    kern = (jnp.einsum('kgnp,gpm->kgnm', ca_re[:Cs], bb_re, precision=hi)
            - jnp.einsum('kgnp,gpm->kgnm', ca_im[:Cs], bb_im, precision=hi))
    jj = jnp.arange(Cs)[:, None]
    ii = jnp.arange(Cs)[None, :]
    tz = jnp.where((ii >= jj)[:, :, None, None, None], kern[jnp.maximum(ii - jj, 0)], 0.0)
    tz = jnp.transpose(tz, (2, 0, 4, 1, 3)).reshape(G, Cs * N, Cs * N)
    rev_re, rev_im = pw_re[Cs - 1::-1][:Cs], pw_im[Cs - 1::-1][:Cs]
    ws_re = rev_re[..., None] * bb_re[None] - rev_im[..., None] * bb_im[None]
    ws_im = rev_re[..., None] * bb_im[None] + rev_im[..., None] * bb_re[None]
    to_ws = lambda w: jnp.pad(jnp.transpose(w, (1, 0, 3, 2)).reshape(G, Cs * N, P), ((0, 0), (0, 0), (0, LANES - P)))
    to_wo = lambda w: jnp.pad(jnp.transpose(w, (1, 3, 0, 2)).reshape(G, P, Cs * N), ((0, 0), (0, LANES - P), (0, 0)))
    a_cs = jnp.stack([jnp.pad(pw_re[Cs], ((0, 0), (0, LANES - P))), jnp.pad(pw_im[Cs], ((0, 0), (0, LANES - P)))], axis=1)
    d_exp = jnp.tile(d_skip.astype(f32).reshape(G, 1, N), (1, 1, Cs))
    bf = jnp.bfloat16
    return (tz.astype(bf), to_ws(ws_re).astype(bf), to_ws(ws_im).astype(bf),
            to_wo(ca_re[1:]).astype(bf), to_wo(-ca_im[1:]).astype(bf), a_cs, d_exp)


def _s5_kernel(u_ref, tz_ref, wsr_ref, wsi_ref, wor_ref, woi_ref, acs_ref, d_ref, y_ref, xr_ref, xi_ref,
               *, nchunk, nbatch):
    u32 = u_ref[0]
    u = u32.astype(jnp.bfloat16)
    xr_ref[...] = jnp.dot(u, wsr_ref[0], preferred_element_type=jnp.float32)
    xi_ref[...] = jnp.dot(u, wsi_ref[0], preferred_element_type=jnp.float32)
    ar = acs_ref[0, 0:1, :]
    ai = acs_ref[0, 1:2, :]

    def step(c, carry):
        new = []
        for b in range(nbatch):
            sr, si = carry[2 * b], carry[2 * b + 1]
            row = pl.ds(b * nchunk + c, 1)
            lr, li = xr_ref[row, :], xi_ref[row, :]
            xr_ref[row, :] = sr
            xi_ref[row, :] = si
            new += [ar * sr - ai * si + lr, ar * si + ai * sr + li]
        return tuple(new)

    zero = jnp.zeros((1, LANES), jnp.float32)
    lax.fori_loop(0, nchunk, step, (zero,) * (2 * nbatch))
    y = jnp.dot(u, tz_ref[0], preferred_element_type=jnp.float32)
    y = y + jnp.dot(xr_ref[...].astype(jnp.bfloat16), wor_ref[0], preferred_element_type=jnp.float32)
    y = y + jnp.dot(xi_ref[...].astype(jnp.bfloat16), woi_ref[0], preferred_element_type=jnp.float32)
    y = y + d_ref[0] * u32
    y_ref[0] = jax.nn.gelu(y)


def _s5_scan(ug, tables, bsz, nchunk):
    G, R, W = ug.shape
    tz, wsr, wsi, wor, woi, a_cs, d_exp = tables
    gmap = lambda g: (g, 0, 0)
    return pl.pallas_call(
        partial(_s5_kernel, nchunk=nchunk, nbatch=bsz),
        grid=(G,),
        in_specs=[pl.BlockSpec((1, R, W), gmap), pl.BlockSpec((1, W, W), gmap),
                  pl.BlockSpec((1, W, LANES), gmap), pl.BlockSpec((1, W, LANES), gmap),
                  pl.BlockSpec((1, LANES, W), gmap), pl.BlockSpec((1, LANES, W), gmap),
                  pl.BlockSpec((1, 2, LANES), gmap), pl.BlockSpec((1, 1, W), gmap)],
        out_specs=pl.BlockSpec((1, R, W), gmap),
        out_shape=jax.ShapeDtypeStruct((G, R, W), jnp.float32),
        scratch_shapes=[pltpu.VMEM((R, LANES), jnp.float32), pltpu.VMEM((R, LANES), jnp.float32)],
        compiler_params=_params("parallel"),
    )(ug, tz, wsr, wsi, wor, woi, a_cs, d_exp)


def _glu_kernel(y_ref, w_ref, b_ref, o_ref):
    y = y_ref[...]
    gate = jnp.dot(y.astype(jnp.bfloat16), w_ref[...], preferred_element_type=jnp.float32) + b_ref[...]
    o_ref[...] = (y * jax.nn.sigmoid(gate)).astype(o_ref.dtype)


def _glu(y, w_glu, b_glu, tm=1024):
    T, W = y.shape
    tm = min(tm, T)
    return pl.pallas_call(
        _glu_kernel,
        grid=(T // tm,),
        in_specs=[pl.BlockSpec((tm, W), lambda i: (i, 0)),
                  pl.BlockSpec((W, W), lambda i: (0, 0)),
                  pl.BlockSpec((1, W), lambda i: (0, 0))],
        out_specs=pl.BlockSpec((tm, W), lambda i: (i, 0)),
        out_shape=jax.ShapeDtypeStruct((T, W), jnp.bfloat16),
        compiler_params=_params("parallel"),
    )(y, w_glu.astype(jnp.bfloat16), b_glu.reshape(1, W))


def _s5(hf, s5_params, w_glu, b_glu, bsz, L):
    T = bsz * L
    Cs, G, N = S5_CHUNK, S5_GROUPS, S5_GROUP
    nchunk = L // Cs
    u = hf[:, EV_U:EV_U + S5_WIDTH].reshape(bsz * nchunk, Cs, G, N)
    ug = jnp.transpose(u, (2, 0, 1, 3)).reshape(G, bsz * nchunk, Cs * N)
    y = _s5_scan(ug, _s5_tables(*s5_params), bsz, nchunk)
    y = jnp.transpose(y.reshape(G, bsz * nchunk, Cs, N), (1, 2, 0, 3)).reshape(T, S5_WIDTH)
    return _glu(y, w_glu, b_glu)


def _even_weights(w_in):
    D = w_in.shape[0]
    q, k, v, r, g_lr, u = _split(w_in, EV_SIZES)
    pad = jnp.zeros((D, EV_COLS_PADDED - EV_G - GLA_GATE_RANK), w_in.dtype)
    return jnp.concatenate([q * GLA_DK ** -0.5, k, v, r, u, g_lr, pad], axis=1).astype(jnp.bfloat16)


def _even_layer_mix(x2, w_in, w_gate2, b_gate2, norm_g, s5_params, w_glu, b_glu, bsz, L):
    hf = _matmul(x2, _even_weights(w_in), jnp.float32, tm=1024, tn=512)
    o_a = _gla(hf, w_gate2, b_gate2, norm_g, bsz, L)
    o_b = _s5(hf, s5_params, w_glu, b_glu, bsz, L)
    return o_a, o_b


def moe_ffn(x, router_w, router_b, w1, b1, w2, b2):
    f32 = jnp.float32
    bsz, L, D = x.shape
    T = bsz * L
    A = T * TOP_K
    n_blocks = -(-(A + N_EXPERTS * (MOE_BLOCK - 1)) // MOE_BLOCK)
    P = n_blocks * MOE_BLOCK
    xt = x.reshape(T, D)
    logits = (xt @ router_w + router_b).astype(f32)
    top_logit, top_e = lax.top_k(logits, TOP_K)
    gate = jax.nn.softmax(top_logit, axis=-1)
    e_flat = top_e.reshape(A)
    order = jnp.argsort(e_flat)
    e_sorted = e_flat[order]
    counts = jnp.bincount(e_flat, length=N_EXPERTS)
    padded = (counts + MOE_BLOCK - 1) // MOE_BLOCK * MOE_BLOCK
    start = jnp.cumsum(counts) - counts
    pend = jnp.cumsum(padded)
    pstart = pend - padded
    dest = pstart[e_sorted] + jnp.arange(A) - start[e_sorted]
    tok_buf = jnp.full((P,), T, jnp.int32).at[dest].set((order // TOP_K).astype(jnp.int32))
    gate_buf = jnp.zeros((P,), f32).at[dest].set(gate.reshape(A)[order])
    x_buf = xt.at[tok_buf].get(mode='fill', fill_value=0)
    block_e = jnp.minimum(jnp.searchsorted(pend, jnp.arange(n_blocks) * MOE_BLOCK, side='right'), N_EXPERTS - 1)

    def expert_block(args):
        xb, e = args
        h = xb @ w1[e] + b1[e]
        glu = jnp.minimum(h[:, :D_FF], SWIGLU_LIMIT)
        lin = jnp.clip(h[:, D_FF:], -SWIGLU_LIMIT, SWIGLU_LIMIT)
        act = glu * jax.nn.sigmoid(SWIGLU_ALPHA * glu) * (lin + 1.0)
        return act @ w2[e] + b2[e]

    y_buf = lax.map(expert_block, (x_buf.reshape(n_blocks, MOE_BLOCK, D), block_e)).reshape(P, D)
    y = jax.ops.segment_sum(y_buf.astype(f32) * gate_buf[:, None], tok_buf, num_segments=T)
    return y.reshape(bsz, L, D).astype(x.dtype)


def kernel(x, ev_w_in, gla_w_gate2, gla_b_gate2, gla_norm_g, s5_a_re, s5_a_im, s5_log_dt, s5_b_re, s5_b_im,
           s5_c_re, s5_c_im, s5_d, s5_w_glu, s5_b_glu, ev_w_out, od_w_in, od_w_out, ln1_g, ln1_b, ln2_g, ln2_b,
           router_w, router_b, moe_w1, moe_b1, moe_w2, moe_b2):
    bsz, L, D = x.shape
    T = bsz * L
    x = x.reshape(T, D)
    for layer in range(DEPTH):
        j = layer // 2
        if layer % 2 == 0:
            s5_params = (s5_a_re[j], s5_a_im[j], s5_log_dt[j], s5_b_re[j], s5_b_im[j], s5_c_re[j], s5_c_im[j], s5_d[j])
            o_1, o_2 = _even_layer_mix(x, ev_w_in[j], gla_w_gate2[j], gla_b_gate2[j], gla_norm_g[j], s5_params,
                                       s5_w_glu[j], s5_b_glu[j], bsz, L)
            w_out = ev_w_out[j]
        else:
            o_1, o_2 = _odd_layer_mix(x, od_w_in[j], bsz, L)
            w_out = od_w_out[j]
        x = _matmul2_res_ln(o_1, o_2, w_out, x, ln1_g[layer], ln1_b[layer])
        ffn = moe_ffn(x.reshape(bsz, L, D), router_w[layer], router_b[layer], moe_w1[layer], moe_b1[layer],
                      moe_w2[layer], moe_b2[layer])
        x = _res_ln(ffn.reshape(T, D), x, ln2_g[layer], ln2_b[layer])
    return x.reshape(bsz, L, D)
```

```python
import math
from functools import partial

import numpy as np
import jax
import jax.numpy as jnp
from jax import lax
from jax.experimental import pallas as pl
from jax.experimental.pallas import tpu as pltpu

D_MODEL = 1024
DEPTH = 4
DEEPNORM_ALPHA = (2.0 * DEPTH) ** 0.25
LN_EPS = 1e-5
MIX_WIDTH = D_MODEL

GLA_HEADS = 4
GLA_DV = MIX_WIDTH // 2 // GLA_HEADS
GLA_DK = GLA_DV // 2
GLA_GATE_RANK = 16
GLA_GATE_TAU = 16.0
GLA_CHUNK = 64

S5_WIDTH = MIX_WIDTH // 2
S5_GROUP = 16
S5_GROUPS = S5_WIDTH // S5_GROUP
S5_STATE = 64
S5_MAX_RE = -1e-4

EV_SIZES = (GLA_HEADS * GLA_DK, GLA_HEADS * GLA_DK, GLA_HEADS * GLA_DV, GLA_HEADS * GLA_DV, GLA_GATE_RANK, S5_WIDTH)

DSA_HEADS = 8
DSA_HEAD_DIM = 64
DSA_WIDTH = DSA_HEADS * DSA_HEAD_DIM
IDX_HEADS = 4
IDX_DIM = 64
DSA_TOPK_MAX = 256

DIL_PATTERNS = ((128, 1), (512, 4), (2048, 16))
DIL_GROUPS = len(DIL_PATTERNS)
DIL_HEADS = 8
DIL_HEAD_DIM = 64
DIL_WIDTH = DIL_HEADS * DIL_HEAD_DIM

OD_SIZES = (DSA_WIDTH, DSA_WIDTH, DSA_WIDTH, IDX_HEADS * IDX_DIM, IDX_DIM, IDX_HEADS, 3 * DIL_GROUPS * DIL_WIDTH)

N_EXPERTS = 32
TOP_K = 4
D_FF = D_MODEL
SWIGLU_ALPHA = 1.702
SWIGLU_LIMIT = 7.0
MOE_BLOCK = 512

LANES = 128
VMEM_LIMIT_BYTES = 48 * 1024 * 1024
NEG_BIG = -1e30
INT_MIN = -2 ** 31
INT_MAX = 2 ** 31 - 1
KEY_NEG_INF = -0x7F800000

OD_Q, OD_K, OD_V, OD_QI, OD_DIL, OD_KI = 0, 512, 1024, 1536, 2048, 2048 + 9 * 512
OD_COLS_PADDED = 7168


def _split(h, sizes):
    return jnp.split(h, [int(i) for i in np.cumsum(sizes)[:-1]], axis=-1)


def _params(*sem):
    return pltpu.CompilerParams(dimension_semantics=sem, vmem_limit_bytes=VMEM_LIMIT_BYTES)


def _mm_kernel(x_ref, w_ref, o_ref):
    o_ref[...] = jnp.dot(x_ref[...].astype(jnp.bfloat16), w_ref[...].astype(jnp.bfloat16),
                         preferred_element_type=jnp.float32).astype(o_ref.dtype)


def _matmul(x, w, out_dtype=jnp.float32, tm=512, tn=512):
    T, K = x.shape
    N = w.shape[1]
    tm, tn = min(tm, T), min(tn, N)
    assert T % tm == 0 and N % tn == 0
    return pl.pallas_call(
        _mm_kernel,
        grid=(T // tm, N // tn),
        in_specs=[pl.BlockSpec((tm, K), lambda i, j: (i, 0)),
                  pl.BlockSpec((K, tn), lambda i, j: (0, j))],
        out_specs=pl.BlockSpec((tm, tn), lambda i, j: (i, j)),
        out_shape=jax.ShapeDtypeStruct((T, N), out_dtype),
        compiler_params=_params("parallel", "arbitrary"),
    )(x, w)


def _layer_norm_rows(z, g, b):
    mu = jnp.mean(z, axis=-1, keepdims=True)
    zc = z - mu
    var = jnp.mean(zc * zc, axis=-1, keepdims=True)
    return zc * lax.rsqrt(var + LN_EPS) * g + b


def _mm_res_ln_kernel(a_ref, w_ref, x_ref, g_ref, b_ref, o_ref):
    mix = jnp.dot(a_ref[...].astype(jnp.bfloat16), w_ref[...].astype(jnp.bfloat16),
                  preferred_element_type=jnp.float32)
    o_ref[...] = _layer_norm_rows(DEEPNORM_ALPHA * x_ref[...] + mix, g_ref[...], b_ref[...])


def _matmul_res_ln(a, w, x, g, b, tm=512):
    T, K = a.shape
    D = w.shape[1]
    tm = min(tm, T)
    return pl.pallas_call(
        _mm_res_ln_kernel,
        grid=(T // tm,),
        in_specs=[pl.BlockSpec((tm, K), lambda i: (i, 0)),
                  pl.BlockSpec((K, D), lambda i: (0, 0)),
                  pl.BlockSpec((tm, D), lambda i: (i, 0)),
                  pl.BlockSpec((1, D), lambda i: (0, 0)),
                  pl.BlockSpec((1, D), lambda i: (0, 0))],
        out_specs=pl.BlockSpec((tm, D), lambda i: (i, 0)),
        out_shape=jax.ShapeDtypeStruct((T, D), jnp.float32),
        compiler_params=_params("parallel"),
    )(a, w, x, g.reshape(1, D), b.reshape(1, D))


def _mm2_res_ln_kernel(a1_ref, a2_ref, w1_ref, w2_ref, x_ref, g_ref, b_ref, o_ref):
    mix = jnp.dot(a1_ref[...], w1_ref[...], preferred_element_type=jnp.float32)
    mix += jnp.dot(a2_ref[...], w2_ref[...], preferred_element_type=jnp.float32)
    o_ref[...] = _layer_norm_rows(DEEPNORM_ALPHA * x_ref[...] + mix, g_ref[...], b_ref[...])


def _matmul2_res_ln(a1, a2, w, x, g, b, tm=512):
    T, K1 = a1.shape
    D = w.shape[1]
    tm = min(tm, T)
    wb = w.astype(jnp.bfloat16)
    return pl.pallas_call(
        _mm2_res_ln_kernel,
        grid=(T // tm,),
        in_specs=[pl.BlockSpec((tm, K1), lambda i: (i, 0)),
                  pl.BlockSpec((tm, a2.shape[1]), lambda i: (i, 0)),
                  pl.BlockSpec((K1, D), lambda i: (0, 0)),
                  pl.BlockSpec((a2.shape[1], D), lambda i: (1, 0)),
                  pl.BlockSpec((tm, D), lambda i: (i, 0)),
                  pl.BlockSpec((1, D), lambda i: (0, 0)),
                  pl.BlockSpec((1, D), lambda i: (0, 0))],
        out_specs=pl.BlockSpec((tm, D), lambda i: (i, 0)),
        out_shape=jax.ShapeDtypeStruct((T, D), jnp.float32),
        compiler_params=_params("parallel"),
    )(a1, a2, wb, wb, x, g.reshape(1, D), b.reshape(1, D))


def _res_ln_kernel(y_ref, x_ref, g_ref, b_ref, o_ref):
    o_ref[...] = _layer_norm_rows(DEEPNORM_ALPHA * x_ref[...] + y_ref[...], g_ref[...], b_ref[...])


def _res_ln(y, x, g, b, tm=512):
    T, D = x.shape
    tm = min(tm, T)
    return pl.pallas_call(
        _res_ln_kernel,
        grid=(T // tm,),
        in_specs=[pl.BlockSpec((tm, D), lambda i: (i, 0)),
                  pl.BlockSpec((tm, D), lambda i: (i, 0)),
                  pl.BlockSpec((1, D), lambda i: (0, 0)),
                  pl.BlockSpec((1, D), lambda i: (0, 0))],
        out_specs=pl.BlockSpec((tm, D), lambda i: (i, 0)),
        out_shape=jax.ShapeDtypeStruct((T, D), jnp.float32),
        compiler_params=_params("parallel"),
    )(y, x, g.reshape(1, D), b.reshape(1, D))


def _index_keys(qi_blk, ki_blk, wi_blk, q0, k0):
    tq, tk = qi_blk.shape[0], ki_blk.shape[0]
    lhs = jnp.concatenate([qi_blk[:, LANES * h:LANES * (h + 1)] for h in range(IDX_HEADS)], axis=0)
    d = lax.dot_general(lhs, ki_blk, (((1,), (1,)), ((), ())), preferred_element_type=jnp.float32)
    sc = wi_blk[:, 0:1] * jnp.maximum(d[0:tq], 0.0)
    for h in range(1, IDX_HEADS):
        sc = sc + wi_blk[:, h:h + 1] * jnp.maximum(d[h * tq:(h + 1) * tq], 0.0)
    qpos = q0 + lax.broadcasted_iota(jnp.int32, (tq, 1), 0)
    kpos = k0 + lax.broadcasted_iota(jnp.int32, (1, tk), 1)
    sc = jnp.where(kpos <= qpos, sc, -jnp.inf)
    bits = lax.bitcast_convert_type(sc, jnp.int32)
    return jnp.where(bits < 0, INT_MIN - bits, bits)


def _dsa_select_kernel(qi_ref, wi_ref, ki_ref, thr_ref, cut_ref, key_ref, *, tq, topk, row_group):
    i = pl.program_id(1)
    nblk = i + 1
    q0 = i * tq
    qi_blk = qi_ref[...]
    wi_blk = wi_ref[...]

    def fill(j, carry):
        k0 = pl.multiple_of(j * tq, tq)
        key_ref[j] = _index_keys(qi_blk, ki_ref[pl.ds(k0, tq), :], wi_blk, q0, k0)
        return carry

    lax.fori_loop(0, nblk, fill, 0)

    nslab = tq // LANES
    lane = lax.broadcasted_iota(jnp.int32, (row_group, LANES), 1)

    def count(r0, pred):
        def body(j, acc):
            for s in range(nslab):
                kk = key_ref[j, r0:r0 + row_group, s * LANES:(s + 1) * LANES]
                acc = acc + jnp.where(pred(kk, j * tq + s * LANES), 1, 0)
            return acc
        acc = lax.fori_loop(0, nblk, body, jnp.zeros((row_group, LANES), jnp.int32))
        return jnp.sum(acc.astype(jnp.float32), axis=1, keepdims=True).astype(jnp.int32)

    for r0 in range(0, tq, row_group):
        def bit_step(p, thr):
            cand = thr ^ lax.shift_left(jnp.int32(1), 31 - p)
            candb = jnp.broadcast_to(cand, (row_group, LANES))
            cnt = count(r0, lambda kk, base: kk >= candb)
            return jnp.where(cnt >= topk, cand, thr)

        thr = lax.fori_loop(0, 32, bit_step, jnp.full((row_group, 1), INT_MIN, jnp.int32))
        thrb = jnp.broadcast_to(thr, (row_group, LANES))
        n_gt = count(r0, lambda kk, base: kk > thrb)
        n_ge = count(r0, lambda kk, base: kk >= thrb)
        need = topk - n_gt
        tie = jnp.logical_and(n_ge > topk, thr > KEY_NEG_INF)
        any_tie = jnp.max(jnp.where(tie, 1.0, 0.0)) > 0.5

        def resolve_ties():
            def col_step(p, c):
                cand = c | lax.shift_left(jnp.int32(1), 14 - p)
                cnt = count(r0, lambda kk, base: jnp.logical_and(kk == thrb, lane < cand - base))
                return jnp.where(cnt < need, cand, c)
            c = lax.fori_loop(0, 15, col_step, jnp.zeros((row_group, 1), jnp.int32))
            return jnp.where(tie, c + 1, INT_MAX)

        cut = lax.cond(any_tie, resolve_ties, lambda: jnp.full((row_group, 1), INT_MAX, jnp.int32))
        thr_ref[r0:r0 + row_group, :] = jnp.maximum(thr, KEY_NEG_INF + 1)
        cut_ref[r0:r0 + row_group, :] = cut


def _dsa_select(hb, wi, bsz, L, tq):
    T = bsz * L
    nq = L // tq
    topk = min(DSA_TOPK_MAX, L // 4)
    kern = partial(_dsa_select_kernel, tq=tq, topk=topk, row_group=min(64, tq))
    return pl.pallas_call(
        kern,
        grid=(bsz, nq),
        in_specs=[pl.BlockSpec((tq, 512), lambda b, i: (b * nq + i, OD_QI // 512)),
                  pl.BlockSpec((tq, LANES), lambda b, i: (b * nq + i, 0)),
                  pl.BlockSpec((L, LANES), lambda b, i: (b, OD_KI // LANES))],
        out_specs=[pl.BlockSpec((tq, 1), lambda b, i: (b * nq + i, 0)),
                   pl.BlockSpec((tq, 1), lambda b, i: (b * nq + i, 0))],
        out_shape=[jax.ShapeDtypeStruct((T, 1), jnp.int32), jax.ShapeDtypeStruct((T, 1), jnp.int32)],
        scratch_shapes=[pltpu.VMEM((nq, tq, tq), jnp.int32)],
        compiler_params=_params("parallel", "arbitrary"),
    )(hb, wi, hb)


def _split_head_pair(x_pair):
    lane = lax.broadcasted_iota(jnp.int32, x_pair.shape, 1)
    zero = jnp.zeros_like(x_pair)
    return jnp.where(lane < 64, x_pair, zero), jnp.where(lane >= 64, x_pair, zero)


def _dsa_attn_kernel(q_ref, qi_ref, wi_ref, thr_ref, cut_ref, k_ref, v_ref, ki_ref, o_ref,
                     qm_ref, m_ref, l_ref, acc_ref, *, tq, tk, nk):
    i = pl.program_id(1)
    j = pl.program_id(2)
    npair = DSA_HEADS // 2

    @pl.when(j == 0)
    def _():
        for p in range(npair):
            qa, qb = _split_head_pair(q_ref[:, LANES * p:LANES * (p + 1)])
            qm_ref[2 * p] = qa
            qm_ref[2 * p + 1] = qb
        m_ref[...] = jnp.full(m_ref.shape, NEG_BIG, jnp.float32)
        l_ref[...] = jnp.zeros(l_ref.shape, jnp.float32)
        acc_ref[...] = jnp.zeros(acc_ref.shape, jnp.float32)

    @pl.when(j * tk < (i + 1) * tq)
    def _():
        key = _index_keys(qi_ref[...], ki_ref[...], wi_ref[...], i * tq, j * tk)
        col = j * tk + lax.broadcasted_iota(jnp.int32, (1, tk), 1)
        thr = thr_ref[...]
        sel = jnp.logical_or(key > thr, jnp.logical_and(key == thr, col < cut_ref[...]))
        for h in range(DSA_HEADS):
            p = h // 2
            s = lax.dot_general(qm_ref[h], k_ref[:, LANES * p:LANES * (p + 1)], (((1,), (1,)), ((), ())),
                                preferred_element_type=jnp.float32)
            s = jnp.where(sel, s, NEG_BIG)
            m_prev = m_ref[h]
            m_new = jnp.maximum(m_prev, jnp.max(s, axis=1, keepdims=True))
            alpha = jnp.exp(m_prev - m_new)
            e = jnp.exp(s - m_new)
            l_ref[h] = alpha * l_ref[h] + jnp.sum(e, axis=1, keepdims=True)
            pv = jnp.dot(e.astype(jnp.bfloat16), v_ref[:, LANES * p:LANES * (p + 1)],
                         preferred_element_type=jnp.float32)
            acc_ref[h] = alpha * acc_ref[h] + pv
            m_ref[h] = m_new

    @pl.when(j == nk - 1)
    def _():
        lane = lax.broadcasted_iota(jnp.int32, (tq, LANES), 1)
        for p in range(npair):
            oa = acc_ref[2 * p] / l_ref[2 * p]
            ob = acc_ref[2 * p + 1] / l_ref[2 * p + 1]
            o_ref[:, LANES * p:LANES * (p + 1)] = jnp.where(lane < 64, oa, ob).astype(o_ref.dtype)


def _dsa_attention(hb, wi, thr, cut, bsz, L, tq, tk):
    T = bsz * L
    nq, nk = L // tq, L // tk
    kern = partial(_dsa_attn_kernel, tq=tq, tk=tk, nk=nk)

    def qmap(col):
        return lambda b, i, j: (b * nq + i, col)

    def kmap(col):
        return lambda b, i, j: (b * nk + jnp.minimum(j, ((i + 1) * tq - 1) // tk), col)

    return pl.pallas_call(
        kern,
        grid=(bsz, nq, nk),
        in_specs=[pl.BlockSpec((tq, 512), qmap(OD_Q // 512)),
                  pl.BlockSpec((tq, 512), qmap(OD_QI // 512)),
                  pl.BlockSpec((tq, LANES), qmap(0)),
                  pl.BlockSpec((tq, 1), qmap(0)),
                  pl.BlockSpec((tq, 1), qmap(0)),
                  pl.BlockSpec((tk, 512), kmap(OD_K // 512)),
                  pl.BlockSpec((tk, 512), kmap(OD_V // 512)),
                  pl.BlockSpec((tk, LANES), kmap(OD_KI // LANES))],
        out_specs=pl.BlockSpec((tq, DSA_WIDTH), qmap(0)),
        out_shape=jax.ShapeDtypeStruct((T, DSA_WIDTH), jnp.bfloat16),
        scratch_shapes=[pltpu.VMEM((DSA_HEADS, tq, LANES), jnp.bfloat16),
                        pltpu.VMEM((DSA_HEADS, tq, 1), jnp.float32),
                        pltpu.VMEM((DSA_HEADS, tq, 1), jnp.float32),
                        pltpu.VMEM((DSA_HEADS, tq, LANES), jnp.float32)],
        compiler_params=_params("parallel", "parallel", "arbitrary"),
    )(hb, hb, wi, thr, cut, hb, hb, hb)


def _dilated_kernel(q_ref, kp_ref, kc_ref, vp_ref, vc_ref, o_ref, lse_ref, *, tq):
    a = pl.program_id(2)
    row = lax.broadcasted_iota(jnp.int32, (tq, 2 * tq), 0)
    c = lax.broadcasted_iota(jnp.int32, (tq, 2 * tq), 1)
    first_col = jnp.where(a == 0, tq, 0)
    valid = jnp.logical_and(jnp.logical_and(c >= row, c <= row + tq), c >= first_col)
    lane = lax.broadcasted_iota(jnp.int32, (tq, LANES), 1)
    for p in range(DIL_HEADS // 2):
        sl = slice(LANES * p, LANES * (p + 1))
        kk = jnp.concatenate([kp_ref[:, sl], kc_ref[:, sl]], axis=0)
        vv = jnp.concatenate([vp_ref[:, sl], vc_ref[:, sl]], axis=0)
        outs, lses = [], []
        for qh in _split_head_pair(q_ref[:, sl]):
            s = lax.dot_general(qh, kk, (((1,), (1,)), ((), ())), preferred_element_type=jnp.float32)
            s = jnp.where(valid, s, NEG_BIG)
            m = jnp.max(s, axis=1, keepdims=True)
            e = jnp.exp(s - m)
            l = jnp.sum(e, axis=1, keepdims=True)
            outs.append(jnp.dot(e.astype(jnp.bfloat16), vv, preferred_element_type=jnp.float32) / l)
            lses.append(m + jnp.log(l))
        o_ref[:, sl] = jnp.where(lane < 64, outs[0], outs[1])
        lse_ref[:, sl] = jnp.where(lane < 64, lses[0], lses[1])


def _dilated_group(hb, bsz, L, g, tq):
    window, dil = DIL_PATTERNS[g]
    assert window // dil == tq
    T = bsz * L
    M = L // dil
    nb = M // tq
    ncol = hb.shape[1] // 512
    hv = hb.reshape(bsz * M, dil * hb.shape[1])
    qcol, kcol, vcol = (OD_DIL // 512 + s * DIL_GROUPS + g for s in range(3))

    def cur(col):
        return lambda b, r, a: (b * nb + a, r * ncol + col)

    def prev(col):
        return lambda b, r, a: (b * nb + jnp.maximum(a - 1, 0), r * ncol + col)

    blk = (tq, 512)
    out_map = lambda b, r, a: (b * nb + a, r)
    o, lse = pl.pallas_call(
        partial(_dilated_kernel, tq=tq),
        grid=(bsz, dil, nb),
        in_specs=[pl.BlockSpec(blk, cur(qcol)), pl.BlockSpec(blk, prev(kcol)), pl.BlockSpec(blk, cur(kcol)),
                  pl.BlockSpec(blk, prev(vcol)), pl.BlockSpec(blk, cur(vcol))],
        out_specs=[pl.BlockSpec(blk, out_map), pl.BlockSpec(blk, out_map)],
        out_shape=[jax.ShapeDtypeStruct((bsz * M, dil * 512), jnp.float32)] * 2,
        compiler_params=_params("parallel", "parallel", "arbitrary"),
    )(hv, hv, hv, hv, hv)
    return o.reshape(T, 512), lse.reshape(T, 512)


def _dilated_combine_kernel(o0, o1, o2, l0, l1, l2, out_ref):
    a, b, c = l0[...], l1[...], l2[...]
    m = jnp.maximum(jnp.maximum(a, b), c)
    ea, eb, ec = jnp.exp(a - m), jnp.exp(b - m), jnp.exp(c - m)
    num = ea * o0[...] + eb * o1[...] + ec * o2[...]
    out_ref[...] = (num / (ea + eb + ec)).astype(out_ref.dtype)


def _dilated_combine(outs, lses, tm=512):
    T, W = outs[0].shape
    tm = min(tm, T)
    spec = pl.BlockSpec((tm, W), lambda i: (i, 0))
    return pl.pallas_call(
        _dilated_combine_kernel,
        grid=(T // tm,),
        in_specs=[spec] * 6,
        out_specs=spec,
        out_shape=jax.ShapeDtypeStruct((T, W), jnp.bfloat16),
        compiler_params=_params("parallel"),
    )(*outs, *lses)


def _odd_weights(w_in):
    D = w_in.shape[0]
    q, k, v, qi, ki, wi, dil = _split(w_in, OD_SIZES)
    zeros = lambda n: jnp.zeros((D, n), w_in.dtype)
    qi_exp = jnp.concatenate(
        [jnp.concatenate([qi[:, IDX_DIM * h:IDX_DIM * (h + 1)] * IDX_DIM ** -0.5, zeros(LANES - IDX_DIM)], axis=1)
         for h in range(IDX_HEADS)], axis=1)
    dil = dil.reshape(D, 3, DIL_GROUPS * DIL_WIDTH)
    dil = jnp.concatenate([dil[:, 0] * DIL_HEAD_DIM ** -0.5, dil[:, 1], dil[:, 2]], axis=1)
    wb = jnp.concatenate([q * DSA_HEAD_DIM ** -0.5, k, v, qi_exp, dil, ki, zeros(LANES - IDX_DIM),
                          zeros(OD_COLS_PADDED - OD_KI - LANES)], axis=1).astype(jnp.bfloat16)
    wwi = jnp.concatenate([wi * IDX_HEADS ** -0.5, zeros(LANES - IDX_HEADS)], axis=1).astype(jnp.bfloat16)
    return wb, wwi


def _odd_layer_mix(x2, w_in, bsz, L):
    wb, wwi = _odd_weights(w_in)
    hb = _matmul(x2, wb, jnp.bfloat16, tm=1024, tn=1024)
    wi = _matmul(x2, wwi, jnp.float32, tm=1024, tn=LANES)
    tq_sel = min(256, L)
    thr, cut = _dsa_select(hb, wi, bsz, L, tq_sel)
    o_c = _dsa_attention(hb, wi, thr, cut, bsz, L, min(256, L), min(512, L))
    groups = [_dilated_group(hb, bsz, L, g, 128) for g in range(DIL_GROUPS)]
    o_d = _dilated_combine([o for o, _ in groups], [l for _, l in groups])
    return o_c, o_d


EV_Q, EV_K, EV_V, EV_R, EV_U, EV_G = 0, 256, 512, 1024, 1536, 2048
EV_COLS_PADDED = 2560
GLA_ROWS = 512
S5_CHUNK = 64


def _gla_kernel(q_ref, k_ref, v_ref, r_ref, g_ref, wg_ref, bg_ref, ng_ref, o_ref, st_ref, *, rows):
    C = GLA_CHUNK

    @pl.when(pl.program_id(1) == 0)
    def _():
        st_ref[...] = jnp.zeros(st_ref.shape, jnp.float32)

    ri = lax.broadcasted_iota(jnp.int32, (C, C), 0)
    ci = lax.broadcasted_iota(jnp.int32, (C, C), 1)
    causal = ci <= ri
    tril = jnp.where(causal, 1.0, 0.0).astype(jnp.float32)
    wg = wg_ref[...].astype(jnp.bfloat16)
    bg = bg_ref[...]
    ng = ng_ref[...]

    def chunk(c, carry):
        r0 = pl.multiple_of(c * C, C)
        rs = pl.ds(r0, C)
        logit = jnp.dot(g_ref[rs, :].astype(jnp.bfloat16), wg, preferred_element_type=jnp.float32) + bg
        log_a = jax.nn.log_sigmoid(logit) / GLA_GATE_TAU
        bcum = jnp.dot(tril, log_a, precision=lax.Precision.HIGHEST, preferred_element_type=jnp.float32)
        b_last = bcum[C - 1:C, :]
        q_t = (q_ref[rs, :] * jnp.exp(bcum)).astype(jnp.bfloat16)
        k_t = (k_ref[rs, :] * jnp.exp(-bcum)).astype(jnp.bfloat16)
        k_end = (k_ref[rs, :] * jnp.exp(b_last - bcum)).astype(jnp.bfloat16)
        dec = jnp.exp(b_last)
        for p in range(GLA_HEADS // 2):
            sl = slice(LANES * p, LANES * (p + 1))
            q_halves = _split_head_pair(q_t[:, sl])
            ke_halves = _split_head_pair(k_end[:, sl])
            for half in range(2):
                h = 2 * p + half
                hs = slice(GLA_DV * h, GLA_DV * (h + 1))
                qm = q_halves[half]
                att = lax.dot_general(qm, k_t[:, sl], (((1,), (1,)), ((), ())), preferred_element_type=jnp.float32)
                att = jnp.where(causal, att, 0.0).astype(jnp.bfloat16)
                v_h = v_ref[rs, hs].astype(jnp.bfloat16)
                st = st_ref[h]
                o = jnp.dot(att, v_h, preferred_element_type=jnp.float32)
                o = o + lax.dot_general(qm, st.astype(jnp.bfloat16), (((1,), (1,)), ((), ())),
                                        preferred_element_type=jnp.float32)
                kv_t = lax.dot_general(v_h, ke_halves[half], (((0,), (0,)), ((), ())),
                                       preferred_element_type=jnp.float32)
                st_ref[h] = st * dec[:, sl] + kv_t
                o = o * lax.rsqrt(jnp.mean(o * o, axis=-1, keepdims=True) + LN_EPS) * ng
                o = o * jax.nn.silu(r_ref[rs, hs])
                o_ref[rs, hs] = o.astype(o_ref.dtype)
        return carry

    lax.fori_loop(0, rows // C, chunk, 0)


def _gla(hf, w_gate2, b_gate2, norm_g, bsz, L):
    T = bsz * L
    rows = min(GLA_ROWS, L)
    nb = L // rows
    dkw = GLA_HEADS * GLA_DK
    dvw = GLA_HEADS * GLA_DV
    wg = jnp.pad(w_gate2, ((0, LANES - GLA_GATE_RANK), (0, 0)))

    def rmap(col):
        return lambda b, i: (b * nb + i, col)

    const = lambda b, i: (0, 0)
    return pl.pallas_call(
        partial(_gla_kernel, rows=rows),
        grid=(bsz, nb),
        in_specs=[pl.BlockSpec((rows, dkw), rmap(EV_Q // dkw)),
                  pl.BlockSpec((rows, dkw), rmap(EV_K // dkw)),
                  pl.BlockSpec((rows, dvw), rmap(EV_V // dvw)),
                  pl.BlockSpec((rows, dvw), rmap(EV_R // dvw)),
                  pl.BlockSpec((rows, LANES), rmap(EV_G // LANES)),
                  pl.BlockSpec((LANES, dkw), const),
                  pl.BlockSpec((1, dkw), const),
                  pl.BlockSpec((1, GLA_DV), const)],
        out_specs=pl.BlockSpec((rows, dvw), rmap(0)),
        out_shape=jax.ShapeDtypeStruct((T, dvw), jnp.bfloat16),
        scratch_shapes=[pltpu.VMEM((GLA_HEADS, GLA_DV, LANES), jnp.float32)],
        compiler_params=_params("parallel", "arbitrary"),
    )(hf, hf, hf, hf, hf, wg, b_gate2.reshape(1, dkw), norm_g.reshape(1, GLA_DV))


def _s5_tables(a_re, a_im, log_dt, b_re, b_im, c_re, c_im, d_skip):
    f32 = jnp.float32
    Cs, G, P, N = S5_CHUNK, S5_GROUPS, S5_STATE, S5_GROUP
    lam_re = jnp.minimum(a_re.astype(f32), S5_MAX_RE)
    lam_im = a_im.astype(f32)
    dt = jnp.exp(log_dt.astype(f32))[:, None]
    mag = jnp.exp(lam_re * dt)
    ab_re = mag * jnp.cos(lam_im * dt)
    ab_im = mag * jnp.sin(lam_im * dt)
    inv = 1.0 / (lam_re * lam_re + lam_im * lam_im)
    z_re = ((ab_re - 1.0) * lam_re + ab_im * lam_im) * inv
    z_im = (ab_im * lam_re - (ab_re - 1.0) * lam_im) * inv
    br, bi = b_re.astype(f32), b_im.astype(f32)
    bb_re = z_re[..., None] * br - z_im[..., None] * bi
    bb_im = z_re[..., None] * bi + z_im[..., None] * br
    kk = jnp.arange(Cs + 1, dtype=f32)[:, None, None]
    pmag = jnp.exp(kk * (lam_re * dt))
    pw_re = pmag * jnp.cos(kk * (lam_im * dt))
    pw_im = pmag * jnp.sin(kk * (lam_im * dt))
    cr, ci = c_re.astype(f32), c_im.astype(f32)
    ca_re = cr[None] * pw_re[:, :, None, :] - ci[None] * pw_im[:, :, None, :]
    ca_im = cr[None] * pw_im[:, :, None, :] + ci[None] * pw_re[:, :, None, :]
    hi = lax.Precision.HIGHEST
    kern = (jnp.einsum('kgnp,gpm->kgnm', ca_re[:Cs], bb_re, precision=hi)
            - jnp.einsum('kgnp,gpm->kgnm', ca_im[:Cs], bb_im, precision=hi))
    jj = jnp.arange(Cs)[:, None]
    ii = jnp.arange(Cs)[None, :]
    tz = jnp.where((ii >= jj)[:, :, None, None, None], kern[jnp.maximum(ii - jj, 0)], 0.0)
    tz = jnp.transpose(tz, (2, 0, 4, 1, 3)).reshape(G, Cs * N, Cs * N)
    rev_re, rev_im = pw_re[Cs - 1::-1][:Cs], pw_im[Cs - 1::-1][:Cs]
    ws_re = rev_re[..., None] * bb_re[None] - rev_im[..., None] * bb_im[None]
    ws_im = rev_re[..., None] * bb_im[None] + rev_im[..., None] * bb_re[None]
    to_ws = lambda w: jnp.pad(jnp.transpose(w, (1, 0, 3, 2)).reshape(G, Cs * N, P), ((0, 0), (0, 0), (0, LANES - P)))
    to_wo = lambda w: jnp.pad(jnp.transpose(w, (1, 3, 0, 2)).reshape(G, P, Cs * N), ((0, 0), (0, LANES - P), (0, 0)))
    a_cs = jnp.stack([jnp.pad(pw_re[Cs], ((0, 0), (0, LANES - P))), jnp.pad(pw_im[Cs], ((0, 0), (0, LANES - P)))], axis=1)
    d_exp = jnp.tile(d_skip.astype(f32).reshape(G, 1, N), (1, 1, Cs))
    bf = jnp.bfloat16
    return (tz.astype(bf), to_ws(ws_re).astype(bf), to_ws(ws_im).astype(bf),
            to_wo(ca_re[1:]).astype(bf), to_wo(-ca_im[1:]).astype(bf), a_cs, d_exp)


def _s5_kernel(u_ref, tz_ref, wsr_ref, wsi_ref, wor_ref, woi_ref, acs_ref, d_ref, y_ref, xr_ref, xi_ref,
               *, nchunk, nbatch):
    u32 = u_ref[0]
    u = u32.astype(jnp.bfloat16)
    xr_ref[...] = jnp.dot(u, wsr_ref[0], preferred_element_type=jnp.float32)
    xi_ref[...] = jnp.dot(u, wsi_ref[0], preferred_element_type=jnp.float32)
    ar = acs_ref[0, 0:1, :]
    ai = acs_ref[0, 1:2, :]

    def step(c, carry):
        new = []
        for b in range(nbatch):
            sr, si = carry[2 * b], carry[2 * b + 1]
            row = pl.ds(b * nchunk + c, 1)
            lr, li = xr_ref[row, :], xi_ref[row, :]
            xr_ref[row, :] = sr
            xi_ref[row, :] = si
            new += [ar * sr - ai * si + lr, ar * si + ai * sr + li]
        return tuple(new)

    zero = jnp.zeros((1, LANES), jnp.float32)
    lax.fori_loop(0, nchunk, step, (zero,) * (2 * nbatch))
    y = jnp.dot(u, tz_ref[0], preferred_element_type=jnp.float32)
    y = y + jnp.dot(xr_ref[...].astype(jnp.bfloat16), wor_ref[0], preferred_element_type=jnp.float32)
    y = y + jnp.dot(xi_ref[...].astype(jnp.bfloat16), woi_ref[0], preferred_element_type=jnp.float32)
    y = y + d_ref[0] * u32
    y_ref[0] = jax.nn.gelu(y)


def _s5_scan(ug, tables, bsz, nchunk):
    G, R, W = ug.shape
    tz, wsr, wsi, wor, woi, a_cs, d_exp = tables
    gmap = lambda g: (g, 0, 0)
    return pl.pallas_call(
        partial(_s5_kernel, nchunk=nchunk, nbatch=bsz),
        grid=(G,),
        in_specs=[pl.BlockSpec((1, R, W), gmap), pl.BlockSpec((1, W, W), gmap),
                  pl.BlockSpec((1, W, LANES), gmap), pl.BlockSpec((1, W, LANES), gmap),
                  pl.BlockSpec((1, LANES, W), gmap), pl.BlockSpec((1, LANES, W), gmap),
                  pl.BlockSpec((1, 2, LANES), gmap), pl.BlockSpec((1, 1, W), gmap)],
        out_specs=pl.BlockSpec((1, R, W), gmap),
        out_shape=jax.ShapeDtypeStruct((G, R, W), jnp.float32),
        scratch_shapes=[pltpu.VMEM((R, LANES), jnp.float32), pltpu.VMEM((R, LANES), jnp.float32)],
        compiler_params=_params("parallel"),
    )(ug, tz, wsr, wsi, wor, woi, a_cs, d_exp)


def _glu_kernel(y_ref, w_ref, b_ref, o_ref):
    y = y_ref[...]
    gate = jnp.dot(y.astype(jnp.bfloat16), w_ref[...], preferred_element_type=jnp.float32) + b_ref[...]
    o_ref[...] = (y * jax.nn.sigmoid(gate)).astype(o_ref.dtype)


def _glu(y, w_glu, b_glu, tm=1024):
    T, W = y.shape
    tm = min(tm, T)
    return pl.pallas_call(
        _glu_kernel,
        grid=(T // tm,),
        in_specs=[pl.BlockSpec((tm, W), lambda i: (i, 0)),
                  pl.BlockSpec((W, W), lambda i: (0, 0)),
                  pl.BlockSpec((1, W), lambda i: (0, 0))],
        out_specs=pl.BlockSpec((tm, W), lambda i: (i, 0)),
        out_shape=jax.ShapeDtypeStruct((T, W), jnp.bfloat16),
        compiler_params=_params("parallel"),
    )(y, w_glu.astype(jnp.bfloat16), b_glu.reshape(1, W))


def _s5(hf, s5_params, w_glu, b_glu, bsz, L):
    T = bsz * L
    Cs, G, N = S5_CHUNK, S5_GROUPS, S5_GROUP
    nchunk = L // Cs
    u = hf[:, EV_U:EV_U + S5_WIDTH].reshape(bsz * nchunk, Cs, G, N)
    ug = jnp.transpose(u, (2, 0, 1, 3)).reshape(G, bsz * nchunk, Cs * N)
    y = _s5_scan(ug, _s5_tables(*s5_params), bsz, nchunk)
    y = jnp.transpose(y.reshape(G, bsz * nchunk, Cs, N), (1, 2, 0, 3)).reshape(T, S5_WIDTH)
    return _glu(y, w_glu, b_glu)


def _even_weights(w_in):
    D = w_in.shape[0]
    q, k, v, r, g_lr, u = _split(w_in, EV_SIZES)
    pad = jnp.zeros((D, EV_COLS_PADDED - EV_G - GLA_GATE_RANK), w_in.dtype)
    return jnp.concatenate([q * GLA_DK ** -0.5, k, v, r, u, g_lr, pad], axis=1).astype(jnp.bfloat16)


def _even_layer_mix(x2, w_in, w_gate2, b_gate2, norm_g, s5_params, w_glu, b_glu, bsz, L):
    hf = _matmul(x2, _even_weights(w_in), jnp.float32, tm=1024, tn=512)
    o_a = _gla(hf, w_gate2, b_gate2, norm_g, bsz, L)
    o_b = _s5(hf, s5_params, w_glu, b_glu, bsz, L)
    return o_a, o_b


MOE_TILE = 512
ROW_TILE = 8
MOE_VMEM_LIMIT_BYTES = 56 * 1024 * 1024


def _router_kernel(x_ref, w_ref, b_ref, e_ref, g_ref):
    logits = jnp.dot(x_ref[...].astype(jnp.bfloat16), w_ref[...], preferred_element_type=jnp.float32) + b_ref[...]
    tm = logits.shape[0]
    lane = lax.broadcasted_iota(jnp.int32, (tm, LANES), 1)
    logits = jnp.where(lane < N_EXPERTS, logits, NEG_BIG)
    tops, idxs = [], []
    for _ in range(TOP_K):
        m = jnp.max(logits, axis=1, keepdims=True)
        idx = jnp.min(jnp.where(logits == m, lane, LANES), axis=1, keepdims=True)
        tops.append(m)
        idxs.append(idx)
        logits = jnp.where(lane == idx, NEG_BIG, logits)
    exps = [jnp.exp(t - tops[0]) for t in tops]
    denom = exps[0]
    for e in exps[1:]:
        denom = denom + e
    lane4 = lax.broadcasted_iota(jnp.int32, (tm, TOP_K), 1)
    e_out = jnp.zeros((tm, TOP_K), jnp.int32)
    g_out = jnp.zeros((tm, TOP_K), jnp.float32)
    for k in range(TOP_K):
        e_out = jnp.where(lane4 == k, idxs[k], e_out)
        g_out = jnp.where(lane4 == k, exps[k] / denom, g_out)
    e_ref[...] = e_out
    g_ref[...] = g_out


def _router(x, router_w, router_b, tm=512):
    T, D = x.shape
    tm = min(tm, T)
    w = jnp.pad(router_w, ((0, 0), (0, LANES - N_EXPERTS))).astype(jnp.bfloat16)
    b = jnp.pad(router_b, (0, LANES - N_EXPERTS)).reshape(1, LANES)
    return pl.pallas_call(
        _router_kernel,
        grid=(T // tm,),
        in_specs=[pl.BlockSpec((tm, D), lambda i: (i, 0)),
                  pl.BlockSpec((D, LANES), lambda i: (0, 0)),
                  pl.BlockSpec((1, LANES), lambda i: (0, 0))],
        out_specs=[pl.BlockSpec((tm, TOP_K), lambda i: (i, 0)), pl.BlockSpec((tm, TOP_K), lambda i: (i, 0))],
        out_shape=[jax.ShapeDtypeStruct((T, TOP_K), jnp.int32), jax.ShapeDtypeStruct((T, TOP_K), jnp.float32)],
        compiler_params=_params("parallel"),
    )(x, w, b)


def _moe_rank_kernel(e_ref, rank_ref, count_ref, carry_ref):
    @pl.when(pl.program_id(0) == 0)
    def _():
        carry_ref[...] = jnp.zeros(carry_ref.shape, jnp.float32)

    e = e_ref[...]
    tm = e.shape[0]
    lane = lax.broadcasted_iota(jnp.int32, (tm, LANES), 1)
    onehot = jnp.zeros((tm, LANES), jnp.float32)
    for k in range(TOP_K):
        onehot = onehot + jnp.where(lane == e[:, k:k + 1], 1.0, 0.0)
    ri = lax.broadcasted_iota(jnp.int32, (tm, tm), 0)
    ci = lax.broadcasted_iota(jnp.int32, (tm, tm), 1)
    strict_lower = jnp.where(ci < ri, 1.0, 0.0).astype(jnp.bfloat16)
    before = jnp.dot(strict_lower, onehot.astype(jnp.bfloat16), preferred_element_type=jnp.float32) + carry_ref[...]
    lane4 = lax.broadcasted_iota(jnp.int32, (tm, TOP_K), 1)
    rank = jnp.zeros((tm, TOP_K), jnp.int32)
    for k in range(TOP_K):
        r_k = jnp.sum(jnp.where(lane == e[:, k:k + 1], before, 0.0), axis=1, keepdims=True).astype(jnp.int32)
        rank = jnp.where(lane4 == k, r_k, rank)
    rank_ref[...] = rank
    carry_ref[...] = carry_ref[...] + jnp.sum(onehot, axis=0, keepdims=True)
    count_ref[...] = carry_ref[...]


def _moe_rank(top_e, tm=256):
    T = top_e.shape[0]
    tm = min(tm, T)
    return pl.pallas_call(
        _moe_rank_kernel,
        grid=(T // tm,),
        in_specs=[pl.BlockSpec((tm, TOP_K), lambda i: (i, 0))],
        out_specs=[pl.BlockSpec((tm, TOP_K), lambda i: (i, 0)), pl.BlockSpec((1, LANES), lambda i: (0, 0))],
        out_shape=[jax.ShapeDtypeStruct((T, TOP_K), jnp.int32), jax.ShapeDtypeStruct((1, LANES), jnp.float32)],
        scratch_shapes=[pltpu.VMEM((1, LANES), jnp.float32)],
        compiler_params=_params("arbitrary"),
    )(top_e)


def _to_token_tiles(x, dst_ref, rows):
    for c in range(ROW_TILE):
        dst_ref[pl.ds(c, rows, stride=ROW_TILE), :] = x[:, LANES * c:LANES * (c + 1)]


def _from_token_tiles(src_ref, rows):
    return jnp.concatenate([src_ref[pl.ds(c, rows, stride=ROW_TILE), :] for c in range(ROW_TILE)], axis=1)


def _moe_dispatch_kernel(dest_ref, x_ref, init_ref, xbuf_ref, xs_ref, sem, *, tm):
    del init_ref
    i = pl.program_id(0)
    n = pl.num_programs(0)
    slot = i % 2

    def row_copy(s, r, d):
        return pltpu.make_async_copy(xs_ref.at[s, pl.ds(pl.multiple_of(r * ROW_TILE, ROW_TILE), ROW_TILE), :],
                                     xbuf_ref.at[pl.ds(pl.multiple_of(d * ROW_TILE, ROW_TILE), ROW_TILE), :], sem.at[s])

    def drain(s):
        def body(a, c):
            row_copy(s, 0, 0).wait()
            return c
        lax.fori_loop(0, tm * TOP_K, body, 0)

    @pl.when(i >= 2)
    def _():
        drain(slot)

    _to_token_tiles(x_ref[...], xs_ref.at[slot], tm)

    def body(r, c):
        for k in range(TOP_K):
            row_copy(slot, r, dest_ref[r * TOP_K + k]).start()
        return c

    lax.fori_loop(0, tm, body, 0)

    @pl.when(i == n - 1)
    def _():
        drain(slot)

        @pl.when(n > 1)
        def _():
            drain(1 - slot)


def _moe_dispatch(x, dest_flat, n_rows, tm=256):
    T, D = x.shape
    tm = min(tm, T)
    assert D == ROW_TILE * LANES
    init = jnp.zeros((n_rows * ROW_TILE, LANES), jnp.float32)
    return pl.pallas_call(
        partial(_moe_dispatch_kernel, tm=tm),
        grid=(T // tm,),
        in_specs=[pl.BlockSpec((tm * TOP_K,), lambda i: (i,), memory_space=pltpu.SMEM),
                  pl.BlockSpec((tm, D), lambda i: (i, 0)),
                  pl.BlockSpec(memory_space=pl.ANY)],
        out_specs=pl.BlockSpec(memory_space=pl.ANY),
        out_shape=jax.ShapeDtypeStruct((n_rows * ROW_TILE, LANES), jnp.float32),
        scratch_shapes=[pltpu.VMEM((2, tm * ROW_TILE, LANES), jnp.float32), pltpu.SemaphoreType.DMA((2,))],
        input_output_aliases={2: 0},
        compiler_params=_params("arbitrary"),
    )(dest_flat, x, init)


def _moe_ffn_kernel(te_ref, nt_ref, x_ref, w1_ref, b1_ref, w2_ref, b2_ref, y_ref, w1b_ref, w2b_ref):
    i = pl.program_id(0)
    prev = te_ref[jnp.maximum(i - 1, 0)]

    @pl.when(jnp.logical_or(i == 0, te_ref[i] != prev))
    def _():
        w1b_ref[...] = w1_ref[0].astype(jnp.bfloat16)
        w2b_ref[...] = w2_ref[0].astype(jnp.bfloat16)

    @pl.when(i < nt_ref[0])
    def _():
        x = _from_token_tiles(x_ref, MOE_TILE).astype(jnp.bfloat16)
        h = jnp.dot(x, w1b_ref[...], preferred_element_type=jnp.float32) + b1_ref[0]
        glu = jnp.minimum(h[:, :D_FF], SWIGLU_LIMIT)
        lin = jnp.clip(h[:, D_FF:], -SWIGLU_LIMIT, SWIGLU_LIMIT)
        act = glu * jax.nn.sigmoid(SWIGLU_ALPHA * glu) * (lin + 1.0)
        y = jnp.dot(act.astype(jnp.bfloat16), w2b_ref[...], preferred_element_type=jnp.float32) + b2_ref[0]
        _to_token_tiles(y, y_ref, MOE_TILE)

    @pl.when(i >= nt_ref[0])
    def _():
        y_ref[...] = jnp.zeros(y_ref.shape, y_ref.dtype)


def _moe_expert_ffn(xbuf, tile_expert, n_used, w1, b1, w2, b2):
    D, F2 = w1.shape[1], w1.shape[2]
    P = xbuf.shape[0] // ROW_TILE
    n_tiles = P // MOE_TILE
    blk = (MOE_TILE * ROW_TILE, LANES)
    grid_spec = pltpu.PrefetchScalarGridSpec(
        num_scalar_prefetch=2,
        grid=(n_tiles,),
        in_specs=[pl.BlockSpec(blk, lambda i, te, nt: (jnp.minimum(i, nt[0] - 1), 0)),
                  pl.BlockSpec((1, D, F2), lambda i, te, nt: (te[i], 0, 0)),
                  pl.BlockSpec((1, 1, F2), lambda i, te, nt: (te[i], 0, 0)),
                  pl.BlockSpec((1, D_FF, D), lambda i, te, nt: (te[i], 0, 0)),
                  pl.BlockSpec((1, 1, D), lambda i, te, nt: (te[i], 0, 0))],
        out_specs=pl.BlockSpec(blk, lambda i, te, nt: (i, 0)),
        scratch_shapes=[pltpu.VMEM((D, F2), jnp.bfloat16), pltpu.VMEM((D_FF, D), jnp.bfloat16)],
    )
    return pl.pallas_call(
        _moe_ffn_kernel,
        grid_spec=grid_spec,
        out_shape=jax.ShapeDtypeStruct(xbuf.shape, jnp.float32),
        compiler_params=pltpu.CompilerParams(dimension_semantics=("arbitrary",),
                                             vmem_limit_bytes=MOE_VMEM_LIMIT_BYTES),
    )(tile_expert, n_used, xbuf, w1, b1.reshape(N_EXPERTS, 1, F2), w2, b2.reshape(N_EXPERTS, 1, D))


def _moe_combine_kernel(dest_ref, gate_ref, x_ref, g_ref, b_ref, ybuf_ref, o_ref, rows_ref, sem, *, tm):
    def row_copy(k, r, d):
        return pltpu.make_async_copy(ybuf_ref.at[pl.ds(pl.multiple_of(d * ROW_TILE, ROW_TILE), ROW_TILE), :],
                                     rows_ref.at[k, pl.ds(pl.multiple_of(r * ROW_TILE, ROW_TILE), ROW_TILE), :], sem.at[0])

    def start(r, c):
        for k in range(TOP_K):
            row_copy(k, r, dest_ref[r * TOP_K + k]).start()
        return c

    lax.fori_loop(0, tm, start, 0)

    def wait(a, c):
        row_copy(0, 0, 0).wait()
        return c

    lax.fori_loop(0, tm * TOP_K, wait, 0)
    gate = gate_ref[...]
    ffn = gate[:, 0:1] * _from_token_tiles(rows_ref.at[0], tm)
    for k in range(1, TOP_K):
        ffn = ffn + gate[:, k:k + 1] * _from_token_tiles(rows_ref.at[k], tm)
    o_ref[...] = _layer_norm_rows(DEEPNORM_ALPHA * x_ref[...] + ffn, g_ref[...], b_ref[...])


def _moe_combine(ybuf, dest_flat, gate, x, g, b, tm=256):
    T, D = x.shape
    tm = min(tm, T)
    return pl.pallas_call(
        partial(_moe_combine_kernel, tm=tm),
        grid=(T // tm,),
        in_specs=[pl.BlockSpec((tm * TOP_K,), lambda i: (i,), memory_space=pltpu.SMEM),
                  pl.BlockSpec((tm, TOP_K), lambda i: (i, 0)),
                  pl.BlockSpec((tm, D), lambda i: (i, 0)),
                  pl.BlockSpec((1, D), lambda i: (0, 0)),
                  pl.BlockSpec((1, D), lambda i: (0, 0)),
                  pl.BlockSpec(memory_space=pl.ANY)],
        out_specs=pl.BlockSpec((tm, D), lambda i: (i, 0)),
        out_shape=jax.ShapeDtypeStruct((T, D), jnp.float32),
        scratch_shapes=[pltpu.VMEM((TOP_K, tm * ROW_TILE, LANES), jnp.float32), pltpu.SemaphoreType.DMA((1,))],
        compiler_params=_params("arbitrary"),
    )(dest_flat, gate, x, g.reshape(1, D), b.reshape(1, D), ybuf)


def _moe_layer(x, router_w, router_b, w1, b1, w2, b2, ln_g, ln_b):
    T, D = x.shape
    A = T * TOP_K
    n_tiles = -(-(A + N_EXPERTS * (MOE_TILE - 1)) // MOE_TILE)
    top_e, gate = _router(x, router_w, router_b)
    rank, counts = _moe_rank(top_e)
    counts = counts[0, :N_EXPERTS].astype(jnp.int32)
    padded = (counts + MOE_TILE - 1) // MOE_TILE * MOE_TILE
    pend = jnp.cumsum(padded)
    pstart = pend - padded
    dest = (pstart[top_e] + rank).reshape(A)
    tile_expert = jnp.minimum(jnp.searchsorted(pend, jnp.arange(n_tiles) * MOE_TILE, side='right'),
                              N_EXPERTS - 1).astype(jnp.int32)
    n_used = (pend[-1:] // MOE_TILE).astype(jnp.int32)
    xbuf = _moe_dispatch(x, dest, n_tiles * MOE_TILE)
    ybuf = _moe_expert_ffn(xbuf, tile_expert, n_used, w1, b1, w2, b2)
    return _moe_combine(ybuf, dest, gate, x, ln_g, ln_b)


def kernel(x, ev_w_in, gla_w_gate2, gla_b_gate2, gla_norm_g, s5_a_re, s5_a_im, s5_log_dt, s5_b_re, s5_b_im,
           s5_c_re, s5_c_im, s5_d, s5_w_glu, s5_b_glu, ev_w_out, od_w_in, od_w_out, ln1_g, ln1_b, ln2_g, ln2_b,
           router_w, router_b, moe_w1, moe_b1, moe_w2, moe_b2):
    bsz, L, D = x.shape
    T = bsz * L
    x = x.reshape(T, D)
    for layer in range(DEPTH):
        j = layer // 2
        if layer % 2 == 0:
            s5_params = (s5_a_re[j], s5_a_im[j], s5_log_dt[j], s5_b_re[j], s5_b_im[j], s5_c_re[j], s5_c_im[j], s5_d[j])
            o_1, o_2 = _even_layer_mix(x, ev_w_in[j], gla_w_gate2[j], gla_b_gate2[j], gla_norm_g[j], s5_params,
                                       s5_w_glu[j], s5_b_glu[j], bsz, L)
            w_out = ev_w_out[j]
        else:
            o_1, o_2 = _odd_layer_mix(x, od_w_in[j], bsz, L)
            w_out = od_w_out[j]
        x = _matmul2_res_ln(o_1, o_2, w_out, x, ln1_g[layer], ln1_b[layer])
        x = _moe_layer(x, router_w[layer], router_b[layer], moe_w1[layer], moe_b1[layer], moe_w2[layer],
                       moe_b2[layer], ln2_g[layer], ln2_b[layer])
    return x.reshape(bsz, L, D)
```

```python
import math
from functools import partial

import numpy as np
import jax
import jax.numpy as jnp
from jax import lax
from jax.experimental import pallas as pl
from jax.experimental.pallas import tpu as pltpu

D_MODEL = 1024
DEPTH = 4
DEEPNORM_ALPHA = (2.0 * DEPTH) ** 0.25
LN_EPS = 1e-5
MIX_WIDTH = D_MODEL

GLA_HEADS = 4
GLA_DV = MIX_WIDTH // 2 // GLA_HEADS
GLA_DK = GLA_DV // 2
GLA_GATE_RANK = 16
GLA_GATE_TAU = 16.0
GLA_CHUNK = 64

S5_WIDTH = MIX_WIDTH // 2
S5_GROUP = 16
S5_GROUPS = S5_WIDTH // S5_GROUP
S5_STATE = 64
S5_MAX_RE = -1e-4

EV_SIZES = (GLA_HEADS * GLA_DK, GLA_HEADS * GLA_DK, GLA_HEADS * GLA_DV, GLA_HEADS * GLA_DV, GLA_GATE_RANK, S5_WIDTH)

DSA_HEADS = 8
DSA_HEAD_DIM = 64
DSA_WIDTH = DSA_HEADS * DSA_HEAD_DIM
IDX_HEADS = 4
IDX_DIM = 64
DSA_TOPK_MAX = 256

DIL_PATTERNS = ((128, 1), (512, 4), (2048, 16))
DIL_GROUPS = len(DIL_PATTERNS)
DIL_HEADS = 8
DIL_HEAD_DIM = 64
DIL_WIDTH = DIL_HEADS * DIL_HEAD_DIM

OD_SIZES = (DSA_WIDTH, DSA_WIDTH, DSA_WIDTH, IDX_HEADS * IDX_DIM, IDX_DIM, IDX_HEADS, 3 * DIL_GROUPS * DIL_WIDTH)

N_EXPERTS = 32
TOP_K = 4
D_FF = D_MODEL
SWIGLU_ALPHA = 1.702
SWIGLU_LIMIT = 7.0
MOE_BLOCK = 512

LANES = 128
VMEM_LIMIT_BYTES = 48 * 1024 * 1024
NEG_BIG = -1e30
INT_MIN = -2 ** 31
INT_MAX = 2 ** 31 - 1
KEY_NEG_INF = -0x7F800000

OD_Q, OD_K, OD_V, OD_QI, OD_DIL, OD_KI = 0, 512, 1024, 1536, 2048, 2048 + 9 * 512
OD_COLS_PADDED = 7168


def _split(h, sizes):
    return jnp.split(h, [int(i) for i in np.cumsum(sizes)[:-1]], axis=-1)


def _params(*sem):
    return pltpu.CompilerParams(dimension_semantics=sem, vmem_limit_bytes=VMEM_LIMIT_BYTES)


def _mm_kernel(x_ref, w_ref, o_ref):
    o_ref[...] = jnp.dot(x_ref[...].astype(jnp.bfloat16), w_ref[...].astype(jnp.bfloat16),
                         preferred_element_type=jnp.float32).astype(o_ref.dtype)


def _matmul(x, w, out_dtype=jnp.float32, tm=512, tn=512):
    T, K = x.shape
    N = w.shape[1]
    tm, tn = min(tm, T), min(tn, N)
    assert T % tm == 0 and N % tn == 0
    return pl.pallas_call(
        _mm_kernel,
        grid=(T // tm, N // tn),
        in_specs=[pl.BlockSpec((tm, K), lambda i, j: (i, 0)),
                  pl.BlockSpec((K, tn), lambda i, j: (0, j))],
        out_specs=pl.BlockSpec((tm, tn), lambda i, j: (i, j)),
        out_shape=jax.ShapeDtypeStruct((T, N), out_dtype),
        compiler_params=_params("parallel", "arbitrary"),
    )(x, w)


def _layer_norm_rows(z, g, b):
    mu = jnp.mean(z, axis=-1, keepdims=True)
    zc = z - mu
    var = jnp.mean(zc * zc, axis=-1, keepdims=True)
    return zc * lax.rsqrt(var + LN_EPS) * g + b


def _mm_res_ln_kernel(a_ref, w_ref, x_ref, g_ref, b_ref, o_ref):
    mix = jnp.dot(a_ref[...].astype(jnp.bfloat16), w_ref[...].astype(jnp.bfloat16),
                  preferred_element_type=jnp.float32)
    o_ref[...] = _layer_norm_rows(DEEPNORM_ALPHA * x_ref[...] + mix, g_ref[...], b_ref[...])


def _matmul_res_ln(a, w, x, g, b, tm=512):
    T, K = a.shape
    D = w.shape[1]
    tm = min(tm, T)
    return pl.pallas_call(
        _mm_res_ln_kernel,
        grid=(T // tm,),
        in_specs=[pl.BlockSpec((tm, K), lambda i: (i, 0)),
                  pl.BlockSpec((K, D), lambda i: (0, 0)),
                  pl.BlockSpec((tm, D), lambda i: (i, 0)),
                  pl.BlockSpec((1, D), lambda i: (0, 0)),
                  pl.BlockSpec((1, D), lambda i: (0, 0))],
        out_specs=pl.BlockSpec((tm, D), lambda i: (i, 0)),
        out_shape=jax.ShapeDtypeStruct((T, D), jnp.float32),
        compiler_params=_params("parallel"),
    )(a, w, x, g.reshape(1, D), b.reshape(1, D))


def _mm2_res_ln_kernel(a1_ref, a2_ref, w1_ref, w2_ref, x_ref, g_ref, b_ref, o_ref):
    mix = jnp.dot(a1_ref[...], w1_ref[...], preferred_element_type=jnp.float32)
    mix += jnp.dot(a2_ref[...], w2_ref[...], preferred_element_type=jnp.float32)
    o_ref[...] = _layer_norm_rows(DEEPNORM_ALPHA * x_ref[...] + mix, g_ref[...], b_ref[...])


def _matmul2_res_ln(a1, a2, w, x, g, b, tm=512):
    T, K1 = a1.shape
    D = w.shape[1]
    tm = min(tm, T)
    wb = w.astype(jnp.bfloat16)
    return pl.pallas_call(
        _mm2_res_ln_kernel,
        grid=(T // tm,),
        in_specs=[pl.BlockSpec((tm, K1), lambda i: (i, 0)),
                  pl.BlockSpec((tm, a2.shape[1]), lambda i: (i, 0)),
                  pl.BlockSpec((K1, D), lambda i: (0, 0)),
                  pl.BlockSpec((a2.shape[1], D), lambda i: (1, 0)),
                  pl.BlockSpec((tm, D), lambda i: (i, 0)),
                  pl.BlockSpec((1, D), lambda i: (0, 0)),
                  pl.BlockSpec((1, D), lambda i: (0, 0))],
        out_specs=pl.BlockSpec((tm, D), lambda i: (i, 0)),
        out_shape=jax.ShapeDtypeStruct((T, D), jnp.float32),
        compiler_params=_params("parallel"),
    )(a1, a2, wb, wb, x, g.reshape(1, D), b.reshape(1, D))


def _res_ln_kernel(y_ref, x_ref, g_ref, b_ref, o_ref):
    o_ref[...] = _layer_norm_rows(DEEPNORM_ALPHA * x_ref[...] + y_ref[...], g_ref[...], b_ref[...])


def _res_ln(y, x, g, b, tm=512):
    T, D = x.shape
    tm = min(tm, T)
    return pl.pallas_call(
        _res_ln_kernel,
        grid=(T // tm,),
        in_specs=[pl.BlockSpec((tm, D), lambda i: (i, 0)),
                  pl.BlockSpec((tm, D), lambda i: (i, 0)),
                  pl.BlockSpec((1, D), lambda i: (0, 0)),
                  pl.BlockSpec((1, D), lambda i: (0, 0))],
        out_specs=pl.BlockSpec((tm, D), lambda i: (i, 0)),
        out_shape=jax.ShapeDtypeStruct((T, D), jnp.float32),
        compiler_params=_params("parallel"),
    )(y, x, g.reshape(1, D), b.reshape(1, D))


DSA_TQ = 256
DSA_TK = 512
COUNT_ROWS = 64


def _stack_index_heads(qi_blk):
    return jnp.concatenate([qi_blk[:, LANES * h:LANES * (h + 1)] for h in range(IDX_HEADS)], axis=0)


def _index_keys_t(ki_blk, qi_all, wi_t, q0, k0, tq):
    tk = ki_blk.shape[0]
    d = lax.dot_general(ki_blk, qi_all, (((1,), (1,)), ((), ())), preferred_element_type=jnp.float32)
    sc = wi_t[0:1, :] * jnp.maximum(d[:, 0:tq], 0.0)
    for h in range(1, IDX_HEADS):
        sc = sc + wi_t[h:h + 1, :] * jnp.maximum(d[:, h * tq:(h + 1) * tq], 0.0)
    kpos = k0 + lax.broadcasted_iota(jnp.int32, (tk, 1), 0)
    qpos = q0 + lax.broadcasted_iota(jnp.int32, (1, tq), 1)
    sc = jnp.where(kpos <= qpos, sc, -jnp.inf)
    bits = lax.bitcast_convert_type(sc, jnp.int32)
    return jnp.where(bits < 0, INT_MIN - bits, bits)


def _dsa_select_kernel(qi_ref, wit_ref, ki_ref, thr_ref, cut_ref, key_ref, *, tq, topk):
    i = pl.program_id(1)
    nblk = i + 1
    q0 = i * tq
    qi_all = _stack_index_heads(qi_ref[...])
    wi_t = wit_ref[...]

    def fill(j, carry):
        k0 = pl.multiple_of(j * tq, tq)
        key_ref[j] = _index_keys_t(ki_ref[pl.ds(k0, tq), :], qi_all, wi_t, q0, k0, tq)
        return carry

    lax.fori_loop(0, nblk, fill, 0)

    kr = COUNT_ROWS
    row = lax.broadcasted_iota(jnp.int32, (kr, 1), 0)

    def count(pred):
        def body(j, acc):
            for s in range(tq // kr):
                kk = key_ref[j, s * kr:(s + 1) * kr, :]
                acc = acc + jnp.where(pred(kk, j * tq + s * kr), 1, 0)
            return acc
        acc = lax.fori_loop(0, nblk, body, jnp.zeros((kr, tq), jnp.int32))
        return jnp.sum(acc.astype(jnp.float32), axis=0, keepdims=True).astype(jnp.int32)

    def any_true(mask):
        return jnp.max(jnp.where(mask, 1.0, 0.0)) > 0.5

    def bit_cond(st):
        return jnp.logical_and(st[0] < 32, st[3])

    def bit_step(st):
        p, thr, cnt, _ = st
        cand = thr ^ lax.shift_left(jnp.int32(1), 31 - p)
        c = count(lambda kk, base: kk >= cand)
        take = c >= topk
        thr = jnp.where(take, cand, thr)
        cnt = jnp.where(take, c, cnt)
        return p + 1, thr, cnt, any_true(cnt != topk)

    ncols = nblk * tq
    init = (jnp.int32(0), jnp.full((1, tq), INT_MIN, jnp.int32), jnp.full((1, tq), ncols, jnp.int32), ncols != topk)
    _, thr, cnt, _ = lax.while_loop(bit_cond, bit_step, init)
    tie = jnp.logical_and(cnt > topk, thr > KEY_NEG_INF)

    def resolve_ties():
        need = topk - count(lambda kk, base: kk > thr)

        def col_step(p, c):
            cand = c | lax.shift_left(jnp.int32(1), 14 - p)
            n = count(lambda kk, base: jnp.logical_and(kk == thr, base + row < cand))
            return jnp.where(n < need, cand, c)
        c = lax.fori_loop(0, 15, col_step, jnp.zeros((1, tq), jnp.int32))
        return jnp.where(tie, c + 1, INT_MAX)

    cut = lax.cond(any_true(tie), resolve_ties, lambda: jnp.full((1, tq), INT_MAX, jnp.int32))
    thr_ref[...] = jnp.maximum(thr, KEY_NEG_INF + 1)
    cut_ref[...] = cut


def _dsa_select(hb, wi_t, bsz, L, tq):
    T = bsz * L
    nq = L // tq
    topk = min(DSA_TOPK_MAX, L // 4)
    qmap = lambda b, i: (0, b * nq + i)
    return pl.pallas_call(
        partial(_dsa_select_kernel, tq=tq, topk=topk),
        grid=(bsz, nq),
        in_specs=[pl.BlockSpec((tq, 512), lambda b, i: (b * nq + i, OD_QI // 512)),
                  pl.BlockSpec((8, tq), qmap),
                  pl.BlockSpec((L, LANES), lambda b, i: (b, OD_KI // LANES))],
        out_specs=[pl.BlockSpec((1, tq), qmap), pl.BlockSpec((1, tq), qmap)],
        out_shape=[jax.ShapeDtypeStruct((1, T), jnp.int32), jax.ShapeDtypeStruct((1, T), jnp.int32)],
        scratch_shapes=[pltpu.VMEM((nq, tq, tq), jnp.int32)],
        compiler_params=_params("parallel", "arbitrary"),
    )(hb, wi_t, hb)


def _split_head_pair(x_pair):
    lane = lax.broadcasted_iota(jnp.int32, x_pair.shape, 1)
    zero = jnp.zeros_like(x_pair)
    return jnp.where(lane < 64, x_pair, zero), jnp.where(lane >= 64, x_pair, zero)


def _dsa_attn_kernel(q_ref, qi_ref, wit_ref, thr_ref, cut_ref, k_ref, vt_ref, ki_ref, o_ref,
                     qm_ref, qiall_ref, m_ref, acc_ref, *, tq, tk, nk):
    i = pl.program_id(1)
    j = pl.program_id(2)
    npair = DSA_HEADS // 2

    @pl.when(j == 0)
    def _():
        for p in range(npair):
            qa, qb = _split_head_pair(q_ref[:, LANES * p:LANES * (p + 1)])
            qm_ref[2 * p] = qa
            qm_ref[2 * p + 1] = qb
        qiall_ref[...] = _stack_index_heads(qi_ref[...])
        m_ref[...] = jnp.full(m_ref.shape, NEG_BIG, jnp.float32)
        acc_ref[...] = jnp.zeros(acc_ref.shape, jnp.float32)

    @pl.when(j * tk < (i + 1) * tq)
    def _():
        key = _index_keys_t(ki_ref[...], qiall_ref[...], wit_ref[...], i * tq, j * tk, tq)
        kpos = j * tk + lax.broadcasted_iota(jnp.int32, (tk, 1), 0)
        thr = thr_ref[...]
        sel = jnp.logical_or(key > thr, jnp.logical_and(key == thr, kpos < cut_ref[...]))
        bias = jnp.where(sel, 0.0, NEG_BIG)
        vrow = lax.broadcasted_iota(jnp.int32, (LANES, tk), 0)
        ones = jnp.ones((LANES, tk), jnp.bfloat16)
        scores = []
        for h in range(DSA_HEADS):
            kp = k_ref[:, LANES * (h // 2):LANES * (h // 2 + 1)]
            s = lax.dot_general(kp, qm_ref[h], (((1,), (1,)), ((), ())), preferred_element_type=jnp.float32)
            scores.append(s + bias)
        for h in range(DSA_HEADS):
            vt = vt_ref[LANES * (h // 2):LANES * (h // 2 + 1), :]
            v_aug = jnp.where((vrow < 64) if h % 2 == 0 else (vrow >= 64), vt, ones)
            s = scores[h]
            m_prev = m_ref[h:h + 1, :]
            m_new = jnp.maximum(m_prev, jnp.max(s, axis=0, keepdims=True))
            alpha = jnp.exp(m_prev - m_new)
            e = jnp.exp(s - m_new).astype(jnp.bfloat16)
            acc_ref[h] = alpha * acc_ref[h] + jnp.dot(v_aug, e, preferred_element_type=jnp.float32)
            m_ref[h:h + 1, :] = m_new

    @pl.when(j == nk - 1)
    def _():
        rowi = lax.broadcasted_iota(jnp.int32, (LANES, tq), 0)
        for p in range(npair):
            a = acc_ref[2 * p]
            b = acc_ref[2 * p + 1]
            out_t = jnp.where(rowi < 64, a / a[64:65, :], b / b[0:1, :])
            o_ref[:, LANES * p:LANES * (p + 1)] = out_t.T.astype(o_ref.dtype)


def _dsa_attention(hb, v_t, wi_t, thr, cut, bsz, L, tq, tk):
    T = bsz * L
    nq, nk = L // tq, L // tk
    kern = partial(_dsa_attn_kernel, tq=tq, tk=tk, nk=nk)

    def qmap(col):
        return lambda b, i, j: (b * nq + i, col)

    qrow = lambda b, i, j: (0, b * nq + i)

    def last_needed(i):
        return ((i + 1) * tq - 1) // tk

    def kmap(col):
        return lambda b, i, j: (b * nk + jnp.minimum(j, last_needed(i)), col)

    return pl.pallas_call(
        kern,
        grid=(bsz, nq, nk),
        in_specs=[pl.BlockSpec((tq, 512), qmap(OD_Q // 512)),
                  pl.BlockSpec((tq, 512), qmap(OD_QI // 512)),
                  pl.BlockSpec((8, tq), qrow),
                  pl.BlockSpec((1, tq), qrow),
                  pl.BlockSpec((1, tq), qrow),
                  pl.BlockSpec((tk, 512), kmap(OD_K // 512)),
                  pl.BlockSpec((DSA_WIDTH, tk), lambda b, i, j: (0, b * nk + jnp.minimum(j, last_needed(i)))),
                  pl.BlockSpec((tk, LANES), kmap(OD_KI // LANES))],
        out_specs=pl.BlockSpec((tq, DSA_WIDTH), qmap(0)),
        out_shape=jax.ShapeDtypeStruct((T, DSA_WIDTH), jnp.bfloat16),
        scratch_shapes=[pltpu.VMEM((DSA_HEADS, tq, LANES), jnp.bfloat16),
                        pltpu.VMEM((IDX_HEADS * tq, LANES), jnp.bfloat16),
                        pltpu.VMEM((DSA_HEADS, tq), jnp.float32),
                        pltpu.VMEM((DSA_HEADS, LANES, tq), jnp.float32)],
        compiler_params=_params("parallel", "parallel", "arbitrary"),
    )(hb, hb, wi_t, thr, cut, hb, v_t, hb)


def _dilated_kernel(q_ref, kp_ref, kc_ref, vp_ref, vc_ref, o_ref, lse_ref, *, tq):
    a = pl.program_id(2)
    row = lax.broadcasted_iota(jnp.int32, (tq, 2 * tq), 0)
    c = lax.broadcasted_iota(jnp.int32, (tq, 2 * tq), 1)
    first_col = jnp.where(a == 0, tq, 0)
    valid = jnp.logical_and(jnp.logical_and(c >= row, c <= row + tq), c >= first_col)
    lane = lax.broadcasted_iota(jnp.int32, (tq, LANES), 1)
    for p in range(DIL_HEADS // 2):
        sl = slice(LANES * p, LANES * (p + 1))
        kk = jnp.concatenate([kp_ref[:, sl], kc_ref[:, sl]], axis=0)
        vv = jnp.concatenate([vp_ref[:, sl], vc_ref[:, sl]], axis=0)
        outs, lses = [], []
        for qh in _split_head_pair(q_ref[:, sl]):
            s = lax.dot_general(qh, kk, (((1,), (1,)), ((), ())), preferred_element_type=jnp.float32)
            s = jnp.where(valid, s, NEG_BIG)
            m = jnp.max(s, axis=1, keepdims=True)
            e = jnp.exp(s - m)
            l = jnp.sum(e, axis=1, keepdims=True)
            outs.append(jnp.dot(e.astype(jnp.bfloat16), vv, preferred_element_type=jnp.float32) / l)
            lses.append(m + jnp.log(l))
        o_ref[:, sl] = jnp.where(lane < 64, outs[0], outs[1])
        lse_ref[:, sl] = jnp.where(lane < 64, lses[0], lses[1])


def _dilated_group(hb, bsz, L, g, tq):
    window, dil = DIL_PATTERNS[g]
    assert window // dil == tq
    T = bsz * L
    M = L // dil
    nb = M // tq
    ncol = hb.shape[1] // 512
    hv = hb.reshape(bsz * M, dil * hb.shape[1])
    qcol, kcol, vcol = (OD_DIL // 512 + s * DIL_GROUPS + g for s in range(3))

    def cur(col):
        return lambda b, r, a: (b * nb + a, r * ncol + col)

    def prev(col):
        return lambda b, r, a: (b * nb + jnp.maximum(a - 1, 0), r * ncol + col)

    blk = (tq, 512)
    out_map = lambda b, r, a: (b * nb + a, r)
    o, lse = pl.pallas_call(
        partial(_dilated_kernel, tq=tq),
        grid=(bsz, dil, nb),
        in_specs=[pl.BlockSpec(blk, cur(qcol)), pl.BlockSpec(blk, prev(kcol)), pl.BlockSpec(blk, cur(kcol)),
                  pl.BlockSpec(blk, prev(vcol)), pl.BlockSpec(blk, cur(vcol))],
        out_specs=[pl.BlockSpec(blk, out_map), pl.BlockSpec(blk, out_map)],
        out_shape=[jax.ShapeDtypeStruct((bsz * M, dil * 512), jnp.float32)] * 2,
        compiler_params=_params("parallel", "parallel", "arbitrary"),
    )(hv, hv, hv, hv, hv)
    return o.reshape(T, 512), lse.reshape(T, 512)


def _dilated_combine_kernel(o0, o1, o2, l0, l1, l2, out_ref):
    a, b, c = l0[...], l1[...], l2[...]
    m = jnp.maximum(jnp.maximum(a, b), c)
    ea, eb, ec = jnp.exp(a - m), jnp.exp(b - m), jnp.exp(c - m)
    num = ea * o0[...] + eb * o1[...] + ec * o2[...]
    out_ref[...] = (num / (ea + eb + ec)).astype(out_ref.dtype)


def _dilated_combine(outs, lses, tm=512):
    T, W = outs[0].shape
    tm = min(tm, T)
    spec = pl.BlockSpec((tm, W), lambda i: (i, 0))
    return pl.pallas_call(
        _dilated_combine_kernel,
        grid=(T // tm,),
        in_specs=[spec] * 6,
        out_specs=spec,
        out_shape=jax.ShapeDtypeStruct((T, W), jnp.bfloat16),
        compiler_params=_params("parallel"),
    )(*outs, *lses)


def _mm_nt_kernel(w_ref, x_ref, o_ref):
    o_ref[...] = lax.dot_general(w_ref[...], x_ref[...].astype(jnp.bfloat16), (((1,), (1,)), ((), ())),
                                 preferred_element_type=jnp.float32).astype(o_ref.dtype)


def _matmul_nt(w_t, x, out_dtype, tm=1024):
    N, K = w_t.shape
    T = x.shape[0]
    tm = min(tm, T)
    return pl.pallas_call(
        _mm_nt_kernel,
        grid=(T // tm,),
        in_specs=[pl.BlockSpec((N, K), lambda i: (0, 0)), pl.BlockSpec((tm, K), lambda i: (i, 0))],
        out_specs=pl.BlockSpec((N, tm), lambda i: (0, i)),
        out_shape=jax.ShapeDtypeStruct((N, T), out_dtype),
        compiler_params=_params("parallel"),
    )(w_t, x)


def _odd_weights(w_in):
    D = w_in.shape[0]
    q, k, v, qi, ki, wi, dil = _split(w_in, OD_SIZES)
    zeros = lambda n: jnp.zeros((D, n), w_in.dtype)
    qi_exp = jnp.concatenate(
        [jnp.concatenate([qi[:, IDX_DIM * h:IDX_DIM * (h + 1)] * IDX_DIM ** -0.5, zeros(LANES - IDX_DIM)], axis=1)
         for h in range(IDX_HEADS)], axis=1)
    dil = dil.reshape(D, 3, DIL_GROUPS * DIL_WIDTH)
    dil = jnp.concatenate([dil[:, 0] * DIL_HEAD_DIM ** -0.5, dil[:, 1], dil[:, 2]], axis=1)
    wb = jnp.concatenate([q * DSA_HEAD_DIM ** -0.5, k, v, qi_exp, dil, ki, zeros(LANES - IDX_DIM),
                          zeros(OD_COLS_PADDED - OD_KI - LANES)], axis=1).astype(jnp.bfloat16)
    wv_t = v.T.astype(jnp.bfloat16)
    wwi_t = jnp.concatenate([wi * IDX_HEADS ** -0.5, zeros(8 - IDX_HEADS)], axis=1).T.astype(jnp.bfloat16)
    return wb, wv_t, wwi_t


def _odd_layer_mix(x2, w_in, bsz, L):
    wb, wv_t, wwi_t = _odd_weights(w_in)
    hb = _matmul(x2, wb, jnp.bfloat16, tm=1024, tn=1024)
    v_t = _matmul_nt(wv_t, x2, jnp.bfloat16)
    wi_t = _matmul_nt(wwi_t, x2, jnp.float32)
    tq, tk = min(DSA_TQ, L), min(DSA_TK, L)
    thr, cut = _dsa_select(hb, wi_t, bsz, L, tq)
    o_c = _dsa_attention(hb, v_t, wi_t, thr, cut, bsz, L, tq, tk)
    groups = [_dilated_group(hb, bsz, L, g, 128) for g in range(DIL_GROUPS)]
    o_d = _dilated_combine([o for o, _ in groups], [l for _, l in groups])
    return o_c, o_d


EV_Q, EV_K, EV_V, EV_R, EV_U, EV_G = 0, 256, 512, 1024, 1536, 2048
EV_COLS_PADDED = 2560
GLA_ROWS = 512
S5_CHUNK = 64


def _gla_kernel(q_ref, k_ref, v_ref, r_ref, g_ref, wg_ref, bg_ref, ng_ref, o_ref, st_ref, *, rows):
    C = GLA_CHUNK

    @pl.when(pl.program_id(1) == 0)
    def _():
        st_ref[...] = jnp.zeros(st_ref.shape, jnp.float32)

    ri = lax.broadcasted_iota(jnp.int32, (C, C), 0)
    ci = lax.broadcasted_iota(jnp.int32, (C, C), 1)
    causal = ci <= ri
    tril = jnp.where(causal, 1.0, 0.0).astype(jnp.float32)
    wg = wg_ref[...].astype(jnp.bfloat16)
    bg = bg_ref[...]
    ng = ng_ref[...]

    def chunk(c, carry):
        r0 = pl.multiple_of(c * C, C)
        rs = pl.ds(r0, C)
        logit = jnp.dot(g_ref[rs, :].astype(jnp.bfloat16), wg, preferred_element_type=jnp.float32) + bg
        log_a = jax.nn.log_sigmoid(logit) / GLA_GATE_TAU
        bcum = jnp.dot(tril, log_a, precision=lax.Precision.HIGHEST, preferred_element_type=jnp.float32)
        b_last = bcum[C - 1:C, :]
        q_t = (q_ref[rs, :] * jnp.exp(bcum)).astype(jnp.bfloat16)
        k_t = (k_ref[rs, :] * jnp.exp(-bcum)).astype(jnp.bfloat16)
        k_end = (k_ref[rs, :] * jnp.exp(b_last - bcum)).astype(jnp.bfloat16)
        dec = jnp.exp(b_last)
        for p in range(GLA_HEADS // 2):
            sl = slice(LANES * p, LANES * (p + 1))
            q_halves = _split_head_pair(q_t[:, sl])
            ke_halves = _split_head_pair(k_end[:, sl])
            for half in range(2):
                h = 2 * p + half
                hs = slice(GLA_DV * h, GLA_DV * (h + 1))
                qm = q_halves[half]
                att = lax.dot_general(qm, k_t[:, sl], (((1,), (1,)), ((), ())), preferred_element_type=jnp.float32)
                att = jnp.where(causal, att, 0.0).astype(jnp.bfloat16)
                v_h = v_ref[rs, hs].astype(jnp.bfloat16)
                st = st_ref[h]
                o = jnp.dot(att, v_h, preferred_element_type=jnp.float32)
                o = o + lax.dot_general(qm, st.astype(jnp.bfloat16), (((1,), (1,)), ((), ())),
                                        preferred_element_type=jnp.float32)
                kv_t = lax.dot_general(v_h, ke_halves[half], (((0,), (0,)), ((), ())),
                                       preferred_element_type=jnp.float32)
                st_ref[h] = st * dec[:, sl] + kv_t
                o = o * lax.rsqrt(jnp.mean(o * o, axis=-1, keepdims=True) + LN_EPS) * ng
                o = o * jax.nn.silu(r_ref[rs, hs])
                o_ref[rs, hs] = o.astype(o_ref.dtype)
        return carry

    lax.fori_loop(0, rows // C, chunk, 0)


def _gla(hf, w_gate2, b_gate2, norm_g, bsz, L):
    T = bsz * L
    rows = min(GLA_ROWS, L)
    nb = L // rows
    dkw = GLA_HEADS * GLA_DK
    dvw = GLA_HEADS * GLA_DV
    wg = jnp.pad(w_gate2, ((0, LANES - GLA_GATE_RANK), (0, 0)))

    def rmap(col):
        return lambda b, i: (b * nb + i, col)

    const = lambda b, i: (0, 0)
    return pl.pallas_call(
        partial(_gla_kernel, rows=rows),
        grid=(bsz, nb),
        in_specs=[pl.BlockSpec((rows, dkw), rmap(EV_Q // dkw)),
                  pl.BlockSpec((rows, dkw), rmap(EV_K // dkw)),
                  pl.BlockSpec((rows, dvw), rmap(EV_V // dvw)),
                  pl.BlockSpec((rows, dvw), rmap(EV_R // dvw)),
                  pl.BlockSpec((rows, LANES), rmap(EV_G // LANES)),
                  pl.BlockSpec((LANES, dkw), const),
                  pl.BlockSpec((1, dkw), const),
                  pl.BlockSpec((1, GLA_DV), const)],
        out_specs=pl.BlockSpec((rows, dvw), rmap(0)),
        out_shape=jax.ShapeDtypeStruct((T, dvw), jnp.bfloat16),
        scratch_shapes=[pltpu.VMEM((GLA_HEADS, GLA_DV, LANES), jnp.float32)],
        compiler_params=_params("parallel", "arbitrary"),
    )(hf, hf, hf, hf, hf, wg, b_gate2.reshape(1, dkw), norm_g.reshape(1, GLA_DV))


def _s5_tables(a_re, a_im, log_dt, b_re, b_im, c_re, c_im, d_skip):
    f32 = jnp.float32
    Cs, G, P, N = S5_CHUNK, S5_GROUPS, S5_STATE, S5_GROUP
    lam_re = jnp.minimum(a_re.astype(f32), S5_MAX_RE)
    lam_im = a_im.astype(f32)
    dt = jnp.exp(log_dt.astype(f32))[:, None]
    mag = jnp.exp(lam_re * dt)
    ab_re = mag * jnp.cos(lam_im * dt)
    ab_im = mag * jnp.sin(lam_im * dt)
    inv = 1.0 / (lam_re * lam_re + lam_im * lam_im)
    z_re = ((ab_re - 1.0) * lam_re + ab_im * lam_im) * inv
    z_im = (ab_im * lam_re - (ab_re - 1.0) * lam_im) * inv
    br, bi = b_re.astype(f32), b_im.astype(f32)
    bb_re = z_re[..., None] * br - z_im[..., None] * bi
    bb_im = z_re[..., None] * bi + z_im[..., None] * br
    kk = jnp.arange(Cs + 1, dtype=f32)[:, None, None]
    pmag = jnp.exp(kk * (lam_re * dt))
    pw_re = pmag * jnp.cos(kk * (lam_im * dt))
    pw_im = pmag * jnp.sin(kk * (lam_im * dt))
    cr, ci = c_re.astype(f32), c_im.astype(f32)
    ca_re = cr[None] * pw_re[:, :, None, :] - ci[None] * pw_im[:, :, None, :]
    ca_im = cr[None] * pw_im[:, :, None, :] + ci[None] * pw_re[:, :, None, :]
    hi = lax.Precision.HIGHEST
    kern = (jnp.einsum('kgnp,gpm->kgnm', ca_re[:Cs], bb_re, precision=hi)
            - jnp.einsum('kgnp,gpm->kgnm', ca_im[:Cs], bb_im, precision=hi))
    jj = jnp.arange(Cs)[:, None]
    ii = jnp.arange(Cs)[None, :]
    tz = jnp.where((ii >= jj)[:, :, None, None, None], kern[jnp.maximum(ii - jj, 0)], 0.0)
    tz = jnp.transpose(tz, (2, 0, 4, 1, 3)).reshape(G, Cs * N, Cs * N)
    rev_re, rev_im = pw_re[Cs - 1::-1][:Cs], pw_im[Cs - 1::-1][:Cs]
    ws_re = rev_re[..., None] * bb_re[None] - rev_im[..., None] * bb_im[None]
    ws_im = rev_re[..., None] * bb_im[None] + rev_im[..., None] * bb_re[None]
    to_ws = lambda w: jnp.pad(jnp.transpose(w, (1, 0, 3, 2)).reshape(G, Cs * N, P), ((0, 0), (0, 0), (0, LANES - P)))
    to_wo = lambda w: jnp.pad(jnp.transpose(w, (1, 3, 0, 2)).reshape(G, P, Cs * N), ((0, 0), (0, LANES - P), (0, 0)))
    a_cs = jnp.stack([jnp.pad(pw_re[Cs], ((0, 0), (0, LANES - P))), jnp.pad(pw_im[Cs], ((0, 0), (0, LANES - P)))], axis=1)
    d_exp = jnp.tile(d_skip.astype(f32).reshape(G, 1, N), (1, 1, Cs))
    bf = jnp.bfloat16
    return (tz.astype(bf), to_ws(ws_re).astype(bf), to_ws(ws_im).astype(bf),
            to_wo(ca_re[1:]).astype(bf), to_wo(-ca_im[1:]).astype(bf), a_cs, d_exp)


def _s5_kernel(u_ref, tz_ref, wsr_ref, wsi_ref, wor_ref, woi_ref, acs_ref, d_ref, y_ref, xr_ref, xi_ref,
               *, nchunk, nbatch):
    u32 = u_ref[0]
    u = u32.astype(jnp.bfloat16)
    xr_ref[...] = jnp.dot(u, wsr_ref[0], preferred_element_type=jnp.float32)
    xi_ref[...] = jnp.dot(u, wsi_ref[0], preferred_element_type=jnp.float32)
    ar = acs_ref[0, 0:1, :]
    ai = acs_ref[0, 1:2, :]

    def step(c, carry):
        new = []
        for b in range(nbatch):
            sr, si = carry[2 * b], carry[2 * b + 1]
            row = pl.ds(b * nchunk + c, 1)
            lr, li = xr_ref[row, :], xi_ref[row, :]
            xr_ref[row, :] = sr
            xi_ref[row, :] = si
            new += [ar * sr - ai * si + lr, ar * si + ai * sr + li]
        return tuple(new)

    zero = jnp.zeros((1, LANES), jnp.float32)
    lax.fori_loop(0, nchunk, step, (zero,) * (2 * nbatch))
    y = jnp.dot(u, tz_ref[0], preferred_element_type=jnp.float32)
    y = y + jnp.dot(xr_ref[...].astype(jnp.bfloat16), wor_ref[0], preferred_element_type=jnp.float32)
    y = y + jnp.dot(xi_ref[...].astype(jnp.bfloat16), woi_ref[0], preferred_element_type=jnp.float32)
    y = y + d_ref[0] * u32
    y_ref[0] = jax.nn.gelu(y)


def _s5_scan(ug, tables, bsz, nchunk):
    G, R, W = ug.shape
    tz, wsr, wsi, wor, woi, a_cs, d_exp = tables
    gmap = lambda g: (g, 0, 0)
    return pl.pallas_call(
        partial(_s5_kernel, nchunk=nchunk, nbatch=bsz),
        grid=(G,),
        in_specs=[pl.BlockSpec((1, R, W), gmap), pl.BlockSpec((1, W, W), gmap),
                  pl.BlockSpec((1, W, LANES), gmap), pl.BlockSpec((1, W, LANES), gmap),
                  pl.BlockSpec((1, LANES, W), gmap), pl.BlockSpec((1, LANES, W), gmap),
                  pl.BlockSpec((1, 2, LANES), gmap), pl.BlockSpec((1, 1, W), gmap)],
        out_specs=pl.BlockSpec((1, R, W), gmap),
        out_shape=jax.ShapeDtypeStruct((G, R, W), jnp.float32),
        scratch_shapes=[pltpu.VMEM((R, LANES), jnp.float32), pltpu.VMEM((R, LANES), jnp.float32)],
        compiler_params=_params("parallel"),
    )(ug, tz, wsr, wsi, wor, woi, a_cs, d_exp)


def _glu_kernel(y_ref, w_ref, b_ref, o_ref):
    y = y_ref[...]
    gate = jnp.dot(y.astype(jnp.bfloat16), w_ref[...], preferred_element_type=jnp.float32) + b_ref[...]
    o_ref[...] = (y * jax.nn.sigmoid(gate)).astype(o_ref.dtype)


def _glu(y, w_glu, b_glu, tm=1024):
    T, W = y.shape
    tm = min(tm, T)
    return pl.pallas_call(
        _glu_kernel,
        grid=(T // tm,),
        in_specs=[pl.BlockSpec((tm, W), lambda i: (i, 0)),
                  pl.BlockSpec((W, W), lambda i: (0, 0)),
                  pl.BlockSpec((1, W), lambda i: (0, 0))],
        out_specs=pl.BlockSpec((tm, W), lambda i: (i, 0)),
        out_shape=jax.ShapeDtypeStruct((T, W), jnp.bfloat16),
        compiler_params=_params("parallel"),
    )(y, w_glu.astype(jnp.bfloat16), b_glu.reshape(1, W))


def _s5(hf, s5_params, w_glu, b_glu, bsz, L):
    T = bsz * L
    Cs, G, N = S5_CHUNK, S5_GROUPS, S5_GROUP
    nchunk = L // Cs
    u = hf[:, EV_U:EV_U + S5_WIDTH].reshape(bsz * nchunk, Cs, G, N)
    ug = jnp.transpose(u, (2, 0, 1, 3)).reshape(G, bsz * nchunk, Cs * N)
    y = _s5_scan(ug, _s5_tables(*s5_params), bsz, nchunk)
    y = jnp.transpose(y.reshape(G, bsz * nchunk, Cs, N), (1, 2, 0, 3)).reshape(T, S5_WIDTH)
    return _glu(y, w_glu, b_glu)


def _even_weights(w_in):
    D = w_in.shape[0]
    q, k, v, r, g_lr, u = _split(w_in, EV_SIZES)
    pad = jnp.zeros((D, EV_COLS_PADDED - EV_G - GLA_GATE_RANK), w_in.dtype)
    return jnp.concatenate([q * GLA_DK ** -0.5, k, v, r, u, g_lr, pad], axis=1).astype(jnp.bfloat16)


def _even_layer_mix(x2, w_in, w_gate2, b_gate2, norm_g, s5_params, w_glu, b_glu, bsz, L):
    hf = _matmul(x2, _even_weights(w_in), jnp.float32, tm=1024, tn=512)
    o_a = _gla(hf, w_gate2, b_gate2, norm_g, bsz, L)
    o_b = _s5(hf, s5_params, w_glu, b_glu, bsz, L)
    return o_a, o_b


MOE_TILE = 512
ROW_TILE = 8
MOE_VMEM_LIMIT_BYTES = 56 * 1024 * 1024


def _router_kernel(x_ref, w_ref, b_ref, e_ref, g_ref):
    logits = jnp.dot(x_ref[...].astype(jnp.bfloat16), w_ref[...], preferred_element_type=jnp.float32) + b_ref[...]
    tm = logits.shape[0]
    lane = lax.broadcasted_iota(jnp.int32, (tm, LANES), 1)
    logits = jnp.where(lane < N_EXPERTS, logits, NEG_BIG)
    tops, idxs = [], []
    for _ in range(TOP_K):
        m = jnp.max(logits, axis=1, keepdims=True)
        idx = jnp.min(jnp.where(logits == m, lane, LANES), axis=1, keepdims=True)
        tops.append(m)
        idxs.append(idx)
        logits = jnp.where(lane == idx, NEG_BIG, logits)
    exps = [jnp.exp(t - tops[0]) for t in tops]
    denom = exps[0]
    for e in exps[1:]:
        denom = denom + e
    lane4 = lax.broadcasted_iota(jnp.int32, (tm, TOP_K), 1)
    e_out = jnp.zeros((tm, TOP_K), jnp.int32)
    g_out = jnp.zeros((tm, TOP_K), jnp.float32)
    for k in range(TOP_K):
        e_out = jnp.where(lane4 == k, idxs[k], e_out)
        g_out = jnp.where(lane4 == k, exps[k] / denom, g_out)
    e_ref[...] = e_out
    g_ref[...] = g_out


def _router(x, router_w, router_b, tm=512):
    T, D = x.shape
    tm = min(tm, T)
    w = jnp.pad(router_w, ((0, 0), (0, LANES - N_EXPERTS))).astype(jnp.bfloat16)
    b = jnp.pad(router_b, (0, LANES - N_EXPERTS)).reshape(1, LANES)
    return pl.pallas_call(
        _router_kernel,
        grid=(T // tm,),
        in_specs=[pl.BlockSpec((tm, D), lambda i: (i, 0)),
                  pl.BlockSpec((D, LANES), lambda i: (0, 0)),
                  pl.BlockSpec((1, LANES), lambda i: (0, 0))],
        out_specs=[pl.BlockSpec((tm, TOP_K), lambda i: (i, 0)), pl.BlockSpec((tm, TOP_K), lambda i: (i, 0))],
        out_shape=[jax.ShapeDtypeStruct((T, TOP_K), jnp.int32), jax.ShapeDtypeStruct((T, TOP_K), jnp.float32)],
        compiler_params=_params("parallel"),
    )(x, w, b)


def _moe_rank_kernel(e_ref, rank_ref, count_ref, carry_ref):
    @pl.when(pl.program_id(0) == 0)
    def _():
        carry_ref[...] = jnp.zeros(carry_ref.shape, jnp.float32)

    e = e_ref[...]
    tm = e.shape[0]
    lane = lax.broadcasted_iota(jnp.int32, (tm, LANES), 1)
    onehot = jnp.zeros((tm, LANES), jnp.float32)
    for k in range(TOP_K):
        onehot = onehot + jnp.where(lane == e[:, k:k + 1], 1.0, 0.0)
    ri = lax.broadcasted_iota(jnp.int32, (tm, tm), 0)
    ci = lax.broadcasted_iota(jnp.int32, (tm, tm), 1)
    strict_lower = jnp.where(ci < ri, 1.0, 0.0).astype(jnp.bfloat16)
    before = jnp.dot(strict_lower, onehot.astype(jnp.bfloat16), preferred_element_type=jnp.float32) + carry_ref[...]
    lane4 = lax.broadcasted_iota(jnp.int32, (tm, TOP_K), 1)
    rank = jnp.zeros((tm, TOP_K), jnp.int32)
    for k in range(TOP_K):
        r_k = jnp.sum(jnp.where(lane == e[:, k:k + 1], before, 0.0), axis=1, keepdims=True).astype(jnp.int32)
        rank = jnp.where(lane4 == k, r_k, rank)
    rank_ref[...] = rank
    carry_ref[...] = carry_ref[...] + jnp.sum(onehot, axis=0, keepdims=True)
    count_ref[...] = carry_ref[...]


def _moe_rank(top_e, tm=256):
    T = top_e.shape[0]
    tm = min(tm, T)
    return pl.pallas_call(
        _moe_rank_kernel,
        grid=(T // tm,),
        in_specs=[pl.BlockSpec((tm, TOP_K), lambda i: (i, 0))],
        out_specs=[pl.BlockSpec((tm, TOP_K), lambda i: (i, 0)), pl.BlockSpec((1, LANES), lambda i: (0, 0))],
        out_shape=[jax.ShapeDtypeStruct((T, TOP_K), jnp.int32), jax.ShapeDtypeStruct((1, LANES), jnp.float32)],
        scratch_shapes=[pltpu.VMEM((1, LANES), jnp.float32)],
        compiler_params=_params("arbitrary"),
    )(top_e)


def _to_token_tiles(x, dst_ref, rows):
    for c in range(ROW_TILE):
        dst_ref[pl.ds(c, rows, stride=ROW_TILE), :] = x[:, LANES * c:LANES * (c + 1)]


def _from_token_tiles(src_ref, rows):
    return jnp.concatenate([src_ref[pl.ds(c, rows, stride=ROW_TILE), :] for c in range(ROW_TILE)], axis=1)


def _moe_dispatch_kernel(dest_ref, x_ref, init_ref, xbuf_ref, xs_ref, sem, *, tm):
    del init_ref
    i = pl.program_id(0)
    n = pl.num_programs(0)
    slot = i % 2

    def row_copy(s, r, d):
        return pltpu.make_async_copy(xs_ref.at[s, pl.ds(pl.multiple_of(r * ROW_TILE, ROW_TILE), ROW_TILE), :],
                                     xbuf_ref.at[pl.ds(pl.multiple_of(d * ROW_TILE, ROW_TILE), ROW_TILE), :], sem.at[s])

    def drain(s):
        def body(a, c):
            row_copy(s, 0, 0).wait()
            return c
        lax.fori_loop(0, tm * TOP_K, body, 0)

    @pl.when(i >= 2)
    def _():
        drain(slot)

    _to_token_tiles(x_ref[...], xs_ref.at[slot], tm)

    def body(r, c):
        for k in range(TOP_K):
            row_copy(slot, r, dest_ref[r * TOP_K + k]).start()
        return c

    lax.fori_loop(0, tm, body, 0)

    @pl.when(i == n - 1)
    def _():
        drain(slot)

        @pl.when(n > 1)
        def _():
            drain(1 - slot)


def _moe_dispatch(x, dest_flat, n_rows, tm=256):
    T, D = x.shape
    tm = min(tm, T)
    assert D == ROW_TILE * LANES
    init = jnp.zeros((n_rows * ROW_TILE, LANES), jnp.float32)
    return pl.pallas_call(
        partial(_moe_dispatch_kernel, tm=tm),
        grid=(T // tm,),
        in_specs=[pl.BlockSpec((tm * TOP_K,), lambda i: (i,), memory_space=pltpu.SMEM),
                  pl.BlockSpec((tm, D), lambda i: (i, 0)),
                  pl.BlockSpec(memory_space=pl.ANY)],
        out_specs=pl.BlockSpec(memory_space=pl.ANY),
        out_shape=jax.ShapeDtypeStruct((n_rows * ROW_TILE, LANES), jnp.float32),
        scratch_shapes=[pltpu.VMEM((2, tm * ROW_TILE, LANES), jnp.float32), pltpu.SemaphoreType.DMA((2,))],
        input_output_aliases={2: 0},
        compiler_params=_params("arbitrary"),
    )(dest_flat, x, init)


def _moe_ffn_kernel(te_ref, nt_ref, x_ref, w1_ref, b1_ref, w2_ref, b2_ref, y_ref, w1b_ref, w2b_ref):
    i = pl.program_id(0)
    prev = te_ref[jnp.maximum(i - 1, 0)]

    @pl.when(jnp.logical_or(i == 0, te_ref[i] != prev))
    def _():
        w1b_ref[...] = w1_ref[0].astype(jnp.bfloat16)
        w2b_ref[...] = w2_ref[0].astype(jnp.bfloat16)

    @pl.when(i < nt_ref[0])
    def _():
        x = _from_token_tiles(x_ref, MOE_TILE).astype(jnp.bfloat16)
        h = jnp.dot(x, w1b_ref[...], preferred_element_type=jnp.float32) + b1_ref[0]
        glu = jnp.minimum(h[:, :D_FF], SWIGLU_LIMIT)
        lin = jnp.clip(h[:, D_FF:], -SWIGLU_LIMIT, SWIGLU_LIMIT)
        act = glu * jax.nn.sigmoid(SWIGLU_ALPHA * glu) * (lin + 1.0)
        y = jnp.dot(act.astype(jnp.bfloat16), w2b_ref[...], preferred_element_type=jnp.float32) + b2_ref[0]
        _to_token_tiles(y, y_ref, MOE_TILE)

    @pl.when(i >= nt_ref[0])
    def _():
        y_ref[...] = jnp.zeros(y_ref.shape, y_ref.dtype)


def _moe_expert_ffn(xbuf, tile_expert, n_used, w1, b1, w2, b2):
    D, F2 = w1.shape[1], w1.shape[2]
    P = xbuf.shape[0] // ROW_TILE
    n_tiles = P // MOE_TILE
    blk = (MOE_TILE * ROW_TILE, LANES)
    grid_spec = pltpu.PrefetchScalarGridSpec(
        num_scalar_prefetch=2,
        grid=(n_tiles,),
        in_specs=[pl.BlockSpec(blk, lambda i, te, nt: (jnp.minimum(i, nt[0] - 1), 0)),
                  pl.BlockSpec((1, D, F2), lambda i, te, nt: (te[i], 0, 0)),
                  pl.BlockSpec((1, 1, F2), lambda i, te, nt: (te[i], 0, 0)),
                  pl.BlockSpec((1, D_FF, D), lambda i, te, nt: (te[i], 0, 0)),
                  pl.BlockSpec((1, 1, D), lambda i, te, nt: (te[i], 0, 0))],
        out_specs=pl.BlockSpec(blk, lambda i, te, nt: (i, 0)),
        scratch_shapes=[pltpu.VMEM((D, F2), jnp.bfloat16), pltpu.VMEM((D_FF, D), jnp.bfloat16)],
    )
    return pl.pallas_call(
        _moe_ffn_kernel,
        grid_spec=grid_spec,
        out_shape=jax.ShapeDtypeStruct(xbuf.shape, jnp.float32),
        compiler_params=pltpu.CompilerParams(dimension_semantics=("arbitrary",),
                                             vmem_limit_bytes=MOE_VMEM_LIMIT_BYTES),
    )(tile_expert, n_used, xbuf, w1, b1.reshape(N_EXPERTS, 1, F2), w2, b2.reshape(N_EXPERTS, 1, D))


def _moe_combine_kernel(dest_ref, gate_ref, x_ref, g_ref, b_ref, ybuf_ref, o_ref, rows_ref, sem, *, tm):
    def row_copy(k, r, d):
        return pltpu.make_async_copy(ybuf_ref.at[pl.ds(pl.multiple_of(d * ROW_TILE, ROW_TILE), ROW_TILE), :],
                                     rows_ref.at[k, pl.ds(pl.multiple_of(r * ROW_TILE, ROW_TILE), ROW_TILE), :], sem.at[0])

    def start(r, c):
        for k in range(TOP_K):
            row_copy(k, r, dest_ref[r * TOP_K + k]).start()
        return c

    lax.fori_loop(0, tm, start, 0)

    def wait(a, c):
        row_copy(0, 0, 0).wait()
        return c

    lax.fori_loop(0, tm * TOP_K, wait, 0)
    gate = gate_ref[...]
    ffn = gate[:, 0:1] * _from_token_tiles(rows_ref.at[0], tm)
    for k in range(1, TOP_K):
        ffn = ffn + gate[:, k:k + 1] * _from_token_tiles(rows_ref.at[k], tm)
    o_ref[...] = _layer_norm_rows(DEEPNORM_ALPHA * x_ref[...] + ffn, g_ref[...], b_ref[...])


def _moe_combine(ybuf, dest_flat, gate, x, g, b, tm=256):
    T, D = x.shape
    tm = min(tm, T)
    return pl.pallas_call(
        partial(_moe_combine_kernel, tm=tm),
        grid=(T // tm,),
        in_specs=[pl.BlockSpec((tm * TOP_K,), lambda i: (i,), memory_space=pltpu.SMEM),
                  pl.BlockSpec((tm, TOP_K), lambda i: (i, 0)),
                  pl.BlockSpec((tm, D), lambda i: (i, 0)),
                  pl.BlockSpec((1, D), lambda i: (0, 0)),
                  pl.BlockSpec((1, D), lambda i: (0, 0)),
                  pl.BlockSpec(memory_space=pl.ANY)],
        out_specs=pl.BlockSpec((tm, D), lambda i: (i, 0)),
        out_shape=jax.ShapeDtypeStruct((T, D), jnp.float32),
        scratch_shapes=[pltpu.VMEM((TOP_K, tm * ROW_TILE, LANES), jnp.float32), pltpu.SemaphoreType.DMA((1,))],
        compiler_params=_params("arbitrary"),
    )(dest_flat, gate, x, g.reshape(1, D), b.reshape(1, D), ybuf)


def _moe_layer(x, router_w, router_b, w1, b1, w2, b2, ln_g, ln_b):
    T, D = x.shape
    A = T * TOP_K
    n_tiles = -(-(A + N_EXPERTS * (MOE_TILE - 1)) // MOE_TILE)
    top_e, gate = _router(x, router_w, router_b)
    rank, counts = _moe_rank(top_e)
    counts = counts[0, :N_EXPERTS].astype(jnp.int32)
    padded = (counts + MOE_TILE - 1) // MOE_TILE * MOE_TILE
    pend = jnp.cumsum(padded)
    pstart = pend - padded
    dest = (pstart[top_e] + rank).reshape(A)
    tile_expert = jnp.minimum(jnp.searchsorted(pend, jnp.arange(n_tiles) * MOE_TILE, side='right'),
                              N_EXPERTS - 1).astype(jnp.int32)
    n_used = (pend[-1:] // MOE_TILE).astype(jnp.int32)
    xbuf = _moe_dispatch(x, dest, n_tiles * MOE_TILE)
    ybuf = _moe_expert_ffn(xbuf, tile_expert, n_used, w1, b1, w2, b2)
    return _moe_combine(ybuf, dest, gate, x, ln_g, ln_b)


def kernel(x, ev_w_in, gla_w_gate2, gla_b_gate2, gla_norm_g, s5_a_re, s5_a_im, s5_log_dt, s5_b_re, s5_b_im,
           s5_c_re, s5_c_im, s5_d, s5_w_glu, s5_b_glu, ev_w_out, od_w_in, od_w_out, ln1_g, ln1_b, ln2_g, ln2_b,
           router_w, router_b, moe_w1, moe_b1, moe_w2, moe_b2):
    bsz, L, D = x.shape
    T = bsz * L
    x = x.reshape(T, D)
    for layer in range(DEPTH):
        j = layer // 2
        if layer % 2 == 0:
            s5_params = (s5_a_re[j], s5_a_im[j], s5_log_dt[j], s5_b_re[j], s5_b_im[j], s5_c_re[j], s5_c_im[j], s5_d[j])
            o_1, o_2 = _even_layer_mix(x, ev_w_in[j], gla_w_gate2[j], gla_b_gate2[j], gla_norm_g[j], s5_params,
                                       s5_w_glu[j], s5_b_glu[j], bsz, L)
            w_out = ev_w_out[j]
        else:
            o_1, o_2 = _odd_layer_mix(x, od_w_in[j], bsz, L)
            w_out = od_w_out[j]
        x = _matmul2_res_ln(o_1, o_2, w_out, x, ln1_g[layer], ln1_b[layer])
        x = _moe_layer(x, router_w[layer], router_b[layer], moe_w1[layer], moe_b1[layer], moe_w2[layer],
                       moe_b2[layer], ln2_g[layer], ln2_b[layer])
    return x.reshape(bsz, L, D)
```

```python
import math
from functools import partial

import numpy as np
import jax
import jax.numpy as jnp
from jax import lax
from jax.experimental import pallas as pl
from jax.experimental.pallas import tpu as pltpu

D_MODEL = 1024
DEPTH = 4
DEEPNORM_ALPHA = (2.0 * DEPTH) ** 0.25
LN_EPS = 1e-5
MIX_WIDTH = D_MODEL

GLA_HEADS = 4
GLA_DV = MIX_WIDTH // 2 // GLA_HEADS
GLA_DK = GLA_DV // 2
GLA_GATE_RANK = 16
GLA_GATE_TAU = 16.0
GLA_CHUNK = 64

S5_WIDTH = MIX_WIDTH // 2
S5_GROUP = 16
S5_GROUPS = S5_WIDTH // S5_GROUP
S5_STATE = 64
S5_MAX_RE = -1e-4

EV_SIZES = (GLA_HEADS * GLA_DK, GLA_HEADS * GLA_DK, GLA_HEADS * GLA_DV, GLA_HEADS * GLA_DV, GLA_GATE_RANK, S5_WIDTH)

DSA_HEADS = 8
DSA_HEAD_DIM = 64
DSA_WIDTH = DSA_HEADS * DSA_HEAD_DIM
IDX_HEADS = 4
IDX_DIM = 64
DSA_TOPK_MAX = 256

DIL_PATTERNS = ((128, 1), (512, 4), (2048, 16))
DIL_GROUPS = len(DIL_PATTERNS)
DIL_HEADS = 8
DIL_HEAD_DIM = 64
DIL_WIDTH = DIL_HEADS * DIL_HEAD_DIM

OD_SIZES = (DSA_WIDTH, DSA_WIDTH, DSA_WIDTH, IDX_HEADS * IDX_DIM, IDX_DIM, IDX_HEADS, 3 * DIL_GROUPS * DIL_WIDTH)

N_EXPERTS = 32
TOP_K = 4
D_FF = D_MODEL
SWIGLU_ALPHA = 1.702
SWIGLU_LIMIT = 7.0
MOE_BLOCK = 512

LANES = 128
VMEM_LIMIT_BYTES = 48 * 1024 * 1024
NEG_BIG = -1e30
INT_MIN = -2 ** 31
INT_MAX = 2 ** 31 - 1
KEY_NEG_INF = -0x7F800000

OD_Q, OD_K, OD_V, OD_QI, OD_DIL, OD_KI = 0, 512, 1024, 1536, 2048, 2048 + 9 * 512
OD_COLS_PADDED = 7168


def _split(h, sizes):
    return jnp.split(h, [int(i) for i in np.cumsum(sizes)[:-1]], axis=-1)


def _params(*sem):
    return pltpu.CompilerParams(dimension_semantics=sem, vmem_limit_bytes=VMEM_LIMIT_BYTES)


def _mm_kernel(x_ref, w_ref, o_ref):
    o_ref[...] = jnp.dot(x_ref[...].astype(jnp.bfloat16), w_ref[...].astype(jnp.bfloat16),
                         preferred_element_type=jnp.float32).astype(o_ref.dtype)


def _matmul(x, w, out_dtype=jnp.float32, tm=512, tn=512):
    T, K = x.shape
    N = w.shape[1]
    tm, tn = min(tm, T), min(tn, N)
    assert T % tm == 0 and N % tn == 0
    return pl.pallas_call(
        _mm_kernel,
        grid=(T // tm, N // tn),
        in_specs=[pl.BlockSpec((tm, K), lambda i, j: (i, 0)),
                  pl.BlockSpec((K, tn), lambda i, j: (0, j))],
        out_specs=pl.BlockSpec((tm, tn), lambda i, j: (i, j)),
        out_shape=jax.ShapeDtypeStruct((T, N), out_dtype),
        compiler_params=_params("parallel", "arbitrary"),
    )(x, w)


def _layer_norm_rows(z, g, b):
    mu = jnp.mean(z, axis=-1, keepdims=True)
    zc = z - mu
    var = jnp.mean(zc * zc, axis=-1, keepdims=True)
    return zc * lax.rsqrt(var + LN_EPS) * g + b


def _mm_res_ln_kernel(a_ref, w_ref, x_ref, g_ref, b_ref, o_ref):
    mix = jnp.dot(a_ref[...].astype(jnp.bfloat16), w_ref[...].astype(jnp.bfloat16),
                  preferred_element_type=jnp.float32)
    o_ref[...] = _layer_norm_rows(DEEPNORM_ALPHA * x_ref[...] + mix, g_ref[...], b_ref[...])


def _matmul_res_ln(a, w, x, g, b, tm=512):
    T, K = a.shape
    D = w.shape[1]
    tm = min(tm, T)
    return pl.pallas_call(
        _mm_res_ln_kernel,
        grid=(T // tm,),
        in_specs=[pl.BlockSpec((tm, K), lambda i: (i, 0)),
                  pl.BlockSpec((K, D), lambda i: (0, 0)),
                  pl.BlockSpec((tm, D), lambda i: (i, 0)),
                  pl.BlockSpec((1, D), lambda i: (0, 0)),
                  pl.BlockSpec((1, D), lambda i: (0, 0))],
        out_specs=pl.BlockSpec((tm, D), lambda i: (i, 0)),
        out_shape=jax.ShapeDtypeStruct((T, D), jnp.float32),
        compiler_params=_params("parallel"),
    )(a, w, x, g.reshape(1, D), b.reshape(1, D))


def _mm2_res_ln_kernel(a1_ref, a2_ref, w1_ref, w2_ref, x_ref, g_ref, b_ref, o_ref):
    mix = jnp.dot(a1_ref[...], w1_ref[...], preferred_element_type=jnp.float32)
    mix += jnp.dot(a2_ref[...], w2_ref[...], preferred_element_type=jnp.float32)
    o_ref[...] = _layer_norm_rows(DEEPNORM_ALPHA * x_ref[...] + mix, g_ref[...], b_ref[...])


def _matmul2_res_ln(a1, a2, w, x, g, b, tm=512):
    T, K1 = a1.shape
    D = w.shape[1]
    tm = min(tm, T)
    wb = w.astype(jnp.bfloat16)
    return pl.pallas_call(
        _mm2_res_ln_kernel,
        grid=(T // tm,),
        in_specs=[pl.BlockSpec((tm, K1), lambda i: (i, 0)),
                  pl.BlockSpec((tm, a2.shape[1]), lambda i: (i, 0)),
                  pl.BlockSpec((K1, D), lambda i: (0, 0)),
                  pl.BlockSpec((a2.shape[1], D), lambda i: (1, 0)),
                  pl.BlockSpec((tm, D), lambda i: (i, 0)),
                  pl.BlockSpec((1, D), lambda i: (0, 0)),
                  pl.BlockSpec((1, D), lambda i: (0, 0))],
        out_specs=pl.BlockSpec((tm, D), lambda i: (i, 0)),
        out_shape=jax.ShapeDtypeStruct((T, D), jnp.float32),
        compiler_params=_params("parallel"),
    )(a1, a2, wb, wb, x, g.reshape(1, D), b.reshape(1, D))


def _res_ln_kernel(y_ref, x_ref, g_ref, b_ref, o_ref):
    o_ref[...] = _layer_norm_rows(DEEPNORM_ALPHA * x_ref[...] + y_ref[...], g_ref[...], b_ref[...])


def _res_ln(y, x, g, b, tm=512):
    T, D = x.shape
    tm = min(tm, T)
    return pl.pallas_call(
        _res_ln_kernel,
        grid=(T // tm,),
        in_specs=[pl.BlockSpec((tm, D), lambda i: (i, 0)),
                  pl.BlockSpec((tm, D), lambda i: (i, 0)),
                  pl.BlockSpec((1, D), lambda i: (0, 0)),
                  pl.BlockSpec((1, D), lambda i: (0, 0))],
        out_specs=pl.BlockSpec((tm, D), lambda i: (i, 0)),
        out_shape=jax.ShapeDtypeStruct((T, D), jnp.float32),
        compiler_params=_params("parallel"),
    )(y, x, g.reshape(1, D), b.reshape(1, D))


DSA_TQ = 256
DSA_TK = 512
COUNT_ROWS = 64


def _stack_index_heads(qi_blk):
    return jnp.concatenate([qi_blk[:, LANES * h:LANES * (h + 1)] for h in range(IDX_HEADS)], axis=0)


def _index_keys_t(ki_blk, qi_all, wi_t, q0, k0, tq, causal_mask=True):
    tk = ki_blk.shape[0]
    d = lax.dot_general(ki_blk, qi_all, (((1,), (1,)), ((), ())), preferred_element_type=jnp.float32)
    sc = wi_t[0:1, :] * jnp.maximum(d[:, 0:tq], 0.0)
    for h in range(1, IDX_HEADS):
        sc = sc + wi_t[h:h + 1, :] * jnp.maximum(d[:, h * tq:(h + 1) * tq], 0.0)
    if causal_mask:
        kpos = k0 + lax.broadcasted_iota(jnp.int32, (tk, 1), 0)
        qpos = q0 + lax.broadcasted_iota(jnp.int32, (1, tq), 1)
        sc = jnp.where(kpos <= qpos, sc, -jnp.inf)
    bits = lax.bitcast_convert_type(sc, jnp.int32)
    return jnp.where(bits < 0, INT_MIN - bits, bits)


def _dsa_select_kernel(qi_ref, wit_ref, ki_ref, thr_ref, cut_ref, key_ref, *, tq, topk):
    i = pl.program_id(1)
    nblk = i + 1
    q0 = i * tq
    qi_all = _stack_index_heads(qi_ref[...])
    wi_t = wit_ref[...]

    def fill(j, carry):
        k0 = pl.multiple_of(j * tq, tq)
        key_ref[j] = _index_keys_t(ki_ref[pl.ds(k0, tq), :], qi_all, wi_t, q0, k0, tq, causal_mask=False)
        return carry

    lax.fori_loop(0, i, fill, 0)
    key_ref[i] = _index_keys_t(ki_ref[pl.ds(pl.multiple_of(q0, tq), tq), :], qi_all, wi_t, q0, q0, tq)

    kr = COUNT_ROWS
    row = lax.broadcasted_iota(jnp.int32, (kr, 1), 0)

    def count(pred):
        def body(j, acc):
            for s in range(tq // kr):
                kk = key_ref[j, s * kr:(s + 1) * kr, :]
                acc = acc + jnp.where(pred(kk, j * tq + s * kr), 1, 0)
            return acc
        acc = lax.fori_loop(0, nblk, body, jnp.zeros((kr, tq), jnp.int32))
        return jnp.sum(acc.astype(jnp.float32), axis=0, keepdims=True).astype(jnp.int32)

    def any_true(mask):
        return jnp.max(jnp.where(mask, 1.0, 0.0)) > 0.5

    def bit_cond(st):
        return jnp.logical_and(st[0] < 32, st[3])

    def bit_step(st):
        p, thr, cnt, _ = st
        cand = thr ^ lax.shift_left(jnp.int32(1), 31 - p)
        c = count(lambda kk, base: kk >= cand)
        take = c >= topk
        thr = jnp.where(take, cand, thr)
        cnt = jnp.where(take, c, cnt)
        return p + 1, thr, cnt, any_true(cnt != topk)

    ncols = nblk * tq
    init = (jnp.int32(0), jnp.full((1, tq), INT_MIN, jnp.int32), jnp.full((1, tq), ncols, jnp.int32), ncols != topk)
    _, thr, cnt, _ = lax.while_loop(bit_cond, bit_step, init)
    tie = jnp.logical_and(cnt > topk, thr > KEY_NEG_INF)

    def resolve_ties():
        need = topk - count(lambda kk, base: kk > thr)

        def col_step(p, c):
            cand = c | lax.shift_left(jnp.int32(1), 14 - p)
            n = count(lambda kk, base: jnp.logical_and(kk == thr, base + row < cand))
            return jnp.where(n < need, cand, c)
        c = lax.fori_loop(0, 15, col_step, jnp.zeros((1, tq), jnp.int32))
        return jnp.where(tie, c + 1, INT_MAX)

    cut = lax.cond(any_true(tie), resolve_ties, lambda: jnp.full((1, tq), INT_MAX, jnp.int32))
    thr_ref[...] = jnp.maximum(thr, KEY_NEG_INF + 1)
    cut_ref[...] = cut


def _dsa_select(hb, wi_t, bsz, L, tq):
    T = bsz * L
    nq = L // tq
    topk = min(DSA_TOPK_MAX, L // 4)
    qmap = lambda b, i: (0, b * nq + i)
    return pl.pallas_call(
        partial(_dsa_select_kernel, tq=tq, topk=topk),
        grid=(bsz, nq),
        in_specs=[pl.BlockSpec((tq, 512), lambda b, i: (b * nq + i, OD_QI // 512)),
                  pl.BlockSpec((8, tq), qmap),
                  pl.BlockSpec((L, LANES), lambda b, i: (b, OD_KI // LANES))],
        out_specs=[pl.BlockSpec((1, tq), qmap), pl.BlockSpec((1, tq), qmap)],
        out_shape=[jax.ShapeDtypeStruct((1, T), jnp.int32), jax.ShapeDtypeStruct((1, T), jnp.int32)],
        scratch_shapes=[pltpu.VMEM((nq, tq, tq), jnp.int32)],
        compiler_params=_params("parallel", "arbitrary"),
    )(hb, wi_t, hb)


def _split_head_pair(x_pair):
    lane = lax.broadcasted_iota(jnp.int32, x_pair.shape, 1)
    zero = jnp.zeros_like(x_pair)
    return jnp.where(lane < 64, x_pair, zero), jnp.where(lane >= 64, x_pair, zero)


def _dsa_attn_kernel(qidx_ref, kidx_ref, q_ref, qi_ref, wit_ref, thr_ref, cut_ref, k_ref, vt_ref, ki_ref, o_ref,
                     qm_ref, qiall_ref, m_ref, acc_ref, *, tq, tk):
    i = qidx_ref[pl.program_id(1)]
    j = kidx_ref[pl.program_id(1)]
    npair = DSA_HEADS // 2

    @pl.when(j == 0)
    def _():
        for p in range(npair):
            qa, qb = _split_head_pair(q_ref[:, LANES * p:LANES * (p + 1)])
            qm_ref[2 * p] = qa
            qm_ref[2 * p + 1] = qb
        qiall_ref[...] = _stack_index_heads(qi_ref[...])
        m_ref[...] = jnp.full(m_ref.shape, NEG_BIG, jnp.float32)
        acc_ref[...] = jnp.zeros(acc_ref.shape, jnp.float32)

    key = _index_keys_t(ki_ref[...], qiall_ref[...], wit_ref[...], i * tq, j * tk, tq)
    kpos = j * tk + lax.broadcasted_iota(jnp.int32, (tk, 1), 0)
    thr = thr_ref[...]
    sel = jnp.logical_or(key > thr, jnp.logical_and(key == thr, kpos < cut_ref[...]))
    bias = jnp.where(sel, 0.0, NEG_BIG)
    vrow = lax.broadcasted_iota(jnp.int32, (LANES, tk), 0)
    ones = jnp.ones((LANES, tk), jnp.bfloat16)
    scores = []
    for h in range(DSA_HEADS):
        kp = k_ref[:, LANES * (h // 2):LANES * (h // 2 + 1)]
        s = lax.dot_general(kp, qm_ref[h], (((1,), (1,)), ((), ())), preferred_element_type=jnp.float32)
        scores.append(s + bias)
    for h in range(DSA_HEADS):
        vt = vt_ref[LANES * (h // 2):LANES * (h // 2 + 1), :]
        v_aug = jnp.where((vrow < 64) if h % 2 == 0 else (vrow >= 64), vt, ones)
        s = scores[h]
        m_prev = m_ref[h:h + 1, :]
        m_new = jnp.maximum(m_prev, jnp.max(s, axis=0, keepdims=True))
        alpha = jnp.exp(m_prev - m_new)
        e = jnp.exp(s - m_new).astype(jnp.bfloat16)
        acc_ref[h] = alpha * acc_ref[h] + jnp.dot(v_aug, e, preferred_element_type=jnp.float32)
        m_ref[h:h + 1, :] = m_new

    @pl.when(j == ((i + 1) * tq - 1) // tk)
    def _():
        rowi = lax.broadcasted_iota(jnp.int32, (LANES, tq), 0)
        for p in range(npair):
            a = acc_ref[2 * p]
            b = acc_ref[2 * p + 1]
            out_t = jnp.where(rowi < 64, a / a[64:65, :], b / b[0:1, :])
            o_ref[:, LANES * p:LANES * (p + 1)] = out_t.T.astype(o_ref.dtype)


def _dsa_attention(hb, v_t, wi_t, thr, cut, bsz, L, tq, tk):
    T = bsz * L
    nq, nk = L // tq, L // tk
    pairs = [(i, j) for i in range(nq) for j in range(((i + 1) * tq - 1) // tk + 1)]
    qidx = jnp.asarray(np.array([p[0] for p in pairs], np.int32))
    kidx = jnp.asarray(np.array([p[1] for p in pairs], np.int32))

    def qmap(col):
        return lambda b, s, qi, ki: (b * nq + qi[s], col)

    def kmap(col):
        return lambda b, s, qi, ki: (b * nk + ki[s], col)

    qrow = lambda b, s, qi, ki: (0, b * nq + qi[s])
    grid_spec = pltpu.PrefetchScalarGridSpec(
        num_scalar_prefetch=2,
        grid=(bsz, len(pairs)),
        in_specs=[pl.BlockSpec((tq, 512), qmap(OD_Q // 512)),
                  pl.BlockSpec((tq, 512), qmap(OD_QI // 512)),
                  pl.BlockSpec((8, tq), qrow),
                  pl.BlockSpec((1, tq), qrow),
                  pl.BlockSpec((1, tq), qrow),
                  pl.BlockSpec((tk, 512), kmap(OD_K // 512)),
                  pl.BlockSpec((DSA_WIDTH, tk), lambda b, s, qi, ki: (0, b * nk + ki[s])),
                  pl.BlockSpec((tk, LANES), kmap(OD_KI // LANES))],
        out_specs=pl.BlockSpec((tq, DSA_WIDTH), qmap(0)),
        scratch_shapes=[pltpu.VMEM((DSA_HEADS, tq, LANES), jnp.bfloat16),
                        pltpu.VMEM((IDX_HEADS * tq, LANES), jnp.bfloat16),
                        pltpu.VMEM((DSA_HEADS, tq), jnp.float32),
                        pltpu.VMEM((DSA_HEADS, LANES, tq), jnp.float32)])
    return pl.pallas_call(
        partial(_dsa_attn_kernel, tq=tq, tk=tk),
        grid_spec=grid_spec,
        out_shape=jax.ShapeDtypeStruct((T, DSA_WIDTH), jnp.bfloat16),
        compiler_params=_params("parallel", "arbitrary"),
    )(qidx, kidx, hb, hb, wi_t, thr, cut, hb, v_t, hb)


def _dilated_kernel(q_ref, kp_ref, kc_ref, vp_ref, vc_ref, o_ref, lse_ref, *, tq):
    a = pl.program_id(2)
    row = lax.broadcasted_iota(jnp.int32, (tq, 2 * tq), 0)
    c = lax.broadcasted_iota(jnp.int32, (tq, 2 * tq), 1)
    first_col = jnp.where(a == 0, tq, 0)
    valid = jnp.logical_and(jnp.logical_and(c >= row, c <= row + tq), c >= first_col)
    lane = lax.broadcasted_iota(jnp.int32, (tq, LANES), 1)
    for p in range(DIL_HEADS // 2):
        sl = slice(LANES * p, LANES * (p + 1))
        kk = jnp.concatenate([kp_ref[:, sl], kc_ref[:, sl]], axis=0)
        vv = jnp.concatenate([vp_ref[:, sl], vc_ref[:, sl]], axis=0)
        outs, lses = [], []
        for qh in _split_head_pair(q_ref[:, sl]):
            s = lax.dot_general(qh, kk, (((1,), (1,)), ((), ())), preferred_element_type=jnp.float32)
            s = jnp.where(valid, s, NEG_BIG)
            m = jnp.max(s, axis=1, keepdims=True)
            e = jnp.exp(s - m)
            l = jnp.sum(e, axis=1, keepdims=True)
            outs.append(jnp.dot(e.astype(jnp.bfloat16), vv, preferred_element_type=jnp.float32) / l)
            lses.append(m + jnp.log(l))
        o_ref[:, sl] = jnp.where(lane < 64, outs[0], outs[1])
        lse_ref[:, sl] = jnp.where(lane < 64, lses[0], lses[1])


def _dilated_group(hb, bsz, L, g, tq):
    window, dil = DIL_PATTERNS[g]
    assert window // dil == tq
    T = bsz * L
    M = L // dil
    nb = M // tq
    ncol = hb.shape[1] // 512
    hv = hb.reshape(bsz * M, dil * hb.shape[1])
    qcol, kcol, vcol = (OD_DIL // 512 + s * DIL_GROUPS + g for s in range(3))

    def cur(col):
        return lambda b, r, a: (b * nb + a, r * ncol + col)

    def prev(col):
        return lambda b, r, a: (b * nb + jnp.maximum(a - 1, 0), r * ncol + col)

    blk = (tq, 512)
    out_map = lambda b, r, a: (b * nb + a, r)
    o, lse = pl.pallas_call(
        partial(_dilated_kernel, tq=tq),
        grid=(bsz, dil, nb),
        in_specs=[pl.BlockSpec(blk, cur(qcol)), pl.BlockSpec(blk, prev(kcol)), pl.BlockSpec(blk, cur(kcol)),
                  pl.BlockSpec(blk, prev(vcol)), pl.BlockSpec(blk, cur(vcol))],
        out_specs=[pl.BlockSpec(blk, out_map), pl.BlockSpec(blk, out_map)],
        out_shape=[jax.ShapeDtypeStruct((bsz * M, dil * 512), jnp.float32)] * 2,
        compiler_params=_params("parallel", "parallel", "arbitrary"),
    )(hv, hv, hv, hv, hv)
    return o.reshape(T, 512), lse.reshape(T, 512)


def _dilated_combine_kernel(o0, o1, o2, l0, l1, l2, out_ref):
    a, b, c = l0[...], l1[...], l2[...]
    m = jnp.maximum(jnp.maximum(a, b), c)
    ea, eb, ec = jnp.exp(a - m), jnp.exp(b - m), jnp.exp(c - m)
    num = ea * o0[...] + eb * o1[...] + ec * o2[...]
    out_ref[...] = (num / (ea + eb + ec)).astype(out_ref.dtype)


def _dilated_combine(outs, lses, tm=512):
    T, W = outs[0].shape
    tm = min(tm, T)
    spec = pl.BlockSpec((tm, W), lambda i: (i, 0))
    return pl.pallas_call(
        _dilated_combine_kernel,
        grid=(T // tm,),
        in_specs=[spec] * 6,
        out_specs=spec,
        out_shape=jax.ShapeDtypeStruct((T, W), jnp.bfloat16),
        compiler_params=_params("parallel"),
    )(*outs, *lses)


def _mm_nt_kernel(w_ref, x_ref, o_ref):
    o_ref[...] = lax.dot_general(w_ref[...], x_ref[...].astype(jnp.bfloat16), (((1,), (1,)), ((), ())),
                                 preferred_element_type=jnp.float32).astype(o_ref.dtype)


def _matmul_nt(w_t, x, out_dtype, tm=1024):
    N, K = w_t.shape
    T = x.shape[0]
    tm = min(tm, T)
    return pl.pallas_call(
        _mm_nt_kernel,
        grid=(T // tm,),
        in_specs=[pl.BlockSpec((N, K), lambda i: (0, 0)), pl.BlockSpec((tm, K), lambda i: (i, 0))],
        out_specs=pl.BlockSpec((N, tm), lambda i: (0, i)),
        out_shape=jax.ShapeDtypeStruct((N, T), out_dtype),
        compiler_params=_params("parallel"),
    )(w_t, x)


def _odd_weights(w_in):
    D = w_in.shape[0]
    q, k, v, qi, ki, wi, dil = _split(w_in, OD_SIZES)
    zeros = lambda n: jnp.zeros((D, n), w_in.dtype)
    qi_exp = jnp.concatenate(
        [jnp.concatenate([qi[:, IDX_DIM * h:IDX_DIM * (h + 1)] * IDX_DIM ** -0.5, zeros(LANES - IDX_DIM)], axis=1)
         for h in range(IDX_HEADS)], axis=1)
    dil = dil.reshape(D, 3, DIL_GROUPS * DIL_WIDTH)
    dil = jnp.concatenate([dil[:, 0] * DIL_HEAD_DIM ** -0.5, dil[:, 1], dil[:, 2]], axis=1)
    wb = jnp.concatenate([q * DSA_HEAD_DIM ** -0.5, k, v, qi_exp, dil, ki, zeros(LANES - IDX_DIM),
                          zeros(OD_COLS_PADDED - OD_KI - LANES)], axis=1).astype(jnp.bfloat16)
    wv_t = v.T.astype(jnp.bfloat16)
    wwi_t = jnp.concatenate([wi * IDX_HEADS ** -0.5, zeros(8 - IDX_HEADS)], axis=1).T.astype(jnp.bfloat16)
    return wb, wv_t, wwi_t


def _odd_layer_mix(x2, w_in, bsz, L):
    wb, wv_t, wwi_t = _odd_weights(w_in)
    hb = _matmul(x2, wb, jnp.bfloat16, tm=1024, tn=1024)
    v_t = _matmul_nt(wv_t, x2, jnp.bfloat16)
    wi_t = _matmul_nt(wwi_t, x2, jnp.float32)
    tq, tk = min(DSA_TQ, L), min(DSA_TK, L)
    thr, cut = _dsa_select(hb, wi_t, bsz, L, tq)
    o_c = _dsa_attention(hb, v_t, wi_t, thr, cut, bsz, L, tq, tk)
    groups = [_dilated_group(hb, bsz, L, g, 128) for g in range(DIL_GROUPS)]
    o_d = _dilated_combine([o for o, _ in groups], [l for _, l in groups])
    return o_c, o_d


EV_Q, EV_K, EV_V, EV_R, EV_U, EV_G = 0, 256, 512, 1024, 1536, 2048
EV_COLS_PADDED = 2560
GLA_ROWS = 512
S5_CHUNK = 64


def _gla_kernel(q_ref, k_ref, v_ref, r_ref, g_ref, wg_ref, bg_ref, ng_ref, o_ref, st_ref, *, rows):
    C = GLA_CHUNK

    @pl.when(pl.program_id(1) == 0)
    def _():
        st_ref[...] = jnp.zeros(st_ref.shape, jnp.float32)

    ri = lax.broadcasted_iota(jnp.int32, (C, C), 0)
    ci = lax.broadcasted_iota(jnp.int32, (C, C), 1)
    causal = ci <= ri
    tril = jnp.where(causal, 1.0, 0.0).astype(jnp.float32)
    wg = wg_ref[...].astype(jnp.bfloat16)
    bg = bg_ref[...]
    ng = ng_ref[...]

    def chunk(c, carry):
        r0 = pl.multiple_of(c * C, C)
        rs = pl.ds(r0, C)
        logit = jnp.dot(g_ref[rs, :].astype(jnp.bfloat16), wg, preferred_element_type=jnp.float32) + bg
        log_a = jax.nn.log_sigmoid(logit) / GLA_GATE_TAU
        bcum = jnp.dot(tril, log_a, precision=lax.Precision.HIGHEST, preferred_element_type=jnp.float32)
        b_last = bcum[C - 1:C, :]
        q_t = (q_ref[rs, :] * jnp.exp(bcum)).astype(jnp.bfloat16)
        k_t = (k_ref[rs, :] * jnp.exp(-bcum)).astype(jnp.bfloat16)
        k_end = (k_ref[rs, :] * jnp.exp(b_last - bcum)).astype(jnp.bfloat16)
        dec = jnp.exp(b_last)
        for p in range(GLA_HEADS // 2):
            sl = slice(LANES * p, LANES * (p + 1))
            q_halves = _split_head_pair(q_t[:, sl])
            ke_halves = _split_head_pair(k_end[:, sl])
            for half in range(2):
                h = 2 * p + half
                hs = slice(GLA_DV * h, GLA_DV * (h + 1))
                qm = q_halves[half]
                att = lax.dot_general(qm, k_t[:, sl], (((1,), (1,)), ((), ())), preferred_element_type=jnp.float32)
                att = jnp.where(causal, att, 0.0).astype(jnp.bfloat16)
                v_h = v_ref[rs, hs].astype(jnp.bfloat16)
                st = st_ref[h]
                o = jnp.dot(att, v_h, preferred_element_type=jnp.float32)
                o = o + lax.dot_general(qm, st.astype(jnp.bfloat16), (((1,), (1,)), ((), ())),
                                        preferred_element_type=jnp.float32)
                kv_t = lax.dot_general(v_h, ke_halves[half], (((0,), (0,)), ((), ())),
                                       preferred_element_type=jnp.float32)
                st_ref[h] = st * dec[:, sl] + kv_t
                o = o * lax.rsqrt(jnp.mean(o * o, axis=-1, keepdims=True) + LN_EPS) * ng
                o = o * jax.nn.silu(r_ref[rs, hs])
                o_ref[rs, hs] = o.astype(o_ref.dtype)
        return carry

    lax.fori_loop(0, rows // C, chunk, 0)


def _gla(hf, w_gate2, b_gate2, norm_g, bsz, L):
    T = bsz * L
    rows = min(GLA_ROWS, L)
    nb = L // rows
    dkw = GLA_HEADS * GLA_DK
    dvw = GLA_HEADS * GLA_DV
    wg = jnp.pad(w_gate2, ((0, LANES - GLA_GATE_RANK), (0, 0)))

    def rmap(col):
        return lambda b, i: (b * nb + i, col)

    const = lambda b, i: (0, 0)
    return pl.pallas_call(
        partial(_gla_kernel, rows=rows),
        grid=(bsz, nb),
        in_specs=[pl.BlockSpec((rows, dkw), rmap(EV_Q // dkw)),
                  pl.BlockSpec((rows, dkw), rmap(EV_K // dkw)),
                  pl.BlockSpec((rows, dvw), rmap(EV_V // dvw)),
                  pl.BlockSpec((rows, dvw), rmap(EV_R // dvw)),
                  pl.BlockSpec((rows, LANES), rmap(EV_G // LANES)),
                  pl.BlockSpec((LANES, dkw), const),
                  pl.BlockSpec((1, dkw), const),
                  pl.BlockSpec((1, GLA_DV), const)],
        out_specs=pl.BlockSpec((rows, dvw), rmap(0)),
        out_shape=jax.ShapeDtypeStruct((T, dvw), jnp.bfloat16),
        scratch_shapes=[pltpu.VMEM((GLA_HEADS, GLA_DV, LANES), jnp.float32)],
        compiler_params=_params("parallel", "arbitrary"),
    )(hf, hf, hf, hf, hf, wg, b_gate2.reshape(1, dkw), norm_g.reshape(1, GLA_DV))


def _s5_tables(a_re, a_im, log_dt, b_re, b_im, c_re, c_im, d_skip):
    f32 = jnp.float32
    Cs, G, P, N = S5_CHUNK, S5_GROUPS, S5_STATE, S5_GROUP
    lam_re = jnp.minimum(a_re.astype(f32), S5_MAX_RE)
    lam_im = a_im.astype(f32)
    dt = jnp.exp(log_dt.astype(f32))[:, None]
    mag = jnp.exp(lam_re * dt)
    ab_re = mag * jnp.cos(lam_im * dt)
    ab_im = mag * jnp.sin(lam_im * dt)
    inv = 1.0 / (lam_re * lam_re + lam_im * lam_im)
    z_re = ((ab_re - 1.0) * lam_re + ab_im * lam_im) * inv
    z_im = (ab_im * lam_re - (ab_re - 1.0) * lam_im) * inv
    br, bi = b_re.astype(f32), b_im.astype(f32)
    bb_re = z_re[..., None] * br - z_im[..., None] * bi
    bb_im = z_re[..., None] * bi + z_im[..., None] * br
    kk = jnp.arange(Cs + 1, dtype=f32)[:, None, None]
    pmag = jnp.exp(kk * (lam_re * dt))
    pw_re = pmag * jnp.cos(kk * (lam_im * dt))
    pw_im = pmag * jnp.sin(kk * (lam_im * dt))
    cr, ci = c_re.astype(f32), c_im.astype(f32)
    ca_re = cr[None] * pw_re[:, :, None, :] - ci[None] * pw_im[:, :, None, :]
    ca_im = cr[None] * pw_im[:, :, None, :] + ci[None] * pw_re[:, :, None, :]
    hi = lax.Precision.HIGHEST
    kern = (jnp.einsum('kgnp,gpm->kgnm', ca_re[:Cs], bb_re, precision=hi)
            - jnp.einsum('kgnp,gpm->kgnm', ca_im[:Cs], bb_im, precision=hi))
    jj = jnp.arange(Cs)[:, None]
    ii = jnp.arange(Cs)[None, :]
    tz = jnp.where((ii >= jj)[:, :, None, None, None], kern[jnp.maximum(ii - jj, 0)], 0.0)
    tz = jnp.transpose(tz, (2, 0, 4, 1, 3)).reshape(G, Cs * N, Cs * N)
    rev_re, rev_im = pw_re[Cs - 1::-1][:Cs], pw_im[Cs - 1::-1][:Cs]
    ws_re = rev_re[..., None] * bb_re[None] - rev_im[..., None] * bb_im[None]
    ws_im = rev_re[..., None] * bb_im[None] + rev_im[..., None] * bb_re[None]
    to_ws = lambda w: jnp.pad(jnp.transpose(w, (1, 0, 3, 2)).reshape(G, Cs * N, P), ((0, 0), (0, 0), (0, LANES - P)))
    to_wo = lambda w: jnp.pad(jnp.transpose(w, (1, 3, 0, 2)).reshape(G, P, Cs * N), ((0, 0), (0, LANES - P), (0, 0)))
    a_cs = jnp.stack([jnp.pad(pw_re[Cs], ((0, 0), (0, LANES - P))), jnp.pad(pw_im[Cs], ((0, 0), (0, LANES - P)))], axis=1)
    d_exp = jnp.tile(d_skip.astype(f32).reshape(G, 1, N), (1, 1, Cs))
    bf = jnp.bfloat16
    return (tz.astype(bf), to_ws(ws_re).astype(bf), to_ws(ws_im).astype(bf),
            to_wo(ca_re[1:]).astype(bf), to_wo(-ca_im[1:]).astype(bf), a_cs, d_exp)


def _s5_kernel(u_ref, tz_ref, wsr_ref, wsi_ref, wor_ref, woi_ref, acs_ref, d_ref, y_ref, xr_ref, xi_ref,
               *, nchunk, nbatch):
    u32 = u_ref[0]
    u = u32.astype(jnp.bfloat16)
    xr_ref[...] = jnp.dot(u, wsr_ref[0], preferred_element_type=jnp.float32)
    xi_ref[...] = jnp.dot(u, wsi_ref[0], preferred_element_type=jnp.float32)
    ar = acs_ref[0, 0:1, :]
    ai = acs_ref[0, 1:2, :]

    def step(c, carry):
        new = []
        for b in range(nbatch):
            sr, si = carry[2 * b], carry[2 * b + 1]
            row = pl.ds(b * nchunk + c, 1)
            lr, li = xr_ref[row, :], xi_ref[row, :]
            xr_ref[row, :] = sr
            xi_ref[row, :] = si
            new += [ar * sr - ai * si + lr, ar * si + ai * sr + li]
        return tuple(new)

    zero = jnp.zeros((1, LANES), jnp.float32)
    lax.fori_loop(0, nchunk, step, (zero,) * (2 * nbatch))
    y = jnp.dot(u, tz_ref[0], preferred_element_type=jnp.float32)
    y = y + jnp.dot(xr_ref[...].astype(jnp.bfloat16), wor_ref[0], preferred_element_type=jnp.float32)
    y = y + jnp.dot(xi_ref[...].astype(jnp.bfloat16), woi_ref[0], preferred_element_type=jnp.float32)
    y = y + d_ref[0] * u32
    y_ref[0] = jax.nn.gelu(y)


def _s5_scan(ug, tables, bsz, nchunk):
    G, R, W = ug.shape
    tz, wsr, wsi, wor, woi, a_cs, d_exp = tables
    gmap = lambda g: (g, 0, 0)
    return pl.pallas_call(
        partial(_s5_kernel, nchunk=nchunk, nbatch=bsz),
        grid=(G,),
        in_specs=[pl.BlockSpec((1, R, W), gmap), pl.BlockSpec((1, W, W), gmap),
                  pl.BlockSpec((1, W, LANES), gmap), pl.BlockSpec((1, W, LANES), gmap),
                  pl.BlockSpec((1, LANES, W), gmap), pl.BlockSpec((1, LANES, W), gmap),
                  pl.BlockSpec((1, 2, LANES), gmap), pl.BlockSpec((1, 1, W), gmap)],
        out_specs=pl.BlockSpec((1, R, W), gmap),
        out_shape=jax.ShapeDtypeStruct((G, R, W), jnp.float32),
        scratch_shapes=[pltpu.VMEM((R, LANES), jnp.float32), pltpu.VMEM((R, LANES), jnp.float32)],
        compiler_params=_params("parallel"),
    )(ug, tz, wsr, wsi, wor, woi, a_cs, d_exp)


def _glu_kernel(y_ref, w_ref, b_ref, o_ref):
    y = y_ref[...]
    gate = jnp.dot(y.astype(jnp.bfloat16), w_ref[...], preferred_element_type=jnp.float32) + b_ref[...]
    o_ref[...] = (y * jax.nn.sigmoid(gate)).astype(o_ref.dtype)


def _glu(y, w_glu, b_glu, tm=1024):
    T, W = y.shape
    tm = min(tm, T)
    return pl.pallas_call(
        _glu_kernel,
        grid=(T // tm,),
        in_specs=[pl.BlockSpec((tm, W), lambda i: (i, 0)),
                  pl.BlockSpec((W, W), lambda i: (0, 0)),
                  pl.BlockSpec((1, W), lambda i: (0, 0))],
        out_specs=pl.BlockSpec((tm, W), lambda i: (i, 0)),
        out_shape=jax.ShapeDtypeStruct((T, W), jnp.bfloat16),
        compiler_params=_params("parallel"),
    )(y, w_glu.astype(jnp.bfloat16), b_glu.reshape(1, W))


def _s5(hf, s5_params, w_glu, b_glu, bsz, L):
    T = bsz * L
    Cs, G, N = S5_CHUNK, S5_GROUPS, S5_GROUP
    nchunk = L // Cs
    u = hf[:, EV_U:EV_U + S5_WIDTH].reshape(bsz * nchunk, Cs, G, N)
    ug = jnp.transpose(u, (2, 0, 1, 3)).reshape(G, bsz * nchunk, Cs * N)
    y = _s5_scan(ug, _s5_tables(*s5_params), bsz, nchunk)
    y = jnp.transpose(y.reshape(G, bsz * nchunk, Cs, N), (1, 2, 0, 3)).reshape(T, S5_WIDTH)
    return _glu(y, w_glu, b_glu)


def _even_weights(w_in):
    D = w_in.shape[0]
    q, k, v, r, g_lr, u = _split(w_in, EV_SIZES)
    pad = jnp.zeros((D, EV_COLS_PADDED - EV_G - GLA_GATE_RANK), w_in.dtype)
    return jnp.concatenate([q * GLA_DK ** -0.5, k, v, r, u, g_lr, pad], axis=1).astype(jnp.bfloat16)


def _even_layer_mix(x2, w_in, w_gate2, b_gate2, norm_g, s5_params, w_glu, b_glu, bsz, L):
    hf = _matmul(x2, _even_weights(w_in), jnp.float32, tm=1024, tn=512)
    o_a = _gla(hf, w_gate2, b_gate2, norm_g, bsz, L)
    o_b = _s5(hf, s5_params, w_glu, b_glu, bsz, L)
    return o_a, o_b


MOE_TILE = 512
ROW_TILE = 8
DMA_UNROLL = 8
MOE_VMEM_LIMIT_BYTES = 56 * 1024 * 1024


def _router_kernel(x_ref, w_ref, b_ref, e_ref, g_ref):
    logits = jnp.dot(x_ref[...].astype(jnp.bfloat16), w_ref[...], preferred_element_type=jnp.float32) + b_ref[...]
    tm = logits.shape[0]
    lane = lax.broadcasted_iota(jnp.int32, (tm, LANES), 1)
    logits = jnp.where(lane < N_EXPERTS, logits, NEG_BIG)
    tops, idxs = [], []
    for _ in range(TOP_K):
        m = jnp.max(logits, axis=1, keepdims=True)
        idx = jnp.min(jnp.where(logits == m, lane, LANES), axis=1, keepdims=True)
        tops.append(m)
        idxs.append(idx)
        logits = jnp.where(lane == idx, NEG_BIG, logits)
    exps = [jnp.exp(t - tops[0]) for t in tops]
    denom = exps[0]
    for e in exps[1:]:
        denom = denom + e
    lane4 = lax.broadcasted_iota(jnp.int32, (tm, TOP_K), 1)
    e_out = jnp.zeros((tm, TOP_K), jnp.int32)
    g_out = jnp.zeros((tm, TOP_K), jnp.float32)
    for k in range(TOP_K):
        e_out = jnp.where(lane4 == k, idxs[k], e_out)
        g_out = jnp.where(lane4 == k, exps[k] / denom, g_out)
    e_ref[...] = e_out
    g_ref[...] = g_out


def _router(x, router_w, router_b, tm=512):
    T, D = x.shape
    tm = min(tm, T)
    w = jnp.pad(router_w, ((0, 0), (0, LANES - N_EXPERTS))).astype(jnp.bfloat16)
    b = jnp.pad(router_b, (0, LANES - N_EXPERTS)).reshape(1, LANES)
    return pl.pallas_call(
        _router_kernel,
        grid=(T // tm,),
        in_specs=[pl.BlockSpec((tm, D), lambda i: (i, 0)),
                  pl.BlockSpec((D, LANES), lambda i: (0, 0)),
                  pl.BlockSpec((1, LANES), lambda i: (0, 0))],
        out_specs=[pl.BlockSpec((tm, TOP_K), lambda i: (i, 0)), pl.BlockSpec((tm, TOP_K), lambda i: (i, 0))],
        out_shape=[jax.ShapeDtypeStruct((T, TOP_K), jnp.int32), jax.ShapeDtypeStruct((T, TOP_K), jnp.float32)],
        compiler_params=_params("parallel"),
    )(x, w, b)


def _moe_rank_kernel(e_ref, rank_ref, count_ref, carry_ref):
    @pl.when(pl.program_id(0) == 0)
    def _():
        carry_ref[...] = jnp.zeros(carry_ref.shape, jnp.float32)

    e = e_ref[...]
    tm = e.shape[0]
    lane = lax.broadcasted_iota(jnp.int32, (tm, LANES), 1)
    onehot = jnp.zeros((tm, LANES), jnp.float32)
    for k in range(TOP_K):
        onehot = onehot + jnp.where(lane == e[:, k:k + 1], 1.0, 0.0)
    ri = lax.broadcasted_iota(jnp.int32, (tm, tm), 0)
    ci = lax.broadcasted_iota(jnp.int32, (tm, tm), 1)
    strict_lower = jnp.where(ci < ri, 1.0, 0.0).astype(jnp.bfloat16)
    before = jnp.dot(strict_lower, onehot.astype(jnp.bfloat16), preferred_element_type=jnp.float32) + carry_ref[...]
    lane4 = lax.broadcasted_iota(jnp.int32, (tm, TOP_K), 1)
    rank = jnp.zeros((tm, TOP_K), jnp.int32)
    for k in range(TOP_K):
        r_k = jnp.sum(jnp.where(lane == e[:, k:k + 1], before, 0.0), axis=1, keepdims=True).astype(jnp.int32)
        rank = jnp.where(lane4 == k, r_k, rank)
    rank_ref[...] = rank
    carry_ref[...] = carry_ref[...] + jnp.sum(onehot, axis=0, keepdims=True)
    count_ref[...] = carry_ref[...]


def _moe_rank(top_e, tm=256):
    T = top_e.shape[0]
    tm = min(tm, T)
    return pl.pallas_call(
        _moe_rank_kernel,
        grid=(T // tm,),
        in_specs=[pl.BlockSpec((tm, TOP_K), lambda i: (i, 0))],
        out_specs=[pl.BlockSpec((tm, TOP_K), lambda i: (i, 0)), pl.BlockSpec((1, LANES), lambda i: (0, 0))],
        out_shape=[jax.ShapeDtypeStruct((T, TOP_K), jnp.int32), jax.ShapeDtypeStruct((1, LANES), jnp.float32)],
        scratch_shapes=[pltpu.VMEM((1, LANES), jnp.float32)],
        compiler_params=_params("arbitrary"),
    )(top_e)


def _to_token_tiles(x, dst_ref, rows):
    for c in range(ROW_TILE):
        dst_ref[pl.ds(c, rows, stride=ROW_TILE), :] = x[:, LANES * c:LANES * (c + 1)]


def _from_token_tiles(src_ref, rows):
    return jnp.concatenate([src_ref[pl.ds(c, rows, stride=ROW_TILE), :] for c in range(ROW_TILE)], axis=1)


def _moe_dispatch_kernel(dest_ref, x_ref, init_ref, xbuf_ref, xs_ref, sem, *, tm):
    del init_ref
    i = pl.program_id(0)
    n = pl.num_programs(0)
    slot = i % 2

    def row_copy(s, r, d):
        return pltpu.make_async_copy(xs_ref.at[s, pl.ds(pl.multiple_of(r * ROW_TILE, ROW_TILE), ROW_TILE), :],
                                     xbuf_ref.at[pl.ds(pl.multiple_of(d * ROW_TILE, ROW_TILE), ROW_TILE), :], sem.at[s])

    def drain(s):
        def body(a, c):
            row_copy(s, 0, 0).wait()
            return c
        lax.fori_loop(0, tm * TOP_K, body, 0, unroll=DMA_UNROLL)

    @pl.when(i >= 2)
    def _():
        drain(slot)

    _to_token_tiles(x_ref[...], xs_ref.at[slot], tm)

    def body(r, c):
        for k in range(TOP_K):
            row_copy(slot, r, dest_ref[r * TOP_K + k]).start()
        return c

    lax.fori_loop(0, tm, body, 0, unroll=DMA_UNROLL)

    @pl.when(i == n - 1)
    def _():
        drain(slot)

        @pl.when(n > 1)
        def _():
            drain(1 - slot)


def _moe_dispatch(x, dest_flat, n_rows, tm=256):
    T, D = x.shape
    tm = min(tm, T)
    assert D == ROW_TILE * LANES
    init = jnp.zeros((n_rows * ROW_TILE, LANES), jnp.float32)
    return pl.pallas_call(
        partial(_moe_dispatch_kernel, tm=tm),
        grid=(T // tm,),
        in_specs=[pl.BlockSpec((tm * TOP_K,), lambda i: (i,), memory_space=pltpu.SMEM),
                  pl.BlockSpec((tm, D), lambda i: (i, 0)),
                  pl.BlockSpec(memory_space=pl.ANY)],
        out_specs=pl.BlockSpec(memory_space=pl.ANY),
        out_shape=jax.ShapeDtypeStruct((n_rows * ROW_TILE, LANES), jnp.float32),
        scratch_shapes=[pltpu.VMEM((2, tm * ROW_TILE, LANES), jnp.float32), pltpu.SemaphoreType.DMA((2,))],
        input_output_aliases={2: 0},
        compiler_params=_params("arbitrary"),
    )(dest_flat, x, init)


def _moe_ffn_kernel(te_ref, nt_ref, x_ref, w1_ref, b1_ref, w2_ref, b2_ref, y_ref, w1b_ref, w2b_ref):
    i = pl.program_id(0)
    prev = te_ref[jnp.maximum(i - 1, 0)]

    @pl.when(jnp.logical_or(i == 0, te_ref[i] != prev))
    def _():
        w1b_ref[...] = w1_ref[0].astype(jnp.bfloat16)
        w2b_ref[...] = w2_ref[0].astype(jnp.bfloat16)

    @pl.when(i < nt_ref[0])
    def _():
        x = _from_token_tiles(x_ref, MOE_TILE).astype(jnp.bfloat16)
        h = jnp.dot(x, w1b_ref[...], preferred_element_type=jnp.float32) + b1_ref[0]
        glu = jnp.minimum(h[:, :D_FF], SWIGLU_LIMIT)
        lin = jnp.clip(h[:, D_FF:], -SWIGLU_LIMIT, SWIGLU_LIMIT)
        act = glu * jax.nn.sigmoid(SWIGLU_ALPHA * glu) * (lin + 1.0)
        y = jnp.dot(act.astype(jnp.bfloat16), w2b_ref[...], preferred_element_type=jnp.float32) + b2_ref[0]
        _to_token_tiles(y, y_ref, MOE_TILE)

    @pl.when(i >= nt_ref[0])
    def _():
        y_ref[...] = jnp.zeros(y_ref.shape, y_ref.dtype)


def _moe_expert_ffn(xbuf, tile_expert, n_used, w1, b1, w2, b2):
    D, F2 = w1.shape[1], w1.shape[2]
    P = xbuf.shape[0] // ROW_TILE
    n_tiles = P // MOE_TILE
    blk = (MOE_TILE * ROW_TILE, LANES)
    grid_spec = pltpu.PrefetchScalarGridSpec(
        num_scalar_prefetch=2,
        grid=(n_tiles,),
        in_specs=[pl.BlockSpec(blk, lambda i, te, nt: (jnp.minimum(i, nt[0] - 1), 0)),
                  pl.BlockSpec((1, D, F2), lambda i, te, nt: (te[i], 0, 0)),
                  pl.BlockSpec((1, 1, F2), lambda i, te, nt: (te[i], 0, 0)),
                  pl.BlockSpec((1, D_FF, D), lambda i, te, nt: (te[i], 0, 0)),
                  pl.BlockSpec((1, 1, D), lambda i, te, nt: (te[i], 0, 0))],
        out_specs=pl.BlockSpec(blk, lambda i, te, nt: (i, 0)),
        scratch_shapes=[pltpu.VMEM((D, F2), jnp.bfloat16), pltpu.VMEM((D_FF, D), jnp.bfloat16)],
    )
    return pl.pallas_call(
        _moe_ffn_kernel,
        grid_spec=grid_spec,
        out_shape=jax.ShapeDtypeStruct(xbuf.shape, jnp.float32),
        compiler_params=pltpu.CompilerParams(dimension_semantics=("arbitrary",),
                                             vmem_limit_bytes=MOE_VMEM_LIMIT_BYTES),
    )(tile_expert, n_used, xbuf, w1, b1.reshape(N_EXPERTS, 1, F2), w2, b2.reshape(N_EXPERTS, 1, D))


def _moe_combine_kernel(dest_ref, gate_ref, x_ref, g_ref, b_ref, ybuf_ref, o_ref, rows_ref, sem, *, tm):
    def row_copy(k, r, d):
        return pltpu.make_async_copy(ybuf_ref.at[pl.ds(pl.multiple_of(d * ROW_TILE, ROW_TILE), ROW_TILE), :],
                                     rows_ref.at[k, pl.ds(pl.multiple_of(r * ROW_TILE, ROW_TILE), ROW_TILE), :], sem.at[0])

    def start(r, c):
        for k in range(TOP_K):
            row_copy(k, r, dest_ref[r * TOP_K + k]).start()
        return c

    lax.fori_loop(0, tm, start, 0, unroll=DMA_UNROLL)

    def wait(a, c):
        row_copy(0, 0, 0).wait()
        return c

    lax.fori_loop(0, tm * TOP_K, wait, 0, unroll=DMA_UNROLL)
    gate = gate_ref[...]
    ffn = gate[:, 0:1] * _from_token_tiles(rows_ref.at[0], tm)
    for k in range(1, TOP_K):
        ffn = ffn + gate[:, k:k + 1] * _from_token_tiles(rows_ref.at[k], tm)
    o_ref[...] = _layer_norm_rows(DEEPNORM_ALPHA * x_ref[...] + ffn, g_ref[...], b_ref[...])


def _moe_combine(ybuf, dest_flat, gate, x, g, b, tm=256):
    T, D = x.shape
    tm = min(tm, T)
    return pl.pallas_call(
        partial(_moe_combine_kernel, tm=tm),
        grid=(T // tm,),
        in_specs=[pl.BlockSpec((tm * TOP_K,), lambda i: (i,), memory_space=pltpu.SMEM),
                  pl.BlockSpec((tm, TOP_K), lambda i: (i, 0)),
                  pl.BlockSpec((tm, D), lambda i: (i, 0)),
                  pl.BlockSpec((1, D), lambda i: (0, 0)),
                  pl.BlockSpec((1, D), lambda i: (0, 0)),
                  pl.BlockSpec(memory_space=pl.ANY)],
        out_specs=pl.BlockSpec((tm, D), lambda i: (i, 0)),
        out_shape=jax.ShapeDtypeStruct((T, D), jnp.float32),
        scratch_shapes=[pltpu.VMEM((TOP_K, tm * ROW_TILE, LANES), jnp.float32), pltpu.SemaphoreType.DMA((1,))],
        compiler_params=_params("arbitrary"),
    )(dest_flat, gate, x, g.reshape(1, D), b.reshape(1, D), ybuf)


def _moe_layer(x, router_w, router_b, w1, b1, w2, b2, ln_g, ln_b):
    T, D = x.shape
    A = T * TOP_K
    n_tiles = -(-(A + N_EXPERTS * (MOE_TILE - 1)) // MOE_TILE)
    top_e, gate = _router(x, router_w, router_b)
    rank, counts = _moe_rank(top_e)
    counts = counts[0, :N_EXPERTS].astype(jnp.int32)
    padded = (counts + MOE_TILE - 1) // MOE_TILE * MOE_TILE
    pend = jnp.cumsum(padded)
    pstart = pend - padded
    dest = (pstart[top_e] + rank).reshape(A)
    tile_expert = jnp.minimum(jnp.searchsorted(pend, jnp.arange(n_tiles) * MOE_TILE, side='right'),
                              N_EXPERTS - 1).astype(jnp.int32)
    n_used = (pend[-1:] // MOE_TILE).astype(jnp.int32)
    xbuf = _moe_dispatch(x, dest, n_tiles * MOE_TILE)
    ybuf = _moe_expert_ffn(xbuf, tile_expert, n_used, w1, b1, w2, b2)
    return _moe_combine(ybuf, dest, gate, x, ln_g, ln_b)


def kernel(x, ev_w_in, gla_w_gate2, gla_b_gate2, gla_norm_g, s5_a_re, s5_a_im, s5_log_dt, s5_b_re, s5_b_im,
           s5_c_re, s5_c_im, s5_d, s5_w_glu, s5_b_glu, ev_w_out, od_w_in, od_w_out, ln1_g, ln1_b, ln2_g, ln2_b,
           router_w, router_b, moe_w1, moe_b1, moe_w2, moe_b2):
    bsz, L, D = x.shape
    T = bsz * L
    x = x.reshape(T, D)
    for layer in range(DEPTH):
        j = layer // 2
        if layer % 2 == 0:
            s5_params = (s5_a_re[j], s5_a_im[j], s5_log_dt[j], s5_b_re[j], s5_b_im[j], s5_c_re[j], s5_c_im[j], s5_d[j])
            o_1, o_2 = _even_layer_mix(x, ev_w_in[j], gla_w_gate2[j], gla_b_gate2[j], gla_norm_g[j], s5_params,
                                       s5_w_glu[j], s5_b_glu[j], bsz, L)
            w_out = ev_w_out[j]
        else:
            o_1, o_2 = _odd_layer_mix(x, od_w_in[j], bsz, L)
            w_out = od_w_out[j]
        x = _matmul2_res_ln(o_1, o_2, w_out, x, ln1_g[layer], ln1_b[layer])
        x = _moe_layer(x, router_w[layer], router_b[layer], moe_w1[layer], moe_b1[layer], moe_w2[layer],
                       moe_b2[layer], ln2_g[layer], ln2_b[layer])
    return x.reshape(bsz, L, D)
```

```python
import math
from functools import partial

import numpy as np
import jax
import jax.numpy as jnp
from jax import lax
from jax.experimental import pallas as pl
from jax.experimental.pallas import tpu as pltpu

D_MODEL = 1024
DEPTH = 4
DEEPNORM_ALPHA = (2.0 * DEPTH) ** 0.25
LN_EPS = 1e-5
MIX_WIDTH = D_MODEL

GLA_HEADS = 4
GLA_DV = MIX_WIDTH // 2 // GLA_HEADS
GLA_DK = GLA_DV // 2
GLA_GATE_RANK = 16
GLA_GATE_TAU = 16.0
GLA_CHUNK = 64

S5_WIDTH = MIX_WIDTH // 2
S5_GROUP = 16
S5_GROUPS = S5_WIDTH // S5_GROUP
S5_STATE = 64
S5_MAX_RE = -1e-4

EV_SIZES = (GLA_HEADS * GLA_DK, GLA_HEADS * GLA_DK, GLA_HEADS * GLA_DV, GLA_HEADS * GLA_DV, GLA_GATE_RANK, S5_WIDTH)

DSA_HEADS = 8
DSA_HEAD_DIM = 64
DSA_WIDTH = DSA_HEADS * DSA_HEAD_DIM
IDX_HEADS = 4
IDX_DIM = 64
DSA_TOPK_MAX = 256

DIL_PATTERNS = ((128, 1), (512, 4), (2048, 16))
DIL_GROUPS = len(DIL_PATTERNS)
DIL_HEADS = 8
DIL_HEAD_DIM = 64
DIL_WIDTH = DIL_HEADS * DIL_HEAD_DIM

OD_SIZES = (DSA_WIDTH, DSA_WIDTH, DSA_WIDTH, IDX_HEADS * IDX_DIM, IDX_DIM, IDX_HEADS, 3 * DIL_GROUPS * DIL_WIDTH)

N_EXPERTS = 32
TOP_K = 4
D_FF = D_MODEL
SWIGLU_ALPHA = 1.702
SWIGLU_LIMIT = 7.0
MOE_BLOCK = 512

LANES = 128
VMEM_LIMIT_BYTES = 48 * 1024 * 1024
NEG_BIG = -1e30
INT_MIN = -2 ** 31
INT_MAX = 2 ** 31 - 1
KEY_NEG_INF = -0x7F800000

OD_Q, OD_K, OD_V, OD_QI, OD_DIL, OD_KI = 0, 512, 1024, 1536, 2048, 2048 + 9 * 512
OD_COLS_PADDED = 7168


def _split(h, sizes):
    return jnp.split(h, [int(i) for i in np.cumsum(sizes)[:-1]], axis=-1)


def _params(*sem):
    return pltpu.CompilerParams(dimension_semantics=sem, vmem_limit_bytes=VMEM_LIMIT_BYTES)


def _mm_kernel(x_ref, w_ref, o_ref):
    o_ref[...] = jnp.dot(x_ref[...].astype(jnp.bfloat16), w_ref[...].astype(jnp.bfloat16),
                         preferred_element_type=jnp.float32).astype(o_ref.dtype)


def _matmul(x, w, out_dtype=jnp.float32, tm=512, tn=512):
    T, K = x.shape
    N = w.shape[1]
    tm, tn = min(tm, T), min(tn, N)
    assert T % tm == 0 and N % tn == 0
    return pl.pallas_call(
        _mm_kernel,
        grid=(T // tm, N // tn),
        in_specs=[pl.BlockSpec((tm, K), lambda i, j: (i, 0)),
                  pl.BlockSpec((K, tn), lambda i, j: (0, j))],
        out_specs=pl.BlockSpec((tm, tn), lambda i, j: (i, j)),
        out_shape=jax.ShapeDtypeStruct((T, N), out_dtype),
        compiler_params=_params("parallel", "arbitrary"),
    )(x, w)


def _layer_norm_rows(z, g, b):
    mu = jnp.mean(z, axis=-1, keepdims=True)
    zc = z - mu
    var = jnp.mean(zc * zc, axis=-1, keepdims=True)
    return zc * lax.rsqrt(var + LN_EPS) * g + b


def _mm_res_ln_kernel(a_ref, w_ref, x_ref, g_ref, b_ref, o_ref):
    mix = jnp.dot(a_ref[...].astype(jnp.bfloat16), w_ref[...].astype(jnp.bfloat16),
                  preferred_element_type=jnp.float32)
    o_ref[...] = _layer_norm_rows(DEEPNORM_ALPHA * x_ref[...] + mix, g_ref[...], b_ref[...])


def _matmul_res_ln(a, w, x, g, b, tm=512):
    T, K = a.shape
    D = w.shape[1]
    tm = min(tm, T)
    return pl.pallas_call(
        _mm_res_ln_kernel,
        grid=(T // tm,),
        in_specs=[pl.BlockSpec((tm, K), lambda i: (i, 0)),
                  pl.BlockSpec((K, D), lambda i: (0, 0)),
                  pl.BlockSpec((tm, D), lambda i: (i, 0)),
                  pl.BlockSpec((1, D), lambda i: (0, 0)),
                  pl.BlockSpec((1, D), lambda i: (0, 0))],
        out_specs=pl.BlockSpec((tm, D), lambda i: (i, 0)),
        out_shape=jax.ShapeDtypeStruct((T, D), jnp.float32),
        compiler_params=_params("parallel"),
    )(a, w, x, g.reshape(1, D), b.reshape(1, D))


def _mm2_res_ln_kernel(a1_ref, a2_ref, w1_ref, w2_ref, x_ref, g_ref, b_ref, o_ref):
    mix = jnp.dot(a1_ref[...], w1_ref[...], preferred_element_type=jnp.float32)
    mix += jnp.dot(a2_ref[...], w2_ref[...], preferred_element_type=jnp.float32)
    o_ref[...] = _layer_norm_rows(DEEPNORM_ALPHA * x_ref[...] + mix, g_ref[...], b_ref[...])


def _matmul2_res_ln(a1, a2, w, x, g, b, tm=512):
    T, K1 = a1.shape
    D = w.shape[1]
    tm = min(tm, T)
    wb = w.astype(jnp.bfloat16)
    return pl.pallas_call(
        _mm2_res_ln_kernel,
        grid=(T // tm,),
        in_specs=[pl.BlockSpec((tm, K1), lambda i: (i, 0)),
                  pl.BlockSpec((tm, a2.shape[1]), lambda i: (i, 0)),
                  pl.BlockSpec((K1, D), lambda i: (0, 0)),
                  pl.BlockSpec((a2.shape[1], D), lambda i: (1, 0)),
                  pl.BlockSpec((tm, D), lambda i: (i, 0)),
                  pl.BlockSpec((1, D), lambda i: (0, 0)),
                  pl.BlockSpec((1, D), lambda i: (0, 0))],
        out_specs=pl.BlockSpec((tm, D), lambda i: (i, 0)),
        out_shape=jax.ShapeDtypeStruct((T, D), jnp.float32),
        compiler_params=_params("parallel"),
    )(a1, a2, wb, wb, x, g.reshape(1, D), b.reshape(1, D))


def _res_ln_kernel(y_ref, x_ref, g_ref, b_ref, o_ref):
    o_ref[...] = _layer_norm_rows(DEEPNORM_ALPHA * x_ref[...] + y_ref[...], g_ref[...], b_ref[...])


def _res_ln(y, x, g, b, tm=512):
    T, D = x.shape
    tm = min(tm, T)
    return pl.pallas_call(
        _res_ln_kernel,
        grid=(T // tm,),
        in_specs=[pl.BlockSpec((tm, D), lambda i: (i, 0)),
                  pl.BlockSpec((tm, D), lambda i: (i, 0)),
                  pl.BlockSpec((1, D), lambda i: (0, 0)),
                  pl.BlockSpec((1, D), lambda i: (0, 0))],
        out_specs=pl.BlockSpec((tm, D), lambda i: (i, 0)),
        out_shape=jax.ShapeDtypeStruct((T, D), jnp.float32),
        compiler_params=_params("parallel"),
    )(y, x, g.reshape(1, D), b.reshape(1, D))


DSA_TQ = 256
DSA_TK = 512
COUNT_ROWS = 64
TIE_CHECK_PASS = 8


def _stack_index_heads(qi_blk):
    return jnp.concatenate([qi_blk[:, LANES * h:LANES * (h + 1)] for h in range(IDX_HEADS)], axis=0)


def _index_keys_t(ki_blk, qi_all, wi_t, q0, k0, tq, causal_mask=True):
    tk = ki_blk.shape[0]
    d = lax.dot_general(ki_blk, qi_all, (((1,), (1,)), ((), ())), preferred_element_type=jnp.float32)
    sc = wi_t[0:1, :] * jnp.maximum(d[:, 0:tq], 0.0)
    for h in range(1, IDX_HEADS):
        sc = sc + wi_t[h:h + 1, :] * jnp.maximum(d[:, h * tq:(h + 1) * tq], 0.0)
    if causal_mask:
        kpos = k0 + lax.broadcasted_iota(jnp.int32, (tk, 1), 0)
        qpos = q0 + lax.broadcasted_iota(jnp.int32, (1, tq), 1)
        sc = jnp.where(kpos <= qpos, sc, -jnp.inf)
    bits = lax.bitcast_convert_type(sc, jnp.int32)
    return jnp.where(bits < 0, INT_MIN - bits, bits)


def _dsa_select_kernel(qi_ref, wit_ref, ki_ref, thr_ref, cut_ref, key_ref, *, tq, topk):
    i = pl.program_id(1)
    nblk = i + 1
    q0 = i * tq
    qi_all = _stack_index_heads(qi_ref[...])
    wi_t = wit_ref[...]

    def fill(j, carry):
        k0 = pl.multiple_of(j * tq, tq)
        key_ref[j] = _index_keys_t(ki_ref[pl.ds(k0, tq), :], qi_all, wi_t, q0, k0, tq, causal_mask=False)
        return carry

    lax.fori_loop(0, i, fill, 0)
    key_ref[i] = _index_keys_t(ki_ref[pl.ds(pl.multiple_of(q0, tq), tq), :], qi_all, wi_t, q0, q0, tq)

    kr = COUNT_ROWS
    row = lax.broadcasted_iota(jnp.int32, (kr, 1), 0)

    def count(pred):
        def body(j, acc):
            for s in range(tq // kr):
                kk = key_ref[j, s * kr:(s + 1) * kr, :]
                acc = acc + jnp.where(pred(kk, j * tq + s * kr), 1, 0)
            return acc
        acc = lax.fori_loop(0, nblk, body, jnp.zeros((kr, tq), jnp.int32))
        return jnp.sum(acc.astype(jnp.float32), axis=0, keepdims=True).astype(jnp.int32)

    def any_true(mask):
        return jnp.max(jnp.where(mask, 1.0, 0.0)) > 0.5

    def bit_cond(st):
        return jnp.logical_and(st[0] < 32, st[4])

    def bit_step(st):
        p, thr, cnt, final, _ = st
        cand = thr ^ lax.shift_left(jnp.int32(1), 31 - p)
        c = count(lambda kk, base: kk >= cand)
        take = c >= topk
        thr = jnp.where(take, cand, thr)
        cnt = jnp.where(take, c, cnt)
        final = lax.cond(p == TIE_CHECK_PASS,
                         lambda: jnp.where(cnt - count(lambda kk, base: kk == thr) < topk, 1, 0),
                         lambda: final)
        return p + 1, thr, cnt, final, any_true(jnp.logical_and(cnt != topk, final == 0))

    ncols = nblk * tq
    init = (jnp.int32(0), jnp.full((1, tq), INT_MIN, jnp.int32), jnp.full((1, tq), ncols, jnp.int32),
            jnp.zeros((1, tq), jnp.int32), ncols != topk)
    _, thr, cnt, _, _ = lax.while_loop(bit_cond, bit_step, init)
    tie = jnp.logical_and(cnt > topk, thr > KEY_NEG_INF)

    def resolve_ties():
        need = (topk - count(lambda kk, base: kk > thr)).astype(jnp.float32)
        ri = lax.broadcasted_iota(jnp.int32, (kr, kr), 0)
        ci = lax.broadcasted_iota(jnp.int32, (kr, kr), 1)
        lower = jnp.where(ci <= ri, 1.0, 0.0).astype(jnp.bfloat16)
        rowf = row.astype(jnp.float32)

        def body(j, st):
            carry, cut_acc = st
            eqs = [key_ref[j, s * kr:(s + 1) * kr, :] == thr for s in range(tq // kr)]
            within = [jnp.dot(lower, jnp.where(eq, 1.0, 0.0).astype(jnp.bfloat16), preferred_element_type=jnp.float32)
                      for eq in eqs]
            for s, (eq, pref) in enumerate(zip(eqs, within)):
                end_pos = rowf + (j * tq + s * kr + 1).astype(jnp.float32)
                cut_acc = jnp.maximum(cut_acc, jnp.where(jnp.logical_and(eq, pref + carry == need), end_pos, 0.0))
                carry = carry + pref[kr - 1:kr, :]
            return carry, cut_acc

        _, cut_acc = lax.fori_loop(0, nblk, body, (jnp.zeros((1, tq), jnp.float32), jnp.zeros((kr, tq), jnp.float32)))
        return jnp.where(tie, jnp.max(cut_acc, axis=0, keepdims=True).astype(jnp.int32), INT_MAX)

    cut = lax.cond(any_true(tie), resolve_ties, lambda: jnp.full((1, tq), INT_MAX, jnp.int32))
    thr_ref[...] = jnp.maximum(thr, KEY_NEG_INF + 1)
    cut_ref[...] = cut


def _dsa_select(hb, wi_t, bsz, L, tq):
    T = bsz * L
    nq = L // tq
    topk = min(DSA_TOPK_MAX, L // 4)
    qmap = lambda b, i: (0, b * nq + i)
    return pl.pallas_call(
        partial(_dsa_select_kernel, tq=tq, topk=topk),
        grid=(bsz, nq),
        in_specs=[pl.BlockSpec((tq, 512), lambda b, i: (b * nq + i, OD_QI // 512)),
                  pl.BlockSpec((8, tq), qmap),
                  pl.BlockSpec((L, LANES), lambda b, i: (b, OD_KI // LANES))],
        out_specs=[pl.BlockSpec((1, tq), qmap), pl.BlockSpec((1, tq), qmap)],
        out_shape=[jax.ShapeDtypeStruct((1, T), jnp.int32), jax.ShapeDtypeStruct((1, T), jnp.int32)],
        scratch_shapes=[pltpu.VMEM((nq, tq, tq), jnp.int32)],
        compiler_params=_params("parallel", "arbitrary"),
    )(hb, wi_t, hb)


def _split_head_pair(x_pair):
    lane = lax.broadcasted_iota(jnp.int32, x_pair.shape, 1)
    zero = jnp.zeros_like(x_pair)
    return jnp.where(lane < 64, x_pair, zero), jnp.where(lane >= 64, x_pair, zero)


def _dsa_attn_kernel(qidx_ref, kidx_ref, q_ref, qi_ref, wit_ref, thr_ref, cut_ref, k_ref, vt_ref, ki_ref, o_ref,
                     qm_ref, qiall_ref, m_ref, acc_ref, *, tq, tk):
    i = qidx_ref[pl.program_id(1)]
    j = kidx_ref[pl.program_id(1)]
    npair = DSA_HEADS // 2

    @pl.when(j == 0)
    def _():
        for p in range(npair):
            qa, qb = _split_head_pair(q_ref[:, LANES * p:LANES * (p + 1)])
            qm_ref[2 * p] = qa
            qm_ref[2 * p + 1] = qb
        qiall_ref[...] = _stack_index_heads(qi_ref[...])
        m_ref[...] = jnp.full(m_ref.shape, NEG_BIG, jnp.float32)
        acc_ref[...] = jnp.zeros(acc_ref.shape, jnp.float32)

    key = _index_keys_t(ki_ref[...], qiall_ref[...], wit_ref[...], i * tq, j * tk, tq)
    kpos = j * tk + lax.broadcasted_iota(jnp.int32, (tk, 1), 0)
    thr = thr_ref[...]
    sel = jnp.logical_or(key > thr, jnp.logical_and(key == thr, kpos < cut_ref[...]))
    bias = jnp.where(sel, 0.0, NEG_BIG)
    vrow = lax.broadcasted_iota(jnp.int32, (LANES, tk), 0)
    ones = jnp.ones((LANES, tk), jnp.bfloat16)
    scores = []
    for h in range(DSA_HEADS):
        kp = k_ref[:, LANES * (h // 2):LANES * (h // 2 + 1)]
        s = lax.dot_general(kp, qm_ref[h], (((1,), (1,)), ((), ())), preferred_element_type=jnp.float32)
        scores.append(s + bias)
    for h in range(DSA_HEADS):
        vt = vt_ref[LANES * (h // 2):LANES * (h // 2 + 1), :]
        v_aug = jnp.where((vrow < 64) if h % 2 == 0 else (vrow >= 64), vt, ones)
        s = scores[h]
        m_prev = m_ref[h:h + 1, :]
        m_new = jnp.maximum(m_prev, jnp.max(s, axis=0, keepdims=True))
        alpha = jnp.exp(m_prev - m_new)
        e = jnp.exp(s - m_new).astype(jnp.bfloat16)
        acc_ref[h] = alpha * acc_ref[h] + jnp.dot(v_aug, e, preferred_element_type=jnp.float32)
        m_ref[h:h + 1, :] = m_new

    @pl.when(j == ((i + 1) * tq - 1) // tk)
    def _():
        rowi = lax.broadcasted_iota(jnp.int32, (LANES, tq), 0)
        for p in range(npair):
            a = acc_ref[2 * p]
            b = acc_ref[2 * p + 1]
            out_t = jnp.where(rowi < 64, a / a[64:65, :], b / b[0:1, :])
            o_ref[:, LANES * p:LANES * (p + 1)] = out_t.T.astype(o_ref.dtype)


def _dsa_attention(hb, v_t, wi_t, thr, cut, bsz, L, tq, tk):
    T = bsz * L
    nq, nk = L // tq, L // tk
    pairs = [(i, j) for i in range(nq) for j in range(((i + 1) * tq - 1) // tk + 1)]
    qidx = jnp.asarray(np.array([p[0] for p in pairs], np.int32))
    kidx = jnp.asarray(np.array([p[1] for p in pairs], np.int32))

    def qmap(col):
        return lambda b, s, qi, ki: (b * nq + qi[s], col)

    def kmap(col):
        return lambda b, s, qi, ki: (b * nk + ki[s], col)

    qrow = lambda b, s, qi, ki: (0, b * nq + qi[s])
    grid_spec = pltpu.PrefetchScalarGridSpec(
        num_scalar_prefetch=2,
        grid=(bsz, len(pairs)),
        in_specs=[pl.BlockSpec((tq, 512), qmap(OD_Q // 512)),
                  pl.BlockSpec((tq, 512), qmap(OD_QI // 512)),
                  pl.BlockSpec((8, tq), qrow),
                  pl.BlockSpec((1, tq), qrow),
                  pl.BlockSpec((1, tq), qrow),
                  pl.BlockSpec((tk, 512), kmap(OD_K // 512)),
                  pl.BlockSpec((DSA_WIDTH, tk), lambda b, s, qi, ki: (0, b * nk + ki[s])),
                  pl.BlockSpec((tk, LANES), kmap(OD_KI // LANES))],
        out_specs=pl.BlockSpec((tq, DSA_WIDTH), qmap(0)),
        scratch_shapes=[pltpu.VMEM((DSA_HEADS, tq, LANES), jnp.bfloat16),
                        pltpu.VMEM((IDX_HEADS * tq, LANES), jnp.bfloat16),
                        pltpu.VMEM((DSA_HEADS, tq), jnp.float32),
                        pltpu.VMEM((DSA_HEADS, LANES, tq), jnp.float32)])
    return pl.pallas_call(
        partial(_dsa_attn_kernel, tq=tq, tk=tk),
        grid_spec=grid_spec,
        out_shape=jax.ShapeDtypeStruct((T, DSA_WIDTH), jnp.bfloat16),
        compiler_params=_params("parallel", "arbitrary"),
    )(qidx, kidx, hb, hb, wi_t, thr, cut, hb, v_t, hb)


def _dilated_kernel(q_ref, kp_ref, kc_ref, vp_ref, vc_ref, o_ref, lse_ref, *, tq):
    a = pl.program_id(2)
    row = lax.broadcasted_iota(jnp.int32, (tq, 2 * tq), 0)
    c = lax.broadcasted_iota(jnp.int32, (tq, 2 * tq), 1)
    first_col = jnp.where(a == 0, tq, 0)
    valid = jnp.logical_and(jnp.logical_and(c >= row, c <= row + tq), c >= first_col)
    lane = lax.broadcasted_iota(jnp.int32, (tq, LANES), 1)
    for p in range(DIL_HEADS // 2):
        sl = slice(LANES * p, LANES * (p + 1))
        kk = jnp.concatenate([kp_ref[:, sl], kc_ref[:, sl]], axis=0)
        vv = jnp.concatenate([vp_ref[:, sl], vc_ref[:, sl]], axis=0)
        outs, lses = [], []
        for qh in _split_head_pair(q_ref[:, sl]):
            s = lax.dot_general(qh, kk, (((1,), (1,)), ((), ())), preferred_element_type=jnp.float32)
            s = jnp.where(valid, s, NEG_BIG)
            m = jnp.max(s, axis=1, keepdims=True)
            e = jnp.exp(s - m)
            l = jnp.sum(e, axis=1, keepdims=True)
            outs.append(jnp.dot(e.astype(jnp.bfloat16), vv, preferred_element_type=jnp.float32) / l)
            lses.append(m + jnp.log(l))
        o_ref[:, sl] = jnp.where(lane < 64, outs[0], outs[1])
        lse_ref[:, sl] = jnp.where(lane < 64, lses[0], lses[1])


def _dilated_group(hb, bsz, L, g, tq):
    window, dil = DIL_PATTERNS[g]
    assert window // dil == tq
    T = bsz * L
    M = L // dil
    nb = M // tq
    ncol = hb.shape[1] // 512
    hv = hb.reshape(bsz * M, dil * hb.shape[1])
    qcol, kcol, vcol = (OD_DIL // 512 + s * DIL_GROUPS + g for s in range(3))

    def cur(col):
        return lambda b, r, a: (b * nb + a, r * ncol + col)

    def prev(col):
        return lambda b, r, a: (b * nb + jnp.maximum(a - 1, 0), r * ncol + col)

    blk = (tq, 512)
    out_map = lambda b, r, a: (b * nb + a, r)
    o, lse = pl.pallas_call(
        partial(_dilated_kernel, tq=tq),
        grid=(bsz, dil, nb),
        in_specs=[pl.BlockSpec(blk, cur(qcol)), pl.BlockSpec(blk, prev(kcol)), pl.BlockSpec(blk, cur(kcol)),
                  pl.BlockSpec(blk, prev(vcol)), pl.BlockSpec(blk, cur(vcol))],
        out_specs=[pl.BlockSpec(blk, out_map), pl.BlockSpec(blk, out_map)],
        out_shape=[jax.ShapeDtypeStruct((bsz * M, dil * 512), jnp.float32)] * 2,
        compiler_params=_params("parallel", "parallel", "arbitrary"),
    )(hv, hv, hv, hv, hv)
    return o.reshape(T, 512), lse.reshape(T, 512)


def _dilated_combine_kernel(o0, o1, o2, l0, l1, l2, out_ref):
    a, b, c = l0[...], l1[...], l2[...]
    m = jnp.maximum(jnp.maximum(a, b), c)
    ea, eb, ec = jnp.exp(a - m), jnp.exp(b - m), jnp.exp(c - m)
    num = ea * o0[...] + eb * o1[...] + ec * o2[...]
    out_ref[...] = (num / (ea + eb + ec)).astype(out_ref.dtype)


def _dilated_combine(outs, lses, tm=512):
    T, W = outs[0].shape
    tm = min(tm, T)
    spec = pl.BlockSpec((tm, W), lambda i: (i, 0))
    return pl.pallas_call(
        _dilated_combine_kernel,
        grid=(T // tm,),
        in_specs=[spec] * 6,
        out_specs=spec,
        out_shape=jax.ShapeDtypeStruct((T, W), jnp.bfloat16),
        compiler_params=_params("parallel"),
    )(*outs, *lses)


def _mm_nt_kernel(w_ref, x_ref, o_ref):
    o_ref[...] = lax.dot_general(w_ref[...], x_ref[...].astype(jnp.bfloat16), (((1,), (1,)), ((), ())),
                                 preferred_element_type=jnp.float32).astype(o_ref.dtype)


def _matmul_nt(w_t, x, out_dtype, tm=1024):
    N, K = w_t.shape
    T = x.shape[0]
    tm = min(tm, T)
    return pl.pallas_call(
        _mm_nt_kernel,
        grid=(T // tm,),
        in_specs=[pl.BlockSpec((N, K), lambda i: (0, 0)), pl.BlockSpec((tm, K), lambda i: (i, 0))],
        out_specs=pl.BlockSpec((N, tm), lambda i: (0, i)),
        out_shape=jax.ShapeDtypeStruct((N, T), out_dtype),
        compiler_params=_params("parallel"),
    )(w_t, x)


def _odd_weights(w_in):
    D = w_in.shape[0]
    q, k, v, qi, ki, wi, dil = _split(w_in, OD_SIZES)
    zeros = lambda n: jnp.zeros((D, n), w_in.dtype)
    qi_exp = jnp.concatenate(
        [jnp.concatenate([qi[:, IDX_DIM * h:IDX_DIM * (h + 1)] * IDX_DIM ** -0.5, zeros(LANES - IDX_DIM)], axis=1)
         for h in range(IDX_HEADS)], axis=1)
    dil = dil.reshape(D, 3, DIL_GROUPS * DIL_WIDTH)
    dil = jnp.concatenate([dil[:, 0] * DIL_HEAD_DIM ** -0.5, dil[:, 1], dil[:, 2]], axis=1)
    wb = jnp.concatenate([q * DSA_HEAD_DIM ** -0.5, k, v, qi_exp, dil, ki, zeros(LANES - IDX_DIM),
                          zeros(OD_COLS_PADDED - OD_KI - LANES)], axis=1).astype(jnp.bfloat16)
    wv_t = v.T.astype(jnp.bfloat16)
    wwi_t = jnp.concatenate([wi * IDX_HEADS ** -0.5, zeros(8 - IDX_HEADS)], axis=1).T.astype(jnp.bfloat16)
    return wb, wv_t, wwi_t


def _odd_layer_mix(x2, w_in, bsz, L):
    wb, wv_t, wwi_t = _odd_weights(w_in)
    hb = _matmul(x2, wb, jnp.bfloat16, tm=1024, tn=1024)
    v_t = _matmul_nt(wv_t, x2, jnp.bfloat16)
    wi_t = _matmul_nt(wwi_t, x2, jnp.float32)
    tq, tk = min(DSA_TQ, L), min(DSA_TK, L)
    thr, cut = _dsa_select(hb, wi_t, bsz, L, tq)
    o_c = _dsa_attention(hb, v_t, wi_t, thr, cut, bsz, L, tq, tk)
    groups = [_dilated_group(hb, bsz, L, g, 128) for g in range(DIL_GROUPS)]
    o_d = _dilated_combine([o for o, _ in groups], [l for _, l in groups])
    return o_c, o_d


EV_Q, EV_K, EV_V, EV_R, EV_U, EV_G = 0, 256, 512, 1024, 1536, 2048
EV_COLS_PADDED = 2560
GLA_ROWS = 512
S5_CHUNK = 64


def _gla_kernel(q_ref, k_ref, v_ref, r_ref, g_ref, wg_ref, bg_ref, ng_ref, o_ref, st_ref, *, rows):
    C = GLA_CHUNK

    @pl.when(pl.program_id(1) == 0)
    def _():
        st_ref[...] = jnp.zeros(st_ref.shape, jnp.float32)

    ri = lax.broadcasted_iota(jnp.int32, (C, C), 0)
    ci = lax.broadcasted_iota(jnp.int32, (C, C), 1)
    causal = ci <= ri
    tril = jnp.where(causal, 1.0, 0.0).astype(jnp.float32)
    wg = wg_ref[...].astype(jnp.bfloat16)
    bg = bg_ref[...]
    ng = ng_ref[...]

    def chunk(c, carry):
        r0 = pl.multiple_of(c * C, C)
        rs = pl.ds(r0, C)
        logit = jnp.dot(g_ref[rs, :].astype(jnp.bfloat16), wg, preferred_element_type=jnp.float32) + bg
        log_a = jax.nn.log_sigmoid(logit) / GLA_GATE_TAU
        bcum = jnp.dot(tril, log_a, precision=lax.Precision.HIGHEST, preferred_element_type=jnp.float32)
        b_last = bcum[C - 1:C, :]
        q_t = (q_ref[rs, :] * jnp.exp(bcum)).astype(jnp.bfloat16)
        k_t = (k_ref[rs, :] * jnp.exp(-bcum)).astype(jnp.bfloat16)
        k_end = (k_ref[rs, :] * jnp.exp(b_last - bcum)).astype(jnp.bfloat16)
        dec = jnp.exp(b_last)
        for p in range(GLA_HEADS // 2):
            sl = slice(LANES * p, LANES * (p + 1))
            q_halves = _split_head_pair(q_t[:, sl])
            ke_halves = _split_head_pair(k_end[:, sl])
            for half in range(2):
                h = 2 * p + half
                hs = slice(GLA_DV * h, GLA_DV * (h + 1))
                qm = q_halves[half]
                att = lax.dot_general(qm, k_t[:, sl], (((1,), (1,)), ((), ())), preferred_element_type=jnp.float32)
                att = jnp.where(causal, att, 0.0).astype(jnp.bfloat16)
                v_h = v_ref[rs, hs].astype(jnp.bfloat16)
                st = st_ref[h]
                o = jnp.dot(att, v_h, preferred_element_type=jnp.float32)
                o = o + lax.dot_general(qm, st.astype(jnp.bfloat16), (((1,), (1,)), ((), ())),
                                        preferred_element_type=jnp.float32)
                kv_t = lax.dot_general(v_h, ke_halves[half], (((0,), (0,)), ((), ())),
                                       preferred_element_type=jnp.float32)
                st_ref[h] = st * dec[:, sl] + kv_t
                o = o * lax.rsqrt(jnp.mean(o * o, axis=-1, keepdims=True) + LN_EPS) * ng
                o = o * jax.nn.silu(r_ref[rs, hs])
                o_ref[rs, hs] = o.astype(o_ref.dtype)
        return carry

    lax.fori_loop(0, rows // C, chunk, 0)


def _gla(hf, w_gate2, b_gate2, norm_g, bsz, L):
    T = bsz * L
    rows = min(GLA_ROWS, L)
    nb = L // rows
    dkw = GLA_HEADS * GLA_DK
    dvw = GLA_HEADS * GLA_DV
    wg = jnp.pad(w_gate2, ((0, LANES - GLA_GATE_RANK), (0, 0)))

    def rmap(col):
        return lambda b, i: (b * nb + i, col)

    const = lambda b, i: (0, 0)
    return pl.pallas_call(
        partial(_gla_kernel, rows=rows),
        grid=(bsz, nb),
        in_specs=[pl.BlockSpec((rows, dkw), rmap(EV_Q // dkw)),
                  pl.BlockSpec((rows, dkw), rmap(EV_K // dkw)),
                  pl.BlockSpec((rows, dvw), rmap(EV_V // dvw)),
                  pl.BlockSpec((rows, dvw), rmap(EV_R // dvw)),
                  pl.BlockSpec((rows, LANES), rmap(EV_G // LANES)),
                  pl.BlockSpec((LANES, dkw), const),
                  pl.BlockSpec((1, dkw), const),
                  pl.BlockSpec((1, GLA_DV), const)],
        out_specs=pl.BlockSpec((rows, dvw), rmap(0)),
        out_shape=jax.ShapeDtypeStruct((T, dvw), jnp.bfloat16),
        scratch_shapes=[pltpu.VMEM((GLA_HEADS, GLA_DV, LANES), jnp.float32)],
        compiler_params=_params("parallel", "arbitrary"),
    )(hf, hf, hf, hf, hf, wg, b_gate2.reshape(1, dkw), norm_g.reshape(1, GLA_DV))


def _s5_tables(a_re, a_im, log_dt, b_re, b_im, c_re, c_im, d_skip):
    f32 = jnp.float32
    Cs, G, P, N = S5_CHUNK, S5_GROUPS, S5_STATE, S5_GROUP
    lam_re = jnp.minimum(a_re.astype(f32), S5_MAX_RE)
    lam_im = a_im.astype(f32)
    dt = jnp.exp(log_dt.astype(f32))[:, None]
    mag = jnp.exp(lam_re * dt)
    ab_re = mag * jnp.cos(lam_im * dt)
    ab_im = mag * jnp.sin(lam_im * dt)
    inv = 1.0 / (lam_re * lam_re + lam_im * lam_im)
    z_re = ((ab_re - 1.0) * lam_re + ab_im * lam_im) * inv
    z_im = (ab_im * lam_re - (ab_re - 1.0) * lam_im) * inv
    br, bi = b_re.astype(f32), b_im.astype(f32)
    bb_re = z_re[..., None] * br - z_im[..., None] * bi
    bb_im = z_re[..., None] * bi + z_im[..., None] * br
    kk = jnp.arange(Cs + 1, dtype=f32)[:, None, None]
    pmag = jnp.exp(kk * (lam_re * dt))
    pw_re = pmag * jnp.cos(kk * (lam_im * dt))
    pw_im = pmag * jnp.sin(kk * (lam_im * dt))
    cr, ci = c_re.astype(f32), c_im.astype(f32)
    ca_re = cr[None] * pw_re[:, :, None, :] - ci[None] * pw_im[:, :, None, :]
    ca_im = cr[None] * pw_im[:, :, None, :] + ci[None] * pw_re[:, :, None, :]
    hi = lax.Precision.HIGHEST
    kern = (jnp.einsum('kgnp,gpm->kgnm', ca_re[:Cs], bb_re, precision=hi)
            - jnp.einsum('kgnp,gpm->kgnm', ca_im[:Cs], bb_im, precision=hi))
    jj = jnp.arange(Cs)[:, None]
    ii = jnp.arange(Cs)[None, :]
    tz = jnp.where((ii >= jj)[:, :, None, None, None], kern[jnp.maximum(ii - jj, 0)], 0.0)
    tz = jnp.transpose(tz, (2, 0, 4, 1, 3)).reshape(G, Cs * N, Cs * N)
    rev_re, rev_im = pw_re[Cs - 1::-1][:Cs], pw_im[Cs - 1::-1][:Cs]
    ws_re = rev_re[..., None] * bb_re[None] - rev_im[..., None] * bb_im[None]
    ws_im = rev_re[..., None] * bb_im[None] + rev_im[..., None] * bb_re[None]
    to_ws = lambda w: jnp.pad(jnp.transpose(w, (1, 0, 3, 2)).reshape(G, Cs * N, P), ((0, 0), (0, 0), (0, LANES - P)))
    to_wo = lambda w: jnp.pad(jnp.transpose(w, (1, 3, 0, 2)).reshape(G, P, Cs * N), ((0, 0), (0, LANES - P), (0, 0)))
    a_cs = jnp.stack([jnp.pad(pw_re[Cs], ((0, 0), (0, LANES - P))), jnp.pad(pw_im[Cs], ((0, 0), (0, LANES - P)))], axis=1)
    d_exp = jnp.tile(d_skip.astype(f32).reshape(G, 1, N), (1, 1, Cs))
    bf = jnp.bfloat16
    return (tz.astype(bf), to_ws(ws_re).astype(bf), to_ws(ws_im).astype(bf),
            to_wo(ca_re[1:]).astype(bf), to_wo(-ca_im[1:]).astype(bf), a_cs, d_exp)


def _s5_kernel(u_ref, tz_ref, wsr_ref, wsi_ref, wor_ref, woi_ref, acs_ref, d_ref, y_ref, xr_ref, xi_ref,
               *, nchunk, nbatch):
    u32 = u_ref[0]
    u = u32.astype(jnp.bfloat16)
    xr_ref[...] = jnp.dot(u, wsr_ref[0], preferred_element_type=jnp.float32)
    xi_ref[...] = jnp.dot(u, wsi_ref[0], preferred_element_type=jnp.float32)
    ar = acs_ref[0, 0:1, :]
    ai = acs_ref[0, 1:2, :]

    def step(c, carry):
        new = []
        for b in range(nbatch):
            sr, si = carry[2 * b], carry[2 * b + 1]
            row = pl.ds(b * nchunk + c, 1)
            lr, li = xr_ref[row, :], xi_ref[row, :]
            xr_ref[row, :] = sr
            xi_ref[row, :] = si
            new += [ar * sr - ai * si + lr, ar * si + ai * sr + li]
        return tuple(new)

    zero = jnp.zeros((1, LANES), jnp.float32)
    lax.fori_loop(0, nchunk, step, (zero,) * (2 * nbatch))
    y = jnp.dot(u, tz_ref[0], preferred_element_type=jnp.float32)
    y = y + jnp.dot(xr_ref[...].astype(jnp.bfloat16), wor_ref[0], preferred_element_type=jnp.float32)
    y = y + jnp.dot(xi_ref[...].astype(jnp.bfloat16), woi_ref[0], preferred_element_type=jnp.float32)
    y = y + d_ref[0] * u32
    y_ref[0] = jax.nn.gelu(y)


def _s5_scan(ug, tables, bsz, nchunk):
    G, R, W = ug.shape
    tz, wsr, wsi, wor, woi, a_cs, d_exp = tables
    gmap = lambda g: (g, 0, 0)
    return pl.pallas_call(
        partial(_s5_kernel, nchunk=nchunk, nbatch=bsz),
        grid=(G,),
        in_specs=[pl.BlockSpec((1, R, W), gmap), pl.BlockSpec((1, W, W), gmap),
                  pl.BlockSpec((1, W, LANES), gmap), pl.BlockSpec((1, W, LANES), gmap),
                  pl.BlockSpec((1, LANES, W), gmap), pl.BlockSpec((1, LANES, W), gmap),
                  pl.BlockSpec((1, 2, LANES), gmap), pl.BlockSpec((1, 1, W), gmap)],
        out_specs=pl.BlockSpec((1, R, W), gmap),
        out_shape=jax.ShapeDtypeStruct((G, R, W), jnp.float32),
        scratch_shapes=[pltpu.VMEM((R, LANES), jnp.float32), pltpu.VMEM((R, LANES), jnp.float32)],
        compiler_params=_params("parallel"),
    )(ug, tz, wsr, wsi, wor, woi, a_cs, d_exp)


def _glu_kernel(y_ref, w_ref, b_ref, o_ref):
    y = y_ref[...]
    gate = jnp.dot(y.astype(jnp.bfloat16), w_ref[...], preferred_element_type=jnp.float32) + b_ref[...]
    o_ref[...] = (y * jax.nn.sigmoid(gate)).astype(o_ref.dtype)


def _glu(y, w_glu, b_glu, tm=1024):
    T, W = y.shape
    tm = min(tm, T)
    return pl.pallas_call(
        _glu_kernel,
        grid=(T // tm,),
        in_specs=[pl.BlockSpec((tm, W), lambda i: (i, 0)),
                  pl.BlockSpec((W, W), lambda i: (0, 0)),
                  pl.BlockSpec((1, W), lambda i: (0, 0))],
        out_specs=pl.BlockSpec((tm, W), lambda i: (i, 0)),
        out_shape=jax.ShapeDtypeStruct((T, W), jnp.bfloat16),
        compiler_params=_params("parallel"),
    )(y, w_glu.astype(jnp.bfloat16), b_glu.reshape(1, W))


def _s5(hf, s5_params, w_glu, b_glu, bsz, L):
    T = bsz * L
    Cs, G, N = S5_CHUNK, S5_GROUPS, S5_GROUP
    nchunk = L // Cs
    u = hf[:, EV_U:EV_U + S5_WIDTH].reshape(bsz * nchunk, Cs, G, N)
    ug = jnp.transpose(u, (2, 0, 1, 3)).reshape(G, bsz * nchunk, Cs * N)
    y = _s5_scan(ug, _s5_tables(*s5_params), bsz, nchunk)
    y = jnp.transpose(y.reshape(G, bsz * nchunk, Cs, N), (1, 2, 0, 3)).reshape(T, S5_WIDTH)
    return _glu(y, w_glu, b_glu)


def _even_weights(w_in):
    D = w_in.shape[0]
    q, k, v, r, g_lr, u = _split(w_in, EV_SIZES)
    pad = jnp.zeros((D, EV_COLS_PADDED - EV_G - GLA_GATE_RANK), w_in.dtype)
    return jnp.concatenate([q * GLA_DK ** -0.5, k, v, r, u, g_lr, pad], axis=1).astype(jnp.bfloat16)


def _even_layer_mix(x2, w_in, w_gate2, b_gate2, norm_g, s5_params, w_glu, b_glu, bsz, L):
    hf = _matmul(x2, _even_weights(w_in), jnp.float32, tm=1024, tn=512)
    o_a = _gla(hf, w_gate2, b_gate2, norm_g, bsz, L)
    o_b = _s5(hf, s5_params, w_glu, b_glu, bsz, L)
    return o_a, o_b


MOE_TILE = 512
ROW_TILE = 8
DMA_UNROLL = 8
MOE_VMEM_LIMIT_BYTES = 56 * 1024 * 1024


def _router_kernel(x_ref, w_ref, b_ref, e_ref, g_ref):
    logits = jnp.dot(x_ref[...].astype(jnp.bfloat16), w_ref[...], preferred_element_type=jnp.float32) + b_ref[...]
    tm = logits.shape[0]
    lane = lax.broadcasted_iota(jnp.int32, (tm, LANES), 1)
    logits = jnp.where(lane < N_EXPERTS, logits, NEG_BIG)
    tops, idxs = [], []
    for _ in range(TOP_K):
        m = jnp.max(logits, axis=1, keepdims=True)
        idx = jnp.min(jnp.where(logits == m, lane, LANES), axis=1, keepdims=True)
        tops.append(m)
        idxs.append(idx)
        logits = jnp.where(lane == idx, NEG_BIG, logits)
    exps = [jnp.exp(t - tops[0]) for t in tops]
    denom = exps[0]
    for e in exps[1:]:
        denom = denom + e
    lane4 = lax.broadcasted_iota(jnp.int32, (tm, TOP_K), 1)
    e_out = jnp.zeros((tm, TOP_K), jnp.int32)
    g_out = jnp.zeros((tm, TOP_K), jnp.float32)
    for k in range(TOP_K):
        e_out = jnp.where(lane4 == k, idxs[k], e_out)
        g_out = jnp.where(lane4 == k, exps[k] / denom, g_out)
    e_ref[...] = e_out
    g_ref[...] = g_out


def _router(x, router_w, router_b, tm=512):
    T, D = x.shape
    tm = min(tm, T)
    w = jnp.pad(router_w, ((0, 0), (0, LANES - N_EXPERTS))).astype(jnp.bfloat16)
    b = jnp.pad(router_b, (0, LANES - N_EXPERTS)).reshape(1, LANES)
    return pl.pallas_call(
        _router_kernel,
        grid=(T // tm,),
        in_specs=[pl.BlockSpec((tm, D), lambda i: (i, 0)),
                  pl.BlockSpec((D, LANES), lambda i: (0, 0)),
                  pl.BlockSpec((1, LANES), lambda i: (0, 0))],
        out_specs=[pl.BlockSpec((tm, TOP_K), lambda i: (i, 0)), pl.BlockSpec((tm, TOP_K), lambda i: (i, 0))],
        out_shape=[jax.ShapeDtypeStruct((T, TOP_K), jnp.int32), jax.ShapeDtypeStruct((T, TOP_K), jnp.float32)],
        compiler_params=_params("parallel"),
    )(x, w, b)


def _moe_rank_kernel(e_ref, rank_ref, count_ref, carry_ref):
    @pl.when(pl.program_id(0) == 0)
    def _():
        carry_ref[...] = jnp.zeros(carry_ref.shape, jnp.float32)

    e = e_ref[...]
    tm = e.shape[0]
    lane = lax.broadcasted_iota(jnp.int32, (tm, LANES), 1)
    onehot = jnp.zeros((tm, LANES), jnp.float32)
    for k in range(TOP_K):
        onehot = onehot + jnp.where(lane == e[:, k:k + 1], 1.0, 0.0)
    ri = lax.broadcasted_iota(jnp.int32, (tm, tm), 0)
    ci = lax.broadcasted_iota(jnp.int32, (tm, tm), 1)
    strict_lower = jnp.where(ci < ri, 1.0, 0.0).astype(jnp.bfloat16)
    before = jnp.dot(strict_lower, onehot.astype(jnp.bfloat16), preferred_element_type=jnp.float32) + carry_ref[...]
    lane4 = lax.broadcasted_iota(jnp.int32, (tm, TOP_K), 1)
    rank = jnp.zeros((tm, TOP_K), jnp.int32)
    for k in range(TOP_K):
        r_k = jnp.sum(jnp.where(lane == e[:, k:k + 1], before, 0.0), axis=1, keepdims=True).astype(jnp.int32)
        rank = jnp.where(lane4 == k, r_k, rank)
    rank_ref[...] = rank
    carry_ref[...] = carry_ref[...] + jnp.sum(onehot, axis=0, keepdims=True)
    count_ref[...] = carry_ref[...]


def _moe_rank(top_e, tm=256):
    T = top_e.shape[0]
    tm = min(tm, T)
    return pl.pallas_call(
        _moe_rank_kernel,
        grid=(T // tm,),
        in_specs=[pl.BlockSpec((tm, TOP_K), lambda i: (i, 0))],
        out_specs=[pl.BlockSpec((tm, TOP_K), lambda i: (i, 0)), pl.BlockSpec((1, LANES), lambda i: (0, 0))],
        out_shape=[jax.ShapeDtypeStruct((T, TOP_K), jnp.int32), jax.ShapeDtypeStruct((1, LANES), jnp.float32)],
        scratch_shapes=[pltpu.VMEM((1, LANES), jnp.float32)],
        compiler_params=_params("arbitrary"),
    )(top_e)


def _to_token_tiles(x, dst_ref, rows):
    for c in range(ROW_TILE):
        dst_ref[pl.ds(c, rows, stride=ROW_TILE), :] = x[:, LANES * c:LANES * (c + 1)]


def _from_token_tiles(src_ref, rows):
    return jnp.concatenate([src_ref[pl.ds(c, rows, stride=ROW_TILE), :] for c in range(ROW_TILE)], axis=1)


def _moe_dispatch_kernel(dest_ref, x_ref, init_ref, xbuf_ref, xs_ref, sem, *, tm):
    del init_ref
    i = pl.program_id(0)
    n = pl.num_programs(0)
    slot = i % 2

    def row_copy(s, r, d):
        return pltpu.make_async_copy(xs_ref.at[s, pl.ds(pl.multiple_of(r * ROW_TILE, ROW_TILE), ROW_TILE), :],
                                     xbuf_ref.at[pl.ds(pl.multiple_of(d * ROW_TILE, ROW_TILE), ROW_TILE), :], sem.at[s])

    def drain(s):
        def body(a, c):
            row_copy(s, 0, 0).wait()
            return c
        lax.fori_loop(0, tm * TOP_K, body, 0, unroll=DMA_UNROLL)

    @pl.when(i >= 2)
    def _():
        drain(slot)

    _to_token_tiles(x_ref[...], xs_ref.at[slot], tm)

    def body(r, c):
        for k in range(TOP_K):
            row_copy(slot, r, dest_ref[r * TOP_K + k]).start()
        return c

    lax.fori_loop(0, tm, body, 0, unroll=DMA_UNROLL)

    @pl.when(i == n - 1)
    def _():
        drain(slot)

        @pl.when(n > 1)
        def _():
            drain(1 - slot)


def _moe_dispatch(x, dest_flat, n_rows, tm=256):
    T, D = x.shape
    tm = min(tm, T)
    assert D == ROW_TILE * LANES
    init = jnp.zeros((n_rows * ROW_TILE, LANES), jnp.float32)
    return pl.pallas_call(
        partial(_moe_dispatch_kernel, tm=tm),
        grid=(T // tm,),
        in_specs=[pl.BlockSpec((tm * TOP_K,), lambda i: (i,), memory_space=pltpu.SMEM),
                  pl.BlockSpec((tm, D), lambda i: (i, 0)),
                  pl.BlockSpec(memory_space=pl.ANY)],
        out_specs=pl.BlockSpec(memory_space=pl.ANY),
        out_shape=jax.ShapeDtypeStruct((n_rows * ROW_TILE, LANES), jnp.float32),
        scratch_shapes=[pltpu.VMEM((2, tm * ROW_TILE, LANES), jnp.float32), pltpu.SemaphoreType.DMA((2,))],
        input_output_aliases={2: 0},
        compiler_params=_params("arbitrary"),
    )(dest_flat, x, init)


def _moe_ffn_kernel(te_ref, nt_ref, x_ref, w1_ref, b1_ref, w2_ref, b2_ref, y_ref, w1b_ref, w2b_ref):
    i = pl.program_id(0)
    prev = te_ref[jnp.maximum(i - 1, 0)]

    @pl.when(jnp.logical_or(i == 0, te_ref[i] != prev))
    def _():
        w1b_ref[...] = w1_ref[0].astype(jnp.bfloat16)
        w2b_ref[...] = w2_ref[0].astype(jnp.bfloat16)

    @pl.when(i < nt_ref[0])
    def _():
        x = _from_token_tiles(x_ref, MOE_TILE).astype(jnp.bfloat16)
        h = jnp.dot(x, w1b_ref[...], preferred_element_type=jnp.float32) + b1_ref[0]
        glu = jnp.minimum(h[:, :D_FF], SWIGLU_LIMIT)
        lin = jnp.clip(h[:, D_FF:], -SWIGLU_LIMIT, SWIGLU_LIMIT)
        act = glu * jax.nn.sigmoid(SWIGLU_ALPHA * glu) * (lin + 1.0)
        y = jnp.dot(act.astype(jnp.bfloat16), w2b_ref[...], preferred_element_type=jnp.float32) + b2_ref[0]
        _to_token_tiles(y, y_ref, MOE_TILE)

    @pl.when(i >= nt_ref[0])
    def _():
        y_ref[...] = jnp.zeros(y_ref.shape, y_ref.dtype)


def _moe_expert_ffn(xbuf, tile_expert, n_used, w1, b1, w2, b2):
    D, F2 = w1.shape[1], w1.shape[2]
    P = xbuf.shape[0] // ROW_TILE
    n_tiles = P // MOE_TILE
    blk = (MOE_TILE * ROW_TILE, LANES)
    grid_spec = pltpu.PrefetchScalarGridSpec(
        num_scalar_prefetch=2,
        grid=(n_tiles,),
        in_specs=[pl.BlockSpec(blk, lambda i, te, nt: (jnp.minimum(i, nt[0] - 1), 0)),
                  pl.BlockSpec((1, D, F2), lambda i, te, nt: (te[i], 0, 0)),
                  pl.BlockSpec((1, 1, F2), lambda i, te, nt: (te[i], 0, 0)),
                  pl.BlockSpec((1, D_FF, D), lambda i, te, nt: (te[i], 0, 0)),
                  pl.BlockSpec((1, 1, D), lambda i, te, nt: (te[i], 0, 0))],
        out_specs=pl.BlockSpec(blk, lambda i, te, nt: (i, 0)),
        scratch_shapes=[pltpu.VMEM((D, F2), jnp.bfloat16), pltpu.VMEM((D_FF, D), jnp.bfloat16)],
    )
    return pl.pallas_call(
        _moe_ffn_kernel,
        grid_spec=grid_spec,
        out_shape=jax.ShapeDtypeStruct(xbuf.shape, jnp.float32),
        compiler_params=pltpu.CompilerParams(dimension_semantics=("arbitrary",),
                                             vmem_limit_bytes=MOE_VMEM_LIMIT_BYTES),
    )(tile_expert, n_used, xbuf, w1, b1.reshape(N_EXPERTS, 1, F2), w2, b2.reshape(N_EXPERTS, 1, D))


def _moe_combine_kernel(dest_ref, gate_ref, x_ref, g_ref, b_ref, ybuf_ref, o_ref, rows_ref, sem, *, tm):
    def row_copy(k, r, d):
        return pltpu.make_async_copy(ybuf_ref.at[pl.ds(pl.multiple_of(d * ROW_TILE, ROW_TILE), ROW_TILE), :],
                                     rows_ref.at[k, pl.ds(pl.multiple_of(r * ROW_TILE, ROW_TILE), ROW_TILE), :], sem.at[0])

    def start(r, c):
        for k in range(TOP_K):
            row_copy(k, r, dest_ref[r * TOP_K + k]).start()
        return c

    lax.fori_loop(0, tm, start, 0, unroll=DMA_UNROLL)

    def wait(a, c):
        row_copy(0, 0, 0).wait()
        return c

    lax.fori_loop(0, tm * TOP_K, wait, 0, unroll=DMA_UNROLL)
    gate = gate_ref[...]
    ffn = gate[:, 0:1] * _from_token_tiles(rows_ref.at[0], tm)
    for k in range(1, TOP_K):
        ffn = ffn + gate[:, k:k + 1] * _from_token_tiles(rows_ref.at[k], tm)
    o_ref[...] = _layer_norm_rows(DEEPNORM_ALPHA * x_ref[...] + ffn, g_ref[...], b_ref[...])


def _moe_combine(ybuf, dest_flat, gate, x, g, b, tm=256):
    T, D = x.shape
    tm = min(tm, T)
    return pl.pallas_call(
        partial(_moe_combine_kernel, tm=tm),
        grid=(T // tm,),
        in_specs=[pl.BlockSpec((tm * TOP_K,), lambda i: (i,), memory_space=pltpu.SMEM),
                  pl.BlockSpec((tm, TOP_K), lambda i: (i, 0)),
                  pl.BlockSpec((tm, D), lambda i: (i, 0)),
                  pl.BlockSpec((1, D), lambda i: (0, 0)),
                  pl.BlockSpec((1, D), lambda i: (0, 0)),
                  pl.BlockSpec(memory_space=pl.ANY)],
        out_specs=pl.BlockSpec((tm, D), lambda i: (i, 0)),
        out_shape=jax.ShapeDtypeStruct((T, D), jnp.float32),
        scratch_shapes=[pltpu.VMEM((TOP_K, tm * ROW_TILE, LANES), jnp.float32), pltpu.SemaphoreType.DMA((1,))],
        compiler_params=_params("arbitrary"),
    )(dest_flat, gate, x, g.reshape(1, D), b.reshape(1, D), ybuf)


def _moe_layer(x, router_w, router_b, w1, b1, w2, b2, ln_g, ln_b):
    T, D = x.shape
    A = T * TOP_K
    n_tiles = -(-(A + N_EXPERTS * (MOE_TILE - 1)) // MOE_TILE)
    top_e, gate = _router(x, router_w, router_b)
    rank, counts = _moe_rank(top_e)
    counts = counts[0, :N_EXPERTS].astype(jnp.int32)
    padded = (counts + MOE_TILE - 1) // MOE_TILE * MOE_TILE
    pend = jnp.cumsum(padded)
    pstart = pend - padded
    dest = (pstart[top_e] + rank).reshape(A)
    tile_expert = jnp.minimum(jnp.searchsorted(pend, jnp.arange(n_tiles) * MOE_TILE, side='right'),
                              N_EXPERTS - 1).astype(jnp.int32)
    n_used = (pend[-1:] // MOE_TILE).astype(jnp.int32)
    xbuf = _moe_dispatch(x, dest, n_tiles * MOE_TILE)
    ybuf = _moe_expert_ffn(xbuf, tile_expert, n_used, w1, b1, w2, b2)
    return _moe_combine(ybuf, dest, gate, x, ln_g, ln_b)


def kernel(x, ev_w_in, gla_w_gate2, gla_b_gate2, gla_norm_g, s5_a_re, s5_a_im, s5_log_dt, s5_b_re, s5_b_im,
           s5_c_re, s5_c_im, s5_d, s5_w_glu, s5_b_glu, ev_w_out, od_w_in, od_w_out, ln1_g, ln1_b, ln2_g, ln2_b,
           router_w, router_b, moe_w1, moe_b1, moe_w2, moe_b2):
    bsz, L, D = x.shape
    T = bsz * L
    x = x.reshape(T, D)
    for layer in range(DEPTH):
        j = layer // 2
        if layer % 2 == 0:
            s5_params = (s5_a_re[j], s5_a_im[j], s5_log_dt[j], s5_b_re[j], s5_b_im[j], s5_c_re[j], s5_c_im[j], s5_d[j])
            o_1, o_2 = _even_layer_mix(x, ev_w_in[j], gla_w_gate2[j], gla_b_gate2[j], gla_norm_g[j], s5_params,
                                       s5_w_glu[j], s5_b_glu[j], bsz, L)
            w_out = ev_w_out[j]
        else:
            o_1, o_2 = _odd_layer_mix(x, od_w_in[j], bsz, L)
            w_out = od_w_out[j]
        x = _matmul2_res_ln(o_1, o_2, w_out, x, ln1_g[layer], ln1_b[layer])
        x = _moe_layer(x, router_w[layer], router_b[layer], moe_w1[layer], moe_b1[layer], moe_w2[layer],
                       moe_b2[layer], ln2_g[layer], ln2_b[layer])
    return x.reshape(bsz, L, D)
```

```python
import math
from functools import partial

import numpy as np
import jax
import jax.numpy as jnp
from jax import lax
from jax.experimental import pallas as pl
from jax.experimental.pallas import tpu as pltpu

D_MODEL = 1024
DEPTH = 4
DEEPNORM_ALPHA = (2.0 * DEPTH) ** 0.25
LN_EPS = 1e-5
MIX_WIDTH = D_MODEL

GLA_HEADS = 4
GLA_DV = MIX_WIDTH // 2 // GLA_HEADS
GLA_DK = GLA_DV // 2
GLA_GATE_RANK = 16
GLA_GATE_TAU = 16.0
GLA_CHUNK = 64

S5_WIDTH = MIX_WIDTH // 2
S5_GROUP = 16
S5_GROUPS = S5_WIDTH // S5_GROUP
S5_STATE = 64
S5_MAX_RE = -1e-4

EV_SIZES = (GLA_HEADS * GLA_DK, GLA_HEADS * GLA_DK, GLA_HEADS * GLA_DV, GLA_HEADS * GLA_DV, GLA_GATE_RANK, S5_WIDTH)

DSA_HEADS = 8
DSA_HEAD_DIM = 64
DSA_WIDTH = DSA_HEADS * DSA_HEAD_DIM
IDX_HEADS = 4
IDX_DIM = 64
DSA_TOPK_MAX = 256

DIL_PATTERNS = ((128, 1), (512, 4), (2048, 16))
DIL_GROUPS = len(DIL_PATTERNS)
DIL_HEADS = 8
DIL_HEAD_DIM = 64
DIL_WIDTH = DIL_HEADS * DIL_HEAD_DIM

OD_SIZES = (DSA_WIDTH, DSA_WIDTH, DSA_WIDTH, IDX_HEADS * IDX_DIM, IDX_DIM, IDX_HEADS, 3 * DIL_GROUPS * DIL_WIDTH)

N_EXPERTS = 32
TOP_K = 4
D_FF = D_MODEL
SWIGLU_ALPHA = 1.702
SWIGLU_LIMIT = 7.0
MOE_BLOCK = 512

LANES = 128
VMEM_LIMIT_BYTES = 48 * 1024 * 1024
NEG_BIG = -1e30
INT_MIN = -2 ** 31
INT_MAX = 2 ** 31 - 1
KEY_NEG_INF = -0x7F800000

OD_Q, OD_K, OD_V, OD_QI, OD_DIL, OD_KI = 0, 512, 1024, 1536, 2048, 2048 + 9 * 512
OD_COLS_PADDED = 7168


def _split(h, sizes):
    return jnp.split(h, [int(i) for i in np.cumsum(sizes)[:-1]], axis=-1)


def _params(*sem):
    return pltpu.CompilerParams(dimension_semantics=sem, vmem_limit_bytes=VMEM_LIMIT_BYTES)


def _mm_kernel(x_ref, w_ref, o_ref):
    o_ref[...] = jnp.dot(x_ref[...].astype(jnp.bfloat16), w_ref[...].astype(jnp.bfloat16),
                         preferred_element_type=jnp.float32).astype(o_ref.dtype)


def _matmul(x, w, out_dtype=jnp.float32, tm=512, tn=512):
    T, K = x.shape
    N = w.shape[1]
    tm, tn = min(tm, T), min(tn, N)
    assert T % tm == 0 and N % tn == 0
    return pl.pallas_call(
        _mm_kernel,
        grid=(T // tm, N // tn),
        in_specs=[pl.BlockSpec((tm, K), lambda i, j: (i, 0)),
                  pl.BlockSpec((K, tn), lambda i, j: (0, j))],
        out_specs=pl.BlockSpec((tm, tn), lambda i, j: (i, j)),
        out_shape=jax.ShapeDtypeStruct((T, N), out_dtype),
        compiler_params=_params("parallel", "arbitrary"),
    )(x, w)


def _layer_norm_rows(z, g, b):
    mu = jnp.mean(z, axis=-1, keepdims=True)
    zc = z - mu
    var = jnp.mean(zc * zc, axis=-1, keepdims=True)
    return zc * lax.rsqrt(var + LN_EPS) * g + b


def _mm_res_ln_kernel(a_ref, w_ref, x_ref, g_ref, b_ref, o_ref):
    mix = jnp.dot(a_ref[...].astype(jnp.bfloat16), w_ref[...].astype(jnp.bfloat16),
                  preferred_element_type=jnp.float32)
    o_ref[...] = _layer_norm_rows(DEEPNORM_ALPHA * x_ref[...] + mix, g_ref[...], b_ref[...])


def _matmul_res_ln(a, w, x, g, b, tm=512):
    T, K = a.shape
    D = w.shape[1]
    tm = min(tm, T)
    return pl.pallas_call(
        _mm_res_ln_kernel,
        grid=(T // tm,),
        in_specs=[pl.BlockSpec((tm, K), lambda i: (i, 0)),
                  pl.BlockSpec((K, D), lambda i: (0, 0)),
                  pl.BlockSpec((tm, D), lambda i: (i, 0)),
                  pl.BlockSpec((1, D), lambda i: (0, 0)),
                  pl.BlockSpec((1, D), lambda i: (0, 0))],
        out_specs=pl.BlockSpec((tm, D), lambda i: (i, 0)),
        out_shape=jax.ShapeDtypeStruct((T, D), jnp.float32),
        compiler_params=_params("parallel"),
    )(a, w, x, g.reshape(1, D), b.reshape(1, D))


def _mm2_res_ln_kernel(a1_ref, a2_ref, w1_ref, w2_ref, x_ref, g_ref, b_ref, o_ref):
    mix = jnp.dot(a1_ref[...], w1_ref[...], preferred_element_type=jnp.float32)
    mix += jnp.dot(a2_ref[...], w2_ref[...], preferred_element_type=jnp.float32)
    o_ref[...] = _layer_norm_rows(DEEPNORM_ALPHA * x_ref[...] + mix, g_ref[...], b_ref[...])


def _matmul2_res_ln(a1, a2, w, x, g, b, tm=512):
    T, K1 = a1.shape
    D = w.shape[1]
    tm = min(tm, T)
    wb = w.astype(jnp.bfloat16)
    return pl.pallas_call(
        _mm2_res_ln_kernel,
        grid=(T // tm,),
        in_specs=[pl.BlockSpec((tm, K1), lambda i: (i, 0)),
                  pl.BlockSpec((tm, a2.shape[1]), lambda i: (i, 0)),
                  pl.BlockSpec((K1, D), lambda i: (0, 0)),
                  pl.BlockSpec((a2.shape[1], D), lambda i: (1, 0)),
                  pl.BlockSpec((tm, D), lambda i: (i, 0)),
                  pl.BlockSpec((1, D), lambda i: (0, 0)),
                  pl.BlockSpec((1, D), lambda i: (0, 0))],
        out_specs=pl.BlockSpec((tm, D), lambda i: (i, 0)),
        out_shape=jax.ShapeDtypeStruct((T, D), jnp.float32),
        compiler_params=_params("parallel"),
    )(a1, a2, wb, wb, x, g.reshape(1, D), b.reshape(1, D))


def _res_ln_kernel(y_ref, x_ref, g_ref, b_ref, o_ref):
    o_ref[...] = _layer_norm_rows(DEEPNORM_ALPHA * x_ref[...] + y_ref[...], g_ref[...], b_ref[...])


def _res_ln(y, x, g, b, tm=512):
    T, D = x.shape
    tm = min(tm, T)
    return pl.pallas_call(
        _res_ln_kernel,
        grid=(T // tm,),
        in_specs=[pl.BlockSpec((tm, D), lambda i: (i, 0)),
                  pl.BlockSpec((tm, D), lambda i: (i, 0)),
                  pl.BlockSpec((1, D), lambda i: (0, 0)),
                  pl.BlockSpec((1, D), lambda i: (0, 0))],
        out_specs=pl.BlockSpec((tm, D), lambda i: (i, 0)),
        out_shape=jax.ShapeDtypeStruct((T, D), jnp.float32),
        compiler_params=_params("parallel"),
    )(y, x, g.reshape(1, D), b.reshape(1, D))


DSA_TQ = 256
DSA_TK = 512
COUNT_ROWS = 64
TIE_CHECK_PASS = 8


def _stack_index_heads(qi_blk):
    return jnp.concatenate([qi_blk[:, LANES * h:LANES * (h + 1)] for h in range(IDX_HEADS)], axis=0)


def _index_keys_t(ki_blk, qi_all, wi_t, q0, k0, tq, causal_mask=True):
    tk = ki_blk.shape[0]
    d = lax.dot_general(ki_blk, qi_all, (((1,), (1,)), ((), ())), preferred_element_type=jnp.float32)
    sc = wi_t[0:1, :] * jnp.maximum(d[:, 0:tq], 0.0)
    for h in range(1, IDX_HEADS):
        sc = sc + wi_t[h:h + 1, :] * jnp.maximum(d[:, h * tq:(h + 1) * tq], 0.0)
    if causal_mask:
        kpos = k0 + lax.broadcasted_iota(jnp.int32, (tk, 1), 0)
        qpos = q0 + lax.broadcasted_iota(jnp.int32, (1, tq), 1)
        sc = jnp.where(kpos <= qpos, sc, -jnp.inf)
    bits = lax.bitcast_convert_type(sc, jnp.int32)
    return jnp.where(bits < 0, INT_MIN - bits, bits)


def _dsa_select_kernel(qi_ref, wit_ref, ki_ref, thr_ref, cut_ref, key_ref, *, tq, topk):
    i = pl.program_id(1)
    nblk = i + 1
    q0 = i * tq
    qi_all = _stack_index_heads(qi_ref[...])
    wi_t = wit_ref[...]

    def fill(j, carry):
        k0 = pl.multiple_of(j * tq, tq)
        key_ref[j] = _index_keys_t(ki_ref[pl.ds(k0, tq), :], qi_all, wi_t, q0, k0, tq, causal_mask=False)
        return carry

    lax.fori_loop(0, i, fill, 0)
    key_ref[i] = _index_keys_t(ki_ref[pl.ds(pl.multiple_of(q0, tq), tq), :], qi_all, wi_t, q0, q0, tq)

    kr = COUNT_ROWS
    row = lax.broadcasted_iota(jnp.int32, (kr, 1), 0)

    def count(pred):
        def body(j, acc):
            for s in range(tq // kr):
                kk = key_ref[j, s * kr:(s + 1) * kr, :]
                acc = acc + jnp.where(pred(kk, j * tq + s * kr), 1, 0)
            return acc
        acc = lax.fori_loop(0, nblk, body, jnp.zeros((kr, tq), jnp.int32))
        return jnp.sum(acc.astype(jnp.float32), axis=0, keepdims=True).astype(jnp.int32)

    def any_true(mask):
        return jnp.max(jnp.where(mask, 1.0, 0.0)) > 0.5

    def bit_cond(st):
        return jnp.logical_and(st[0] < 32, st[4])

    def bit_step(st):
        p, thr, cnt, final, _ = st
        cand = thr ^ lax.shift_left(jnp.int32(1), 31 - p)
        c = count(lambda kk, base: kk >= cand)
        take = c >= topk
        thr = jnp.where(take, cand, thr)
        cnt = jnp.where(take, c, cnt)
        final = lax.cond(p == TIE_CHECK_PASS,
                         lambda: jnp.where(cnt - count(lambda kk, base: kk == thr) < topk, 1, 0),
                         lambda: final)
        return p + 1, thr, cnt, final, any_true(jnp.logical_and(cnt != topk, final == 0))

    ncols = nblk * tq
    init = (jnp.int32(0), jnp.full((1, tq), INT_MIN, jnp.int32), jnp.full((1, tq), ncols, jnp.int32),
            jnp.zeros((1, tq), jnp.int32), ncols != topk)
    _, thr, cnt, _, _ = lax.while_loop(bit_cond, bit_step, init)
    tie = jnp.logical_and(cnt > topk, thr > KEY_NEG_INF)

    def resolve_ties():
        need = (topk - count(lambda kk, base: kk > thr)).astype(jnp.float32)
        ri = lax.broadcasted_iota(jnp.int32, (kr, kr), 0)
        ci = lax.broadcasted_iota(jnp.int32, (kr, kr), 1)
        lower = jnp.where(ci <= ri, 1.0, 0.0).astype(jnp.bfloat16)
        rowf = row.astype(jnp.float32)

        def body(j, st):
            carry, cut_acc = st
            eqs = [key_ref[j, s * kr:(s + 1) * kr, :] == thr for s in range(tq // kr)]
            within = [jnp.dot(lower, jnp.where(eq, 1.0, 0.0).astype(jnp.bfloat16), preferred_element_type=jnp.float32)
                      for eq in eqs]
            for s, (eq, pref) in enumerate(zip(eqs, within)):
                end_pos = rowf + (j * tq + s * kr + 1).astype(jnp.float32)
                cut_acc = jnp.maximum(cut_acc, jnp.where(jnp.logical_and(eq, pref + carry == need), end_pos, 0.0))
                carry = carry + pref[kr - 1:kr, :]
            return carry, cut_acc

        _, cut_acc = lax.fori_loop(0, nblk, body, (jnp.zeros((1, tq), jnp.float32), jnp.zeros((kr, tq), jnp.float32)))
        return jnp.where(tie, jnp.max(cut_acc, axis=0, keepdims=True).astype(jnp.int32), INT_MAX)

    cut = lax.cond(any_true(tie), resolve_ties, lambda: jnp.full((1, tq), INT_MAX, jnp.int32))
    thr_ref[...] = jnp.maximum(thr, KEY_NEG_INF + 1)
    cut_ref[...] = cut


def _dsa_select(hb, wi_t, bsz, L, tq):
    T = bsz * L
    nq = L // tq
    topk = min(DSA_TOPK_MAX, L // 4)
    qmap = lambda b, i: (0, b * nq + i)
    return pl.pallas_call(
        partial(_dsa_select_kernel, tq=tq, topk=topk),
        grid=(bsz, nq),
        in_specs=[pl.BlockSpec((tq, 512), lambda b, i: (b * nq + i, OD_QI // 512)),
                  pl.BlockSpec((8, tq), qmap),
                  pl.BlockSpec((L, LANES), lambda b, i: (b, OD_KI // LANES))],
        out_specs=[pl.BlockSpec((1, tq), qmap), pl.BlockSpec((1, tq), qmap)],
        out_shape=[jax.ShapeDtypeStruct((1, T), jnp.int32), jax.ShapeDtypeStruct((1, T), jnp.int32)],
        scratch_shapes=[pltpu.VMEM((nq, tq, tq), jnp.int32)],
        compiler_params=_params("parallel", "arbitrary"),
    )(hb, wi_t, hb)


def _split_head_pair(x_pair):
    lane = lax.broadcasted_iota(jnp.int32, x_pair.shape, 1)
    zero = jnp.zeros_like(x_pair)
    return jnp.where(lane < 64, x_pair, zero), jnp.where(lane >= 64, x_pair, zero)


def _dsa_attn_kernel(qidx_ref, kidx_ref, q_ref, qi_ref, wit_ref, thr_ref, cut_ref, k_ref, vt_ref, ki_ref, o_ref,
                     qm_ref, qiall_ref, m_ref, acc_ref, *, tq, tk):
    i = qidx_ref[pl.program_id(1)]
    j = kidx_ref[pl.program_id(1)]
    npair = DSA_HEADS // 2

    @pl.when(j == 0)
    def _():
        for p in range(npair):
            qa, qb = _split_head_pair(q_ref[:, LANES * p:LANES * (p + 1)])
            qm_ref[2 * p] = qa
            qm_ref[2 * p + 1] = qb
        qiall_ref[...] = _stack_index_heads(qi_ref[...])
        m_ref[...] = jnp.full(m_ref.shape, NEG_BIG, jnp.float32)
        acc_ref[...] = jnp.zeros(acc_ref.shape, jnp.float32)

    key = _index_keys_t(ki_ref[...], qiall_ref[...], wit_ref[...], i * tq, j * tk, tq)
    kpos = j * tk + lax.broadcasted_iota(jnp.int32, (tk, 1), 0)
    thr = thr_ref[...]
    sel = jnp.logical_or(key > thr, jnp.logical_and(key == thr, kpos < cut_ref[...]))
    bias = jnp.where(sel, 0.0, NEG_BIG)
    vrow = lax.broadcasted_iota(jnp.int32, (LANES, tk), 0)
    ones = jnp.ones((LANES, tk), jnp.bfloat16)
    scores = []
    for h in range(DSA_HEADS):
        kp = k_ref[:, LANES * (h // 2):LANES * (h // 2 + 1)]
        s = lax.dot_general(kp, qm_ref[h], (((1,), (1,)), ((), ())), preferred_element_type=jnp.float32)
        scores.append(s + bias)
    for h in range(DSA_HEADS):
        vt = vt_ref[LANES * (h // 2):LANES * (h // 2 + 1), :]
        v_aug = jnp.where((vrow < 64) if h % 2 == 0 else (vrow >= 64), vt, ones)
        s = scores[h]
        m_prev = m_ref[h:h + 1, :]
        m_new = jnp.maximum(m_prev, jnp.max(s, axis=0, keepdims=True))
        alpha = jnp.exp(m_prev - m_new)
        e = jnp.exp(s - m_new).astype(jnp.bfloat16)
        acc_ref[h] = alpha * acc_ref[h] + jnp.dot(v_aug, e, preferred_element_type=jnp.float32)
        m_ref[h:h + 1, :] = m_new

    @pl.when(j == ((i + 1) * tq - 1) // tk)
    def _():
        rowi = lax.broadcasted_iota(jnp.int32, (LANES, tq), 0)
        for p in range(npair):
            a = acc_ref[2 * p]
            b = acc_ref[2 * p + 1]
            out_t = jnp.where(rowi < 64, a / a[64:65, :], b / b[0:1, :])
            o_ref[:, LANES * p:LANES * (p + 1)] = out_t.T.astype(o_ref.dtype)


def _dsa_attention(hb, v_t, wi_t, thr, cut, bsz, L, tq, tk):
    T = bsz * L
    nq, nk = L // tq, L // tk
    pairs = [(i, j) for i in range(nq) for j in range(((i + 1) * tq - 1) // tk + 1)]
    qidx = jnp.asarray(np.array([p[0] for p in pairs], np.int32))
    kidx = jnp.asarray(np.array([p[1] for p in pairs], np.int32))

    def qmap(col):
        return lambda b, s, qi, ki: (b * nq + qi[s], col)

    def kmap(col):
        return lambda b, s, qi, ki: (b * nk + ki[s], col)

    qrow = lambda b, s, qi, ki: (0, b * nq + qi[s])
    grid_spec = pltpu.PrefetchScalarGridSpec(
        num_scalar_prefetch=2,
        grid=(bsz, len(pairs)),
        in_specs=[pl.BlockSpec((tq, 512), qmap(OD_Q // 512)),
                  pl.BlockSpec((tq, 512), qmap(OD_QI // 512)),
                  pl.BlockSpec((8, tq), qrow),
                  pl.BlockSpec((1, tq), qrow),
                  pl.BlockSpec((1, tq), qrow),
                  pl.BlockSpec((tk, 512), kmap(OD_K // 512)),
                  pl.BlockSpec((DSA_WIDTH, tk), lambda b, s, qi, ki: (0, b * nk + ki[s])),
                  pl.BlockSpec((tk, LANES), kmap(OD_KI // LANES))],
        out_specs=pl.BlockSpec((tq, DSA_WIDTH), qmap(0)),
        scratch_shapes=[pltpu.VMEM((DSA_HEADS, tq, LANES), jnp.bfloat16),
                        pltpu.VMEM((IDX_HEADS * tq, LANES), jnp.bfloat16),
                        pltpu.VMEM((DSA_HEADS, tq), jnp.float32),
                        pltpu.VMEM((DSA_HEADS, LANES, tq), jnp.float32)])
    return pl.pallas_call(
        partial(_dsa_attn_kernel, tq=tq, tk=tk),
        grid_spec=grid_spec,
        out_shape=jax.ShapeDtypeStruct((T, DSA_WIDTH), jnp.bfloat16),
        compiler_params=_params("parallel", "arbitrary"),
    )(qidx, kidx, hb, hb, wi_t, thr, cut, hb, v_t, hb)


def _dilated_kernel(q_ref, kp_ref, kc_ref, vp_ref, vc_ref, o_ref, lse_ref, *, tq):
    a = pl.program_id(2)
    row = lax.broadcasted_iota(jnp.int32, (tq, 2 * tq), 0)
    c = lax.broadcasted_iota(jnp.int32, (tq, 2 * tq), 1)
    first_col = jnp.where(a == 0, tq, 0)
    valid = jnp.logical_and(jnp.logical_and(c >= row, c <= row + tq), c >= first_col)
    lane = lax.broadcasted_iota(jnp.int32, (tq, LANES), 1)
    for p in range(DIL_HEADS // 2):
        sl = slice(LANES * p, LANES * (p + 1))
        kk = jnp.concatenate([kp_ref[:, sl], kc_ref[:, sl]], axis=0)
        vv = jnp.concatenate([vp_ref[:, sl], vc_ref[:, sl]], axis=0)
        outs, lses = [], []
        for qh in _split_head_pair(q_ref[:, sl]):
            s = lax.dot_general(qh, kk, (((1,), (1,)), ((), ())), preferred_element_type=jnp.float32)
            s = jnp.where(valid, s, NEG_BIG)
            m = jnp.max(s, axis=1, keepdims=True)
            e = jnp.exp(s - m)
            l = jnp.sum(e, axis=1, keepdims=True)
            outs.append(jnp.dot(e.astype(jnp.bfloat16), vv, preferred_element_type=jnp.float32) / l)
            lses.append(m + jnp.log(l))
        o_ref[:, sl] = jnp.where(lane < 64, outs[0], outs[1])
        lse_ref[:, sl] = jnp.where(lane < 64, lses[0], lses[1])


def _dilated_group(hb, bsz, L, g, tq):
    window, dil = DIL_PATTERNS[g]
    assert window // dil == tq
    T = bsz * L
    M = L // dil
    nb = M // tq
    ncol = hb.shape[1] // 512
    hv = hb.reshape(bsz * M, dil * hb.shape[1])
    qcol, kcol, vcol = (OD_DIL // 512 + s * DIL_GROUPS + g for s in range(3))

    def cur(col):
        return lambda b, r, a: (b * nb + a, r * ncol + col)

    def prev(col):
        return lambda b, r, a: (b * nb + jnp.maximum(a - 1, 0), r * ncol + col)

    blk = (tq, 512)
    out_map = lambda b, r, a: (b * nb + a, r)
    o, lse = pl.pallas_call(
        partial(_dilated_kernel, tq=tq),
        grid=(bsz, dil, nb),
        in_specs=[pl.BlockSpec(blk, cur(qcol)), pl.BlockSpec(blk, prev(kcol)), pl.BlockSpec(blk, cur(kcol)),
                  pl.BlockSpec(blk, prev(vcol)), pl.BlockSpec(blk, cur(vcol))],
        out_specs=[pl.BlockSpec(blk, out_map), pl.BlockSpec(blk, out_map)],
        out_shape=[jax.ShapeDtypeStruct((bsz * M, dil * 512), jnp.float32)] * 2,
        compiler_params=_params("parallel", "parallel", "arbitrary"),
    )(hv, hv, hv, hv, hv)
    return o.reshape(T, 512), lse.reshape(T, 512)


def _dilated_combine_kernel(o0, o1, o2, l0, l1, l2, out_ref):
    a, b, c = l0[...], l1[...], l2[...]
    m = jnp.maximum(jnp.maximum(a, b), c)
    ea, eb, ec = jnp.exp(a - m), jnp.exp(b - m), jnp.exp(c - m)
    num = ea * o0[...] + eb * o1[...] + ec * o2[...]
    out_ref[...] = (num / (ea + eb + ec)).astype(out_ref.dtype)


def _dilated_combine(outs, lses, tm=512):
    T, W = outs[0].shape
    tm = min(tm, T)
    spec = pl.BlockSpec((tm, W), lambda i: (i, 0))
    return pl.pallas_call(
        _dilated_combine_kernel,
        grid=(T // tm,),
        in_specs=[spec] * 6,
        out_specs=spec,
        out_shape=jax.ShapeDtypeStruct((T, W), jnp.bfloat16),
        compiler_params=_params("parallel"),
    )(*outs, *lses)


def _mm_nt_kernel(w_ref, x_ref, o_ref):
    o_ref[...] = lax.dot_general(w_ref[...], x_ref[...].astype(jnp.bfloat16), (((1,), (1,)), ((), ())),
                                 preferred_element_type=jnp.float32).astype(o_ref.dtype)


def _matmul_nt(w_t, x, out_dtype, tm=1024):
    N, K = w_t.shape
    T = x.shape[0]
    tm = min(tm, T)
    return pl.pallas_call(
        _mm_nt_kernel,
        grid=(T // tm,),
        in_specs=[pl.BlockSpec((N, K), lambda i: (0, 0)), pl.BlockSpec((tm, K), lambda i: (i, 0))],
        out_specs=pl.BlockSpec((N, tm), lambda i: (0, i)),
        out_shape=jax.ShapeDtypeStruct((N, T), out_dtype),
        compiler_params=_params("parallel"),
    )(w_t, x)


def _odd_weights(w_in):
    D = w_in.shape[0]
    q, k, v, qi, ki, wi, dil = _split(w_in, OD_SIZES)
    zeros = lambda n: jnp.zeros((D, n), w_in.dtype)
    qi_exp = jnp.concatenate(
        [jnp.concatenate([qi[:, IDX_DIM * h:IDX_DIM * (h + 1)] * IDX_DIM ** -0.5, zeros(LANES - IDX_DIM)], axis=1)
         for h in range(IDX_HEADS)], axis=1)
    dil = dil.reshape(D, 3, DIL_GROUPS * DIL_WIDTH)
    dil = jnp.concatenate([dil[:, 0] * DIL_HEAD_DIM ** -0.5, dil[:, 1], dil[:, 2]], axis=1)
    wb = jnp.concatenate([q * DSA_HEAD_DIM ** -0.5, k, v, qi_exp, dil, ki, zeros(LANES - IDX_DIM),
                          zeros(OD_COLS_PADDED - OD_KI - LANES)], axis=1).astype(jnp.bfloat16)
    wv_t = v.T.astype(jnp.bfloat16)
    wwi_t = jnp.concatenate([wi * IDX_HEADS ** -0.5, zeros(8 - IDX_HEADS)], axis=1).T.astype(jnp.bfloat16)
    return wb, wv_t, wwi_t


def _odd_layer_mix(x2, w_in, bsz, L):
    wb, wv_t, wwi_t = _odd_weights(w_in)
    hb = _matmul(x2, wb, jnp.bfloat16, tm=1024, tn=1024)
    v_t = _matmul_nt(wv_t, x2, jnp.bfloat16)
    wi_t = _matmul_nt(wwi_t, x2, jnp.float32)
    tq, tk = min(DSA_TQ, L), min(DSA_TK, L)
    thr, cut = _dsa_select(hb, wi_t, bsz, L, tq)
    o_c = _dsa_attention(hb, v_t, wi_t, thr, cut, bsz, L, tq, tk)
    groups = [_dilated_group(hb, bsz, L, g, 128) for g in range(DIL_GROUPS)]
    o_d = _dilated_combine([o for o, _ in groups], [l for _, l in groups])
    return o_c, o_d


EV_Q, EV_K, EV_V, EV_R, EV_G = 0, 256, 512, 1024, 1536
EV_COLS_PADDED = 2048
GLA_ROWS = 512
S5_CHUNK = 64


def _gla_kernel(q_ref, k_ref, v_ref, r_ref, g_ref, wg_ref, bg_ref, ng_ref, o_ref, st_ref, *, rows):
    C = GLA_CHUNK

    @pl.when(pl.program_id(1) == 0)
    def _():
        st_ref[...] = jnp.zeros(st_ref.shape, jnp.float32)

    ri = lax.broadcasted_iota(jnp.int32, (C, C), 0)
    ci = lax.broadcasted_iota(jnp.int32, (C, C), 1)
    causal = ci <= ri
    tril = jnp.where(causal, 1.0, 0.0).astype(jnp.float32)
    wg = wg_ref[...].astype(jnp.bfloat16)
    bg = bg_ref[...]
    ng = ng_ref[...]

    def chunk(c, carry):
        r0 = pl.multiple_of(c * C, C)
        rs = pl.ds(r0, C)
        logit = jnp.dot(g_ref[rs, :].astype(jnp.bfloat16), wg, preferred_element_type=jnp.float32) + bg
        log_a = jax.nn.log_sigmoid(logit) / GLA_GATE_TAU
        bcum = jnp.dot(tril, log_a, precision=lax.Precision.HIGHEST, preferred_element_type=jnp.float32)
        b_last = bcum[C - 1:C, :]
        q_t = (q_ref[rs, :] * jnp.exp(bcum)).astype(jnp.bfloat16)
        k_t = (k_ref[rs, :] * jnp.exp(-bcum)).astype(jnp.bfloat16)
        k_end = (k_ref[rs, :] * jnp.exp(b_last - bcum)).astype(jnp.bfloat16)
        dec = jnp.exp(b_last)
        for p in range(GLA_HEADS // 2):
            sl = slice(LANES * p, LANES * (p + 1))
            q_halves = _split_head_pair(q_t[:, sl])
            ke_halves = _split_head_pair(k_end[:, sl])
            for half in range(2):
                h = 2 * p + half
                hs = slice(GLA_DV * h, GLA_DV * (h + 1))
                qm = q_halves[half]
                att = lax.dot_general(qm, k_t[:, sl], (((1,), (1,)), ((), ())), preferred_element_type=jnp.float32)
                att = jnp.where(causal, att, 0.0).astype(jnp.bfloat16)
                v_h = v_ref[rs, hs].astype(jnp.bfloat16)
                st = st_ref[h]
                o = jnp.dot(att, v_h, preferred_element_type=jnp.float32)
                o = o + lax.dot_general(qm, st.astype(jnp.bfloat16), (((1,), (1,)), ((), ())),
                                        preferred_element_type=jnp.float32)
                kv_t = lax.dot_general(v_h, ke_halves[half], (((0,), (0,)), ((), ())),
                                       preferred_element_type=jnp.float32)
                st_ref[h] = st * dec[:, sl] + kv_t
                o = o * lax.rsqrt(jnp.mean(o * o, axis=-1, keepdims=True) + LN_EPS) * ng
                o = o * jax.nn.silu(r_ref[rs, hs])
                o_ref[rs, hs] = o.astype(o_ref.dtype)
        return carry

    lax.fori_loop(0, rows // C, chunk, 0)


def _gla(hf, w_gate2, b_gate2, norm_g, bsz, L):
    T = bsz * L
    rows = min(GLA_ROWS, L)
    nb = L // rows
    dkw = GLA_HEADS * GLA_DK
    dvw = GLA_HEADS * GLA_DV
    wg = jnp.pad(w_gate2, ((0, LANES - GLA_GATE_RANK), (0, 0)))

    def rmap(col):
        return lambda b, i: (b * nb + i, col)

    const = lambda b, i: (0, 0)
    return pl.pallas_call(
        partial(_gla_kernel, rows=rows),
        grid=(bsz, nb),
        in_specs=[pl.BlockSpec((rows, dkw), rmap(EV_Q // dkw)),
                  pl.BlockSpec((rows, dkw), rmap(EV_K // dkw)),
                  pl.BlockSpec((rows, dvw), rmap(EV_V // dvw)),
                  pl.BlockSpec((rows, dvw), rmap(EV_R // dvw)),
                  pl.BlockSpec((rows, LANES), rmap(EV_G // LANES)),
                  pl.BlockSpec((LANES, dkw), const),
                  pl.BlockSpec((1, dkw), const),
                  pl.BlockSpec((1, GLA_DV), const)],
        out_specs=pl.BlockSpec((rows, dvw), rmap(0)),
        out_shape=jax.ShapeDtypeStruct((T, dvw), jnp.bfloat16),
        scratch_shapes=[pltpu.VMEM((GLA_HEADS, GLA_DV, LANES), jnp.float32)],
        compiler_params=_params("parallel", "arbitrary"),
    )(hf, hf, hf, hf, hf, wg, b_gate2.reshape(1, dkw), norm_g.reshape(1, GLA_DV))


def _s5_tables(a_re, a_im, log_dt, b_re, b_im, c_re, c_im, d_skip):
    f32 = jnp.float32
    Cs, G, P, N = S5_CHUNK, S5_GROUPS, S5_STATE, S5_GROUP
    lam_re = jnp.minimum(a_re.astype(f32), S5_MAX_RE)
    lam_im = a_im.astype(f32)
    dt = jnp.exp(log_dt.astype(f32))[:, None]
    mag = jnp.exp(lam_re * dt)
    ab_re = mag * jnp.cos(lam_im * dt)
    ab_im = mag * jnp.sin(lam_im * dt)
    inv = 1.0 / (lam_re * lam_re + lam_im * lam_im)
    z_re = ((ab_re - 1.0) * lam_re + ab_im * lam_im) * inv
    z_im = (ab_im * lam_re - (ab_re - 1.0) * lam_im) * inv
    br, bi = b_re.astype(f32), b_im.astype(f32)
    bb_re = z_re[..., None] * br - z_im[..., None] * bi
    bb_im = z_re[..., None] * bi + z_im[..., None] * br
    kk = jnp.arange(Cs + 1, dtype=f32)[:, None, None]
    pmag = jnp.exp(kk * (lam_re * dt))
    pw_re = pmag * jnp.cos(kk * (lam_im * dt))
    pw_im = pmag * jnp.sin(kk * (lam_im * dt))
    cr, ci = c_re.astype(f32), c_im.astype(f32)
    ca_re = cr[None] * pw_re[:, :, None, :] - ci[None] * pw_im[:, :, None, :]
    ca_im = cr[None] * pw_im[:, :, None, :] + ci[None] * pw_re[:, :, None, :]
    hi = lax.Precision.HIGHEST
    kern = (jnp.einsum('kgnp,gpm->kgnm', ca_re[:Cs], bb_re, precision=hi)
            - jnp.einsum('kgnp,gpm->kgnm', ca_im[:Cs], bb_im, precision=hi))
    jj = jnp.arange(Cs)[:, None]
    ii = jnp.arange(Cs)[None, :]
    tz = jnp.where((ii >= jj)[:, :, None, None, None], kern[jnp.maximum(ii - jj, 0)], 0.0)
    tz = jnp.transpose(tz, (2, 4, 0, 3, 1)).reshape(G, N * Cs, N * Cs)
    rev_re, rev_im = pw_re[Cs - 1::-1][:Cs], pw_im[Cs - 1::-1][:Cs]
    ws_re = rev_re[..., None] * bb_re[None] - rev_im[..., None] * bb_im[None]
    ws_im = rev_re[..., None] * bb_im[None] + rev_im[..., None] * bb_re[None]
    to_ws = lambda w: jnp.pad(jnp.transpose(w, (1, 3, 0, 2)).reshape(G, N * Cs, P), ((0, 0), (0, 0), (0, LANES - P)))
    to_wo = lambda w: jnp.pad(jnp.transpose(w, (1, 3, 2, 0)).reshape(G, P, N * Cs), ((0, 0), (0, LANES - P), (0, 0)))
    a_cs = jnp.stack([jnp.pad(pw_re[Cs], ((0, 0), (0, LANES - P))), jnp.pad(pw_im[Cs], ((0, 0), (0, LANES - P)))], axis=1)
    d_exp = jnp.repeat(d_skip.astype(f32).reshape(G, 1, N), Cs, axis=2)
    bf = jnp.bfloat16
    return (tz.astype(bf), to_ws(ws_re).astype(bf), to_ws(ws_im).astype(bf),
            to_wo(ca_re[1:]).astype(bf), to_wo(-ca_im[1:]).astype(bf), a_cs, d_exp)


def _s5_kernel(u_ref, tz_ref, wsr_ref, wsi_ref, wor_ref, woi_ref, acs_ref, d_ref, y_ref, xr_ref, xi_ref,
               *, nchunk, nbatch):
    u32 = jnp.concatenate([u_ref[m] for m in range(S5_GROUP)], axis=1)
    u = u32.astype(jnp.bfloat16)
    xr_ref[...] = jnp.dot(u, wsr_ref[0], preferred_element_type=jnp.float32)
    xi_ref[...] = jnp.dot(u, wsi_ref[0], preferred_element_type=jnp.float32)
    ar = acs_ref[0, 0:1, :]
    ai = acs_ref[0, 1:2, :]

    def step(c, carry):
        new = []
        for b in range(nbatch):
            sr, si = carry[2 * b], carry[2 * b + 1]
            row = pl.ds(b * nchunk + c, 1)
            lr, li = xr_ref[row, :], xi_ref[row, :]
            xr_ref[row, :] = sr
            xi_ref[row, :] = si
            new += [ar * sr - ai * si + lr, ar * si + ai * sr + li]
        return tuple(new)

    zero = jnp.zeros((1, LANES), jnp.float32)
    lax.fori_loop(0, nchunk, step, (zero,) * (2 * nbatch))
    y = jnp.dot(u, tz_ref[0], preferred_element_type=jnp.float32)
    y = y + jnp.dot(xr_ref[...].astype(jnp.bfloat16), wor_ref[0], preferred_element_type=jnp.float32)
    y = y + jnp.dot(xi_ref[...].astype(jnp.bfloat16), woi_ref[0], preferred_element_type=jnp.float32)
    y = jax.nn.gelu(y + d_ref[0] * u32)
    for n in range(S5_GROUP):
        y_ref[n] = y[:, S5_CHUNK * n:S5_CHUNK * (n + 1)]


def _s5_scan(u_t, tables, bsz, nchunk):
    width, R, Cs = u_t.shape
    G, N = S5_GROUPS, S5_GROUP
    W = N * Cs
    tz, wsr, wsi, wor, woi, a_cs, d_exp = tables
    gmap = lambda g: (g, 0, 0)
    return pl.pallas_call(
        partial(_s5_kernel, nchunk=nchunk, nbatch=bsz),
        grid=(G,),
        in_specs=[pl.BlockSpec((N, R, Cs), gmap), pl.BlockSpec((1, W, W), gmap),
                  pl.BlockSpec((1, W, LANES), gmap), pl.BlockSpec((1, W, LANES), gmap),
                  pl.BlockSpec((1, LANES, W), gmap), pl.BlockSpec((1, LANES, W), gmap),
                  pl.BlockSpec((1, 2, LANES), gmap), pl.BlockSpec((1, 1, W), gmap)],
        out_specs=pl.BlockSpec((N, R, Cs), gmap),
        out_shape=jax.ShapeDtypeStruct((width, R, Cs), jnp.float32),
        scratch_shapes=[pltpu.VMEM((R, LANES), jnp.float32), pltpu.VMEM((R, LANES), jnp.float32)],
        compiler_params=_params("parallel"),
    )(u_t, tz, wsr, wsi, wor, woi, a_cs, d_exp)


def _glu_kernel(yt_ref, wt_ref, b_ref, o_ref):
    y = yt_ref[...]
    gate = jnp.dot(wt_ref[...], y.astype(jnp.bfloat16), preferred_element_type=jnp.float32) + b_ref[...]
    o_ref[...] = (y * jax.nn.sigmoid(gate)).T.astype(o_ref.dtype)


def _glu(y_t, w_glu, b_glu, tm=1024):
    W, T = y_t.shape
    tm = min(tm, T)
    return pl.pallas_call(
        _glu_kernel,
        grid=(T // tm,),
        in_specs=[pl.BlockSpec((W, tm), lambda i: (0, i)),
                  pl.BlockSpec((W, W), lambda i: (0, 0)),
                  pl.BlockSpec((W, 1), lambda i: (0, 0))],
        out_specs=pl.BlockSpec((tm, W), lambda i: (i, 0)),
        out_shape=jax.ShapeDtypeStruct((T, W), jnp.bfloat16),
        compiler_params=_params("parallel"),
    )(y_t, w_glu.T.astype(jnp.bfloat16), b_glu.reshape(W, 1))


def _s5(u_t, s5_params, w_glu, b_glu, bsz, L):
    T = bsz * L
    nchunk = L // S5_CHUNK
    y_t = _s5_scan(u_t.reshape(S5_WIDTH, bsz * nchunk, S5_CHUNK), _s5_tables(*s5_params), bsz, nchunk)
    return _glu(y_t.reshape(S5_WIDTH, T), w_glu, b_glu)


def _even_weights(w_in):
    D = w_in.shape[0]
    q, k, v, r, g_lr, u = _split(w_in, EV_SIZES)
    pad = jnp.zeros((D, EV_COLS_PADDED - EV_G - GLA_GATE_RANK), w_in.dtype)
    wb = jnp.concatenate([q * GLA_DK ** -0.5, k, v, r, g_lr, pad], axis=1).astype(jnp.bfloat16)
    return wb, u.T.astype(jnp.bfloat16)


def _even_layer_mix(x2, w_in, w_gate2, b_gate2, norm_g, s5_params, w_glu, b_glu, bsz, L):
    wb, wu_t = _even_weights(w_in)
    hf = _matmul(x2, wb, jnp.float32, tm=1024, tn=512)
    u_t = _matmul_nt(wu_t, x2, jnp.float32)
    o_a = _gla(hf, w_gate2, b_gate2, norm_g, bsz, L)
    o_b = _s5(u_t, s5_params, w_glu, b_glu, bsz, L)
    return o_a, o_b


MOE_TILE = 512
ROW_TILE = 8
DMA_UNROLL = 8
MOE_VMEM_LIMIT_BYTES = 56 * 1024 * 1024


def _router_kernel(x_ref, w_ref, b_ref, e_ref, g_ref):
    logits = jnp.dot(x_ref[...].astype(jnp.bfloat16), w_ref[...], preferred_element_type=jnp.float32) + b_ref[...]
    tm = logits.shape[0]
    lane = lax.broadcasted_iota(jnp.int32, (tm, LANES), 1)
    logits = jnp.where(lane < N_EXPERTS, logits, NEG_BIG)
    tops, idxs = [], []
    for _ in range(TOP_K):
        m = jnp.max(logits, axis=1, keepdims=True)
        idx = jnp.min(jnp.where(logits == m, lane, LANES), axis=1, keepdims=True)
        tops.append(m)
        idxs.append(idx)
        logits = jnp.where(lane == idx, NEG_BIG, logits)
    exps = [jnp.exp(t - tops[0]) for t in tops]
    denom = exps[0]
    for e in exps[1:]:
        denom = denom + e
    lane4 = lax.broadcasted_iota(jnp.int32, (tm, TOP_K), 1)
    e_out = jnp.zeros((tm, TOP_K), jnp.int32)
    g_out = jnp.zeros((tm, TOP_K), jnp.float32)
    for k in range(TOP_K):
        e_out = jnp.where(lane4 == k, idxs[k], e_out)
        g_out = jnp.where(lane4 == k, exps[k] / denom, g_out)
    e_ref[...] = e_out
    g_ref[...] = g_out


def _router(x, router_w, router_b, tm=512):
    T, D = x.shape
    tm = min(tm, T)
    w = jnp.pad(router_w, ((0, 0), (0, LANES - N_EXPERTS))).astype(jnp.bfloat16)
    b = jnp.pad(router_b, (0, LANES - N_EXPERTS)).reshape(1, LANES)
    return pl.pallas_call(
        _router_kernel,
        grid=(T // tm,),
        in_specs=[pl.BlockSpec((tm, D), lambda i: (i, 0)),
                  pl.BlockSpec((D, LANES), lambda i: (0, 0)),
                  pl.BlockSpec((1, LANES), lambda i: (0, 0))],
        out_specs=[pl.BlockSpec((tm, TOP_K), lambda i: (i, 0)), pl.BlockSpec((tm, TOP_K), lambda i: (i, 0))],
        out_shape=[jax.ShapeDtypeStruct((T, TOP_K), jnp.int32), jax.ShapeDtypeStruct((T, TOP_K), jnp.float32)],
        compiler_params=_params("parallel"),
    )(x, w, b)


def _moe_rank_kernel(e_ref, rank_ref, count_ref, carry_ref):
    @pl.when(pl.program_id(0) == 0)
    def _():
        carry_ref[...] = jnp.zeros(carry_ref.shape, jnp.float32)

    e = e_ref[...]
    tm = e.shape[0]
    lane = lax.broadcasted_iota(jnp.int32, (tm, LANES), 1)
    onehot = jnp.zeros((tm, LANES), jnp.float32)
    for k in range(TOP_K):
        onehot = onehot + jnp.where(lane == e[:, k:k + 1], 1.0, 0.0)
    ri = lax.broadcasted_iota(jnp.int32, (tm, tm), 0)
    ci = lax.broadcasted_iota(jnp.int32, (tm, tm), 1)
    strict_lower = jnp.where(ci < ri, 1.0, 0.0).astype(jnp.bfloat16)
    before = jnp.dot(strict_lower, onehot.astype(jnp.bfloat16), preferred_element_type=jnp.float32) + carry_ref[...]
    lane4 = lax.broadcasted_iota(jnp.int32, (tm, TOP_K), 1)
    rank = jnp.zeros((tm, TOP_K), jnp.int32)
    for k in range(TOP_K):
        r_k = jnp.sum(jnp.where(lane == e[:, k:k + 1], before, 0.0), axis=1, keepdims=True).astype(jnp.int32)
        rank = jnp.where(lane4 == k, r_k, rank)
    rank_ref[...] = rank
    carry_ref[...] = carry_ref[...] + jnp.sum(onehot, axis=0, keepdims=True)
    count_ref[...] = carry_ref[...]


def _moe_rank(top_e, tm=256):
    T = top_e.shape[0]
    tm = min(tm, T)
    return pl.pallas_call(
        _moe_rank_kernel,
        grid=(T // tm,),
        in_specs=[pl.BlockSpec((tm, TOP_K), lambda i: (i, 0))],
        out_specs=[pl.BlockSpec((tm, TOP_K), lambda i: (i, 0)), pl.BlockSpec((1, LANES), lambda i: (0, 0))],
        out_shape=[jax.ShapeDtypeStruct((T, TOP_K), jnp.int32), jax.ShapeDtypeStruct((1, LANES), jnp.float32)],
        scratch_shapes=[pltpu.VMEM((1, LANES), jnp.float32)],
        compiler_params=_params("arbitrary"),
    )(top_e)


def _to_token_tiles(x, dst_ref, rows):
    for c in range(ROW_TILE):
        dst_ref[pl.ds(c, rows, stride=ROW_TILE), :] = x[:, LANES * c:LANES * (c + 1)]


def _from_token_tiles(src_ref, rows):
    return jnp.concatenate([src_ref[pl.ds(c, rows, stride=ROW_TILE), :] for c in range(ROW_TILE)], axis=1)


def _moe_dispatch_kernel(dest_ref, x_ref, init_ref, xbuf_ref, xs_ref, sem, *, tm):
    del init_ref
    i = pl.program_id(0)
    n = pl.num_programs(0)
    slot = i % 2

    def row_copy(s, r, d):
        return pltpu.make_async_copy(xs_ref.at[s, pl.ds(pl.multiple_of(r * ROW_TILE, ROW_TILE), ROW_TILE), :],
                                     xbuf_ref.at[pl.ds(pl.multiple_of(d * ROW_TILE, ROW_TILE), ROW_TILE), :], sem.at[s])

    def drain(s):
        def body(a, c):
            row_copy(s, 0, 0).wait()
            return c
        lax.fori_loop(0, tm * TOP_K, body, 0, unroll=DMA_UNROLL)

    @pl.when(i >= 2)
    def _():
        drain(slot)

    _to_token_tiles(x_ref[...], xs_ref.at[slot], tm)

    def body(r, c):
        for k in range(TOP_K):
            row_copy(slot, r, dest_ref[r * TOP_K + k]).start()
        return c

    lax.fori_loop(0, tm, body, 0, unroll=DMA_UNROLL)

    @pl.when(i == n - 1)
    def _():
        drain(slot)

        @pl.when(n > 1)
        def _():
            drain(1 - slot)


def _moe_dispatch(x, dest_flat, n_rows, tm=256):
    T, D = x.shape
    tm = min(tm, T)
    assert D == ROW_TILE * LANES
    init = jnp.zeros((n_rows * ROW_TILE, LANES), jnp.float32)
    return pl.pallas_call(
        partial(_moe_dispatch_kernel, tm=tm),
        grid=(T // tm,),
        in_specs=[pl.BlockSpec((tm * TOP_K,), lambda i: (i,), memory_space=pltpu.SMEM),
                  pl.BlockSpec((tm, D), lambda i: (i, 0)),
                  pl.BlockSpec(memory_space=pl.ANY)],
        out_specs=pl.BlockSpec(memory_space=pl.ANY),
        out_shape=jax.ShapeDtypeStruct((n_rows * ROW_TILE, LANES), jnp.float32),
        scratch_shapes=[pltpu.VMEM((2, tm * ROW_TILE, LANES), jnp.float32), pltpu.SemaphoreType.DMA((2,))],
        input_output_aliases={2: 0},
        compiler_params=_params("arbitrary"),
    )(dest_flat, x, init)


def _moe_ffn_kernel(te_ref, nt_ref, x_ref, w1_ref, b1_ref, w2_ref, b2_ref, y_ref, w1b_ref, w2b_ref):
    i = pl.program_id(0)
    prev = te_ref[jnp.maximum(i - 1, 0)]

    @pl.when(jnp.logical_or(i == 0, te_ref[i] != prev))
    def _():
        w1b_ref[...] = w1_ref[0].astype(jnp.bfloat16)
        w2b_ref[...] = w2_ref[0].astype(jnp.bfloat16)

    @pl.when(i < nt_ref[0])
    def _():
        x = _from_token_tiles(x_ref, MOE_TILE).astype(jnp.bfloat16)
        h = jnp.dot(x, w1b_ref[...], preferred_element_type=jnp.float32) + b1_ref[0]
        glu = jnp.minimum(h[:, :D_FF], SWIGLU_LIMIT)
        lin = jnp.clip(h[:, D_FF:], -SWIGLU_LIMIT, SWIGLU_LIMIT)
        act = glu * jax.nn.sigmoid(SWIGLU_ALPHA * glu) * (lin + 1.0)
        y = jnp.dot(act.astype(jnp.bfloat16), w2b_ref[...], preferred_element_type=jnp.float32) + b2_ref[0]
        _to_token_tiles(y, y_ref, MOE_TILE)

    @pl.when(i >= nt_ref[0])
    def _():
        y_ref[...] = jnp.zeros(y_ref.shape, y_ref.dtype)


def _moe_expert_ffn(xbuf, tile_expert, n_used, w1, b1, w2, b2):
    D, F2 = w1.shape[1], w1.shape[2]
    P = xbuf.shape[0] // ROW_TILE
    n_tiles = P // MOE_TILE
    blk = (MOE_TILE * ROW_TILE, LANES)
    grid_spec = pltpu.PrefetchScalarGridSpec(
        num_scalar_prefetch=2,
        grid=(n_tiles,),
        in_specs=[pl.BlockSpec(blk, lambda i, te, nt: (jnp.minimum(i, nt[0] - 1), 0)),
                  pl.BlockSpec((1, D, F2), lambda i, te, nt: (te[i], 0, 0)),
                  pl.BlockSpec((1, 1, F2), lambda i, te, nt: (te[i], 0, 0)),
                  pl.BlockSpec((1, D_FF, D), lambda i, te, nt: (te[i], 0, 0)),
                  pl.BlockSpec((1, 1, D), lambda i, te, nt: (te[i], 0, 0))],
        out_specs=pl.BlockSpec(blk, lambda i, te, nt: (i, 0)),
        scratch_shapes=[pltpu.VMEM((D, F2), jnp.bfloat16), pltpu.VMEM((D_FF, D), jnp.bfloat16)],
    )
    return pl.pallas_call(
        _moe_ffn_kernel,
        grid_spec=grid_spec,
        out_shape=jax.ShapeDtypeStruct(xbuf.shape, jnp.float32),
        compiler_params=pltpu.CompilerParams(dimension_semantics=("arbitrary",),
                                             vmem_limit_bytes=MOE_VMEM_LIMIT_BYTES),
    )(tile_expert, n_used, xbuf, w1, b1.reshape(N_EXPERTS, 1, F2), w2, b2.reshape(N_EXPERTS, 1, D))


def _moe_combine_kernel(dest_ref, gate_ref, x_ref, g_ref, b_ref, ybuf_ref, o_ref, rows_ref, sem, *, tm):
    def row_copy(k, r, d):
        return pltpu.make_async_copy(ybuf_ref.at[pl.ds(pl.multiple_of(d * ROW_TILE, ROW_TILE), ROW_TILE), :],
                                     rows_ref.at[k, pl.ds(pl.multiple_of(r * ROW_TILE, ROW_TILE), ROW_TILE), :], sem.at[0])

    def start(r, c):
        for k in range(TOP_K):
            row_copy(k, r, dest_ref[r * TOP_K + k]).start()
        return c

    lax.fori_loop(0, tm, start, 0, unroll=DMA_UNROLL)

    def wait(a, c):
        row_copy(0, 0, 0).wait()
        return c

    lax.fori_loop(0, tm * TOP_K, wait, 0, unroll=DMA_UNROLL)
    gate = gate_ref[...]
    ffn = gate[:, 0:1] * _from_token_tiles(rows_ref.at[0], tm)
    for k in range(1, TOP_K):
        ffn = ffn + gate[:, k:k + 1] * _from_token_tiles(rows_ref.at[k], tm)
    o_ref[...] = _layer_norm_rows(DEEPNORM_ALPHA * x_ref[...] + ffn, g_ref[...], b_ref[...])


def _moe_combine(ybuf, dest_flat, gate, x, g, b, tm=256):
    T, D = x.shape
    tm = min(tm, T)
    return pl.pallas_call(
        partial(_moe_combine_kernel, tm=tm),
        grid=(T // tm,),
        in_specs=[pl.BlockSpec((tm * TOP_K,), lambda i: (i,), memory_space=pltpu.SMEM),
                  pl.BlockSpec((tm, TOP_K), lambda i: (i, 0)),
                  pl.BlockSpec((tm, D), lambda i: (i, 0)),
                  pl.BlockSpec((1, D), lambda i: (0, 0)),
                  pl.BlockSpec((1, D), lambda i: (0, 0)),
                  pl.BlockSpec(memory_space=pl.ANY)],
        out_specs=pl.BlockSpec((tm, D), lambda i: (i, 0)),
        out_shape=jax.ShapeDtypeStruct((T, D), jnp.float32),
        scratch_shapes=[pltpu.VMEM((TOP_K, tm * ROW_TILE, LANES), jnp.float32), pltpu.SemaphoreType.DMA((1,))],
        compiler_params=_params("arbitrary"),
    )(dest_flat, gate, x, g.reshape(1, D), b.reshape(1, D), ybuf)


def _moe_layer(x, router_w, router_b, w1, b1, w2, b2, ln_g, ln_b):
    T, D = x.shape
    A = T * TOP_K
    n_tiles = -(-(A + N_EXPERTS * (MOE_TILE - 1)) // MOE_TILE)
    top_e, gate = _router(x, router_w, router_b)
    rank, counts = _moe_rank(top_e)
    counts = counts[0, :N_EXPERTS].astype(jnp.int32)
    padded = (counts + MOE_TILE - 1) // MOE_TILE * MOE_TILE
    pend = jnp.cumsum(padded)
    pstart = pend - padded
    dest = (pstart[top_e] + rank).reshape(A)
    tile_expert = jnp.minimum(jnp.searchsorted(pend, jnp.arange(n_tiles) * MOE_TILE, side='right'),
                              N_EXPERTS - 1).astype(jnp.int32)
    n_used = (pend[-1:] // MOE_TILE).astype(jnp.int32)
    xbuf = _moe_dispatch(x, dest, n_tiles * MOE_TILE)
    ybuf = _moe_expert_ffn(xbuf, tile_expert, n_used, w1, b1, w2, b2)
    return _moe_combine(ybuf, dest, gate, x, ln_g, ln_b)


def kernel(x, ev_w_in, gla_w_gate2, gla_b_gate2, gla_norm_g, s5_a_re, s5_a_im, s5_log_dt, s5_b_re, s5_b_im,
           s5_c_re, s5_c_im, s5_d, s5_w_glu, s5_b_glu, ev_w_out, od_w_in, od_w_out, ln1_g, ln1_b, ln2_g, ln2_b,
           router_w, router_b, moe_w1, moe_b1, moe_w2, moe_b2):
    bsz, L, D = x.shape
    T = bsz * L
    x = x.reshape(T, D)
    for layer in range(DEPTH):
        j = layer // 2
        if layer % 2 == 0:
            s5_params = (s5_a_re[j], s5_a_im[j], s5_log_dt[j], s5_b_re[j], s5_b_im[j], s5_c_re[j], s5_c_im[j], s5_d[j])
            o_1, o_2 = _even_layer_mix(x, ev_w_in[j], gla_w_gate2[j], gla_b_gate2[j], gla_norm_g[j], s5_params,
                                       s5_w_glu[j], s5_b_glu[j], bsz, L)
            w_out = ev_w_out[j]
        else:
            o_1, o_2 = _odd_layer_mix(x, od_w_in[j], bsz, L)
            w_out = od_w_out[j]
        x = _matmul2_res_ln(o_1, o_2, w_out, x, ln1_g[layer], ln1_b[layer])
        x = _moe_layer(x, router_w[layer], router_b[layer], moe_w1[layer], moe_b1[layer], moe_w2[layer],
                       moe_b2[layer], ln2_g[layer], ln2_b[layer])
    return x.reshape(bsz, L, D)
```

```python
import math
from functools import partial

import numpy as np
import jax
import jax.numpy as jnp
from jax import lax
from jax.experimental import pallas as pl
from jax.experimental.pallas import tpu as pltpu

D_MODEL = 1024
DEPTH = 4
DEEPNORM_ALPHA = (2.0 * DEPTH) ** 0.25
LN_EPS = 1e-5
MIX_WIDTH = D_MODEL

GLA_HEADS = 4
GLA_DV = MIX_WIDTH // 2 // GLA_HEADS
GLA_DK = GLA_DV // 2
GLA_GATE_RANK = 16
GLA_GATE_TAU = 16.0
GLA_CHUNK = 64

S5_WIDTH = MIX_WIDTH // 2
S5_GROUP = 16
S5_GROUPS = S5_WIDTH // S5_GROUP
S5_STATE = 64
S5_MAX_RE = -1e-4

EV_SIZES = (GLA_HEADS * GLA_DK, GLA_HEADS * GLA_DK, GLA_HEADS * GLA_DV, GLA_HEADS * GLA_DV, GLA_GATE_RANK, S5_WIDTH)

DSA_HEADS = 8
DSA_HEAD_DIM = 64
DSA_WIDTH = DSA_HEADS * DSA_HEAD_DIM
IDX_HEADS = 4
IDX_DIM = 64
DSA_TOPK_MAX = 256

DIL_PATTERNS = ((128, 1), (512, 4), (2048, 16))
DIL_GROUPS = len(DIL_PATTERNS)
DIL_HEADS = 8
DIL_HEAD_DIM = 64
DIL_WIDTH = DIL_HEADS * DIL_HEAD_DIM

OD_SIZES = (DSA_WIDTH, DSA_WIDTH, DSA_WIDTH, IDX_HEADS * IDX_DIM, IDX_DIM, IDX_HEADS, 3 * DIL_GROUPS * DIL_WIDTH)

N_EXPERTS = 32
TOP_K = 4
D_FF = D_MODEL
SWIGLU_ALPHA = 1.702
SWIGLU_LIMIT = 7.0
MOE_BLOCK = 512

LANES = 128
VMEM_LIMIT_BYTES = 48 * 1024 * 1024
NEG_BIG = -1e30
INT_MIN = -2 ** 31
INT_MAX = 2 ** 31 - 1
KEY_NEG_INF = -0x7F800000

OD_Q, OD_K, OD_QI, OD_KI = 0, 512, 1024, 1536
OD_COLS_PADDED = 2048
DIL_COLS = 3 * DIL_WIDTH


def _split(h, sizes):
    return jnp.split(h, [int(i) for i in np.cumsum(sizes)[:-1]], axis=-1)


def _params(*sem):
    return pltpu.CompilerParams(dimension_semantics=sem, vmem_limit_bytes=VMEM_LIMIT_BYTES)


def _mm_kernel(x_ref, w_ref, o_ref):
    o_ref[...] = jnp.dot(x_ref[...].astype(jnp.bfloat16), w_ref[...].astype(jnp.bfloat16),
                         preferred_element_type=jnp.float32).astype(o_ref.dtype)


def _matmul(x, w, out_dtype=jnp.float32, tm=512, tn=512):
    T, K = x.shape
    N = w.shape[1]
    tm, tn = min(tm, T), min(tn, N)
    assert T % tm == 0 and N % tn == 0
    return pl.pallas_call(
        _mm_kernel,
        grid=(T // tm, N // tn),
        in_specs=[pl.BlockSpec((tm, K), lambda i, j: (i, 0)),
                  pl.BlockSpec((K, tn), lambda i, j: (0, j))],
        out_specs=pl.BlockSpec((tm, tn), lambda i, j: (i, j)),
        out_shape=jax.ShapeDtypeStruct((T, N), out_dtype),
        compiler_params=_params("parallel", "arbitrary"),
    )(x, w)


def _layer_norm_rows(z, g, b):
    mu = jnp.mean(z, axis=-1, keepdims=True)
    zc = z - mu
    var = jnp.mean(zc * zc, axis=-1, keepdims=True)
    return zc * lax.rsqrt(var + LN_EPS) * g + b


def _mm_res_ln_kernel(a_ref, w_ref, x_ref, g_ref, b_ref, o_ref):
    mix = jnp.dot(a_ref[...].astype(jnp.bfloat16), w_ref[...].astype(jnp.bfloat16),
                  preferred_element_type=jnp.float32)
    o_ref[...] = _layer_norm_rows(DEEPNORM_ALPHA * x_ref[...] + mix, g_ref[...], b_ref[...])


def _matmul_res_ln(a, w, x, g, b, tm=512):
    T, K = a.shape
    D = w.shape[1]
    tm = min(tm, T)
    return pl.pallas_call(
        _mm_res_ln_kernel,
        grid=(T // tm,),
        in_specs=[pl.BlockSpec((tm, K), lambda i: (i, 0)),
                  pl.BlockSpec((K, D), lambda i: (0, 0)),
                  pl.BlockSpec((tm, D), lambda i: (i, 0)),
                  pl.BlockSpec((1, D), lambda i: (0, 0)),
                  pl.BlockSpec((1, D), lambda i: (0, 0))],
        out_specs=pl.BlockSpec((tm, D), lambda i: (i, 0)),
        out_shape=jax.ShapeDtypeStruct((T, D), jnp.float32),
        compiler_params=_params("parallel"),
    )(a, w, x, g.reshape(1, D), b.reshape(1, D))


def _mm2_res_ln_kernel(a1_ref, a2_ref, w1_ref, w2_ref, x_ref, g_ref, b_ref, o_ref):
    mix = jnp.dot(a1_ref[...], w1_ref[...], preferred_element_type=jnp.float32)
    mix += jnp.dot(a2_ref[...], w2_ref[...], preferred_element_type=jnp.float32)
    o_ref[...] = _layer_norm_rows(DEEPNORM_ALPHA * x_ref[...] + mix, g_ref[...], b_ref[...])


def _matmul2_res_ln(a1, a2, w, x, g, b, tm=512):
    T, K1 = a1.shape
    D = w.shape[1]
    tm = min(tm, T)
    wb = w.astype(jnp.bfloat16)
    return pl.pallas_call(
        _mm2_res_ln_kernel,
        grid=(T // tm,),
        in_specs=[pl.BlockSpec((tm, K1), lambda i: (i, 0)),
                  pl.BlockSpec((tm, a2.shape[1]), lambda i: (i, 0)),
                  pl.BlockSpec((K1, D), lambda i: (0, 0)),
                  pl.BlockSpec((a2.shape[1], D), lambda i: (1, 0)),
                  pl.BlockSpec((tm, D), lambda i: (i, 0)),
                  pl.BlockSpec((1, D), lambda i: (0, 0)),
                  pl.BlockSpec((1, D), lambda i: (0, 0))],
        out_specs=pl.BlockSpec((tm, D), lambda i: (i, 0)),
        out_shape=jax.ShapeDtypeStruct((T, D), jnp.float32),
        compiler_params=_params("parallel"),
    )(a1, a2, wb, wb, x, g.reshape(1, D), b.reshape(1, D))


def _res_ln_kernel(y_ref, x_ref, g_ref, b_ref, o_ref):
    o_ref[...] = _layer_norm_rows(DEEPNORM_ALPHA * x_ref[...] + y_ref[...], g_ref[...], b_ref[...])


def _res_ln(y, x, g, b, tm=512):
    T, D = x.shape
    tm = min(tm, T)
    return pl.pallas_call(
        _res_ln_kernel,
        grid=(T // tm,),
        in_specs=[pl.BlockSpec((tm, D), lambda i: (i, 0)),
                  pl.BlockSpec((tm, D), lambda i: (i, 0)),
                  pl.BlockSpec((1, D), lambda i: (0, 0)),
                  pl.BlockSpec((1, D), lambda i: (0, 0))],
        out_specs=pl.BlockSpec((tm, D), lambda i: (i, 0)),
        out_shape=jax.ShapeDtypeStruct((T, D), jnp.float32),
        compiler_params=_params("parallel"),
    )(y, x, g.reshape(1, D), b.reshape(1, D))


DSA_TQ = 256
DSA_TK = 512
COUNT_ROWS = 64
TIE_CHECK_PASS = 8


def _stack_index_heads(qi_blk):
    return jnp.concatenate([qi_blk[:, LANES * h:LANES * (h + 1)] for h in range(IDX_HEADS)], axis=0)


def _index_keys_t(ki_blk, qi_all, wi_t, q0, k0, tq, causal_mask=True):
    tk = ki_blk.shape[0]
    d = lax.dot_general(ki_blk, qi_all, (((1,), (1,)), ((), ())), preferred_element_type=jnp.float32)
    sc = wi_t[0:1, :] * jnp.maximum(d[:, 0:tq], 0.0)
    for h in range(1, IDX_HEADS):
        sc = sc + wi_t[h:h + 1, :] * jnp.maximum(d[:, h * tq:(h + 1) * tq], 0.0)
    if causal_mask:
        kpos = k0 + lax.broadcasted_iota(jnp.int32, (tk, 1), 0)
        qpos = q0 + lax.broadcasted_iota(jnp.int32, (1, tq), 1)
        sc = jnp.where(kpos <= qpos, sc, -jnp.inf)
    bits = lax.bitcast_convert_type(sc, jnp.int32)
    return jnp.where(bits < 0, INT_MIN - bits, bits)


def _dsa_select_kernel(qi_ref, wit_ref, ki_ref, thr_ref, cut_ref, key_ref, *, tq, topk):
    i = pl.program_id(1)
    nblk = i + 1
    q0 = i * tq
    qi_all = _stack_index_heads(qi_ref[...])
    wi_t = wit_ref[...]

    def fill(j, carry):
        k0 = pl.multiple_of(j * tq, tq)
        key_ref[j] = _index_keys_t(ki_ref[pl.ds(k0, tq), :], qi_all, wi_t, q0, k0, tq, causal_mask=False)
        return carry

    lax.fori_loop(0, i, fill, 0)
    key_ref[i] = _index_keys_t(ki_ref[pl.ds(pl.multiple_of(q0, tq), tq), :], qi_all, wi_t, q0, q0, tq)

    kr = COUNT_ROWS
    row = lax.broadcasted_iota(jnp.int32, (kr, 1), 0)

    def count(pred):
        def body(j, acc):
            for s in range(tq // kr):
                kk = key_ref[j, s * kr:(s + 1) * kr, :]
                acc = acc + jnp.where(pred(kk, j * tq + s * kr), 1, 0)
            return acc
        acc = lax.fori_loop(0, nblk, body, jnp.zeros((kr, tq), jnp.int32))
        return jnp.sum(acc.astype(jnp.float32), axis=0, keepdims=True).astype(jnp.int32)

    def any_true(mask):
        return jnp.max(jnp.where(mask, 1.0, 0.0)) > 0.5

    def bit_cond(st):
        return jnp.logical_and(st[0] < 32, st[4])

    def bit_step(st):
        p, thr, cnt, final, _ = st
        cand = thr ^ lax.shift_left(jnp.int32(1), 31 - p)
        c = count(lambda kk, base: kk >= cand)
        take = c >= topk
        thr = jnp.where(take, cand, thr)
        cnt = jnp.where(take, c, cnt)
        final = lax.cond(p == TIE_CHECK_PASS,
                         lambda: jnp.where(cnt - count(lambda kk, base: kk == thr) < topk, 1, 0),
                         lambda: final)
        return p + 1, thr, cnt, final, any_true(jnp.logical_and(cnt != topk, final == 0))

    ncols = nblk * tq
    init = (jnp.int32(0), jnp.full((1, tq), INT_MIN, jnp.int32), jnp.full((1, tq), ncols, jnp.int32),
            jnp.zeros((1, tq), jnp.int32), ncols != topk)
    _, thr, cnt, _, _ = lax.while_loop(bit_cond, bit_step, init)
    tie = jnp.logical_and(cnt > topk, thr > KEY_NEG_INF)

    def resolve_ties():
        need = (topk - count(lambda kk, base: kk > thr)).astype(jnp.float32)
        ri = lax.broadcasted_iota(jnp.int32, (kr, kr), 0)
        ci = lax.broadcasted_iota(jnp.int32, (kr, kr), 1)
        lower = jnp.where(ci <= ri, 1.0, 0.0).astype(jnp.bfloat16)
        rowf = row.astype(jnp.float32)

        def body(j, st):
            carry, cut_acc = st
            eqs = [key_ref[j, s * kr:(s + 1) * kr, :] == thr for s in range(tq // kr)]
            within = [jnp.dot(lower, jnp.where(eq, 1.0, 0.0).astype(jnp.bfloat16), preferred_element_type=jnp.float32)
                      for eq in eqs]
            for s, (eq, pref) in enumerate(zip(eqs, within)):
                end_pos = rowf + (j * tq + s * kr + 1).astype(jnp.float32)
                cut_acc = jnp.maximum(cut_acc, jnp.where(jnp.logical_and(eq, pref + carry == need), end_pos, 0.0))
                carry = carry + pref[kr - 1:kr, :]
            return carry, cut_acc

        _, cut_acc = lax.fori_loop(0, nblk, body, (jnp.zeros((1, tq), jnp.float32), jnp.zeros((kr, tq), jnp.float32)))
        return jnp.where(tie, jnp.max(cut_acc, axis=0, keepdims=True).astype(jnp.int32), INT_MAX)

    cut = lax.cond(any_true(tie), resolve_ties, lambda: jnp.full((1, tq), INT_MAX, jnp.int32))
    thr_ref[...] = jnp.maximum(thr, KEY_NEG_INF + 1)
    cut_ref[...] = cut


def _dsa_select(hb, wi_t, bsz, L, tq):
    T = bsz * L
    nq = L // tq
    topk = min(DSA_TOPK_MAX, L // 4)
    qmap = lambda b, i: (0, b * nq + i)
    return pl.pallas_call(
        partial(_dsa_select_kernel, tq=tq, topk=topk),
        grid=(bsz, nq),
        in_specs=[pl.BlockSpec((tq, 512), lambda b, i: (b * nq + i, OD_QI // 512)),
                  pl.BlockSpec((8, tq), qmap),
                  pl.BlockSpec((L, LANES), lambda b, i: (b, OD_KI // LANES))],
        out_specs=[pl.BlockSpec((1, tq), qmap), pl.BlockSpec((1, tq), qmap)],
        out_shape=[jax.ShapeDtypeStruct((1, T), jnp.int32), jax.ShapeDtypeStruct((1, T), jnp.int32)],
        scratch_shapes=[pltpu.VMEM((nq, tq, tq), jnp.int32)],
        compiler_params=_params("parallel", "arbitrary"),
    )(hb, wi_t, hb)


def _split_head_pair(x_pair):
    lane = lax.broadcasted_iota(jnp.int32, x_pair.shape, 1)
    zero = jnp.zeros_like(x_pair)
    return jnp.where(lane < 64, x_pair, zero), jnp.where(lane >= 64, x_pair, zero)


def _dsa_attn_kernel(qidx_ref, kidx_ref, q_ref, qi_ref, wit_ref, thr_ref, cut_ref, k_ref, vt_ref, ki_ref, o_ref,
                     qm_ref, qiall_ref, m_ref, acc_ref, *, tq, tk):
    i = qidx_ref[pl.program_id(1)]
    j = kidx_ref[pl.program_id(1)]
    npair = DSA_HEADS // 2

    @pl.when(j == 0)
    def _():
        for p in range(npair):
            qa, qb = _split_head_pair(q_ref[:, LANES * p:LANES * (p + 1)])
            qm_ref[2 * p] = qa
            qm_ref[2 * p + 1] = qb
        qiall_ref[...] = _stack_index_heads(qi_ref[...])
        m_ref[...] = jnp.full(m_ref.shape, NEG_BIG, jnp.float32)
        acc_ref[...] = jnp.zeros(acc_ref.shape, jnp.float32)

    key = _index_keys_t(ki_ref[...], qiall_ref[...], wit_ref[...], i * tq, j * tk, tq)
    kpos = j * tk + lax.broadcasted_iota(jnp.int32, (tk, 1), 0)
    thr = thr_ref[...]
    sel = jnp.logical_or(key > thr, jnp.logical_and(key == thr, kpos < cut_ref[...]))
    bias = jnp.where(sel, 0.0, NEG_BIG)
    vrow = lax.broadcasted_iota(jnp.int32, (LANES, tk), 0)
    ones = jnp.ones((LANES, tk), jnp.bfloat16)
    scores = []
    for h in range(DSA_HEADS):
        kp = k_ref[:, LANES * (h // 2):LANES * (h // 2 + 1)]
        s = lax.dot_general(kp, qm_ref[h], (((1,), (1,)), ((), ())), preferred_element_type=jnp.float32)
        scores.append(s + bias)
    for h in range(DSA_HEADS):
        vt = vt_ref[LANES * (h // 2):LANES * (h // 2 + 1), :]
        v_aug = jnp.where((vrow < 64) if h % 2 == 0 else (vrow >= 64), vt, ones)
        s = scores[h]
        m_prev = m_ref[h:h + 1, :]
        m_new = jnp.maximum(m_prev, jnp.max(s, axis=0, keepdims=True))
        alpha = jnp.exp(m_prev - m_new)
        e = jnp.exp(s - m_new).astype(jnp.bfloat16)
        acc_ref[h] = alpha * acc_ref[h] + jnp.dot(v_aug, e, preferred_element_type=jnp.float32)
        m_ref[h:h + 1, :] = m_new

    @pl.when(j == ((i + 1) * tq - 1) // tk)
    def _():
        rowi = lax.broadcasted_iota(jnp.int32, (LANES, tq), 0)
        for p in range(npair):
            a = acc_ref[2 * p]
            b = acc_ref[2 * p + 1]
            out_t = jnp.where(rowi < 64, a / a[64:65, :], b / b[0:1, :])
            o_ref[:, LANES * p:LANES * (p + 1)] = out_t.T.astype(o_ref.dtype)


def _dsa_attention(hb, v_t, wi_t, thr, cut, bsz, L, tq, tk):
    T = bsz * L
    nq, nk = L // tq, L // tk
    pairs = [(i, j) for i in range(nq) for j in range(((i + 1) * tq - 1) // tk + 1)]
    qidx = jnp.asarray(np.array([p[0] for p in pairs], np.int32))
    kidx = jnp.asarray(np.array([p[1] for p in pairs], np.int32))

    def qmap(col):
        return lambda b, s, qi, ki: (b * nq + qi[s], col)

    def kmap(col):
        return lambda b, s, qi, ki: (b * nk + ki[s], col)

    qrow = lambda b, s, qi, ki: (0, b * nq + qi[s])
    grid_spec = pltpu.PrefetchScalarGridSpec(
        num_scalar_prefetch=2,
        grid=(bsz, len(pairs)),
        in_specs=[pl.BlockSpec((tq, 512), qmap(OD_Q // 512)),
                  pl.BlockSpec((tq, 512), qmap(OD_QI // 512)),
                  pl.BlockSpec((8, tq), qrow),
                  pl.BlockSpec((1, tq), qrow),
                  pl.BlockSpec((1, tq), qrow),
                  pl.BlockSpec((tk, 512), kmap(OD_K // 512)),
                  pl.BlockSpec((DSA_WIDTH, tk), lambda b, s, qi, ki: (0, b * nk + ki[s])),
                  pl.BlockSpec((tk, LANES), kmap(OD_KI // LANES))],
        out_specs=pl.BlockSpec((tq, DSA_WIDTH), qmap(0)),
        scratch_shapes=[pltpu.VMEM((DSA_HEADS, tq, LANES), jnp.bfloat16),
                        pltpu.VMEM((IDX_HEADS * tq, LANES), jnp.bfloat16),
                        pltpu.VMEM((DSA_HEADS, tq), jnp.float32),
                        pltpu.VMEM((DSA_HEADS, LANES, tq), jnp.float32)])
    return pl.pallas_call(
        partial(_dsa_attn_kernel, tq=tq, tk=tk),
        grid_spec=grid_spec,
        out_shape=jax.ShapeDtypeStruct((T, DSA_WIDTH), jnp.bfloat16),
        compiler_params=_params("parallel", "arbitrary"),
    )(qidx, kidx, hb, hb, wi_t, thr, cut, hb, v_t, hb)


def _dilated_kernel(q_ref, kp_ref, kc_ref, vp_ref, vc_ref, o_ref, lse_ref, *, tq):
    a = pl.program_id(2)
    row = lax.broadcasted_iota(jnp.int32, (tq, 2 * tq), 0)
    c = lax.broadcasted_iota(jnp.int32, (tq, 2 * tq), 1)
    first_col = jnp.where(a == 0, tq, 0)
    valid = jnp.logical_and(jnp.logical_and(c >= row, c <= row + tq), c >= first_col)
    lane = lax.broadcasted_iota(jnp.int32, (tq, LANES), 1)
    for p in range(DIL_HEADS // 2):
        sl = slice(LANES * p, LANES * (p + 1))
        kk = jnp.concatenate([kp_ref[:, sl], kc_ref[:, sl]], axis=0)
        vv = jnp.concatenate([vp_ref[:, sl], vc_ref[:, sl]], axis=0)
        outs, lses = [], []
        for qh in _split_head_pair(q_ref[:, sl]):
            s = lax.dot_general(qh, kk, (((1,), (1,)), ((), ())), preferred_element_type=jnp.float32)
            s = jnp.where(valid, s, NEG_BIG)
            m = jnp.max(s, axis=1, keepdims=True)
            e = jnp.exp(s - m)
            l = jnp.sum(e, axis=1, keepdims=True)
            outs.append(jnp.dot(e.astype(jnp.bfloat16), vv, preferred_element_type=jnp.float32) / l)
            lses.append(m + jnp.log(l))
        o_ref[:, sl] = jnp.where(lane < 64, outs[0], outs[1])
        lse_ref[:, sl] = jnp.where(lane < 64, lses[0], lses[1])


def _dil_proj_kernel(x_ref, w_ref, o_ref, *scratch, dil):
    res = jnp.dot(x_ref[...].astype(jnp.bfloat16), w_ref[...], preferred_element_type=jnp.float32)
    if dil == 1:
        o_ref[...] = res.astype(o_ref.dtype)
        return
    res_ref, = scratch
    rows, cols = res.shape[0] // dil, res.shape[1]
    for c in range(cols // LANES):
        res_ref[c] = res[:, LANES * c:LANES * (c + 1)]
    for r in range(dil):
        for c in range(cols // LANES):
            o_ref[:, cols * r + LANES * c:cols * r + LANES * (c + 1)] = (
                res_ref.at[c][pl.ds(r, rows, stride=dil), :].astype(o_ref.dtype))


def _dil_proj(x, w, dil, tm=1024):
    T, K = x.shape
    C = w.shape[1]
    tm = min(tm, T)
    return pl.pallas_call(
        partial(_dil_proj_kernel, dil=dil),
        grid=(T // tm,),
        in_specs=[pl.BlockSpec((tm, K), lambda i: (i, 0)), pl.BlockSpec((K, C), lambda i: (0, 0))],
        out_specs=pl.BlockSpec((tm // dil, dil * C), lambda i: (i, 0)),
        out_shape=jax.ShapeDtypeStruct((T // dil, dil * C), jnp.bfloat16),
        scratch_shapes=[] if dil == 1 else [pltpu.VMEM((C // LANES, tm, LANES), jnp.float32)],
        compiler_params=_params("parallel"),
    )(x, w)


def _dilated_group(pg, bsz, L, g, tq):
    window, dil = DIL_PATTERNS[g]
    assert window // dil == tq
    M = L // dil
    nb = M // tq
    ncol = DIL_COLS // DIL_WIDTH

    def cur(col):
        return lambda b, r, a: (b * nb + a, r * ncol + col)

    def prev(col):
        return lambda b, r, a: (b * nb + jnp.maximum(a - 1, 0), r * ncol + col)

    blk = (tq, DIL_WIDTH)
    out_map = lambda b, r, a: (b * nb + a, r)
    return pl.pallas_call(
        partial(_dilated_kernel, tq=tq),
        grid=(bsz, dil, nb),
        in_specs=[pl.BlockSpec(blk, cur(0)), pl.BlockSpec(blk, prev(1)), pl.BlockSpec(blk, cur(1)),
                  pl.BlockSpec(blk, prev(2)), pl.BlockSpec(blk, cur(2))],
        out_specs=[pl.BlockSpec(blk, out_map), pl.BlockSpec(blk, out_map)],
        out_shape=[jax.ShapeDtypeStruct((bsz * M, dil * DIL_WIDTH), jnp.float32)] * 2,
        compiler_params=_params("parallel", "parallel", "arbitrary"),
    )(pg, pg, pg, pg, pg)


def _dilated_combine_kernel(*refs, dils, tm):
    ng = len(dils)
    o_refs, l_refs, out_ref, scratch = refs[:ng], refs[ng:2 * ng], refs[2 * ng], refs[2 * ng + 1:]

    def token_major(ref, dil, buf):
        if dil == 1:
            return ref[...]
        nslab = DIL_WIDTH // LANES
        for r in range(dil):
            for c in range(nslab):
                buf.at[c][pl.ds(r, tm // dil, stride=dil), :] = ref[:, DIL_WIDTH * r + LANES * c:DIL_WIDTH * r + LANES * (c + 1)]
        return jnp.concatenate([buf[c] for c in range(nslab)], axis=1)

    bufs = iter(scratch)
    outs = [token_major(o_refs[g], d, None if d == 1 else next(bufs)) for g, d in enumerate(dils)]
    lses = [token_major(l_refs[g], d, None if d == 1 else next(bufs)) for g, d in enumerate(dils)]
    m = lses[0]
    for l in lses[1:]:
        m = jnp.maximum(m, l)
    ws = [jnp.exp(l - m) for l in lses]
    num, den = ws[0] * outs[0], ws[0]
    for w, o in zip(ws[1:], outs[1:]):
        num, den = num + w * o, den + w
    out_ref[...] = (num / den).astype(out_ref.dtype)


def _dilated_combine(outs, lses, dils, T, tm=512):
    tm = min(tm, T)
    specs = [pl.BlockSpec((tm // d, d * DIL_WIDTH), lambda i: (i, 0)) for d in dils]
    n_buf = 2 * sum(1 for d in dils if d != 1)
    return pl.pallas_call(
        partial(_dilated_combine_kernel, dils=tuple(dils), tm=tm),
        grid=(T // tm,),
        in_specs=specs + specs,
        out_specs=pl.BlockSpec((tm, DIL_WIDTH), lambda i: (i, 0)),
        out_shape=jax.ShapeDtypeStruct((T, DIL_WIDTH), jnp.bfloat16),
        scratch_shapes=[pltpu.VMEM((DIL_WIDTH // LANES, tm, LANES), jnp.float32)] * n_buf,
        compiler_params=_params("parallel"),
    )(*outs, *lses)


def _mm_nt_kernel(w_ref, x_ref, o_ref):
    o_ref[...] = lax.dot_general(w_ref[...], x_ref[...].astype(jnp.bfloat16), (((1,), (1,)), ((), ())),
                                 preferred_element_type=jnp.float32).astype(o_ref.dtype)


def _matmul_nt(w_t, x, out_dtype, tm=1024):
    N, K = w_t.shape
    T = x.shape[0]
    tm = min(tm, T)
    return pl.pallas_call(
        _mm_nt_kernel,
        grid=(T // tm,),
        in_specs=[pl.BlockSpec((N, K), lambda i: (0, 0)), pl.BlockSpec((tm, K), lambda i: (i, 0))],
        out_specs=pl.BlockSpec((N, tm), lambda i: (0, i)),
        out_shape=jax.ShapeDtypeStruct((N, T), out_dtype),
        compiler_params=_params("parallel"),
    )(w_t, x)


def _odd_weights(w_in):
    D = w_in.shape[0]
    q, k, v, qi, ki, wi, dil = _split(w_in, OD_SIZES)
    zeros = lambda n: jnp.zeros((D, n), w_in.dtype)
    qi_exp = jnp.concatenate(
        [jnp.concatenate([qi[:, IDX_DIM * h:IDX_DIM * (h + 1)] * IDX_DIM ** -0.5, zeros(LANES - IDX_DIM)], axis=1)
         for h in range(IDX_HEADS)], axis=1)
    dil = dil.reshape(D, 3, DIL_GROUPS, DIL_WIDTH)
    w_dil = [jnp.concatenate([dil[:, 0, g] * DIL_HEAD_DIM ** -0.5, dil[:, 1, g], dil[:, 2, g]], axis=1).astype(jnp.bfloat16)
             for g in range(DIL_GROUPS)]
    wb = jnp.concatenate([q * DSA_HEAD_DIM ** -0.5, k, qi_exp, ki, zeros(LANES - IDX_DIM),
                          zeros(OD_COLS_PADDED - OD_KI - LANES)], axis=1).astype(jnp.bfloat16)
    wv_t = v.T.astype(jnp.bfloat16)
    wwi_t = jnp.concatenate([wi * IDX_HEADS ** -0.5, zeros(8 - IDX_HEADS)], axis=1).T.astype(jnp.bfloat16)
    return wb, w_dil, wv_t, wwi_t


def _odd_layer_mix(x2, w_in, bsz, L):
    wb, w_dil, wv_t, wwi_t = _odd_weights(w_in)
    hb = _matmul(x2, wb, jnp.bfloat16, tm=1024, tn=1024)
    v_t = _matmul_nt(wv_t, x2, jnp.bfloat16)
    wi_t = _matmul_nt(wwi_t, x2, jnp.float32)
    tq, tk = min(DSA_TQ, L), min(DSA_TK, L)
    thr, cut = _dsa_select(hb, wi_t, bsz, L, tq)
    o_c = _dsa_attention(hb, v_t, wi_t, thr, cut, bsz, L, tq, tk)
    dils = [d for _, d in DIL_PATTERNS]
    groups = [_dilated_group(_dil_proj(x2, w_dil[g], dils[g]), bsz, L, g, DIL_PATTERNS[g][0] // dils[g])
              for g in range(DIL_GROUPS)]
    o_d = _dilated_combine([o for o, _ in groups], [l for _, l in groups], dils, bsz * L)
    return o_c, o_d


EV_Q, EV_K, EV_V, EV_R, EV_G = 0, 256, 512, 1024, 1536
EV_COLS_PADDED = 2048
GLA_ROWS = 512
S5_CHUNK = 64


def _gla_kernel(q_ref, k_ref, v_ref, r_ref, g_ref, wg_ref, bg_ref, ng_ref, o_ref, st_ref, *, rows):
    C = GLA_CHUNK

    @pl.when(pl.program_id(1) == 0)
    def _():
        st_ref[...] = jnp.zeros(st_ref.shape, jnp.float32)

    ri = lax.broadcasted_iota(jnp.int32, (C, C), 0)
    ci = lax.broadcasted_iota(jnp.int32, (C, C), 1)
    causal = ci <= ri
    tril = jnp.where(causal, 1.0, 0.0).astype(jnp.float32)
    wg = wg_ref[...].astype(jnp.bfloat16)
    bg = bg_ref[...]
    ng = ng_ref[...]

    def chunk(c, carry):
        r0 = pl.multiple_of(c * C, C)
        rs = pl.ds(r0, C)
        logit = jnp.dot(g_ref[rs, :].astype(jnp.bfloat16), wg, preferred_element_type=jnp.float32) + bg
        log_a = jax.nn.log_sigmoid(logit) / GLA_GATE_TAU
        bcum = jnp.dot(tril, log_a, precision=lax.Precision.HIGHEST, preferred_element_type=jnp.float32)
        b_last = bcum[C - 1:C, :]
        q_t = (q_ref[rs, :] * jnp.exp(bcum)).astype(jnp.bfloat16)
        k_t = (k_ref[rs, :] * jnp.exp(-bcum)).astype(jnp.bfloat16)
        k_end = (k_ref[rs, :] * jnp.exp(b_last - bcum)).astype(jnp.bfloat16)
        dec = jnp.exp(b_last)
        for p in range(GLA_HEADS // 2):
            sl = slice(LANES * p, LANES * (p + 1))
            q_halves = _split_head_pair(q_t[:, sl])
            ke_halves = _split_head_pair(k_end[:, sl])
            for half in range(2):
                h = 2 * p + half
                hs = slice(GLA_DV * h, GLA_DV * (h + 1))
                qm = q_halves[half]
                att = lax.dot_general(qm, k_t[:, sl], (((1,), (1,)), ((), ())), preferred_element_type=jnp.float32)
                att = jnp.where(causal, att, 0.0).astype(jnp.bfloat16)
                v_h = v_ref[rs, hs].astype(jnp.bfloat16)
                st = st_ref[h]
                o = jnp.dot(att, v_h, preferred_element_type=jnp.float32)
                o = o + lax.dot_general(qm, st.astype(jnp.bfloat16), (((1,), (1,)), ((), ())),
                                        preferred_element_type=jnp.float32)
                kv_t = lax.dot_general(v_h, ke_halves[half], (((0,), (0,)), ((), ())),
                                       preferred_element_type=jnp.float32)
                st_ref[h] = st * dec[:, sl] + kv_t
                o = o * lax.rsqrt(jnp.mean(o * o, axis=-1, keepdims=True) + LN_EPS) * ng
                o = o * jax.nn.silu(r_ref[rs, hs])
                o_ref[rs, hs] = o.astype(o_ref.dtype)
        return carry

    lax.fori_loop(0, rows // C, chunk, 0)


def _gla(hf, w_gate2, b_gate2, norm_g, bsz, L):
    T = bsz * L
    rows = min(GLA_ROWS, L)
    nb = L // rows
    dkw = GLA_HEADS * GLA_DK
    dvw = GLA_HEADS * GLA_DV
    wg = jnp.pad(w_gate2, ((0, LANES - GLA_GATE_RANK), (0, 0)))

    def rmap(col):
        return lambda b, i: (b * nb + i, col)

    const = lambda b, i: (0, 0)
    return pl.pallas_call(
        partial(_gla_kernel, rows=rows),
        grid=(bsz, nb),
        in_specs=[pl.BlockSpec((rows, dkw), rmap(EV_Q // dkw)),
                  pl.BlockSpec((rows, dkw), rmap(EV_K // dkw)),
                  pl.BlockSpec((rows, dvw), rmap(EV_V // dvw)),
                  pl.BlockSpec((rows, dvw), rmap(EV_R // dvw)),
                  pl.BlockSpec((rows, LANES), rmap(EV_G // LANES)),
                  pl.BlockSpec((LANES, dkw), const),
                  pl.BlockSpec((1, dkw), const),
                  pl.BlockSpec((1, GLA_DV), const)],
        out_specs=pl.BlockSpec((rows, dvw), rmap(0)),
        out_shape=jax.ShapeDtypeStruct((T, dvw), jnp.bfloat16),
        scratch_shapes=[pltpu.VMEM((GLA_HEADS, GLA_DV, LANES), jnp.float32)],
        compiler_params=_params("parallel", "arbitrary"),
    )(hf, hf, hf, hf, hf, wg, b_gate2.reshape(1, dkw), norm_g.reshape(1, GLA_DV))


def _s5_tables(a_re, a_im, log_dt, b_re, b_im, c_re, c_im, d_skip):
    f32 = jnp.float32
    Cs, G, P, N = S5_CHUNK, S5_GROUPS, S5_STATE, S5_GROUP
    lam_re = jnp.minimum(a_re.astype(f32), S5_MAX_RE)
    lam_im = a_im.astype(f32)
    dt = jnp.exp(log_dt.astype(f32))[:, None]
    mag = jnp.exp(lam_re * dt)
    ab_re = mag * jnp.cos(lam_im * dt)
    ab_im = mag * jnp.sin(lam_im * dt)
    inv = 1.0 / (lam_re * lam_re + lam_im * lam_im)
    z_re = ((ab_re - 1.0) * lam_re + ab_im * lam_im) * inv
    z_im = (ab_im * lam_re - (ab_re - 1.0) * lam_im) * inv
    br, bi = b_re.astype(f32), b_im.astype(f32)
    bb_re = z_re[..., None] * br - z_im[..., None] * bi
    bb_im = z_re[..., None] * bi + z_im[..., None] * br
    kk = jnp.arange(Cs + 1, dtype=f32)[:, None, None]
    pmag = jnp.exp(kk * (lam_re * dt))
    pw_re = pmag * jnp.cos(kk * (lam_im * dt))
    pw_im = pmag * jnp.sin(kk * (lam_im * dt))
    cr, ci = c_re.astype(f32), c_im.astype(f32)
    ca_re = cr[None] * pw_re[:, :, None, :] - ci[None] * pw_im[:, :, None, :]
    ca_im = cr[None] * pw_im[:, :, None, :] + ci[None] * pw_re[:, :, None, :]
    hi = lax.Precision.HIGHEST
    kern = (jnp.einsum('kgnp,gpm->kgnm', ca_re[:Cs], bb_re, precision=hi)
            - jnp.einsum('kgnp,gpm->kgnm', ca_im[:Cs], bb_im, precision=hi))
    jj = jnp.arange(Cs)[:, None]
    ii = jnp.arange(Cs)[None, :]
    tz = jnp.where((ii >= jj)[:, :, None, None, None], kern[jnp.maximum(ii - jj, 0)], 0.0)
    tz = jnp.transpose(tz, (2, 4, 0, 3, 1)).reshape(G, N * Cs, N * Cs)
    rev_re, rev_im = pw_re[Cs - 1::-1][:Cs], pw_im[Cs - 1::-1][:Cs]
    ws_re = rev_re[..., None] * bb_re[None] - rev_im[..., None] * bb_im[None]
    ws_im = rev_re[..., None] * bb_im[None] + rev_im[..., None] * bb_re[None]
    to_ws = lambda w: jnp.pad(jnp.transpose(w, (1, 3, 0, 2)).reshape(G, N * Cs, P), ((0, 0), (0, 0), (0, LANES - P)))
    to_wo = lambda w: jnp.pad(jnp.transpose(w, (1, 3, 2, 0)).reshape(G, P, N * Cs), ((0, 0), (0, LANES - P), (0, 0)))
    a_cs = jnp.stack([jnp.pad(pw_re[Cs], ((0, 0), (0, LANES - P))), jnp.pad(pw_im[Cs], ((0, 0), (0, LANES - P)))], axis=1)
    d_exp = jnp.repeat(d_skip.astype(f32).reshape(G, 1, N), Cs, axis=2)
    bf = jnp.bfloat16
    return (tz.astype(bf), to_ws(ws_re).astype(bf), to_ws(ws_im).astype(bf),
            to_wo(ca_re[1:]).astype(bf), to_wo(-ca_im[1:]).astype(bf), a_cs, d_exp)


def _s5_kernel(u_ref, tz_ref, wsr_ref, wsi_ref, wor_ref, woi_ref, acs_ref, d_ref, y_ref, xr_ref, xi_ref,
               *, nchunk, nbatch):
    u32 = jnp.concatenate([u_ref[m] for m in range(S5_GROUP)], axis=1)
    u = u32.astype(jnp.bfloat16)
    xr_ref[...] = jnp.dot(u, wsr_ref[0], preferred_element_type=jnp.float32)
    xi_ref[...] = jnp.dot(u, wsi_ref[0], preferred_element_type=jnp.float32)
    ar = acs_ref[0, 0:1, :]
    ai = acs_ref[0, 1:2, :]

    def step(c, carry):
        new = []
        for b in range(nbatch):
            sr, si = carry[2 * b], carry[2 * b + 1]
            row = pl.ds(b * nchunk + c, 1)
            lr, li = xr_ref[row, :], xi_ref[row, :]
            xr_ref[row, :] = sr
            xi_ref[row, :] = si
            new += [ar * sr - ai * si + lr, ar * si + ai * sr + li]
        return tuple(new)

    zero = jnp.zeros((1, LANES), jnp.float32)
    lax.fori_loop(0, nchunk, step, (zero,) * (2 * nbatch))
    y = jnp.dot(u, tz_ref[0], preferred_element_type=jnp.float32)
    y = y + jnp.dot(xr_ref[...].astype(jnp.bfloat16), wor_ref[0], preferred_element_type=jnp.float32)
    y = y + jnp.dot(xi_ref[...].astype(jnp.bfloat16), woi_ref[0], preferred_element_type=jnp.float32)
    y = jax.nn.gelu(y + d_ref[0] * u32)
    for n in range(S5_GROUP):
        y_ref[n] = y[:, S5_CHUNK * n:S5_CHUNK * (n + 1)]


def _s5_scan(u_t, tables, bsz, nchunk):
    width, R, Cs = u_t.shape
    G, N = S5_GROUPS, S5_GROUP
    W = N * Cs
    tz, wsr, wsi, wor, woi, a_cs, d_exp = tables
    gmap = lambda g: (g, 0, 0)
    return pl.pallas_call(
        partial(_s5_kernel, nchunk=nchunk, nbatch=bsz),
        grid=(G,),
        in_specs=[pl.BlockSpec((N, R, Cs), gmap), pl.BlockSpec((1, W, W), gmap),
                  pl.BlockSpec((1, W, LANES), gmap), pl.BlockSpec((1, W, LANES), gmap),
                  pl.BlockSpec((1, LANES, W), gmap), pl.BlockSpec((1, LANES, W), gmap),
                  pl.BlockSpec((1, 2, LANES), gmap), pl.BlockSpec((1, 1, W), gmap)],
        out_specs=pl.BlockSpec((N, R, Cs), gmap),
        out_shape=jax.ShapeDtypeStruct((width, R, Cs), jnp.float32),
        scratch_shapes=[pltpu.VMEM((R, LANES), jnp.float32), pltpu.VMEM((R, LANES), jnp.float32)],
        compiler_params=_params("parallel"),
    )(u_t, tz, wsr, wsi, wor, woi, a_cs, d_exp)


def _glu_kernel(yt_ref, wt_ref, b_ref, o_ref):
    y = yt_ref[...]
    gate = jnp.dot(wt_ref[...], y.astype(jnp.bfloat16), preferred_element_type=jnp.float32) + b_ref[...]
    o_ref[...] = (y * jax.nn.sigmoid(gate)).T.astype(o_ref.dtype)


def _glu(y_t, w_glu, b_glu, tm=1024):
    W, T = y_t.shape
    tm = min(tm, T)
    return pl.pallas_call(
        _glu_kernel,
        grid=(T // tm,),
        in_specs=[pl.BlockSpec((W, tm), lambda i: (0, i)),
                  pl.BlockSpec((W, W), lambda i: (0, 0)),
                  pl.BlockSpec((W, 1), lambda i: (0, 0))],
        out_specs=pl.BlockSpec((tm, W), lambda i: (i, 0)),
        out_shape=jax.ShapeDtypeStruct((T, W), jnp.bfloat16),
        compiler_params=_params("parallel"),
    )(y_t, w_glu.T.astype(jnp.bfloat16), b_glu.reshape(W, 1))


def _s5(u_t, s5_params, w_glu, b_glu, bsz, L):
    T = bsz * L
    nchunk = L // S5_CHUNK
    y_t = _s5_scan(u_t.reshape(S5_WIDTH, bsz * nchunk, S5_CHUNK), _s5_tables(*s5_params), bsz, nchunk)
    return _glu(y_t.reshape(S5_WIDTH, T), w_glu, b_glu)


def _even_weights(w_in):
    D = w_in.shape[0]
    q, k, v, r, g_lr, u = _split(w_in, EV_SIZES)
    pad = jnp.zeros((D, EV_COLS_PADDED - EV_G - GLA_GATE_RANK), w_in.dtype)
    wb = jnp.concatenate([q * GLA_DK ** -0.5, k, v, r, g_lr, pad], axis=1).astype(jnp.bfloat16)
    return wb, u.T.astype(jnp.bfloat16)


def _even_layer_mix(x2, w_in, w_gate2, b_gate2, norm_g, s5_params, w_glu, b_glu, bsz, L):
    wb, wu_t = _even_weights(w_in)
    hf = _matmul(x2, wb, jnp.float32, tm=1024, tn=512)
    u_t = _matmul_nt(wu_t, x2, jnp.float32)
    o_a = _gla(hf, w_gate2, b_gate2, norm_g, bsz, L)
    o_b = _s5(u_t, s5_params, w_glu, b_glu, bsz, L)
    return o_a, o_b


MOE_TILE = 512
ROW_TILE = 8
DMA_UNROLL = 8
MOE_VMEM_LIMIT_BYTES = 56 * 1024 * 1024


def _router_kernel(x_ref, w_ref, b_ref, e_ref, g_ref):
    logits = jnp.dot(x_ref[...].astype(jnp.bfloat16), w_ref[...], preferred_element_type=jnp.float32) + b_ref[...]
    tm = logits.shape[0]
    lane = lax.broadcasted_iota(jnp.int32, (tm, LANES), 1)
    logits = jnp.where(lane < N_EXPERTS, logits, NEG_BIG)
    tops, idxs = [], []
    for _ in range(TOP_K):
        m = jnp.max(logits, axis=1, keepdims=True)
        idx = jnp.min(jnp.where(logits == m, lane, LANES), axis=1, keepdims=True)
        tops.append(m)
        idxs.append(idx)
        logits = jnp.where(lane == idx, NEG_BIG, logits)
    exps = [jnp.exp(t - tops[0]) for t in tops]
    denom = exps[0]
    for e in exps[1:]:
        denom = denom + e
    lane4 = lax.broadcasted_iota(jnp.int32, (tm, TOP_K), 1)
    e_out = jnp.zeros((tm, TOP_K), jnp.int32)
    g_out = jnp.zeros((tm, TOP_K), jnp.float32)
    for k in range(TOP_K):
        e_out = jnp.where(lane4 == k, idxs[k], e_out)
        g_out = jnp.where(lane4 == k, exps[k] / denom, g_out)
    e_ref[...] = e_out
    g_ref[...] = g_out


def _router(x, router_w, router_b, tm=512):
    T, D = x.shape
    tm = min(tm, T)
    w = jnp.pad(router_w, ((0, 0), (0, LANES - N_EXPERTS))).astype(jnp.bfloat16)
    b = jnp.pad(router_b, (0, LANES - N_EXPERTS)).reshape(1, LANES)
    return pl.pallas_call(
        _router_kernel,
        grid=(T // tm,),
        in_specs=[pl.BlockSpec((tm, D), lambda i: (i, 0)),
                  pl.BlockSpec((D, LANES), lambda i: (0, 0)),
                  pl.BlockSpec((1, LANES), lambda i: (0, 0))],
        out_specs=[pl.BlockSpec((tm, TOP_K), lambda i: (i, 0)), pl.BlockSpec((tm, TOP_K), lambda i: (i, 0))],
        out_shape=[jax.ShapeDtypeStruct((T, TOP_K), jnp.int32), jax.ShapeDtypeStruct((T, TOP_K), jnp.float32)],
        compiler_params=_params("parallel"),
    )(x, w, b)


def _moe_rank_kernel(e_ref, rank_ref, count_ref, carry_ref):
    @pl.when(pl.program_id(0) == 0)
    def _():
        carry_ref[...] = jnp.zeros(carry_ref.shape, jnp.float32)

    e = e_ref[...]
    tm = e.shape[0]
    lane = lax.broadcasted_iota(jnp.int32, (tm, LANES), 1)
    onehot = jnp.zeros((tm, LANES), jnp.float32)
    for k in range(TOP_K):
        onehot = onehot + jnp.where(lane == e[:, k:k + 1], 1.0, 0.0)
    ri = lax.broadcasted_iota(jnp.int32, (tm, tm), 0)
    ci = lax.broadcasted_iota(jnp.int32, (tm, tm), 1)
    strict_lower = jnp.where(ci < ri, 1.0, 0.0).astype(jnp.bfloat16)
    before = jnp.dot(strict_lower, onehot.astype(jnp.bfloat16), preferred_element_type=jnp.float32) + carry_ref[...]
    lane4 = lax.broadcasted_iota(jnp.int32, (tm, TOP_K), 1)
    rank = jnp.zeros((tm, TOP_K), jnp.int32)
    for k in range(TOP_K):
        r_k = jnp.sum(jnp.where(lane == e[:, k:k + 1], before, 0.0), axis=1, keepdims=True).astype(jnp.int32)
        rank = jnp.where(lane4 == k, r_k, rank)
    rank_ref[...] = rank
    carry_ref[...] = carry_ref[...] + jnp.sum(onehot, axis=0, keepdims=True)
    count_ref[...] = carry_ref[...]


def _moe_rank(top_e, tm=256):
    T = top_e.shape[0]
    tm = min(tm, T)
    return pl.pallas_call(
        _moe_rank_kernel,
        grid=(T // tm,),
        in_specs=[pl.BlockSpec((tm, TOP_K), lambda i: (i, 0))],
        out_specs=[pl.BlockSpec((tm, TOP_K), lambda i: (i, 0)), pl.BlockSpec((1, LANES), lambda i: (0, 0))],
        out_shape=[jax.ShapeDtypeStruct((T, TOP_K), jnp.int32), jax.ShapeDtypeStruct((1, LANES), jnp.float32)],
        scratch_shapes=[pltpu.VMEM((1, LANES), jnp.float32)],
        compiler_params=_params("arbitrary"),
    )(top_e)


def _to_token_tiles(x, dst_ref, rows):
    for c in range(ROW_TILE):
        dst_ref[pl.ds(c, rows, stride=ROW_TILE), :] = x[:, LANES * c:LANES * (c + 1)]


def _from_token_tiles(src_ref, rows):
    return jnp.concatenate([src_ref[pl.ds(c, rows, stride=ROW_TILE), :] for c in range(ROW_TILE)], axis=1)


def _moe_dispatch_kernel(dest_ref, x_ref, init_ref, xbuf_ref, xs_ref, sem, *, tm):
    del init_ref
    i = pl.program_id(0)
    n = pl.num_programs(0)
    slot = i % 2

    def row_copy(s, r, d):
        return pltpu.make_async_copy(xs_ref.at[s, pl.ds(pl.multiple_of(r * ROW_TILE, ROW_TILE), ROW_TILE), :],
                                     xbuf_ref.at[pl.ds(pl.multiple_of(d * ROW_TILE, ROW_TILE), ROW_TILE), :], sem.at[s])

    def drain(s):
        def body(a, c):
            row_copy(s, 0, 0).wait()
            return c
        lax.fori_loop(0, tm * TOP_K, body, 0, unroll=DMA_UNROLL)

    @pl.when(i >= 2)
    def _():
        drain(slot)

    _to_token_tiles(x_ref[...], xs_ref.at[slot], tm)

    def body(r, c):
        for k in range(TOP_K):
            row_copy(slot, r, dest_ref[r * TOP_K + k]).start()
        return c

    lax.fori_loop(0, tm, body, 0, unroll=DMA_UNROLL)

    @pl.when(i == n - 1)
    def _():
        drain(slot)

        @pl.when(n > 1)
        def _():
            drain(1 - slot)


def _moe_dispatch(x, dest_flat, n_rows, tm=256):
    T, D = x.shape
    tm = min(tm, T)
    assert D == ROW_TILE * LANES
    init = jnp.zeros((n_rows * ROW_TILE, LANES), jnp.float32)
    return pl.pallas_call(
        partial(_moe_dispatch_kernel, tm=tm),
        grid=(T // tm,),
        in_specs=[pl.BlockSpec((tm * TOP_K,), lambda i: (i,), memory_space=pltpu.SMEM),
                  pl.BlockSpec((tm, D), lambda i: (i, 0)),
                  pl.BlockSpec(memory_space=pl.ANY)],
        out_specs=pl.BlockSpec(memory_space=pl.ANY),
        out_shape=jax.ShapeDtypeStruct((n_rows * ROW_TILE, LANES), jnp.float32),
        scratch_shapes=[pltpu.VMEM((2, tm * ROW_TILE, LANES), jnp.float32), pltpu.SemaphoreType.DMA((2,))],
        input_output_aliases={2: 0},
        compiler_params=_params("arbitrary"),
    )(dest_flat, x, init)


def _moe_ffn_kernel(te_ref, nt_ref, x_ref, w1_ref, b1_ref, w2_ref, b2_ref, y_ref, w1b_ref, w2b_ref):
    i = pl.program_id(0)
    prev = te_ref[jnp.maximum(i - 1, 0)]

    @pl.when(jnp.logical_or(i == 0, te_ref[i] != prev))
    def _():
        w1b_ref[...] = w1_ref[0].astype(jnp.bfloat16)
        w2b_ref[...] = w2_ref[0].astype(jnp.bfloat16)

    @pl.when(i < nt_ref[0])
    def _():
        x = _from_token_tiles(x_ref, MOE_TILE).astype(jnp.bfloat16)
        h = jnp.dot(x, w1b_ref[...], preferred_element_type=jnp.float32) + b1_ref[0]
        glu = jnp.minimum(h[:, :D_FF], SWIGLU_LIMIT)
        lin = jnp.clip(h[:, D_FF:], -SWIGLU_LIMIT, SWIGLU_LIMIT)
        act = glu * jax.nn.sigmoid(SWIGLU_ALPHA * glu) * (lin + 1.0)
        y = jnp.dot(act.astype(jnp.bfloat16), w2b_ref[...], preferred_element_type=jnp.float32) + b2_ref[0]
        _to_token_tiles(y, y_ref, MOE_TILE)

    @pl.when(i >= nt_ref[0])
    def _():
        y_ref[...] = jnp.zeros(y_ref.shape, y_ref.dtype)


def _moe_expert_ffn(xbuf, tile_expert, n_used, w1, b1, w2, b2):
    D, F2 = w1.shape[1], w1.shape[2]
    P = xbuf.shape[0] // ROW_TILE
    n_tiles = P // MOE_TILE
    blk = (MOE_TILE * ROW_TILE, LANES)
    grid_spec = pltpu.PrefetchScalarGridSpec(
        num_scalar_prefetch=2,
        grid=(n_tiles,),
        in_specs=[pl.BlockSpec(blk, lambda i, te, nt: (jnp.minimum(i, nt[0] - 1), 0)),
                  pl.BlockSpec((1, D, F2), lambda i, te, nt: (te[i], 0, 0)),
                  pl.BlockSpec((1, 1, F2), lambda i, te, nt: (te[i], 0, 0)),
                  pl.BlockSpec((1, D_FF, D), lambda i, te, nt: (te[i], 0, 0)),
                  pl.BlockSpec((1, 1, D), lambda i, te, nt: (te[i], 0, 0))],
        out_specs=pl.BlockSpec(blk, lambda i, te, nt: (i, 0)),
        scratch_shapes=[pltpu.VMEM((D, F2), jnp.bfloat16), pltpu.VMEM((D_FF, D), jnp.bfloat16)],
    )
    return pl.pallas_call(
        _moe_ffn_kernel,
        grid_spec=grid_spec,
        out_shape=jax.ShapeDtypeStruct(xbuf.shape, jnp.float32),
        compiler_params=pltpu.CompilerParams(dimension_semantics=("arbitrary",),
                                             vmem_limit_bytes=MOE_VMEM_LIMIT_BYTES),
    )(tile_expert, n_used, xbuf, w1, b1.reshape(N_EXPERTS, 1, F2), w2, b2.reshape(N_EXPERTS, 1, D))


def _moe_combine_kernel(dest_ref, gate_ref, x_ref, g_ref, b_ref, ybuf_ref, o_ref, rows_ref, sem, *, tm):
    def row_copy(k, r, d):
        return pltpu.make_async_copy(ybuf_ref.at[pl.ds(pl.multiple_of(d * ROW_TILE, ROW_TILE), ROW_TILE), :],
                                     rows_ref.at[k, pl.ds(pl.multiple_of(r * ROW_TILE, ROW_TILE), ROW_TILE), :], sem.at[0])

    def start(r, c):
        for k in range(TOP_K):
            row_copy(k, r, dest_ref[r * TOP_K + k]).start()
        return c

    lax.fori_loop(0, tm, start, 0, unroll=DMA_UNROLL)

    def wait(a, c):
        row_copy(0, 0, 0).wait()
        return c

    lax.fori_loop(0, tm * TOP_K, wait, 0, unroll=DMA_UNROLL)
    gate = gate_ref[...]
    ffn = gate[:, 0:1] * _from_token_tiles(rows_ref.at[0], tm)
    for k in range(1, TOP_K):
        ffn = ffn + gate[:, k:k + 1] * _from_token_tiles(rows_ref.at[k], tm)
    o_ref[...] = _layer_norm_rows(DEEPNORM_ALPHA * x_ref[...] + ffn, g_ref[...], b_ref[...])


def _moe_combine(ybuf, dest_flat, gate, x, g, b, tm=256):
    T, D = x.shape
    tm = min(tm, T)
    return pl.pallas_call(
        partial(_moe_combine_kernel, tm=tm),
        grid=(T // tm,),
        in_specs=[pl.BlockSpec((tm * TOP_K,), lambda i: (i,), memory_space=pltpu.SMEM),
                  pl.BlockSpec((tm, TOP_K), lambda i: (i, 0)),
                  pl.BlockSpec((tm, D), lambda i: (i, 0)),
                  pl.BlockSpec((1, D), lambda i: (0, 0)),
                  pl.BlockSpec((1, D), lambda i: (0, 0)),
                  pl.BlockSpec(memory_space=pl.ANY)],
        out_specs=pl.BlockSpec((tm, D), lambda i: (i, 0)),
        out_shape=jax.ShapeDtypeStruct((T, D), jnp.float32),
        scratch_shapes=[pltpu.VMEM((TOP_K, tm * ROW_TILE, LANES), jnp.float32), pltpu.SemaphoreType.DMA((1,))],
        compiler_params=_params("arbitrary"),
    )(dest_flat, gate, x, g.reshape(1, D), b.reshape(1, D), ybuf)


def _moe_layer(x, router_w, router_b, w1, b1, w2, b2, ln_g, ln_b):
    T, D = x.shape
    A = T * TOP_K
    n_tiles = -(-(A + N_EXPERTS * (MOE_TILE - 1)) // MOE_TILE)
    top_e, gate = _router(x, router_w, router_b)
    rank, counts = _moe_rank(top_e)
    counts = counts[0, :N_EXPERTS].astype(jnp.int32)
    padded = (counts + MOE_TILE - 1) // MOE_TILE * MOE_TILE
    pend = jnp.cumsum(padded)
    pstart = pend - padded
    dest = (pstart[top_e] + rank).reshape(A)
    tile_expert = jnp.minimum(jnp.searchsorted(pend, jnp.arange(n_tiles) * MOE_TILE, side='right'),
                              N_EXPERTS - 1).astype(jnp.int32)
    n_used = (pend[-1:] // MOE_TILE).astype(jnp.int32)
    xbuf = _moe_dispatch(x, dest, n_tiles * MOE_TILE)
    ybuf = _moe_expert_ffn(xbuf, tile_expert, n_used, w1, b1, w2, b2)
    return _moe_combine(ybuf, dest, gate, x, ln_g, ln_b)


def kernel(x, ev_w_in, gla_w_gate2, gla_b_gate2, gla_norm_g, s5_a_re, s5_a_im, s5_log_dt, s5_b_re, s5_b_im,
           s5_c_re, s5_c_im, s5_d, s5_w_glu, s5_b_glu, ev_w_out, od_w_in, od_w_out, ln1_g, ln1_b, ln2_g, ln2_b,
           router_w, router_b, moe_w1, moe_b1, moe_w2, moe_b2):
    bsz, L, D = x.shape
    T = bsz * L
    x = x.reshape(T, D)
    for layer in range(DEPTH):
        j = layer // 2
        if layer % 2 == 0:
            s5_params = (s5_a_re[j], s5_a_im[j], s5_log_dt[j], s5_b_re[j], s5_b_im[j], s5_c_re[j], s5_c_im[j], s5_d[j])
            o_1, o_2 = _even_layer_mix(x, ev_w_in[j], gla_w_gate2[j], gla_b_gate2[j], gla_norm_g[j], s5_params,
                                       s5_w_glu[j], s5_b_glu[j], bsz, L)
            w_out = ev_w_out[j]
        else:
            o_1, o_2 = _odd_layer_mix(x, od_w_in[j], bsz, L)
            w_out = od_w_out[j]
        x = _matmul2_res_ln(o_1, o_2, w_out, x, ln1_g[layer], ln1_b[layer])
        x = _moe_layer(x, router_w[layer], router_b[layer], moe_w1[layer], moe_b1[layer], moe_w2[layer],
                       moe_b2[layer], ln2_g[layer], ln2_b[layer])
    return x.reshape(bsz, L, D)
```

```python
import math
from functools import partial

import numpy as np
import jax
import jax.numpy as jnp
from jax import lax
from jax.experimental import pallas as pl
from jax.experimental.pallas import tpu as pltpu

D_MODEL = 1024
DEPTH = 4
DEEPNORM_ALPHA = (2.0 * DEPTH) ** 0.25
LN_EPS = 1e-5
MIX_WIDTH = D_MODEL

GLA_HEADS = 4
GLA_DV = MIX_WIDTH // 2 // GLA_HEADS
GLA_DK = GLA_DV // 2
GLA_GATE_RANK = 16
GLA_GATE_TAU = 16.0
GLA_CHUNK = 64

S5_WIDTH = MIX_WIDTH // 2
S5_GROUP = 16
S5_GROUPS = S5_WIDTH // S5_GROUP
S5_STATE = 64
S5_MAX_RE = -1e-4

EV_SIZES = (GLA_HEADS * GLA_DK, GLA_HEADS * GLA_DK, GLA_HEADS * GLA_DV, GLA_HEADS * GLA_DV, GLA_GATE_RANK, S5_WIDTH)

DSA_HEADS = 8
DSA_HEAD_DIM = 64
DSA_WIDTH = DSA_HEADS * DSA_HEAD_DIM
IDX_HEADS = 4
IDX_DIM = 64
DSA_TOPK_MAX = 256

DIL_PATTERNS = ((128, 1), (512, 4), (2048, 16))
DIL_GROUPS = len(DIL_PATTERNS)
DIL_HEADS = 8
DIL_HEAD_DIM = 64
DIL_WIDTH = DIL_HEADS * DIL_HEAD_DIM

OD_SIZES = (DSA_WIDTH, DSA_WIDTH, DSA_WIDTH, IDX_HEADS * IDX_DIM, IDX_DIM, IDX_HEADS, 3 * DIL_GROUPS * DIL_WIDTH)

N_EXPERTS = 32
TOP_K = 4
D_FF = D_MODEL
SWIGLU_ALPHA = 1.702
SWIGLU_LIMIT = 7.0
MOE_BLOCK = 512

LANES = 128
VMEM_LIMIT_BYTES = 48 * 1024 * 1024
NEG_BIG = -1e30
INT_MIN = -2 ** 31
INT_MAX = 2 ** 31 - 1
KEY_NEG_INF = -0x7F800000

OD_Q, OD_K, OD_QI, OD_KI = 0, 512, 1024, 1536
OD_COLS_PADDED = 2048
DIL_COLS = 3 * DIL_WIDTH


def _split(h, sizes):
    return jnp.split(h, [int(i) for i in np.cumsum(sizes)[:-1]], axis=-1)


def _params(*sem):
    return pltpu.CompilerParams(dimension_semantics=sem, vmem_limit_bytes=VMEM_LIMIT_BYTES)


def _mm_kernel(x_ref, w_ref, o_ref):
    o_ref[...] = jnp.dot(x_ref[...].astype(jnp.bfloat16), w_ref[...].astype(jnp.bfloat16),
                         preferred_element_type=jnp.float32).astype(o_ref.dtype)


def _matmul(x, w, out_dtype=jnp.float32, tm=512, tn=512):
    T, K = x.shape
    N = w.shape[1]
    tm, tn = min(tm, T), min(tn, N)
    assert T % tm == 0 and N % tn == 0
    return pl.pallas_call(
        _mm_kernel,
        grid=(T // tm, N // tn),
        in_specs=[pl.BlockSpec((tm, K), lambda i, j: (i, 0)),
                  pl.BlockSpec((K, tn), lambda i, j: (0, j))],
        out_specs=pl.BlockSpec((tm, tn), lambda i, j: (i, j)),
        out_shape=jax.ShapeDtypeStruct((T, N), out_dtype),
        compiler_params=_params("parallel", "arbitrary"),
    )(x, w)


def _layer_norm_rows(z, g, b):
    mu = jnp.mean(z, axis=-1, keepdims=True)
    zc = z - mu
    var = jnp.mean(zc * zc, axis=-1, keepdims=True)
    return zc * lax.rsqrt(var + LN_EPS) * g + b


def _mm_res_ln_kernel(a_ref, w_ref, x_ref, g_ref, b_ref, o_ref):
    mix = jnp.dot(a_ref[...].astype(jnp.bfloat16), w_ref[...].astype(jnp.bfloat16),
                  preferred_element_type=jnp.float32)
    o_ref[...] = _layer_norm_rows(DEEPNORM_ALPHA * x_ref[...] + mix, g_ref[...], b_ref[...])


def _matmul_res_ln(a, w, x, g, b, tm=512):
    T, K = a.shape
    D = w.shape[1]
    tm = min(tm, T)
    return pl.pallas_call(
        _mm_res_ln_kernel,
        grid=(T // tm,),
        in_specs=[pl.BlockSpec((tm, K), lambda i: (i, 0)),
                  pl.BlockSpec((K, D), lambda i: (0, 0)),
                  pl.BlockSpec((tm, D), lambda i: (i, 0)),
                  pl.BlockSpec((1, D), lambda i: (0, 0)),
                  pl.BlockSpec((1, D), lambda i: (0, 0))],
        out_specs=pl.BlockSpec((tm, D), lambda i: (i, 0)),
        out_shape=jax.ShapeDtypeStruct((T, D), jnp.float32),
        compiler_params=_params("parallel"),
    )(a, w, x, g.reshape(1, D), b.reshape(1, D))


def _mm2_res_ln_kernel(a1_ref, a2_ref, w1_ref, w2_ref, x_ref, g_ref, b_ref, o_ref):
    mix = jnp.dot(a1_ref[...], w1_ref[...], preferred_element_type=jnp.float32)
    mix += jnp.dot(a2_ref[...], w2_ref[...], preferred_element_type=jnp.float32)
    o_ref[...] = _layer_norm_rows(DEEPNORM_ALPHA * x_ref[...] + mix, g_ref[...], b_ref[...])


def _matmul2_res_ln(a1, a2, w, x, g, b, tm=512):
    T, K1 = a1.shape
    D = w.shape[1]
    tm = min(tm, T)
    wb = w.astype(jnp.bfloat16)
    return pl.pallas_call(
        _mm2_res_ln_kernel,
        grid=(T // tm,),
        in_specs=[pl.BlockSpec((tm, K1), lambda i: (i, 0)),
                  pl.BlockSpec((tm, a2.shape[1]), lambda i: (i, 0)),
                  pl.BlockSpec((K1, D), lambda i: (0, 0)),
                  pl.BlockSpec((a2.shape[1], D), lambda i: (1, 0)),
                  pl.BlockSpec((tm, D), lambda i: (i, 0)),
                  pl.BlockSpec((1, D), lambda i: (0, 0)),
                  pl.BlockSpec((1, D), lambda i: (0, 0))],
        out_specs=pl.BlockSpec((tm, D), lambda i: (i, 0)),
        out_shape=jax.ShapeDtypeStruct((T, D), jnp.float32),
        compiler_params=_params("parallel"),
    )(a1, a2, wb, wb, x, g.reshape(1, D), b.reshape(1, D))


def _res_ln_kernel(y_ref, x_ref, g_ref, b_ref, o_ref):
    o_ref[...] = _layer_norm_rows(DEEPNORM_ALPHA * x_ref[...] + y_ref[...], g_ref[...], b_ref[...])


def _res_ln(y, x, g, b, tm=512):
    T, D = x.shape
    tm = min(tm, T)
    return pl.pallas_call(
        _res_ln_kernel,
        grid=(T // tm,),
        in_specs=[pl.BlockSpec((tm, D), lambda i: (i, 0)),
                  pl.BlockSpec((tm, D), lambda i: (i, 0)),
                  pl.BlockSpec((1, D), lambda i: (0, 0)),
                  pl.BlockSpec((1, D), lambda i: (0, 0))],
        out_specs=pl.BlockSpec((tm, D), lambda i: (i, 0)),
        out_shape=jax.ShapeDtypeStruct((T, D), jnp.float32),
        compiler_params=_params("parallel"),
    )(y, x, g.reshape(1, D), b.reshape(1, D))


DSA_TQ = 256
DSA_TK = 1024
COUNT_ROWS = 64
TIE_CHECK_PASS = 8


def _stack_index_heads(qi_blk):
    return jnp.concatenate([qi_blk[:, LANES * h:LANES * (h + 1)] for h in range(IDX_HEADS)], axis=0)


def _index_keys_t(ki_blk, qi_all, wi_t, q0, k0, tq, causal_mask=True):
    tk = ki_blk.shape[0]
    d = lax.dot_general(ki_blk, qi_all, (((1,), (1,)), ((), ())), preferred_element_type=jnp.float32)
    sc = wi_t[0:1, :] * jnp.maximum(d[:, 0:tq], 0.0)
    for h in range(1, IDX_HEADS):
        sc = sc + wi_t[h:h + 1, :] * jnp.maximum(d[:, h * tq:(h + 1) * tq], 0.0)
    if causal_mask:
        kpos = k0 + lax.broadcasted_iota(jnp.int32, (tk, 1), 0)
        qpos = q0 + lax.broadcasted_iota(jnp.int32, (1, tq), 1)
        sc = jnp.where(kpos <= qpos, sc, -jnp.inf)
    bits = lax.bitcast_convert_type(sc, jnp.int32)
    return jnp.where(bits < 0, INT_MIN - bits, bits)


def _dsa_select_kernel(qi_ref, wit_ref, ki_ref, thr_ref, cut_ref, key_ref, *, tq, topk):
    i = pl.program_id(1)
    nblk = i + 1
    q0 = i * tq
    qi_all = _stack_index_heads(qi_ref[...])
    wi_t = wit_ref[...]

    def fill_block(j):
        k0 = pl.multiple_of(j * tq, tq)
        key_ref[j] = _index_keys_t(ki_ref[pl.ds(k0, tq), :], qi_all, wi_t, q0, k0, tq, causal_mask=False)

    def fill_pair(t, carry):
        fill_block(2 * t)
        fill_block(2 * t + 1)
        return carry

    lax.fori_loop(0, i // 2, fill_pair, 0)

    @pl.when(i % 2 == 1)
    def _():
        fill_block(i - 1)

    key_ref[i] = _index_keys_t(ki_ref[pl.ds(pl.multiple_of(q0, tq), tq), :], qi_all, wi_t, q0, q0, tq)

    kr = COUNT_ROWS
    row = lax.broadcasted_iota(jnp.int32, (kr, 1), 0)

    def count(pred):
        def body(j, acc):
            for s in range(tq // kr):
                kk = key_ref[j, s * kr:(s + 1) * kr, :]
                acc = acc + jnp.where(pred(kk, j * tq + s * kr), 1, 0)
            return acc
        acc = lax.fori_loop(0, nblk, body, jnp.zeros((kr, tq), jnp.int32))
        return jnp.sum(acc.astype(jnp.float32), axis=0, keepdims=True).astype(jnp.int32)

    def any_true(mask):
        return jnp.max(jnp.where(mask, 1.0, 0.0)) > 0.5

    def bit_cond(st):
        return jnp.logical_and(st[0] < 32, st[4])

    def bit_step(st):
        p, thr, cnt, final, _ = st
        cand = thr ^ lax.shift_left(jnp.int32(1), 31 - p)
        c = count(lambda kk, base: kk >= cand)
        take = c >= topk
        thr = jnp.where(take, cand, thr)
        cnt = jnp.where(take, c, cnt)
        final = lax.cond(p == TIE_CHECK_PASS,
                         lambda: jnp.where(cnt - count(lambda kk, base: kk == thr) < topk, 1, 0),
                         lambda: final)
        return p + 1, thr, cnt, final, any_true(jnp.logical_and(cnt != topk, final == 0))

    ncols = nblk * tq
    init = (jnp.int32(0), jnp.full((1, tq), INT_MIN, jnp.int32), jnp.full((1, tq), ncols, jnp.int32),
            jnp.zeros((1, tq), jnp.int32), ncols != topk)
    _, thr, cnt, _, _ = lax.while_loop(bit_cond, bit_step, init)
    tie = jnp.logical_and(cnt > topk, thr > KEY_NEG_INF)

    def resolve_ties():
        need = (topk - count(lambda kk, base: kk > thr)).astype(jnp.float32)
        ri = lax.broadcasted_iota(jnp.int32, (kr, kr), 0)
        ci = lax.broadcasted_iota(jnp.int32, (kr, kr), 1)
        lower = jnp.where(ci <= ri, 1.0, 0.0).astype(jnp.bfloat16)
        rowf = row.astype(jnp.float32)

        def body(j, st):
            carry, cut_acc = st
            eqs = [key_ref[j, s * kr:(s + 1) * kr, :] == thr for s in range(tq // kr)]
            within = [jnp.dot(lower, jnp.where(eq, 1.0, 0.0).astype(jnp.bfloat16), preferred_element_type=jnp.float32)
                      for eq in eqs]
            for s, (eq, pref) in enumerate(zip(eqs, within)):
                end_pos = rowf + (j * tq + s * kr + 1).astype(jnp.float32)
                cut_acc = jnp.maximum(cut_acc, jnp.where(jnp.logical_and(eq, pref + carry == need), end_pos, 0.0))
                carry = carry + pref[kr - 1:kr, :]
            return carry, cut_acc

        _, cut_acc = lax.fori_loop(0, nblk, body, (jnp.zeros((1, tq), jnp.float32), jnp.zeros((kr, tq), jnp.float32)))
        return jnp.where(tie, jnp.max(cut_acc, axis=0, keepdims=True).astype(jnp.int32), INT_MAX)

    cut = lax.cond(any_true(tie), resolve_ties, lambda: jnp.full((1, tq), INT_MAX, jnp.int32))
    thr_ref[...] = jnp.maximum(thr, KEY_NEG_INF + 1)
    cut_ref[...] = cut


def _dsa_select(hb, wi_t, bsz, L, tq):
    T = bsz * L
    nq = L // tq
    topk = min(DSA_TOPK_MAX, L // 4)
    qmap = lambda b, i: (0, b * nq + i)
    return pl.pallas_call(
        partial(_dsa_select_kernel, tq=tq, topk=topk),
        grid=(bsz, nq),
        in_specs=[pl.BlockSpec((tq, 512), lambda b, i: (b * nq + i, OD_QI // 512)),
                  pl.BlockSpec((8, tq), qmap),
                  pl.BlockSpec((L, LANES), lambda b, i: (b, OD_KI // LANES))],
        out_specs=[pl.BlockSpec((1, tq), qmap), pl.BlockSpec((1, tq), qmap)],
        out_shape=[jax.ShapeDtypeStruct((1, T), jnp.int32), jax.ShapeDtypeStruct((1, T), jnp.int32)],
        scratch_shapes=[pltpu.VMEM((nq, tq, tq), jnp.int32)],
        compiler_params=_params("parallel", "arbitrary"),
    )(hb, wi_t, hb)


def _split_head_pair(x_pair):
    lane = lax.broadcasted_iota(jnp.int32, x_pair.shape, 1)
    zero = jnp.zeros_like(x_pair)
    return jnp.where(lane < 64, x_pair, zero), jnp.where(lane >= 64, x_pair, zero)


def _dsa_attn_kernel(qidx_ref, kidx_ref, q_ref, qi_ref, wit_ref, thr_ref, cut_ref, k_ref, vt_ref, ki_ref, o_ref,
                     qm_ref, qiall_ref, m_ref, acc_ref, *, tq, tk):
    i = qidx_ref[pl.program_id(1)]
    j = kidx_ref[pl.program_id(1)]
    npair = DSA_HEADS // 2

    @pl.when(j == 0)
    def _():
        for p in range(npair):
            qa, qb = _split_head_pair(q_ref[:, LANES * p:LANES * (p + 1)])
            qm_ref[2 * p] = qa
            qm_ref[2 * p + 1] = qb
        qiall_ref[...] = _stack_index_heads(qi_ref[...])
        m_ref[...] = jnp.full(m_ref.shape, NEG_BIG, jnp.float32)
        acc_ref[...] = jnp.zeros(acc_ref.shape, jnp.float32)

    key = _index_keys_t(ki_ref[...], qiall_ref[...], wit_ref[...], i * tq, j * tk, tq)
    kpos = j * tk + lax.broadcasted_iota(jnp.int32, (tk, 1), 0)
    thr = thr_ref[...]
    sel = jnp.logical_or(key > thr, jnp.logical_and(key == thr, kpos < cut_ref[...]))
    bias = jnp.where(sel, 0.0, NEG_BIG)
    vrow = lax.broadcasted_iota(jnp.int32, (LANES, tk), 0)
    ones = jnp.ones((LANES, tk), jnp.bfloat16)
    scores = []
    for h in range(DSA_HEADS):
        kp = k_ref[:, LANES * (h // 2):LANES * (h // 2 + 1)]
        s = lax.dot_general(kp, qm_ref[h], (((1,), (1,)), ((), ())), preferred_element_type=jnp.float32)
        scores.append(s + bias)
    for h in range(DSA_HEADS):
        vt = vt_ref[LANES * (h // 2):LANES * (h // 2 + 1), :]
        v_aug = jnp.where((vrow < 64) if h % 2 == 0 else (vrow >= 64), vt, ones)
        s = scores[h]
        m_prev = m_ref[h:h + 1, :]
        m_new = jnp.maximum(m_prev, jnp.max(s, axis=0, keepdims=True))
        alpha = jnp.exp(m_prev - m_new)
        e = jnp.exp(s - m_new).astype(jnp.bfloat16)
        acc_ref[h] = alpha * acc_ref[h] + jnp.dot(v_aug, e, preferred_element_type=jnp.float32)
        m_ref[h:h + 1, :] = m_new

    @pl.when(j == ((i + 1) * tq - 1) // tk)
    def _():
        rowi = lax.broadcasted_iota(jnp.int32, (LANES, tq), 0)
        for p in range(npair):
            a = acc_ref[2 * p]
            b = acc_ref[2 * p + 1]
            out_t = jnp.where(rowi < 64, a / a[64:65, :], b / b[0:1, :])
            o_ref[:, LANES * p:LANES * (p + 1)] = out_t.T.astype(o_ref.dtype)


def _dsa_attention(hb, v_t, wi_t, thr, cut, bsz, L, tq, tk):
    T = bsz * L
    nq, nk = L // tq, L // tk
    pairs = [(i, j) for i in range(nq) for j in range(((i + 1) * tq - 1) // tk + 1)]
    qidx = jnp.asarray(np.array([p[0] for p in pairs], np.int32))
    kidx = jnp.asarray(np.array([p[1] for p in pairs], np.int32))

    def qmap(col):
        return lambda b, s, qi, ki: (b * nq + qi[s], col)

    def kmap(col):
        return lambda b, s, qi, ki: (b * nk + ki[s], col)

    qrow = lambda b, s, qi, ki: (0, b * nq + qi[s])
    grid_spec = pltpu.PrefetchScalarGridSpec(
        num_scalar_prefetch=2,
        grid=(bsz, len(pairs)),
        in_specs=[pl.BlockSpec((tq, 512), qmap(OD_Q // 512)),
                  pl.BlockSpec((tq, 512), qmap(OD_QI // 512)),
                  pl.BlockSpec((8, tq), qrow),
                  pl.BlockSpec((1, tq), qrow),
                  pl.BlockSpec((1, tq), qrow),
                  pl.BlockSpec((tk, 512), kmap(OD_K // 512)),
                  pl.BlockSpec((DSA_WIDTH, tk), lambda b, s, qi, ki: (0, b * nk + ki[s])),
                  pl.BlockSpec((tk, LANES), kmap(OD_KI // LANES))],
        out_specs=pl.BlockSpec((tq, DSA_WIDTH), qmap(0)),
        scratch_shapes=[pltpu.VMEM((DSA_HEADS, tq, LANES), jnp.bfloat16),
                        pltpu.VMEM((IDX_HEADS * tq, LANES), jnp.bfloat16),
                        pltpu.VMEM((DSA_HEADS, tq), jnp.float32),
                        pltpu.VMEM((DSA_HEADS, LANES, tq), jnp.float32)])
    return pl.pallas_call(
        partial(_dsa_attn_kernel, tq=tq, tk=tk),
        grid_spec=grid_spec,
        out_shape=jax.ShapeDtypeStruct((T, DSA_WIDTH), jnp.bfloat16),
        compiler_params=_params("parallel", "arbitrary"),
    )(qidx, kidx, hb, hb, wi_t, thr, cut, hb, v_t, hb)


def _dilated_kernel(q_ref, kp_ref, kc_ref, vp_ref, vc_ref, o_ref, lse_ref, *, tq):
    a = pl.program_id(2)
    row = lax.broadcasted_iota(jnp.int32, (tq, 2 * tq), 0)
    c = lax.broadcasted_iota(jnp.int32, (tq, 2 * tq), 1)
    first_col = jnp.where(a == 0, tq, 0)
    valid = jnp.logical_and(jnp.logical_and(c >= row, c <= row + tq), c >= first_col)
    lane = lax.broadcasted_iota(jnp.int32, (tq, LANES), 1)
    for p in range(DIL_HEADS // 2):
        sl = slice(LANES * p, LANES * (p + 1))
        kk = jnp.concatenate([kp_ref[:, sl], kc_ref[:, sl]], axis=0)
        vv = jnp.concatenate([vp_ref[:, sl], vc_ref[:, sl]], axis=0)
        outs, lses = [], []
        for qh in _split_head_pair(q_ref[:, sl]):
            s = lax.dot_general(qh, kk, (((1,), (1,)), ((), ())), preferred_element_type=jnp.float32)
            s = jnp.where(valid, s, NEG_BIG)
            m = jnp.max(s, axis=1, keepdims=True)
            e = jnp.exp(s - m)
            l = jnp.sum(e, axis=1, keepdims=True)
            outs.append(jnp.dot(e.astype(jnp.bfloat16), vv, preferred_element_type=jnp.float32) / l)
            lses.append(m + jnp.log(l))
        o_ref[:, sl] = jnp.where(lane < 64, outs[0], outs[1])
        lse_ref[:, sl] = jnp.where(lane < 64, lses[0], lses[1])


def _dil_proj_kernel(x_ref, w_ref, o_ref, *scratch, dil):
    res = jnp.dot(x_ref[...].astype(jnp.bfloat16), w_ref[...], preferred_element_type=jnp.float32)
    if dil == 1:
        o_ref[...] = res.astype(o_ref.dtype)
        return
    res_ref, = scratch
    rows, cols = res.shape[0] // dil, res.shape[1]
    for c in range(cols // LANES):
        res_ref[c] = res[:, LANES * c:LANES * (c + 1)]
    for r in range(dil):
        for c in range(cols // LANES):
            o_ref[:, cols * r + LANES * c:cols * r + LANES * (c + 1)] = (
                res_ref.at[c][pl.ds(r, rows, stride=dil), :].astype(o_ref.dtype))


def _dil_proj(x, w, dil, tm=1024):
    T, K = x.shape
    C = w.shape[1]
    tm = min(tm, T)
    return pl.pallas_call(
        partial(_dil_proj_kernel, dil=dil),
        grid=(T // tm,),
        in_specs=[pl.BlockSpec((tm, K), lambda i: (i, 0)), pl.BlockSpec((K, C), lambda i: (0, 0))],
        out_specs=pl.BlockSpec((tm // dil, dil * C), lambda i: (i, 0)),
        out_shape=jax.ShapeDtypeStruct((T // dil, dil * C), jnp.bfloat16),
        scratch_shapes=[] if dil == 1 else [pltpu.VMEM((C // LANES, tm, LANES), jnp.float32)],
        compiler_params=_params("parallel"),
    )(x, w)


def _dilated_group(pg, bsz, L, g, tq):
    window, dil = DIL_PATTERNS[g]
    assert window // dil == tq
    M = L // dil
    nb = M // tq
    ncol = DIL_COLS // DIL_WIDTH

    def cur(col):
        return lambda b, r, a: (b * nb + a, r * ncol + col)

    def prev(col):
        return lambda b, r, a: (b * nb + jnp.maximum(a - 1, 0), r * ncol + col)

    blk = (tq, DIL_WIDTH)
    out_map = lambda b, r, a: (b * nb + a, r)
    return pl.pallas_call(
        partial(_dilated_kernel, tq=tq),
        grid=(bsz, dil, nb),
        in_specs=[pl.BlockSpec(blk, cur(0)), pl.BlockSpec(blk, prev(1)), pl.BlockSpec(blk, cur(1)),
                  pl.BlockSpec(blk, prev(2)), pl.BlockSpec(blk, cur(2))],
        out_specs=[pl.BlockSpec(blk, out_map), pl.BlockSpec(blk, out_map)],
        out_shape=[jax.ShapeDtypeStruct((bsz * M, dil * DIL_WIDTH), jnp.float32)] * 2,
        compiler_params=_params("parallel", "parallel", "arbitrary"),
    )(pg, pg, pg, pg, pg)


def _dilated_combine_kernel(*refs, dils, tm):
    ng = len(dils)
    o_refs, l_refs, out_ref, scratch = refs[:ng], refs[ng:2 * ng], refs[2 * ng], refs[2 * ng + 1:]

    def token_major(ref, dil, buf):
        if dil == 1:
            return ref[...]
        nslab = DIL_WIDTH // LANES
        for r in range(dil):
            for c in range(nslab):
                buf.at[c][pl.ds(r, tm // dil, stride=dil), :] = ref[:, DIL_WIDTH * r + LANES * c:DIL_WIDTH * r + LANES * (c + 1)]
        return jnp.concatenate([buf[c] for c in range(nslab)], axis=1)

    bufs = iter(scratch)
    outs = [token_major(o_refs[g], d, None if d == 1 else next(bufs)) for g, d in enumerate(dils)]
    lses = [token_major(l_refs[g], d, None if d == 1 else next(bufs)) for g, d in enumerate(dils)]
    m = lses[0]
    for l in lses[1:]:
        m = jnp.maximum(m, l)
    ws = [jnp.exp(l - m) for l in lses]
    num, den = ws[0] * outs[0], ws[0]
    for w, o in zip(ws[1:], outs[1:]):
        num, den = num + w * o, den + w
    out_ref[...] = (num / den).astype(out_ref.dtype)


def _dilated_combine(outs, lses, dils, T, tm=512):
    tm = min(tm, T)
    specs = [pl.BlockSpec((tm // d, d * DIL_WIDTH), lambda i: (i, 0)) for d in dils]
    n_buf = 2 * sum(1 for d in dils if d != 1)
    return pl.pallas_call(
        partial(_dilated_combine_kernel, dils=tuple(dils), tm=tm),
        grid=(T // tm,),
        in_specs=specs + specs,
        out_specs=pl.BlockSpec((tm, DIL_WIDTH), lambda i: (i, 0)),
        out_shape=jax.ShapeDtypeStruct((T, DIL_WIDTH), jnp.bfloat16),
        scratch_shapes=[pltpu.VMEM((DIL_WIDTH // LANES, tm, LANES), jnp.float32)] * n_buf,
        compiler_params=_params("parallel"),
    )(*outs, *lses)


def _mm_nt_kernel(w_ref, x_ref, o_ref):
    o_ref[...] = lax.dot_general(w_ref[...], x_ref[...].astype(jnp.bfloat16), (((1,), (1,)), ((), ())),
                                 preferred_element_type=jnp.float32).astype(o_ref.dtype)


def _matmul_nt(w_t, x, out_dtype, tm=1024):
    N, K = w_t.shape
    T = x.shape[0]
    tm = min(tm, T)
    return pl.pallas_call(
        _mm_nt_kernel,
        grid=(T // tm,),
        in_specs=[pl.BlockSpec((N, K), lambda i: (0, 0)), pl.BlockSpec((tm, K), lambda i: (i, 0))],
        out_specs=pl.BlockSpec((N, tm), lambda i: (0, i)),
        out_shape=jax.ShapeDtypeStruct((N, T), out_dtype),
        compiler_params=_params("parallel"),
    )(w_t, x)


def _odd_weights(w_in):
    D = w_in.shape[0]
    q, k, v, qi, ki, wi, dil = _split(w_in, OD_SIZES)
    zeros = lambda n: jnp.zeros((D, n), w_in.dtype)
    qi_exp = jnp.concatenate(
        [jnp.concatenate([qi[:, IDX_DIM * h:IDX_DIM * (h + 1)] * IDX_DIM ** -0.5, zeros(LANES - IDX_DIM)], axis=1)
         for h in range(IDX_HEADS)], axis=1)
    dil = dil.reshape(D, 3, DIL_GROUPS, DIL_WIDTH)
    w_dil = [jnp.concatenate([dil[:, 0, g] * DIL_HEAD_DIM ** -0.5, dil[:, 1, g], dil[:, 2, g]], axis=1).astype(jnp.bfloat16)
             for g in range(DIL_GROUPS)]
    wb = jnp.concatenate([q * DSA_HEAD_DIM ** -0.5, k, qi_exp, ki, zeros(LANES - IDX_DIM),
                          zeros(OD_COLS_PADDED - OD_KI - LANES)], axis=1).astype(jnp.bfloat16)
    wv_t = v.T.astype(jnp.bfloat16)
    wwi_t = jnp.concatenate([wi * IDX_HEADS ** -0.5, zeros(8 - IDX_HEADS)], axis=1).T.astype(jnp.bfloat16)
    return wb, w_dil, wv_t, wwi_t


def _odd_layer_mix(x2, w_in, bsz, L):
    wb, w_dil, wv_t, wwi_t = _odd_weights(w_in)
    hb = _matmul(x2, wb, jnp.bfloat16, tm=1024, tn=1024)
    v_t = _matmul_nt(wv_t, x2, jnp.bfloat16)
    wi_t = _matmul_nt(wwi_t, x2, jnp.float32)
    tq, tk = min(DSA_TQ, L), min(DSA_TK, L)
    thr, cut = _dsa_select(hb, wi_t, bsz, L, tq)
    o_c = _dsa_attention(hb, v_t, wi_t, thr, cut, bsz, L, tq, tk)
    dils = [d for _, d in DIL_PATTERNS]
    groups = [_dilated_group(_dil_proj(x2, w_dil[g], dils[g]), bsz, L, g, DIL_PATTERNS[g][0] // dils[g])
              for g in range(DIL_GROUPS)]
    o_d = _dilated_combine([o for o, _ in groups], [l for _, l in groups], dils, bsz * L)
    return o_c, o_d


EV_Q, EV_K, EV_V, EV_R, EV_G = 0, 256, 512, 1024, 1536
EV_COLS_PADDED = 2048
GLA_ROWS = 512
S5_CHUNK = 64


def _gla_kernel(q_ref, k_ref, v_ref, r_ref, g_ref, wg_ref, bg_ref, ng_ref, o_ref, st_ref, *, rows):
    C = GLA_CHUNK

    @pl.when(pl.program_id(1) == 0)
    def _():
        st_ref[...] = jnp.zeros(st_ref.shape, jnp.float32)

    ri = lax.broadcasted_iota(jnp.int32, (C, C), 0)
    ci = lax.broadcasted_iota(jnp.int32, (C, C), 1)
    causal = ci <= ri
    tril = jnp.where(causal, 1.0, 0.0).astype(jnp.float32)
    wg = wg_ref[...].astype(jnp.bfloat16)
    bg = bg_ref[...]
    ng = ng_ref[...]

    def chunk(c, carry):
        r0 = pl.multiple_of(c * C, C)
        rs = pl.ds(r0, C)
        logit = jnp.dot(g_ref[rs, :].astype(jnp.bfloat16), wg, preferred_element_type=jnp.float32) + bg
        log_a = jax.nn.log_sigmoid(logit) / GLA_GATE_TAU
        bcum = jnp.dot(tril, log_a, precision=lax.Precision.HIGHEST, preferred_element_type=jnp.float32)
        b_last = bcum[C - 1:C, :]
        q_t = (q_ref[rs, :] * jnp.exp(bcum)).astype(jnp.bfloat16)
        k_t = (k_ref[rs, :] * jnp.exp(-bcum)).astype(jnp.bfloat16)
        k_end = (k_ref[rs, :] * jnp.exp(b_last - bcum)).astype(jnp.bfloat16)
        dec = jnp.exp(b_last)
        for p in range(GLA_HEADS // 2):
            sl = slice(LANES * p, LANES * (p + 1))
            q_halves = _split_head_pair(q_t[:, sl])
            ke_halves = _split_head_pair(k_end[:, sl])
            for half in range(2):
                h = 2 * p + half
                hs = slice(GLA_DV * h, GLA_DV * (h + 1))
                qm = q_halves[half]
                att = lax.dot_general(qm, k_t[:, sl], (((1,), (1,)), ((), ())), preferred_element_type=jnp.float32)
                att = jnp.where(causal, att, 0.0).astype(jnp.bfloat16)
                v_h = v_ref[rs, hs].astype(jnp.bfloat16)
                st = st_ref[h]
                o = jnp.dot(att, v_h, preferred_element_type=jnp.float32)
                o = o + lax.dot_general(qm, st.astype(jnp.bfloat16), (((1,), (1,)), ((), ())),
                                        preferred_element_type=jnp.float32)
                kv_t = lax.dot_general(v_h, ke_halves[half], (((0,), (0,)), ((), ())),
                                       preferred_element_type=jnp.float32)
                st_ref[h] = st * dec[:, sl] + kv_t
                o = o * lax.rsqrt(jnp.mean(o * o, axis=-1, keepdims=True) + LN_EPS) * ng
                o = o * jax.nn.silu(r_ref[rs, hs])
                o_ref[rs, hs] = o.astype(o_ref.dtype)
        return carry

    lax.fori_loop(0, rows // C, chunk, 0)


def _gla(hf, w_gate2, b_gate2, norm_g, bsz, L):
    T = bsz * L
    rows = min(GLA_ROWS, L)
    nb = L // rows
    dkw = GLA_HEADS * GLA_DK
    dvw = GLA_HEADS * GLA_DV
    wg = jnp.pad(w_gate2, ((0, LANES - GLA_GATE_RANK), (0, 0)))

    def rmap(col):
        return lambda b, i: (b * nb + i, col)

    const = lambda b, i: (0, 0)
    return pl.pallas_call(
        partial(_gla_kernel, rows=rows),
        grid=(bsz, nb),
        in_specs=[pl.BlockSpec((rows, dkw), rmap(EV_Q // dkw)),
                  pl.BlockSpec((rows, dkw), rmap(EV_K // dkw)),
                  pl.BlockSpec((rows, dvw), rmap(EV_V // dvw)),
                  pl.BlockSpec((rows, dvw), rmap(EV_R // dvw)),
                  pl.BlockSpec((rows, LANES), rmap(EV_G // LANES)),
                  pl.BlockSpec((LANES, dkw), const),
                  pl.BlockSpec((1, dkw), const),
                  pl.BlockSpec((1, GLA_DV), const)],
        out_specs=pl.BlockSpec((rows, dvw), rmap(0)),
        out_shape=jax.ShapeDtypeStruct((T, dvw), jnp.bfloat16),
        scratch_shapes=[pltpu.VMEM((GLA_HEADS, GLA_DV, LANES), jnp.float32)],
        compiler_params=_params("parallel", "arbitrary"),
    )(hf, hf, hf, hf, hf, wg, b_gate2.reshape(1, dkw), norm_g.reshape(1, GLA_DV))


def _s5_tables(a_re, a_im, log_dt, b_re, b_im, c_re, c_im, d_skip):
    f32 = jnp.float32
    Cs, G, P, N = S5_CHUNK, S5_GROUPS, S5_STATE, S5_GROUP
    lam_re = jnp.minimum(a_re.astype(f32), S5_MAX_RE)
    lam_im = a_im.astype(f32)
    dt = jnp.exp(log_dt.astype(f32))[:, None]
    mag = jnp.exp(lam_re * dt)
    ab_re = mag * jnp.cos(lam_im * dt)
    ab_im = mag * jnp.sin(lam_im * dt)
    inv = 1.0 / (lam_re * lam_re + lam_im * lam_im)
    z_re = ((ab_re - 1.0) * lam_re + ab_im * lam_im) * inv
    z_im = (ab_im * lam_re - (ab_re - 1.0) * lam_im) * inv
    br, bi = b_re.astype(f32), b_im.astype(f32)
    bb_re = z_re[..., None] * br - z_im[..., None] * bi
    bb_im = z_re[..., None] * bi + z_im[..., None] * br
    kk = jnp.arange(Cs + 1, dtype=f32)[:, None, None]
    pmag = jnp.exp(kk * (lam_re * dt))
    pw_re = pmag * jnp.cos(kk * (lam_im * dt))
    pw_im = pmag * jnp.sin(kk * (lam_im * dt))
    cr, ci = c_re.astype(f32), c_im.astype(f32)
    ca_re = cr[None] * pw_re[:, :, None, :] - ci[None] * pw_im[:, :, None, :]
    ca_im = cr[None] * pw_im[:, :, None, :] + ci[None] * pw_re[:, :, None, :]
    hi = lax.Precision.HIGHEST
    kern = (jnp.einsum('kgnp,gpm->kgnm', ca_re[:Cs], bb_re, precision=hi)
            - jnp.einsum('kgnp,gpm->kgnm', ca_im[:Cs], bb_im, precision=hi))
    jj = jnp.arange(Cs)[:, None]
    ii = jnp.arange(Cs)[None, :]
    tz = jnp.where((ii >= jj)[:, :, None, None, None], kern[jnp.maximum(ii - jj, 0)], 0.0)
    tz = jnp.transpose(tz, (2, 4, 0, 3, 1)).reshape(G, N * Cs, N * Cs)
    rev_re, rev_im = pw_re[Cs - 1::-1][:Cs], pw_im[Cs - 1::-1][:Cs]
    ws_re = rev_re[..., None] * bb_re[None] - rev_im[..., None] * bb_im[None]
    ws_im = rev_re[..., None] * bb_im[None] + rev_im[..., None] * bb_re[None]
    to_ws = lambda w: jnp.pad(jnp.transpose(w, (1, 3, 0, 2)).reshape(G, N * Cs, P), ((0, 0), (0, 0), (0, LANES - P)))
    to_wo = lambda w: jnp.pad(jnp.transpose(w, (1, 3, 2, 0)).reshape(G, P, N * Cs), ((0, 0), (0, LANES - P), (0, 0)))
    a_cs = jnp.stack([jnp.pad(pw_re[Cs], ((0, 0), (0, LANES - P))), jnp.pad(pw_im[Cs], ((0, 0), (0, LANES - P)))], axis=1)
    d_exp = jnp.repeat(d_skip.astype(f32).reshape(G, 1, N), Cs, axis=2)
    bf = jnp.bfloat16
    return (tz.astype(bf), to_ws(ws_re).astype(bf), to_ws(ws_im).astype(bf),
            to_wo(ca_re[1:]).astype(bf), to_wo(-ca_im[1:]).astype(bf), a_cs, d_exp)


def _s5_kernel(u_ref, tz_ref, wsr_ref, wsi_ref, wor_ref, woi_ref, acs_ref, d_ref, y_ref, xr_ref, xi_ref,
               *, nchunk, nbatch):
    u32 = jnp.concatenate([u_ref[m] for m in range(S5_GROUP)], axis=1)
    u = u32.astype(jnp.bfloat16)
    xr_ref[...] = jnp.dot(u, wsr_ref[0], preferred_element_type=jnp.float32)
    xi_ref[...] = jnp.dot(u, wsi_ref[0], preferred_element_type=jnp.float32)
    ar = acs_ref[0, 0:1, :]
    ai = acs_ref[0, 1:2, :]

    def step(c, carry):
        new = []
        for b in range(nbatch):
            sr, si = carry[2 * b], carry[2 * b + 1]
            row = pl.ds(b * nchunk + c, 1)
            lr, li = xr_ref[row, :], xi_ref[row, :]
            xr_ref[row, :] = sr
            xi_ref[row, :] = si
            new += [ar * sr - ai * si + lr, ar * si + ai * sr + li]
        return tuple(new)

    zero = jnp.zeros((1, LANES), jnp.float32)
    lax.fori_loop(0, nchunk, step, (zero,) * (2 * nbatch))
    y = jnp.dot(u, tz_ref[0], preferred_element_type=jnp.float32)
    y = y + jnp.dot(xr_ref[...].astype(jnp.bfloat16), wor_ref[0], preferred_element_type=jnp.float32)
    y = y + jnp.dot(xi_ref[...].astype(jnp.bfloat16), woi_ref[0], preferred_element_type=jnp.float32)
    y = jax.nn.gelu(y + d_ref[0] * u32)
    for n in range(S5_GROUP):
        y_ref[n] = y[:, S5_CHUNK * n:S5_CHUNK * (n + 1)]


def _s5_scan(u_t, tables, bsz, nchunk):
    width, R, Cs = u_t.shape
    G, N = S5_GROUPS, S5_GROUP
    W = N * Cs
    tz, wsr, wsi, wor, woi, a_cs, d_exp = tables
    gmap = lambda g: (g, 0, 0)
    return pl.pallas_call(
        partial(_s5_kernel, nchunk=nchunk, nbatch=bsz),
        grid=(G,),
        in_specs=[pl.BlockSpec((N, R, Cs), gmap), pl.BlockSpec((1, W, W), gmap),
                  pl.BlockSpec((1, W, LANES), gmap), pl.BlockSpec((1, W, LANES), gmap),
                  pl.BlockSpec((1, LANES, W), gmap), pl.BlockSpec((1, LANES, W), gmap),
                  pl.BlockSpec((1, 2, LANES), gmap), pl.BlockSpec((1, 1, W), gmap)],
        out_specs=pl.BlockSpec((N, R, Cs), gmap),
        out_shape=jax.ShapeDtypeStruct((width, R, Cs), jnp.float32),
        scratch_shapes=[pltpu.VMEM((R, LANES), jnp.float32), pltpu.VMEM((R, LANES), jnp.float32)],
        compiler_params=_params("parallel"),
    )(u_t, tz, wsr, wsi, wor, woi, a_cs, d_exp)


def _glu_kernel(yt_ref, wt_ref, b_ref, o_ref):
    y = yt_ref[...]
    gate = jnp.dot(wt_ref[...], y.astype(jnp.bfloat16), preferred_element_type=jnp.float32) + b_ref[...]
    o_ref[...] = (y * jax.nn.sigmoid(gate)).T.astype(o_ref.dtype)


def _glu(y_t, w_glu, b_glu, tm=1024):
    W, T = y_t.shape
    tm = min(tm, T)
    return pl.pallas_call(
        _glu_kernel,
        grid=(T // tm,),
        in_specs=[pl.BlockSpec((W, tm), lambda i: (0, i)),
                  pl.BlockSpec((W, W), lambda i: (0, 0)),
                  pl.BlockSpec((W, 1), lambda i: (0, 0))],
        out_specs=pl.BlockSpec((tm, W), lambda i: (i, 0)),
        out_shape=jax.ShapeDtypeStruct((T, W), jnp.bfloat16),
        compiler_params=_params("parallel"),
    )(y_t, w_glu.T.astype(jnp.bfloat16), b_glu.reshape(W, 1))


def _s5(u_t, s5_params, w_glu, b_glu, bsz, L):
    T = bsz * L
    nchunk = L // S5_CHUNK
    y_t = _s5_scan(u_t.reshape(S5_WIDTH, bsz * nchunk, S5_CHUNK), _s5_tables(*s5_params), bsz, nchunk)
    return _glu(y_t.reshape(S5_WIDTH, T), w_glu, b_glu)


def _even_weights(w_in):
    D = w_in.shape[0]
    q, k, v, r, g_lr, u = _split(w_in, EV_SIZES)
    pad = jnp.zeros((D, EV_COLS_PADDED - EV_G - GLA_GATE_RANK), w_in.dtype)
    wb = jnp.concatenate([q * GLA_DK ** -0.5, k, v, r, g_lr, pad], axis=1).astype(jnp.bfloat16)
    return wb, u.T.astype(jnp.bfloat16)


def _even_layer_mix(x2, w_in, w_gate2, b_gate2, norm_g, s5_params, w_glu, b_glu, bsz, L):
    wb, wu_t = _even_weights(w_in)
    hf = _matmul(x2, wb, jnp.float32, tm=1024, tn=512)
    u_t = _matmul_nt(wu_t, x2, jnp.float32)
    o_a = _gla(hf, w_gate2, b_gate2, norm_g, bsz, L)
    o_b = _s5(u_t, s5_params, w_glu, b_glu, bsz, L)
    return o_a, o_b


MOE_TILE = 512
ROW_TILE = 8
DMA_UNROLL = 8
MOE_VMEM_LIMIT_BYTES = 56 * 1024 * 1024


def _router_kernel(x_ref, w_ref, b_ref, e_ref, g_ref):
    logits = jnp.dot(x_ref[...].astype(jnp.bfloat16), w_ref[...], preferred_element_type=jnp.float32) + b_ref[...]
    tm = logits.shape[0]
    lane = lax.broadcasted_iota(jnp.int32, (tm, LANES), 1)
    logits = jnp.where(lane < N_EXPERTS, logits, NEG_BIG)
    tops, idxs = [], []
    for _ in range(TOP_K):
        m = jnp.max(logits, axis=1, keepdims=True)
        idx = jnp.min(jnp.where(logits == m, lane, LANES), axis=1, keepdims=True)
        tops.append(m)
        idxs.append(idx)
        logits = jnp.where(lane == idx, NEG_BIG, logits)
    exps = [jnp.exp(t - tops[0]) for t in tops]
    denom = exps[0]
    for e in exps[1:]:
        denom = denom + e
    lane4 = lax.broadcasted_iota(jnp.int32, (tm, TOP_K), 1)
    e_out = jnp.zeros((tm, TOP_K), jnp.int32)
    g_out = jnp.zeros((tm, TOP_K), jnp.float32)
    for k in range(TOP_K):
        e_out = jnp.where(lane4 == k, idxs[k], e_out)
        g_out = jnp.where(lane4 == k, exps[k] / denom, g_out)
    e_ref[...] = e_out
    g_ref[...] = g_out


def _router(x, router_w, router_b, tm=512):
    T, D = x.shape
    tm = min(tm, T)
    w = jnp.pad(router_w, ((0, 0), (0, LANES - N_EXPERTS))).astype(jnp.bfloat16)
    b = jnp.pad(router_b, (0, LANES - N_EXPERTS)).reshape(1, LANES)
    return pl.pallas_call(
        _router_kernel,
        grid=(T // tm,),
        in_specs=[pl.BlockSpec((tm, D), lambda i: (i, 0)),
                  pl.BlockSpec((D, LANES), lambda i: (0, 0)),
                  pl.BlockSpec((1, LANES), lambda i: (0, 0))],
        out_specs=[pl.BlockSpec((tm, TOP_K), lambda i: (i, 0)), pl.BlockSpec((tm, TOP_K), lambda i: (i, 0))],
        out_shape=[jax.ShapeDtypeStruct((T, TOP_K), jnp.int32), jax.ShapeDtypeStruct((T, TOP_K), jnp.float32)],
        compiler_params=_params("parallel"),
    )(x, w, b)


def _moe_rank_kernel(e_ref, rank_ref, count_ref, carry_ref):
    @pl.when(pl.program_id(0) == 0)
    def _():
        carry_ref[...] = jnp.zeros(carry_ref.shape, jnp.float32)

    e = e_ref[...]
    tm = e.shape[0]
    lane = lax.broadcasted_iota(jnp.int32, (tm, LANES), 1)
    onehot = jnp.zeros((tm, LANES), jnp.float32)
    for k in range(TOP_K):
        onehot = onehot + jnp.where(lane == e[:, k:k + 1], 1.0, 0.0)
    ri = lax.broadcasted_iota(jnp.int32, (tm, tm), 0)
    ci = lax.broadcasted_iota(jnp.int32, (tm, tm), 1)
    strict_lower = jnp.where(ci < ri, 1.0, 0.0).astype(jnp.bfloat16)
    before = jnp.dot(strict_lower, onehot.astype(jnp.bfloat16), preferred_element_type=jnp.float32) + carry_ref[...]
    lane4 = lax.broadcasted_iota(jnp.int32, (tm, TOP_K), 1)
    rank = jnp.zeros((tm, TOP_K), jnp.int32)
    for k in range(TOP_K):
        r_k = jnp.sum(jnp.where(lane == e[:, k:k + 1], before, 0.0), axis=1, keepdims=True).astype(jnp.int32)
        rank = jnp.where(lane4 == k, r_k, rank)
    rank_ref[...] = rank
    carry_ref[...] = carry_ref[...] + jnp.sum(onehot, axis=0, keepdims=True)
    count_ref[...] = carry_ref[...]


def _moe_rank(top_e, tm=256):
    T = top_e.shape[0]
    tm = min(tm, T)
    return pl.pallas_call(
        _moe_rank_kernel,
        grid=(T // tm,),
        in_specs=[pl.BlockSpec((tm, TOP_K), lambda i: (i, 0))],
        out_specs=[pl.BlockSpec((tm, TOP_K), lambda i: (i, 0)), pl.BlockSpec((1, LANES), lambda i: (0, 0))],
        out_shape=[jax.ShapeDtypeStruct((T, TOP_K), jnp.int32), jax.ShapeDtypeStruct((1, LANES), jnp.float32)],
        scratch_shapes=[pltpu.VMEM((1, LANES), jnp.float32)],
        compiler_params=_params("arbitrary"),
    )(top_e)


def _to_token_tiles(x, dst_ref, rows):
    for c in range(ROW_TILE):
        dst_ref[pl.ds(c, rows, stride=ROW_TILE), :] = x[:, LANES * c:LANES * (c + 1)]


def _from_token_tiles(src_ref, rows):
    return jnp.concatenate([src_ref[pl.ds(c, rows, stride=ROW_TILE), :] for c in range(ROW_TILE)], axis=1)


def _moe_dispatch_kernel(dest_ref, x_ref, init_ref, xbuf_ref, xs_ref, sem, *, tm):
    del init_ref
    i = pl.program_id(0)
    n = pl.num_programs(0)
    slot = i % 2

    def row_copy(s, r, d):
        return pltpu.make_async_copy(xs_ref.at[s, pl.ds(pl.multiple_of(r * ROW_TILE, ROW_TILE), ROW_TILE), :],
                                     xbuf_ref.at[pl.ds(pl.multiple_of(d * ROW_TILE, ROW_TILE), ROW_TILE), :], sem.at[s])

    def drain(s):
        def body(a, c):
            row_copy(s, 0, 0).wait()
            return c
        lax.fori_loop(0, tm * TOP_K, body, 0, unroll=DMA_UNROLL)

    @pl.when(i >= 2)
    def _():
        drain(slot)

    _to_token_tiles(x_ref[...], xs_ref.at[slot], tm)

    def body(r, c):
        for k in range(TOP_K):
            row_copy(slot, r, dest_ref[r * TOP_K + k]).start()
        return c

    lax.fori_loop(0, tm, body, 0, unroll=DMA_UNROLL)

    @pl.when(i == n - 1)
    def _():
        drain(slot)

        @pl.when(n > 1)
        def _():
            drain(1 - slot)


def _moe_dispatch(x, dest_flat, n_rows, tm=256):
    T, D = x.shape
    tm = min(tm, T)
    assert D == ROW_TILE * LANES
    init = jnp.zeros((n_rows * ROW_TILE, LANES), jnp.float32)
    return pl.pallas_call(
        partial(_moe_dispatch_kernel, tm=tm),
        grid=(T // tm,),
        in_specs=[pl.BlockSpec((tm * TOP_K,), lambda i: (i,), memory_space=pltpu.SMEM),
                  pl.BlockSpec((tm, D), lambda i: (i, 0)),
                  pl.BlockSpec(memory_space=pl.ANY)],
        out_specs=pl.BlockSpec(memory_space=pl.ANY),
        out_shape=jax.ShapeDtypeStruct((n_rows * ROW_TILE, LANES), jnp.float32),
        scratch_shapes=[pltpu.VMEM((2, tm * ROW_TILE, LANES), jnp.float32), pltpu.SemaphoreType.DMA((2,))],
        input_output_aliases={2: 0},
        compiler_params=_params("arbitrary"),
    )(dest_flat, x, init)


def _moe_ffn_kernel(te_ref, nt_ref, x_ref, w1_ref, b1_ref, w2_ref, b2_ref, y_ref, w1b_ref, w2b_ref):
    i = pl.program_id(0)
    prev = te_ref[jnp.maximum(i - 1, 0)]

    @pl.when(jnp.logical_or(i == 0, te_ref[i] != prev))
    def _():
        w1b_ref[...] = w1_ref[0].astype(jnp.bfloat16)
        w2b_ref[...] = w2_ref[0].astype(jnp.bfloat16)

    @pl.when(i < nt_ref[0])
    def _():
        x = _from_token_tiles(x_ref, MOE_TILE).astype(jnp.bfloat16)
        h = jnp.dot(x, w1b_ref[...], preferred_element_type=jnp.float32) + b1_ref[0]
        glu = jnp.minimum(h[:, :D_FF], SWIGLU_LIMIT)
        lin = jnp.clip(h[:, D_FF:], -SWIGLU_LIMIT, SWIGLU_LIMIT)
        act = glu * jax.nn.sigmoid(SWIGLU_ALPHA * glu) * (lin + 1.0)
        y = jnp.dot(act.astype(jnp.bfloat16), w2b_ref[...], preferred_element_type=jnp.float32) + b2_ref[0]
        _to_token_tiles(y, y_ref, MOE_TILE)

    @pl.when(i >= nt_ref[0])
    def _():
        y_ref[...] = jnp.zeros(y_ref.shape, y_ref.dtype)


def _moe_expert_ffn(xbuf, tile_expert, n_used, w1, b1, w2, b2):
    D, F2 = w1.shape[1], w1.shape[2]
    P = xbuf.shape[0] // ROW_TILE
    n_tiles = P // MOE_TILE
    blk = (MOE_TILE * ROW_TILE, LANES)
    grid_spec = pltpu.PrefetchScalarGridSpec(
        num_scalar_prefetch=2,
        grid=(n_tiles,),
        in_specs=[pl.BlockSpec(blk, lambda i, te, nt: (jnp.minimum(i, nt[0] - 1), 0)),
                  pl.BlockSpec((1, D, F2), lambda i, te, nt: (te[i], 0, 0)),
                  pl.BlockSpec((1, 1, F2), lambda i, te, nt: (te[i], 0, 0)),
                  pl.BlockSpec((1, D_FF, D), lambda i, te, nt: (te[i], 0, 0)),
                  pl.BlockSpec((1, 1, D), lambda i, te, nt: (te[i], 0, 0))],
        out_specs=pl.BlockSpec(blk, lambda i, te, nt: (i, 0)),
        scratch_shapes=[pltpu.VMEM((D, F2), jnp.bfloat16), pltpu.VMEM((D_FF, D), jnp.bfloat16)],
    )
    return pl.pallas_call(
        _moe_ffn_kernel,
        grid_spec=grid_spec,
        out_shape=jax.ShapeDtypeStruct(xbuf.shape, jnp.float32),
        compiler_params=pltpu.CompilerParams(dimension_semantics=("arbitrary",),
                                             vmem_limit_bytes=MOE_VMEM_LIMIT_BYTES),
    )(tile_expert, n_used, xbuf, w1, b1.reshape(-1, 1, F2), w2, b2.reshape(-1, 1, D))


def _moe_combine_kernel(dest_ref, gate_ref, x_ref, g_ref, b_ref, ybuf_ref, o_ref, rows_ref, sem, *, tm):
    def row_copy(k, r, d):
        return pltpu.make_async_copy(ybuf_ref.at[pl.ds(pl.multiple_of(d * ROW_TILE, ROW_TILE), ROW_TILE), :],
                                     rows_ref.at[k, pl.ds(pl.multiple_of(r * ROW_TILE, ROW_TILE), ROW_TILE), :], sem.at[0])

    def start(r, c):
        for k in range(TOP_K):
            row_copy(k, r, dest_ref[r * TOP_K + k]).start()
        return c

    lax.fori_loop(0, tm, start, 0, unroll=DMA_UNROLL)

    def wait(a, c):
        row_copy(0, 0, 0).wait()
        return c

    lax.fori_loop(0, tm * TOP_K, wait, 0, unroll=DMA_UNROLL)
    gate = gate_ref[...]
    ffn = gate[:, 0:1] * _from_token_tiles(rows_ref.at[0], tm)
    for k in range(1, TOP_K):
        ffn = ffn + gate[:, k:k + 1] * _from_token_tiles(rows_ref.at[k], tm)
    o_ref[...] = _layer_norm_rows(DEEPNORM_ALPHA * x_ref[...] + ffn, g_ref[...], b_ref[...])


def _moe_combine(ybuf, dest_flat, gate, x, g, b, tm=256):
    T, D = x.shape
    tm = min(tm, T)
    return pl.pallas_call(
        partial(_moe_combine_kernel, tm=tm),
        grid=(T // tm,),
        in_specs=[pl.BlockSpec((tm * TOP_K,), lambda i: (i,), memory_space=pltpu.SMEM),
                  pl.BlockSpec((tm, TOP_K), lambda i: (i, 0)),
                  pl.BlockSpec((tm, D), lambda i: (i, 0)),
                  pl.BlockSpec((1, D), lambda i: (0, 0)),
                  pl.BlockSpec((1, D), lambda i: (0, 0)),
                  pl.BlockSpec(memory_space=pl.ANY)],
        out_specs=pl.BlockSpec((tm, D), lambda i: (i, 0)),
        out_shape=jax.ShapeDtypeStruct((T, D), jnp.float32),
        scratch_shapes=[pltpu.VMEM((TOP_K, tm * ROW_TILE, LANES), jnp.float32), pltpu.SemaphoreType.DMA((1,))],
        compiler_params=_params("arbitrary"),
    )(dest_flat, gate, x, g.reshape(1, D), b.reshape(1, D), ybuf)


def _moe_layer(x, router_w, router_b, w1, b1, w2, b2, first_expert, ln_g, ln_b):
    T, D = x.shape
    A = T * TOP_K
    n_tiles = -(-(A + N_EXPERTS * (MOE_TILE - 1)) // MOE_TILE)
    top_e, gate = _router(x, router_w, router_b)
    rank, counts = _moe_rank(top_e)
    counts = counts[0, :N_EXPERTS].astype(jnp.int32)
    padded = (counts + MOE_TILE - 1) // MOE_TILE * MOE_TILE
    pend = jnp.cumsum(padded)
    pstart = pend - padded
    dest = (pstart[top_e] + rank).reshape(A)
    tile_start = jnp.arange(n_tiles, dtype=jnp.int32) * MOE_TILE
    tile_expert = jnp.minimum(jnp.sum(pend[None, :] <= tile_start[:, None], axis=1), N_EXPERTS - 1).astype(jnp.int32)
    n_used = (pend[-1:] // MOE_TILE).astype(jnp.int32)
    xbuf = _moe_dispatch(x, dest, n_tiles * MOE_TILE)
    ybuf = _moe_expert_ffn(xbuf, tile_expert + first_expert, n_used, w1, b1, w2, b2)
    return _moe_combine(ybuf, dest, gate, x, ln_g, ln_b)


def kernel(x, ev_w_in, gla_w_gate2, gla_b_gate2, gla_norm_g, s5_a_re, s5_a_im, s5_log_dt, s5_b_re, s5_b_im,
           s5_c_re, s5_c_im, s5_d, s5_w_glu, s5_b_glu, ev_w_out, od_w_in, od_w_out, ln1_g, ln1_b, ln2_g, ln2_b,
           router_w, router_b, moe_w1, moe_b1, moe_w2, moe_b2):
    bsz, L, D = x.shape
    T = bsz * L
    x = x.reshape(T, D)
    w1_all, b1_all = moe_w1.reshape((-1,) + moe_w1.shape[2:]), moe_b1.reshape(-1, moe_b1.shape[-1])
    w2_all, b2_all = moe_w2.reshape((-1,) + moe_w2.shape[2:]), moe_b2.reshape(-1, moe_b2.shape[-1])
    for layer in range(DEPTH):
        j = layer // 2
        if layer % 2 == 0:
            s5_params = (s5_a_re[j], s5_a_im[j], s5_log_dt[j], s5_b_re[j], s5_b_im[j], s5_c_re[j], s5_c_im[j], s5_d[j])
            o_1, o_2 = _even_layer_mix(x, ev_w_in[j], gla_w_gate2[j], gla_b_gate2[j], gla_norm_g[j], s5_params,
                                       s5_w_glu[j], s5_b_glu[j], bsz, L)
            w_out = ev_w_out[j]
        else:
            o_1, o_2 = _odd_layer_mix(x, od_w_in[j], bsz, L)
            w_out = od_w_out[j]
        x = _matmul2_res_ln(o_1, o_2, w_out, x, ln1_g[layer], ln1_b[layer])
        x = _moe_layer(x, router_w[layer], router_b[layer], w1_all, b1_all, w2_all, b2_all, layer * N_EXPERTS,
                       ln2_g[layer], ln2_b[layer])
    return x.reshape(bsz, L, D)
```

```python
import math
from functools import partial

import numpy as np
import jax
import jax.numpy as jnp
from jax import lax
from jax.experimental import pallas as pl
from jax.experimental.pallas import tpu as pltpu

D_MODEL = 1024
DEPTH = 4
DEEPNORM_ALPHA = (2.0 * DEPTH) ** 0.25
LN_EPS = 1e-5
MIX_WIDTH = D_MODEL

GLA_HEADS = 4
GLA_DV = MIX_WIDTH // 2 // GLA_HEADS
GLA_DK = GLA_DV // 2
GLA_GATE_RANK = 16
GLA_GATE_TAU = 16.0
GLA_CHUNK = 64

S5_WIDTH = MIX_WIDTH // 2
S5_GROUP = 16
S5_GROUPS = S5_WIDTH // S5_GROUP
S5_STATE = 64
S5_MAX_RE = -1e-4

EV_SIZES = (GLA_HEADS * GLA_DK, GLA_HEADS * GLA_DK, GLA_HEADS * GLA_DV, GLA_HEADS * GLA_DV, GLA_GATE_RANK, S5_WIDTH)

DSA_HEADS = 8
DSA_HEAD_DIM = 64
DSA_WIDTH = DSA_HEADS * DSA_HEAD_DIM
IDX_HEADS = 4
IDX_DIM = 64
DSA_TOPK_MAX = 256

DIL_PATTERNS = ((128, 1), (512, 4), (2048, 16))
DIL_GROUPS = len(DIL_PATTERNS)
DIL_HEADS = 8
DIL_HEAD_DIM = 64
DIL_WIDTH = DIL_HEADS * DIL_HEAD_DIM

OD_SIZES = (DSA_WIDTH, DSA_WIDTH, DSA_WIDTH, IDX_HEADS * IDX_DIM, IDX_DIM, IDX_HEADS, 3 * DIL_GROUPS * DIL_WIDTH)

N_EXPERTS = 32
TOP_K = 4
D_FF = D_MODEL
SWIGLU_ALPHA = 1.702
SWIGLU_LIMIT = 7.0
MOE_BLOCK = 512

LANES = 128
VMEM_LIMIT_BYTES = 48 * 1024 * 1024
NEG_BIG = -1e30
INT_MIN = -2 ** 31
INT_MAX = 2 ** 31 - 1
KEY_NEG_INF = -0x7F800000

OD_Q, OD_K, OD_QI, OD_KI = 0, 512, 1024, 1536
OD_COLS_PADDED = 2048
DIL_COLS = 3 * DIL_WIDTH


def _split(h, sizes):
    return jnp.split(h, [int(i) for i in np.cumsum(sizes)[:-1]], axis=-1)


def _params(*sem):
    return pltpu.CompilerParams(dimension_semantics=sem, vmem_limit_bytes=VMEM_LIMIT_BYTES)


def _mm_kernel(x_ref, w_ref, o_ref):
    o_ref[...] = jnp.dot(x_ref[...].astype(jnp.bfloat16), w_ref[...].astype(jnp.bfloat16),
                         preferred_element_type=jnp.float32).astype(o_ref.dtype)


def _matmul(x, w, out_dtype=jnp.float32, tm=512, tn=512):
    T, K = x.shape
    N = w.shape[1]
    tm, tn = min(tm, T), min(tn, N)
    assert T % tm == 0 and N % tn == 0
    return pl.pallas_call(
        _mm_kernel,
        grid=(T // tm, N // tn),
        in_specs=[pl.BlockSpec((tm, K), lambda i, j: (i, 0)),
                  pl.BlockSpec((K, tn), lambda i, j: (0, j))],
        out_specs=pl.BlockSpec((tm, tn), lambda i, j: (i, j)),
        out_shape=jax.ShapeDtypeStruct((T, N), out_dtype),
        compiler_params=_params("parallel", "arbitrary"),
    )(x, w)


def _layer_norm_rows(z, g, b):
    mu = jnp.mean(z, axis=-1, keepdims=True)
    zc = z - mu
    var = jnp.mean(zc * zc, axis=-1, keepdims=True)
    return zc * lax.rsqrt(var + LN_EPS) * g + b


def _mm_res_ln_kernel(a_ref, w_ref, x_ref, g_ref, b_ref, o_ref):
    mix = jnp.dot(a_ref[...].astype(jnp.bfloat16), w_ref[...].astype(jnp.bfloat16),
                  preferred_element_type=jnp.float32)
    o_ref[...] = _layer_norm_rows(DEEPNORM_ALPHA * x_ref[...] + mix, g_ref[...], b_ref[...])


def _matmul_res_ln(a, w, x, g, b, tm=512):
    T, K = a.shape
    D = w.shape[1]
    tm = min(tm, T)
    return pl.pallas_call(
        _mm_res_ln_kernel,
        grid=(T // tm,),
        in_specs=[pl.BlockSpec((tm, K), lambda i: (i, 0)),
                  pl.BlockSpec((K, D), lambda i: (0, 0)),
                  pl.BlockSpec((tm, D), lambda i: (i, 0)),
                  pl.BlockSpec((1, D), lambda i: (0, 0)),
                  pl.BlockSpec((1, D), lambda i: (0, 0))],
        out_specs=pl.BlockSpec((tm, D), lambda i: (i, 0)),
        out_shape=jax.ShapeDtypeStruct((T, D), jnp.float32),
        compiler_params=_params("parallel"),
    )(a, w, x, g.reshape(1, D), b.reshape(1, D))


def _mm2_res_ln_kernel(a1_ref, a2_ref, w1_ref, w2_ref, x_ref, g_ref, b_ref, o_ref):
    mix = jnp.dot(a1_ref[...], w1_ref[...], preferred_element_type=jnp.float32)
    mix += jnp.dot(a2_ref[...], w2_ref[...], preferred_element_type=jnp.float32)
    o_ref[...] = _layer_norm_rows(DEEPNORM_ALPHA * x_ref[...] + mix, g_ref[...], b_ref[...])


def _matmul2_res_ln(a1, a2, w, x, g, b, tm=512):
    T, K1 = a1.shape
    D = w.shape[1]
    tm = min(tm, T)
    wb = w.astype(jnp.bfloat16)
    return pl.pallas_call(
        _mm2_res_ln_kernel,
        grid=(T // tm,),
        in_specs=[pl.BlockSpec((tm, K1), lambda i: (i, 0)),
                  pl.BlockSpec((tm, a2.shape[1]), lambda i: (i, 0)),
                  pl.BlockSpec((K1, D), lambda i: (0, 0)),
                  pl.BlockSpec((a2.shape[1], D), lambda i: (1, 0)),
                  pl.BlockSpec((tm, D), lambda i: (i, 0)),
                  pl.BlockSpec((1, D), lambda i: (0, 0)),
                  pl.BlockSpec((1, D), lambda i: (0, 0))],
        out_specs=pl.BlockSpec((tm, D), lambda i: (i, 0)),
        out_shape=jax.ShapeDtypeStruct((T, D), jnp.float32),
        compiler_params=_params("parallel"),
    )(a1, a2, wb, wb, x, g.reshape(1, D), b.reshape(1, D))


def _res_ln_kernel(y_ref, x_ref, g_ref, b_ref, o_ref):
    o_ref[...] = _layer_norm_rows(DEEPNORM_ALPHA * x_ref[...] + y_ref[...], g_ref[...], b_ref[...])


def _res_ln(y, x, g, b, tm=512):
    T, D = x.shape
    tm = min(tm, T)
    return pl.pallas_call(
        _res_ln_kernel,
        grid=(T // tm,),
        in_specs=[pl.BlockSpec((tm, D), lambda i: (i, 0)),
                  pl.BlockSpec((tm, D), lambda i: (i, 0)),
                  pl.BlockSpec((1, D), lambda i: (0, 0)),
                  pl.BlockSpec((1, D), lambda i: (0, 0))],
        out_specs=pl.BlockSpec((tm, D), lambda i: (i, 0)),
        out_shape=jax.ShapeDtypeStruct((T, D), jnp.float32),
        compiler_params=_params("parallel"),
    )(y, x, g.reshape(1, D), b.reshape(1, D))


DSA_TQ = 256
DSA_TK = 1024
COUNT_ROWS = 64
TIE_CHECK_PASS = 8


def _stack_index_heads(qi_blk):
    return jnp.concatenate([qi_blk[:, LANES * h:LANES * (h + 1)] for h in range(IDX_HEADS)], axis=0)


def _index_keys_t(ki_blk, qi_all, wi_t, q0, k0, tq, causal_mask=True):
    tk = ki_blk.shape[0]
    d = lax.dot_general(ki_blk, qi_all, (((1,), (1,)), ((), ())), preferred_element_type=jnp.float32)
    sc = wi_t[0:1, :] * jnp.maximum(d[:, 0:tq], 0.0)
    for h in range(1, IDX_HEADS):
        sc = sc + wi_t[h:h + 1, :] * jnp.maximum(d[:, h * tq:(h + 1) * tq], 0.0)
    if causal_mask:
        kpos = k0 + lax.broadcasted_iota(jnp.int32, (tk, 1), 0)
        qpos = q0 + lax.broadcasted_iota(jnp.int32, (1, tq), 1)
        sc = jnp.where(kpos <= qpos, sc, -jnp.inf)
    bits = lax.bitcast_convert_type(sc, jnp.int32)
    return jnp.where(bits < 0, INT_MIN - bits, bits)


def _dsa_select_kernel(qi_ref, wit_ref, ki_ref, sel_ref, key_ref, *, tq, topk, nq):
    i = pl.program_id(1)
    nblk = i + 1
    q0 = i * tq
    qi_all = _stack_index_heads(qi_ref[...])
    wi_t = wit_ref[...]

    def fill_block(j):
        k0 = pl.multiple_of(j * tq, tq)
        key_ref[j] = _index_keys_t(ki_ref[pl.ds(k0, tq), :], qi_all, wi_t, q0, k0, tq, causal_mask=False)

    def fill_pair(t, carry):
        fill_block(2 * t)
        fill_block(2 * t + 1)
        return carry

    lax.fori_loop(0, i // 2, fill_pair, 0)

    @pl.when(i % 2 == 1)
    def _():
        fill_block(i - 1)

    key_ref[i] = _index_keys_t(ki_ref[pl.ds(pl.multiple_of(q0, tq), tq), :], qi_all, wi_t, q0, q0, tq)

    kr = COUNT_ROWS
    row = lax.broadcasted_iota(jnp.int32, (kr, 1), 0)

    def count(pred):
        def body(j, acc):
            for s in range(tq // kr):
                kk = key_ref[j, s * kr:(s + 1) * kr, :]
                acc = acc + jnp.where(pred(kk, j * tq + s * kr), 1, 0)
            return acc
        acc = lax.fori_loop(0, nblk, body, jnp.zeros((kr, tq), jnp.int32))
        return jnp.sum(acc.astype(jnp.float32), axis=0, keepdims=True).astype(jnp.int32)

    def any_true(mask):
        return jnp.max(jnp.where(mask, 1.0, 0.0)) > 0.5

    def bit_cond(st):
        return jnp.logical_and(st[0] < 32, st[4])

    def bit_step(st):
        p, thr, cnt, final, _ = st
        cand = thr ^ lax.shift_left(jnp.int32(1), 31 - p)
        c = count(lambda kk, base: kk >= cand)
        take = c >= topk
        thr = jnp.where(take, cand, thr)
        cnt = jnp.where(take, c, cnt)
        final = lax.cond(p == TIE_CHECK_PASS,
                         lambda: jnp.where(cnt - count(lambda kk, base: kk == thr) < topk, 1, 0),
                         lambda: final)
        return p + 1, thr, cnt, final, any_true(jnp.logical_and(cnt != topk, final == 0))

    ncols = nblk * tq
    init = (jnp.int32(0), jnp.full((1, tq), INT_MIN, jnp.int32), jnp.full((1, tq), ncols, jnp.int32),
            jnp.zeros((1, tq), jnp.int32), ncols != topk)
    _, thr, cnt, _, _ = lax.while_loop(bit_cond, bit_step, init)
    tie = jnp.logical_and(cnt > topk, thr > KEY_NEG_INF)

    def resolve_ties():
        need = (topk - count(lambda kk, base: kk > thr)).astype(jnp.float32)
        ri = lax.broadcasted_iota(jnp.int32, (kr, kr), 0)
        ci = lax.broadcasted_iota(jnp.int32, (kr, kr), 1)
        lower = jnp.where(ci <= ri, 1.0, 0.0).astype(jnp.bfloat16)
        rowf = row.astype(jnp.float32)

        def body(j, st):
            carry, cut_acc = st
            eqs = [key_ref[j, s * kr:(s + 1) * kr, :] == thr for s in range(tq // kr)]
            within = [jnp.dot(lower, jnp.where(eq, 1.0, 0.0).astype(jnp.bfloat16), preferred_element_type=jnp.float32)
                      for eq in eqs]
            for s, (eq, pref) in enumerate(zip(eqs, within)):
                end_pos = rowf + (j * tq + s * kr + 1).astype(jnp.float32)
                cut_acc = jnp.maximum(cut_acc, jnp.where(jnp.logical_and(eq, pref + carry == need), end_pos, 0.0))
                carry = carry + pref[kr - 1:kr, :]
            return carry, cut_acc

        _, cut_acc = lax.fori_loop(0, nblk, body, (jnp.zeros((1, tq), jnp.float32), jnp.zeros((kr, tq), jnp.float32)))
        return jnp.where(tie, jnp.max(cut_acc, axis=0, keepdims=True).astype(jnp.int32), INT_MAX)

    cut = lax.cond(any_true(tie), resolve_ties, lambda: jnp.full((1, tq), INT_MAX, jnp.int32))
    thr = jnp.maximum(thr, KEY_NEG_INF + 1)

    def emit(j, carry):
        kk = key_ref[j]
        kpos = j * tq + lax.broadcasted_iota(jnp.int32, (tq, 1), 0)
        sel = jnp.logical_or(kk > thr, jnp.logical_and(kk == thr, kpos < cut))
        sel_ref[0, j] = jnp.where(sel, 1, 0).astype(jnp.int8)
        return carry

    lax.fori_loop(0, nblk, emit, 0)

    def clear(j, carry):
        sel_ref[0, j] = jnp.zeros((tq, tq), jnp.int8)
        return carry

    lax.fori_loop(nblk, nq, clear, 0)


def _dsa_select(hb, wi_t, bsz, L, tq):
    T = bsz * L
    nq = L // tq
    topk = min(DSA_TOPK_MAX, L // 4)
    qmap = lambda b, i: (0, b * nq + i)
    return pl.pallas_call(
        partial(_dsa_select_kernel, tq=tq, topk=topk, nq=nq),
        grid=(bsz, nq),
        in_specs=[pl.BlockSpec((tq, 512), lambda b, i: (b * nq + i, OD_QI // 512)),
                  pl.BlockSpec((8, tq), qmap),
                  pl.BlockSpec((L, LANES), lambda b, i: (b, OD_KI // LANES))],
        out_specs=pl.BlockSpec((1, nq, tq, tq), lambda b, i: (b * nq + i, 0, 0, 0)),
        out_shape=jax.ShapeDtypeStruct((bsz * nq, nq, tq, tq), jnp.int8),
        scratch_shapes=[pltpu.VMEM((nq, tq, tq), jnp.int32)],
        compiler_params=_params("parallel", "arbitrary"),
    )(hb, wi_t, hb)


def _split_head_pair(x_pair):
    lane = lax.broadcasted_iota(jnp.int32, x_pair.shape, 1)
    zero = jnp.zeros_like(x_pair)
    return jnp.where(lane < 64, x_pair, zero), jnp.where(lane >= 64, x_pair, zero)


def _dsa_attn_kernel(qidx_ref, kidx_ref, q_ref, sel_ref, k_ref, vt_ref, o_ref, qm_ref, m_ref, acc_ref, *, tq, tk):
    i = qidx_ref[pl.program_id(1)]
    j = kidx_ref[pl.program_id(1)]
    npair = DSA_HEADS // 2

    @pl.when(j == 0)
    def _():
        for p in range(npair):
            qa, qb = _split_head_pair(q_ref[:, LANES * p:LANES * (p + 1)])
            qm_ref[2 * p] = qa
            qm_ref[2 * p + 1] = qb
        m_ref[...] = jnp.full(m_ref.shape, NEG_BIG, jnp.float32)
        acc_ref[...] = jnp.zeros(acc_ref.shape, jnp.float32)

    sel = jnp.concatenate([sel_ref[0, c] for c in range(tk // tq)], axis=0)
    bias = (sel.astype(jnp.float32) - 1.0) * (-NEG_BIG)
    vrow = lax.broadcasted_iota(jnp.int32, (LANES, tk), 0)
    ones = jnp.ones((LANES, tk), jnp.bfloat16)
    scores = []
    for h in range(DSA_HEADS):
        kp = k_ref[:, LANES * (h // 2):LANES * (h // 2 + 1)]
        s = lax.dot_general(kp, qm_ref[h], (((1,), (1,)), ((), ())), preferred_element_type=jnp.float32)
        scores.append(s + bias)
    for h in range(DSA_HEADS):
        vt = vt_ref[LANES * (h // 2):LANES * (h // 2 + 1), :]
        v_aug = jnp.where((vrow < 64) if h % 2 == 0 else (vrow >= 64), vt, ones)
        s = scores[h]
        m_prev = m_ref[h:h + 1, :]
        m_new = jnp.maximum(m_prev, jnp.max(s, axis=0, keepdims=True))
        alpha = jnp.exp(m_prev - m_new)
        e = jnp.exp(s - m_new).astype(jnp.bfloat16)
        acc_ref[h] = alpha * acc_ref[h] + jnp.dot(v_aug, e, preferred_element_type=jnp.float32)
        m_ref[h:h + 1, :] = m_new

    @pl.when(j == ((i + 1) * tq - 1) // tk)
    def _():
        rowi = lax.broadcasted_iota(jnp.int32, (LANES, tq), 0)
        for p in range(npair):
            a = acc_ref[2 * p]
            b = acc_ref[2 * p + 1]
            out_t = jnp.where(rowi < 64, a / a[64:65, :], b / b[0:1, :])
            o_ref[:, LANES * p:LANES * (p + 1)] = out_t.T.astype(o_ref.dtype)


def _dsa_attention(hb, v_t, sel, bsz, L, tq, tk):
    T = bsz * L
    nq, nk = L // tq, L // tk
    pairs = [(i, j) for i in range(nq) for j in range(((i + 1) * tq - 1) // tk + 1)]
    qidx = jnp.asarray(np.array([p[0] for p in pairs], np.int32))
    kidx = jnp.asarray(np.array([p[1] for p in pairs], np.int32))

    def qmap(col):
        return lambda b, s, qi, ki: (b * nq + qi[s], col)

    def kmap(col):
        return lambda b, s, qi, ki: (b * nk + ki[s], col)

    grid_spec = pltpu.PrefetchScalarGridSpec(
        num_scalar_prefetch=2,
        grid=(bsz, len(pairs)),
        in_specs=[pl.BlockSpec((tq, 512), qmap(OD_Q // 512)),
                  pl.BlockSpec((1, tk // tq, tq, tq), lambda b, s, qi, ki: (b * nq + qi[s], ki[s], 0, 0)),
                  pl.BlockSpec((tk, 512), kmap(OD_K // 512)),
                  pl.BlockSpec((DSA_WIDTH, tk), lambda b, s, qi, ki: (0, b * nk + ki[s]))],
        out_specs=pl.BlockSpec((tq, DSA_WIDTH), qmap(0)),
        scratch_shapes=[pltpu.VMEM((DSA_HEADS, tq, LANES), jnp.bfloat16),
                        pltpu.VMEM((DSA_HEADS, tq), jnp.float32),
                        pltpu.VMEM((DSA_HEADS, LANES, tq), jnp.float32)])
    return pl.pallas_call(
        partial(_dsa_attn_kernel, tq=tq, tk=tk),
        grid_spec=grid_spec,
        out_shape=jax.ShapeDtypeStruct((T, DSA_WIDTH), jnp.bfloat16),
        compiler_params=_params("parallel", "arbitrary"),
    )(qidx, kidx, hb, sel, hb, v_t)


def _dilated_kernel(q_ref, kp_ref, kc_ref, vp_ref, vc_ref, o_ref, lse_ref, *, tq):
    a = pl.program_id(2)
    row = lax.broadcasted_iota(jnp.int32, (tq, 2 * tq), 0)
    c = lax.broadcasted_iota(jnp.int32, (tq, 2 * tq), 1)
    first_col = jnp.where(a == 0, tq, 0)
    valid = jnp.logical_and(jnp.logical_and(c >= row, c <= row + tq), c >= first_col)
    lane = lax.broadcasted_iota(jnp.int32, (tq, LANES), 1)
    for p in range(DIL_HEADS // 2):
        sl = slice(LANES * p, LANES * (p + 1))
        kk = jnp.concatenate([kp_ref[:, sl], kc_ref[:, sl]], axis=0)
        vv = jnp.concatenate([vp_ref[:, sl], vc_ref[:, sl]], axis=0)
        outs, lses = [], []
        for qh in _split_head_pair(q_ref[:, sl]):
            s = lax.dot_general(qh, kk, (((1,), (1,)), ((), ())), preferred_element_type=jnp.float32)
            s = jnp.where(valid, s, NEG_BIG)
            m = jnp.max(s, axis=1, keepdims=True)
            e = jnp.exp(s - m)
            l = jnp.sum(e, axis=1, keepdims=True)
            outs.append(jnp.dot(e.astype(jnp.bfloat16), vv, preferred_element_type=jnp.float32) / l)
            lses.append(m + jnp.log(l))
        o_ref[:, sl] = jnp.where(lane < 64, outs[0], outs[1])
        lse_ref[:, sl] = jnp.where(lane < 64, lses[0], lses[1])


def _dil_proj_kernel(x_ref, w_ref, o_ref, *scratch, dil):
    res = jnp.dot(x_ref[...].astype(jnp.bfloat16), w_ref[...], preferred_element_type=jnp.float32)
    if dil == 1:
        o_ref[...] = res.astype(o_ref.dtype)
        return
    res_ref, = scratch
    rows, cols = res.shape[0] // dil, res.shape[1]
    for c in range(cols // LANES):
        res_ref[c] = res[:, LANES * c:LANES * (c + 1)]
    for r in range(dil):
        for c in range(cols // LANES):
            o_ref[:, cols * r + LANES * c:cols * r + LANES * (c + 1)] = (
                res_ref.at[c][pl.ds(r, rows, stride=dil), :].astype(o_ref.dtype))


def _dil_proj(x, w, dil, tm=1024):
    T, K = x.shape
    C = w.shape[1]
    tm = min(tm, T)
    return pl.pallas_call(
        partial(_dil_proj_kernel, dil=dil),
        grid=(T // tm,),
        in_specs=[pl.BlockSpec((tm, K), lambda i: (i, 0)), pl.BlockSpec((K, C), lambda i: (0, 0))],
        out_specs=pl.BlockSpec((tm // dil, dil * C), lambda i: (i, 0)),
        out_shape=jax.ShapeDtypeStruct((T // dil, dil * C), jnp.bfloat16),
        scratch_shapes=[] if dil == 1 else [pltpu.VMEM((C // LANES, tm, LANES), jnp.float32)],
        compiler_params=_params("parallel"),
    )(x, w)


def _dilated_group(pg, bsz, L, g, tq):
    window, dil = DIL_PATTERNS[g]
    assert window // dil == tq
    M = L // dil
    nb = M // tq
    ncol = DIL_COLS // DIL_WIDTH

    def cur(col):
        return lambda b, r, a: (b * nb + a, r * ncol + col)

    def prev(col):
        return lambda b, r, a: (b * nb + jnp.maximum(a - 1, 0), r * ncol + col)

    blk = (tq, DIL_WIDTH)
    out_map = lambda b, r, a: (b * nb + a, r)
    return pl.pallas_call(
        partial(_dilated_kernel, tq=tq),
        grid=(bsz, dil, nb),
        in_specs=[pl.BlockSpec(blk, cur(0)), pl.BlockSpec(blk, prev(1)), pl.BlockSpec(blk, cur(1)),
                  pl.BlockSpec(blk, prev(2)), pl.BlockSpec(blk, cur(2))],
        out_specs=[pl.BlockSpec(blk, out_map), pl.BlockSpec(blk, out_map)],
        out_shape=[jax.ShapeDtypeStruct((bsz * M, dil * DIL_WIDTH), jnp.float32)] * 2,
        compiler_params=_params("parallel", "parallel", "arbitrary"),
    )(pg, pg, pg, pg, pg)


def _dilated_combine_kernel(*refs, dils, tm):
    ng = len(dils)
    o_refs, l_refs, out_ref, scratch = refs[:ng], refs[ng:2 * ng], refs[2 * ng], refs[2 * ng + 1:]

    def token_major(ref, dil, buf):
        if dil == 1:
            return ref[...]
        nslab = DIL_WIDTH // LANES
        for r in range(dil):
            for c in range(nslab):
                buf.at[c][pl.ds(r, tm // dil, stride=dil), :] = ref[:, DIL_WIDTH * r + LANES * c:DIL_WIDTH * r + LANES * (c + 1)]
        return jnp.concatenate([buf[c] for c in range(nslab)], axis=1)

    bufs = iter(scratch)
    outs = [token_major(o_refs[g], d, None if d == 1 else next(bufs)) for g, d in enumerate(dils)]
    lses = [token_major(l_refs[g], d, None if d == 1 else next(bufs)) for g, d in enumerate(dils)]
    m = lses[0]
    for l in lses[1:]:
        m = jnp.maximum(m, l)
    ws = [jnp.exp(l - m) for l in lses]
    num, den = ws[0] * outs[0], ws[0]
    for w, o in zip(ws[1:], outs[1:]):
        num, den = num + w * o, den + w
    out_ref[...] = (num / den).astype(out_ref.dtype)


def _dilated_combine(outs, lses, dils, T, tm=512):
    tm = min(tm, T)
    specs = [pl.BlockSpec((tm // d, d * DIL_WIDTH), lambda i: (i, 0)) for d in dils]
    n_buf = 2 * sum(1 for d in dils if d != 1)
    return pl.pallas_call(
        partial(_dilated_combine_kernel, dils=tuple(dils), tm=tm),
        grid=(T // tm,),
        in_specs=specs + specs,
        out_specs=pl.BlockSpec((tm, DIL_WIDTH), lambda i: (i, 0)),
        out_shape=jax.ShapeDtypeStruct((T, DIL_WIDTH), jnp.bfloat16),
        scratch_shapes=[pltpu.VMEM((DIL_WIDTH // LANES, tm, LANES), jnp.float32)] * n_buf,
        compiler_params=_params("parallel"),
    )(*outs, *lses)


def _mm_nt_kernel(w_ref, x_ref, o_ref):
    o_ref[...] = lax.dot_general(w_ref[...], x_ref[...].astype(jnp.bfloat16), (((1,), (1,)), ((), ())),
                                 preferred_element_type=jnp.float32).astype(o_ref.dtype)


def _matmul_nt(w_t, x, out_dtype, tm=1024):
    N, K = w_t.shape
    T = x.shape[0]
    tm = min(tm, T)
    return pl.pallas_call(
        _mm_nt_kernel,
        grid=(T // tm,),
        in_specs=[pl.BlockSpec((N, K), lambda i: (0, 0)), pl.BlockSpec((tm, K), lambda i: (i, 0))],
        out_specs=pl.BlockSpec((N, tm), lambda i: (0, i)),
        out_shape=jax.ShapeDtypeStruct((N, T), out_dtype),
        compiler_params=_params("parallel"),
    )(w_t, x)


def _odd_weights(w_in):
    D = w_in.shape[0]
    q, k, v, qi, ki, wi, dil = _split(w_in, OD_SIZES)
    zeros = lambda n: jnp.zeros((D, n), w_in.dtype)
    qi_exp = jnp.concatenate(
        [jnp.concatenate([qi[:, IDX_DIM * h:IDX_DIM * (h + 1)] * IDX_DIM ** -0.5, zeros(LANES - IDX_DIM)], axis=1)
         for h in range(IDX_HEADS)], axis=1)
    dil = dil.reshape(D, 3, DIL_GROUPS, DIL_WIDTH)
    w_dil = [jnp.concatenate([dil[:, 0, g] * DIL_HEAD_DIM ** -0.5, dil[:, 1, g], dil[:, 2, g]], axis=1).astype(jnp.bfloat16)
             for g in range(DIL_GROUPS)]
    wb = jnp.concatenate([q * DSA_HEAD_DIM ** -0.5, k, qi_exp, ki, zeros(LANES - IDX_DIM),
                          zeros(OD_COLS_PADDED - OD_KI - LANES)], axis=1).astype(jnp.bfloat16)
    wv_t = v.T.astype(jnp.bfloat16)
    wwi_t = jnp.concatenate([wi * IDX_HEADS ** -0.5, zeros(8 - IDX_HEADS)], axis=1).T.astype(jnp.bfloat16)
    return wb, w_dil, wv_t, wwi_t


def _odd_layer_mix(x2, w_in, bsz, L):
    wb, w_dil, wv_t, wwi_t = _odd_weights(w_in)
    hb = _matmul(x2, wb, jnp.bfloat16, tm=1024, tn=1024)
    v_t = _matmul_nt(wv_t, x2, jnp.bfloat16)
    wi_t = _matmul_nt(wwi_t, x2, jnp.float32)
    tq, tk = min(DSA_TQ, L), min(DSA_TK, L)
    sel = _dsa_select(hb, wi_t, bsz, L, tq)
    o_c = _dsa_attention(hb, v_t, sel, bsz, L, tq, tk)
    dils = [d for _, d in DIL_PATTERNS]
    groups = [_dilated_group(_dil_proj(x2, w_dil[g], dils[g]), bsz, L, g, DIL_PATTERNS[g][0] // dils[g])
              for g in range(DIL_GROUPS)]
    o_d = _dilated_combine([o for o, _ in groups], [l for _, l in groups], dils, bsz * L)
    return o_c, o_d


EV_Q, EV_K, EV_V, EV_R, EV_G = 0, 256, 512, 1024, 1536
EV_COLS_PADDED = 2048
GLA_ROWS = 512
S5_CHUNK = 64


def _gla_kernel(q_ref, k_ref, v_ref, r_ref, g_ref, wg_ref, bg_ref, ng_ref, o_ref, st_ref, *, rows):
    C = GLA_CHUNK

    @pl.when(pl.program_id(1) == 0)
    def _():
        st_ref[...] = jnp.zeros(st_ref.shape, jnp.float32)

    ri = lax.broadcasted_iota(jnp.int32, (C, C), 0)
    ci = lax.broadcasted_iota(jnp.int32, (C, C), 1)
    causal = ci <= ri
    tril = jnp.where(causal, 1.0, 0.0).astype(jnp.float32)
    wg = wg_ref[...].astype(jnp.bfloat16)
    bg = bg_ref[...]
    ng = ng_ref[...]

    def chunk(c, carry):
        r0 = pl.multiple_of(c * C, C)
        rs = pl.ds(r0, C)
        logit = jnp.dot(g_ref[rs, :].astype(jnp.bfloat16), wg, preferred_element_type=jnp.float32) + bg
        log_a = jax.nn.log_sigmoid(logit) / GLA_GATE_TAU
        bcum = jnp.dot(tril, log_a, precision=lax.Precision.HIGHEST, preferred_element_type=jnp.float32)
        b_last = bcum[C - 1:C, :]
        q_t = (q_ref[rs, :] * jnp.exp(bcum)).astype(jnp.bfloat16)
        k_t = (k_ref[rs, :] * jnp.exp(-bcum)).astype(jnp.bfloat16)
        k_end = (k_ref[rs, :] * jnp.exp(b_last - bcum)).astype(jnp.bfloat16)
        dec = jnp.exp(b_last)
        for p in range(GLA_HEADS // 2):
            sl = slice(LANES * p, LANES * (p + 1))
            q_halves = _split_head_pair(q_t[:, sl])
            ke_halves = _split_head_pair(k_end[:, sl])
            for half in range(2):
                h = 2 * p + half
                hs = slice(GLA_DV * h, GLA_DV * (h + 1))
                qm = q_halves[half]
                att = lax.dot_general(qm, k_t[:, sl], (((1,), (1,)), ((), ())), preferred_element_type=jnp.float32)
                att = jnp.where(causal, att, 0.0).astype(jnp.bfloat16)
                v_h = v_ref[rs, hs].astype(jnp.bfloat16)
                st = st_ref[h]
                o = jnp.dot(att, v_h, preferred_element_type=jnp.float32)
                o = o + lax.dot_general(qm, st.astype(jnp.bfloat16), (((1,), (1,)), ((), ())),
                                        preferred_element_type=jnp.float32)
                kv_t = lax.dot_general(v_h, ke_halves[half], (((0,), (0,)), ((), ())),
                                       preferred_element_type=jnp.float32)
                st_ref[h] = st * dec[:, sl] + kv_t
                o = o * lax.rsqrt(jnp.mean(o * o, axis=-1, keepdims=True) + LN_EPS) * ng
                o = o * jax.nn.silu(r_ref[rs, hs])
                o_ref[rs, hs] = o.astype(o_ref.dtype)
        return carry

    lax.fori_loop(0, rows // C, chunk, 0)


def _gla(hf, w_gate2, b_gate2, norm_g, bsz, L):
    T = bsz * L
    rows = min(GLA_ROWS, L)
    nb = L // rows
    dkw = GLA_HEADS * GLA_DK
    dvw = GLA_HEADS * GLA_DV
    wg = jnp.pad(w_gate2, ((0, LANES - GLA_GATE_RANK), (0, 0)))

    def rmap(col):
        return lambda b, i: (b * nb + i, col)

    const = lambda b, i: (0, 0)
    return pl.pallas_call(
        partial(_gla_kernel, rows=rows),
        grid=(bsz, nb),
        in_specs=[pl.BlockSpec((rows, dkw), rmap(EV_Q // dkw)),
                  pl.BlockSpec((rows, dkw), rmap(EV_K // dkw)),
                  pl.BlockSpec((rows, dvw), rmap(EV_V // dvw)),
                  pl.BlockSpec((rows, dvw), rmap(EV_R // dvw)),
                  pl.BlockSpec((rows, LANES), rmap(EV_G // LANES)),
                  pl.BlockSpec((LANES, dkw), const),
                  pl.BlockSpec((1, dkw), const),
                  pl.BlockSpec((1, GLA_DV), const)],
        out_specs=pl.BlockSpec((rows, dvw), rmap(0)),
        out_shape=jax.ShapeDtypeStruct((T, dvw), jnp.bfloat16),
        scratch_shapes=[pltpu.VMEM((GLA_HEADS, GLA_DV, LANES), jnp.float32)],
        compiler_params=_params("parallel", "arbitrary"),
    )(hf, hf, hf, hf, hf, wg, b_gate2.reshape(1, dkw), norm_g.reshape(1, GLA_DV))


def _s5_tables(a_re, a_im, log_dt, b_re, b_im, c_re, c_im, d_skip):
    f32 = jnp.float32
    Cs, G, P, N = S5_CHUNK, S5_GROUPS, S5_STATE, S5_GROUP
    lam_re = jnp.minimum(a_re.astype(f32), S5_MAX_RE)
    lam_im = a_im.astype(f32)
    dt = jnp.exp(log_dt.astype(f32))[:, None]
    mag = jnp.exp(lam_re * dt)
    ab_re = mag * jnp.cos(lam_im * dt)
    ab_im = mag * jnp.sin(lam_im * dt)
    inv = 1.0 / (lam_re * lam_re + lam_im * lam_im)
    z_re = ((ab_re - 1.0) * lam_re + ab_im * lam_im) * inv
    z_im = (ab_im * lam_re - (ab_re - 1.0) * lam_im) * inv
    br, bi = b_re.astype(f32), b_im.astype(f32)
    bb_re = z_re[..., None] * br - z_im[..., None] * bi
    bb_im = z_re[..., None] * bi + z_im[..., None] * br
    kk = jnp.arange(Cs + 1, dtype=f32)[:, None, None]
    pmag = jnp.exp(kk * (lam_re * dt))
    pw_re = pmag * jnp.cos(kk * (lam_im * dt))
    pw_im = pmag * jnp.sin(kk * (lam_im * dt))
    cr, ci = c_re.astype(f32), c_im.astype(f32)
    ca_re = cr[None] * pw_re[:, :, None, :] - ci[None] * pw_im[:, :, None, :]
    ca_im = cr[None] * pw_im[:, :, None, :] + ci[None] * pw_re[:, :, None, :]
    hi = lax.Precision.HIGHEST
    kern = (jnp.einsum('kgnp,gpm->kgnm', ca_re[:Cs], bb_re, precision=hi)
            - jnp.einsum('kgnp,gpm->kgnm', ca_im[:Cs], bb_im, precision=hi))
    jj = jnp.arange(Cs)[:, None]
    ii = jnp.arange(Cs)[None, :]
    tz = jnp.where((ii >= jj)[:, :, None, None, None], kern[jnp.maximum(ii - jj, 0)], 0.0)
    tz = jnp.transpose(tz, (2, 4, 0, 3, 1)).reshape(G, N * Cs, N * Cs)
    rev_re, rev_im = pw_re[Cs - 1::-1][:Cs], pw_im[Cs - 1::-1][:Cs]
    ws_re = rev_re[..., None] * bb_re[None] - rev_im[..., None] * bb_im[None]
    ws_im = rev_re[..., None] * bb_im[None] + rev_im[..., None] * bb_re[None]
    to_ws = lambda w: jnp.pad(jnp.transpose(w, (1, 3, 0, 2)).reshape(G, N * Cs, P), ((0, 0), (0, 0), (0, LANES - P)))
    to_wo = lambda w: jnp.pad(jnp.transpose(w, (1, 3, 2, 0)).reshape(G, P, N * Cs), ((0, 0), (0, LANES - P), (0, 0)))
    a_cs = jnp.stack([jnp.pad(pw_re[Cs], ((0, 0), (0, LANES - P))), jnp.pad(pw_im[Cs], ((0, 0), (0, LANES - P)))], axis=1)
    d_exp = jnp.repeat(d_skip.astype(f32).reshape(G, 1, N), Cs, axis=2)
    bf = jnp.bfloat16
    return (tz.astype(bf), to_ws(ws_re).astype(bf), to_ws(ws_im).astype(bf),
            to_wo(ca_re[1:]).astype(bf), to_wo(-ca_im[1:]).astype(bf), a_cs, d_exp)


def _s5_kernel(u_ref, tz_ref, wsr_ref, wsi_ref, wor_ref, woi_ref, acs_ref, d_ref, y_ref, xr_ref, xi_ref,
               *, nchunk, nbatch):
    u32 = jnp.concatenate([u_ref[m] for m in range(S5_GROUP)], axis=1)
    u = u32.astype(jnp.bfloat16)
    xr_ref[...] = jnp.dot(u, wsr_ref[0], preferred_element_type=jnp.float32)
    xi_ref[...] = jnp.dot(u, wsi_ref[0], preferred_element_type=jnp.float32)
    ar = acs_ref[0, 0:1, :]
    ai = acs_ref[0, 1:2, :]

    def step(c, carry):
        new = []
        for b in range(nbatch):
            sr, si = carry[2 * b], carry[2 * b + 1]
            row = pl.ds(b * nchunk + c, 1)
            lr, li = xr_ref[row, :], xi_ref[row, :]
            xr_ref[row, :] = sr
            xi_ref[row, :] = si
            new += [ar * sr - ai * si + lr, ar * si + ai * sr + li]
        return tuple(new)

    zero = jnp.zeros((1, LANES), jnp.float32)
    lax.fori_loop(0, nchunk, step, (zero,) * (2 * nbatch))
    y = jnp.dot(u, tz_ref[0], preferred_element_type=jnp.float32)
    y = y + jnp.dot(xr_ref[...].astype(jnp.bfloat16), wor_ref[0], preferred_element_type=jnp.float32)
    y = y + jnp.dot(xi_ref[...].astype(jnp.bfloat16), woi_ref[0], preferred_element_type=jnp.float32)
    y = jax.nn.gelu(y + d_ref[0] * u32)
    for n in range(S5_GROUP):
        y_ref[n] = y[:, S5_CHUNK * n:S5_CHUNK * (n + 1)]


def _s5_scan(u_t, tables, bsz, nchunk):
    width, R, Cs = u_t.shape
    G, N = S5_GROUPS, S5_GROUP
    W = N * Cs
    tz, wsr, wsi, wor, woi, a_cs, d_exp = tables
    gmap = lambda g: (g, 0, 0)
    return pl.pallas_call(
        partial(_s5_kernel, nchunk=nchunk, nbatch=bsz),
        grid=(G,),
        in_specs=[pl.BlockSpec((N, R, Cs), gmap), pl.BlockSpec((1, W, W), gmap),
                  pl.BlockSpec((1, W, LANES), gmap), pl.BlockSpec((1, W, LANES), gmap),
                  pl.BlockSpec((1, LANES, W), gmap), pl.BlockSpec((1, LANES, W), gmap),
                  pl.BlockSpec((1, 2, LANES), gmap), pl.BlockSpec((1, 1, W), gmap)],
        out_specs=pl.BlockSpec((N, R, Cs), gmap),
        out_shape=jax.ShapeDtypeStruct((width, R, Cs), jnp.float32),
        scratch_shapes=[pltpu.VMEM((R, LANES), jnp.float32), pltpu.VMEM((R, LANES), jnp.float32)],
        compiler_params=_params("parallel"),
    )(u_t, tz, wsr, wsi, wor, woi, a_cs, d_exp)


def _glu_kernel(yt_ref, wt_ref, b_ref, o_ref):
    y = yt_ref[...]
    gate = jnp.dot(wt_ref[...], y.astype(jnp.bfloat16), preferred_element_type=jnp.float32) + b_ref[...]
    o_ref[...] = (y * jax.nn.sigmoid(gate)).T.astype(o_ref.dtype)


def _glu(y_t, w_glu, b_glu, tm=1024):
    W, T = y_t.shape
    tm = min(tm, T)
    return pl.pallas_call(
        _glu_kernel,
        grid=(T // tm,),
        in_specs=[pl.BlockSpec((W, tm), lambda i: (0, i)),
                  pl.BlockSpec((W, W), lambda i: (0, 0)),
                  pl.BlockSpec((W, 1), lambda i: (0, 0))],
        out_specs=pl.BlockSpec((tm, W), lambda i: (i, 0)),
        out_shape=jax.ShapeDtypeStruct((T, W), jnp.bfloat16),
        compiler_params=_params("parallel"),
    )(y_t, w_glu.T.astype(jnp.bfloat16), b_glu.reshape(W, 1))


def _s5(u_t, s5_params, w_glu, b_glu, bsz, L):
    T = bsz * L
    nchunk = L // S5_CHUNK
    y_t = _s5_scan(u_t.reshape(S5_WIDTH, bsz * nchunk, S5_CHUNK), _s5_tables(*s5_params), bsz, nchunk)
    return _glu(y_t.reshape(S5_WIDTH, T), w_glu, b_glu)


def _even_weights(w_in):
    D = w_in.shape[0]
    q, k, v, r, g_lr, u = _split(w_in, EV_SIZES)
    pad = jnp.zeros((D, EV_COLS_PADDED - EV_G - GLA_GATE_RANK), w_in.dtype)
    wb = jnp.concatenate([q * GLA_DK ** -0.5, k, v, r, g_lr, pad], axis=1).astype(jnp.bfloat16)
    return wb, u.T.astype(jnp.bfloat16)


def _even_layer_mix(x2, w_in, w_gate2, b_gate2, norm_g, s5_params, w_glu, b_glu, bsz, L):
    wb, wu_t = _even_weights(w_in)
    hf = _matmul(x2, wb, jnp.float32, tm=1024, tn=512)
    u_t = _matmul_nt(wu_t, x2, jnp.float32)
    o_a = _gla(hf, w_gate2, b_gate2, norm_g, bsz, L)
    o_b = _s5(u_t, s5_params, w_glu, b_glu, bsz, L)
    return o_a, o_b


MOE_TILE = 512
ROW_TILE = 8
DMA_UNROLL = 8
MOE_VMEM_LIMIT_BYTES = 56 * 1024 * 1024


def _router_kernel(x_ref, w_ref, b_ref, e_ref, g_ref):
    logits = jnp.dot(x_ref[...].astype(jnp.bfloat16), w_ref[...], preferred_element_type=jnp.float32) + b_ref[...]
    tm = logits.shape[0]
    lane = lax.broadcasted_iota(jnp.int32, (tm, LANES), 1)
    logits = jnp.where(lane < N_EXPERTS, logits, NEG_BIG)
    tops, idxs = [], []
    for _ in range(TOP_K):
        m = jnp.max(logits, axis=1, keepdims=True)
        idx = jnp.min(jnp.where(logits == m, lane, LANES), axis=1, keepdims=True)
        tops.append(m)
        idxs.append(idx)
        logits = jnp.where(lane == idx, NEG_BIG, logits)
    exps = [jnp.exp(t - tops[0]) for t in tops]
    denom = exps[0]
    for e in exps[1:]:
        denom = denom + e
    lane4 = lax.broadcasted_iota(jnp.int32, (tm, TOP_K), 1)
    e_out = jnp.zeros((tm, TOP_K), jnp.int32)
    g_out = jnp.zeros((tm, TOP_K), jnp.float32)
    for k in range(TOP_K):
        e_out = jnp.where(lane4 == k, idxs[k], e_out)
        g_out = jnp.where(lane4 == k, exps[k] / denom, g_out)
    e_ref[...] = e_out
    g_ref[...] = g_out


def _router(x, router_w, router_b, tm=512):
    T, D = x.shape
    tm = min(tm, T)
    w = jnp.pad(router_w, ((0, 0), (0, LANES - N_EXPERTS))).astype(jnp.bfloat16)
    b = jnp.pad(router_b, (0, LANES - N_EXPERTS)).reshape(1, LANES)
    return pl.pallas_call(
        _router_kernel,
        grid=(T // tm,),
        in_specs=[pl.BlockSpec((tm, D), lambda i: (i, 0)),
                  pl.BlockSpec((D, LANES), lambda i: (0, 0)),
                  pl.BlockSpec((1, LANES), lambda i: (0, 0))],
        out_specs=[pl.BlockSpec((tm, TOP_K), lambda i: (i, 0)), pl.BlockSpec((tm, TOP_K), lambda i: (i, 0))],
        out_shape=[jax.ShapeDtypeStruct((T, TOP_K), jnp.int32), jax.ShapeDtypeStruct((T, TOP_K), jnp.float32)],
        compiler_params=_params("parallel"),
    )(x, w, b)


def _moe_rank_kernel(e_ref, rank_ref, count_ref, carry_ref):
    @pl.when(pl.program_id(0) == 0)
    def _():
        carry_ref[...] = jnp.zeros(carry_ref.shape, jnp.float32)

    e = e_ref[...]
    tm = e.shape[0]
    lane = lax.broadcasted_iota(jnp.int32, (tm, LANES), 1)
    onehot = jnp.zeros((tm, LANES), jnp.float32)
    for k in range(TOP_K):
        onehot = onehot + jnp.where(lane == e[:, k:k + 1], 1.0, 0.0)
    ri = lax.broadcasted_iota(jnp.int32, (tm, tm), 0)
    ci = lax.broadcasted_iota(jnp.int32, (tm, tm), 1)
    strict_lower = jnp.where(ci < ri, 1.0, 0.0).astype(jnp.bfloat16)
    before = jnp.dot(strict_lower, onehot.astype(jnp.bfloat16), preferred_element_type=jnp.float32) + carry_ref[...]
    lane4 = lax.broadcasted_iota(jnp.int32, (tm, TOP_K), 1)
    rank = jnp.zeros((tm, TOP_K), jnp.int32)
    for k in range(TOP_K):
        r_k = jnp.sum(jnp.where(lane == e[:, k:k + 1], before, 0.0), axis=1, keepdims=True).astype(jnp.int32)
        rank = jnp.where(lane4 == k, r_k, rank)
    rank_ref[...] = rank
    carry_ref[...] = carry_ref[...] + jnp.sum(onehot, axis=0, keepdims=True)
    count_ref[...] = carry_ref[...]


def _moe_rank(top_e, tm=256):
    T = top_e.shape[0]
    tm = min(tm, T)
    return pl.pallas_call(
        _moe_rank_kernel,
        grid=(T // tm,),
        in_specs=[pl.BlockSpec((tm, TOP_K), lambda i: (i, 0))],
        out_specs=[pl.BlockSpec((tm, TOP_K), lambda i: (i, 0)), pl.BlockSpec((1, LANES), lambda i: (0, 0))],
        out_shape=[jax.ShapeDtypeStruct((T, TOP_K), jnp.int32), jax.ShapeDtypeStruct((1, LANES), jnp.float32)],
        scratch_shapes=[pltpu.VMEM((1, LANES), jnp.float32)],
        compiler_params=_params("arbitrary"),
    )(top_e)


def _to_token_tiles(x, dst_ref, rows):
    for c in range(ROW_TILE):
        dst_ref[pl.ds(c, rows, stride=ROW_TILE), :] = x[:, LANES * c:LANES * (c + 1)]


def _from_token_tiles(src_ref, rows):
    return jnp.concatenate([src_ref[pl.ds(c, rows, stride=ROW_TILE), :] for c in range(ROW_TILE)], axis=1)


def _moe_dispatch_kernel(dest_ref, x_ref, init_ref, xbuf_ref, xs_ref, sem, *, tm):
    del init_ref
    i = pl.program_id(0)
    n = pl.num_programs(0)
    slot = i % 2

    def row_copy(s, r, d):
        return pltpu.make_async_copy(xs_ref.at[s, pl.ds(pl.multiple_of(r * ROW_TILE, ROW_TILE), ROW_TILE), :],
                                     xbuf_ref.at[pl.ds(pl.multiple_of(d * ROW_TILE, ROW_TILE), ROW_TILE), :], sem.at[s])

    def drain(s):
        def body(a, c):
            row_copy(s, 0, 0).wait()
            return c
        lax.fori_loop(0, tm * TOP_K, body, 0, unroll=DMA_UNROLL)

    @pl.when(i >= 2)
    def _():
        drain(slot)

    _to_token_tiles(x_ref[...], xs_ref.at[slot], tm)

    def body(r, c):
        for k in range(TOP_K):
            row_copy(slot, r, dest_ref[r * TOP_K + k]).start()
        return c

    lax.fori_loop(0, tm, body, 0, unroll=DMA_UNROLL)

    @pl.when(i == n - 1)
    def _():
        drain(slot)

        @pl.when(n > 1)
        def _():
            drain(1 - slot)


def _moe_dispatch(x, dest_flat, n_rows, tm=256):
    T, D = x.shape
    tm = min(tm, T)
    assert D == ROW_TILE * LANES
    init = jnp.zeros((n_rows * ROW_TILE, LANES), jnp.float32)
    return pl.pallas_call(
        partial(_moe_dispatch_kernel, tm=tm),
        grid=(T // tm,),
        in_specs=[pl.BlockSpec((tm * TOP_K,), lambda i: (i,), memory_space=pltpu.SMEM),
                  pl.BlockSpec((tm, D), lambda i: (i, 0)),
                  pl.BlockSpec(memory_space=pl.ANY)],
        out_specs=pl.BlockSpec(memory_space=pl.ANY),
        out_shape=jax.ShapeDtypeStruct((n_rows * ROW_TILE, LANES), jnp.float32),
        scratch_shapes=[pltpu.VMEM((2, tm * ROW_TILE, LANES), jnp.float32), pltpu.SemaphoreType.DMA((2,))],
        input_output_aliases={2: 0},
        compiler_params=_params("arbitrary"),
    )(dest_flat, x, init)


def _moe_ffn_kernel(te_ref, nt_ref, x_ref, w1_ref, b1_ref, w2_ref, b2_ref, y_ref, w1b_ref, w2b_ref):
    i = pl.program_id(0)
    prev = te_ref[jnp.maximum(i - 1, 0)]

    @pl.when(jnp.logical_or(i == 0, te_ref[i] != prev))
    def _():
        w1b_ref[...] = w1_ref[0].astype(jnp.bfloat16)
        w2b_ref[...] = w2_ref[0].astype(jnp.bfloat16)

    @pl.when(i < nt_ref[0])
    def _():
        x = _from_token_tiles(x_ref, MOE_TILE).astype(jnp.bfloat16)
        h = jnp.dot(x, w1b_ref[...], preferred_element_type=jnp.float32) + b1_ref[0]
        glu = jnp.minimum(h[:, :D_FF], SWIGLU_LIMIT)
        lin = jnp.clip(h[:, D_FF:], -SWIGLU_LIMIT, SWIGLU_LIMIT)
        act = glu * jax.nn.sigmoid(SWIGLU_ALPHA * glu) * (lin + 1.0)
        y = jnp.dot(act.astype(jnp.bfloat16), w2b_ref[...], preferred_element_type=jnp.float32) + b2_ref[0]
        _to_token_tiles(y, y_ref, MOE_TILE)

    @pl.when(i >= nt_ref[0])
    def _():
        y_ref[...] = jnp.zeros(y_ref.shape, y_ref.dtype)


def _moe_expert_ffn(xbuf, tile_expert, n_used, w1, b1, w2, b2):
    D, F2 = w1.shape[1], w1.shape[2]
    P = xbuf.shape[0] // ROW_TILE
    n_tiles = P // MOE_TILE
    blk = (MOE_TILE * ROW_TILE, LANES)
    grid_spec = pltpu.PrefetchScalarGridSpec(
        num_scalar_prefetch=2,
        grid=(n_tiles,),
        in_specs=[pl.BlockSpec(blk, lambda i, te, nt: (jnp.minimum(i, nt[0] - 1), 0)),
                  pl.BlockSpec((1, D, F2), lambda i, te, nt: (te[i], 0, 0)),
                  pl.BlockSpec((1, 1, F2), lambda i, te, nt: (te[i], 0, 0)),
                  pl.BlockSpec((1, D_FF, D), lambda i, te, nt: (te[i], 0, 0)),
                  pl.BlockSpec((1, 1, D), lambda i, te, nt: (te[i], 0, 0))],
        out_specs=pl.BlockSpec(blk, lambda i, te, nt: (i, 0)),
        scratch_shapes=[pltpu.VMEM((D, F2), jnp.bfloat16), pltpu.VMEM((D_FF, D), jnp.bfloat16)],
    )
    return pl.pallas_call(
        _moe_ffn_kernel,
        grid_spec=grid_spec,
        out_shape=jax.ShapeDtypeStruct(xbuf.shape, jnp.float32),
        compiler_params=pltpu.CompilerParams(dimension_semantics=("arbitrary",),
                                             vmem_limit_bytes=MOE_VMEM_LIMIT_BYTES),
    )(tile_expert, n_used, xbuf, w1, b1.reshape(-1, 1, F2), w2, b2.reshape(-1, 1, D))


def _moe_combine_kernel(dest_ref, gate_ref, x_ref, g_ref, b_ref, ybuf_ref, o_ref, rows_ref, sem, *, tm):
    def row_copy(k, r, d):
        return pltpu.make_async_copy(ybuf_ref.at[pl.ds(pl.multiple_of(d * ROW_TILE, ROW_TILE), ROW_TILE), :],
                                     rows_ref.at[k, pl.ds(pl.multiple_of(r * ROW_TILE, ROW_TILE), ROW_TILE), :], sem.at[0])

    def start(r, c):
        for k in range(TOP_K):
            row_copy(k, r, dest_ref[r * TOP_K + k]).start()
        return c

    lax.fori_loop(0, tm, start, 0, unroll=DMA_UNROLL)

    def wait(a, c):
        row_copy(0, 0, 0).wait()
        return c

    lax.fori_loop(0, tm * TOP_K, wait, 0, unroll=DMA_UNROLL)
    gate = gate_ref[...]
    ffn = gate[:, 0:1] * _from_token_tiles(rows_ref.at[0], tm)
    for k in range(1, TOP_K):
        ffn = ffn + gate[:, k:k + 1] * _from_token_tiles(rows_ref.at[k], tm)
    o_ref[...] = _layer_norm_rows(DEEPNORM_ALPHA * x_ref[...] + ffn, g_ref[...], b_ref[...])


def _moe_combine(ybuf, dest_flat, gate, x, g, b, tm=256):
    T, D = x.shape
    tm = min(tm, T)
    return pl.pallas_call(
        partial(_moe_combine_kernel, tm=tm),
        grid=(T // tm,),
        in_specs=[pl.BlockSpec((tm * TOP_K,), lambda i: (i,), memory_space=pltpu.SMEM),
                  pl.BlockSpec((tm, TOP_K), lambda i: (i, 0)),
                  pl.BlockSpec((tm, D), lambda i: (i, 0)),
                  pl.BlockSpec((1, D), lambda i: (0, 0)),
                  pl.BlockSpec((1, D), lambda i: (0, 0)),
                  pl.BlockSpec(memory_space=pl.ANY)],
        out_specs=pl.BlockSpec((tm, D), lambda i: (i, 0)),
        out_shape=jax.ShapeDtypeStruct((T, D), jnp.float32),
        scratch_shapes=[pltpu.VMEM((TOP_K, tm * ROW_TILE, LANES), jnp.float32), pltpu.SemaphoreType.DMA((1,))],
        compiler_params=_params("arbitrary"),
    )(dest_flat, gate, x, g.reshape(1, D), b.reshape(1, D), ybuf)


def _moe_layer(x, router_w, router_b, w1, b1, w2, b2, first_expert, ln_g, ln_b):
    T, D = x.shape
    A = T * TOP_K
    n_tiles = -(-(A + N_EXPERTS * (MOE_TILE - 1)) // MOE_TILE)
    top_e, gate = _router(x, router_w, router_b)
    rank, counts = _moe_rank(top_e)
    counts = counts[0, :N_EXPERTS].astype(jnp.int32)
    padded = (counts + MOE_TILE - 1) // MOE_TILE * MOE_TILE
    pend = jnp.cumsum(padded)
    pstart = pend - padded
    dest = (pstart[top_e] + rank).reshape(A)
    tile_start = jnp.arange(n_tiles, dtype=jnp.int32) * MOE_TILE
    tile_expert = jnp.minimum(jnp.sum(pend[None, :] <= tile_start[:, None], axis=1), N_EXPERTS - 1).astype(jnp.int32)
    n_used = (pend[-1:] // MOE_TILE).astype(jnp.int32)
    xbuf = _moe_dispatch(x, dest, n_tiles * MOE_TILE)
    ybuf = _moe_expert_ffn(xbuf, tile_expert + first_expert, n_used, w1, b1, w2, b2)
    return _moe_combine(ybuf, dest, gate, x, ln_g, ln_b)


def kernel(x, ev_w_in, gla_w_gate2, gla_b_gate2, gla_norm_g, s5_a_re, s5_a_im, s5_log_dt, s5_b_re, s5_b_im,
           s5_c_re, s5_c_im, s5_d, s5_w_glu, s5_b_glu, ev_w_out, od_w_in, od_w_out, ln1_g, ln1_b, ln2_g, ln2_b,
           router_w, router_b, moe_w1, moe_b1, moe_w2, moe_b2):
    bsz, L, D = x.shape
    T = bsz * L
    x = x.reshape(T, D)
    w1_all, b1_all = moe_w1.reshape((-1,) + moe_w1.shape[2:]), moe_b1.reshape(-1, moe_b1.shape[-1])
    w2_all, b2_all = moe_w2.reshape((-1,) + moe_w2.shape[2:]), moe_b2.reshape(-1, moe_b2.shape[-1])
    for layer in range(DEPTH):
        j = layer // 2
        if layer % 2 == 0:
            s5_params = (s5_a_re[j], s5_a_im[j], s5_log_dt[j], s5_b_re[j], s5_b_im[j], s5_c_re[j], s5_c_im[j], s5_d[j])
            o_1, o_2 = _even_layer_mix(x, ev_w_in[j], gla_w_gate2[j], gla_b_gate2[j], gla_norm_g[j], s5_params,
                                       s5_w_glu[j], s5_b_glu[j], bsz, L)
            w_out = ev_w_out[j]
        else:
            o_1, o_2 = _odd_layer_mix(x, od_w_in[j], bsz, L)
            w_out = od_w_out[j]
        x = _matmul2_res_ln(o_1, o_2, w_out, x, ln1_g[layer], ln1_b[layer])
        x = _moe_layer(x, router_w[layer], router_b[layer], w1_all, b1_all, w2_all, b2_all, layer * N_EXPERTS,
                       ln2_g[layer], ln2_b[layer])
    return x.reshape(bsz, L, D)
```

```python
from functools import partial

import numpy as np
import jax
import jax.numpy as jnp
from jax import lax
from jax.experimental import pallas as pl
from jax.experimental.pallas import tpu as pltpu

D_MODEL = 1024
DEPTH = 4
DEEPNORM_ALPHA = (2.0 * DEPTH) ** 0.25
LN_EPS = 1e-5
MIX_WIDTH = D_MODEL

GLA_HEADS = 4
GLA_DV = MIX_WIDTH // 2 // GLA_HEADS
GLA_DK = GLA_DV // 2
GLA_GATE_RANK = 16
GLA_GATE_TAU = 16.0
GLA_CHUNK = 64

S5_WIDTH = MIX_WIDTH // 2
S5_GROUP = 16
S5_GROUPS = S5_WIDTH // S5_GROUP
S5_STATE = 64
S5_MAX_RE = -1e-4

EV_SIZES = (GLA_HEADS * GLA_DK, GLA_HEADS * GLA_DK, GLA_HEADS * GLA_DV, GLA_HEADS * GLA_DV, GLA_GATE_RANK, S5_WIDTH)

DSA_HEADS = 8
DSA_HEAD_DIM = 64
DSA_WIDTH = DSA_HEADS * DSA_HEAD_DIM
IDX_HEADS = 4
IDX_DIM = 64
DSA_TOPK_MAX = 256

DIL_PATTERNS = ((128, 1), (512, 4), (2048, 16))
DIL_GROUPS = len(DIL_PATTERNS)
DIL_HEADS = 8
DIL_HEAD_DIM = 64
DIL_WIDTH = DIL_HEADS * DIL_HEAD_DIM

OD_SIZES = (DSA_WIDTH, DSA_WIDTH, DSA_WIDTH, IDX_HEADS * IDX_DIM, IDX_DIM, IDX_HEADS, 3 * DIL_GROUPS * DIL_WIDTH)

N_EXPERTS = 32
TOP_K = 4
D_FF = D_MODEL
SWIGLU_ALPHA = 1.702
SWIGLU_LIMIT = 7.0

LANES = 128
VMEM_LIMIT_BYTES = 48 * 1024 * 1024
NEG_BIG = -1e30
INT_MIN = -2 ** 31
INT_MAX = 2 ** 31 - 1
KEY_NEG_INF = -0x7F800000

IDX_EXP_WIDTH = IDX_HEADS * LANES
OD_Q, OD_K, OD_QI, OD_KI = 0, DSA_WIDTH, 2 * DSA_WIDTH, 2 * DSA_WIDTH + IDX_EXP_WIDTH
OD_COLS_PADDED = 2048
DIL_COLS = 3 * DIL_WIDTH


def _split(h, sizes):
    return jnp.split(h, [int(i) for i in np.cumsum(sizes)[:-1]], axis=-1)


def _params(*sem):
    return pltpu.CompilerParams(dimension_semantics=sem, vmem_limit_bytes=VMEM_LIMIT_BYTES)


def _mm_kernel(x_ref, w_ref, o_ref):
    o_ref[...] = jnp.dot(x_ref[...].astype(jnp.bfloat16), w_ref[...].astype(jnp.bfloat16),
                         preferred_element_type=jnp.float32).astype(o_ref.dtype)


def _matmul(x, w, out_dtype=jnp.float32, tm=512, tn=512):
    T, K = x.shape
    N = w.shape[1]
    tm, tn = min(tm, T), min(tn, N)
    assert T % tm == 0 and N % tn == 0
    return pl.pallas_call(
        _mm_kernel,
        grid=(T // tm, N // tn),
        in_specs=[pl.BlockSpec((tm, K), lambda i, j: (i, 0)),
                  pl.BlockSpec((K, tn), lambda i, j: (0, j))],
        out_specs=pl.BlockSpec((tm, tn), lambda i, j: (i, j)),
        out_shape=jax.ShapeDtypeStruct((T, N), out_dtype),
        compiler_params=_params("parallel", "arbitrary"),
    )(x, w)


def _layer_norm_rows(z, g, b):
    mu = jnp.mean(z, axis=-1, keepdims=True)
    zc = z - mu
    var = jnp.mean(zc * zc, axis=-1, keepdims=True)
    return zc * lax.rsqrt(var + LN_EPS) * g + b


def _mm2_res_ln_kernel(a1_ref, a2_ref, w1_ref, w2_ref, x_ref, g_ref, b_ref, o_ref):
    mix = jnp.dot(a1_ref[...], w1_ref[...], preferred_element_type=jnp.float32)
    mix += jnp.dot(a2_ref[...], w2_ref[...], preferred_element_type=jnp.float32)
    o_ref[...] = _layer_norm_rows(DEEPNORM_ALPHA * x_ref[...] + mix, g_ref[...], b_ref[...])


def _matmul2_res_ln(a1, a2, w, x, g, b, tm=512):
    T, K1 = a1.shape
    D = w.shape[1]
    tm = min(tm, T)
    wb = w.astype(jnp.bfloat16)
    return pl.pallas_call(
        _mm2_res_ln_kernel,
        grid=(T // tm,),
        in_specs=[pl.BlockSpec((tm, K1), lambda i: (i, 0)),
                  pl.BlockSpec((tm, a2.shape[1]), lambda i: (i, 0)),
                  pl.BlockSpec((K1, D), lambda i: (0, 0)),
                  pl.BlockSpec((a2.shape[1], D), lambda i: (1, 0)),
                  pl.BlockSpec((tm, D), lambda i: (i, 0)),
                  pl.BlockSpec((1, D), lambda i: (0, 0)),
                  pl.BlockSpec((1, D), lambda i: (0, 0))],
        out_specs=pl.BlockSpec((tm, D), lambda i: (i, 0)),
        out_shape=jax.ShapeDtypeStruct((T, D), jnp.float32),
        compiler_params=_params("parallel"),
    )(a1, a2, wb, wb, x, g.reshape(1, D), b.reshape(1, D))


DSA_TQ = 256
DSA_TK = 1024
COUNT_ROWS = 64
TIE_CHECK_PASS = 8


def _stack_index_heads(qi_blk):
    return jnp.concatenate([qi_blk[:, LANES * h:LANES * (h + 1)] for h in range(IDX_HEADS)], axis=0)


def _index_keys_t(ki_blk, qi_all, wi_t, q0, k0, tq, causal_mask=True):
    tk = ki_blk.shape[0]
    d = lax.dot_general(ki_blk, qi_all, (((1,), (1,)), ((), ())), preferred_element_type=jnp.float32)
    sc = wi_t[0:1, :] * jnp.maximum(d[:, 0:tq], 0.0)
    for h in range(1, IDX_HEADS):
        sc = sc + wi_t[h:h + 1, :] * jnp.maximum(d[:, h * tq:(h + 1) * tq], 0.0)
    if causal_mask:
        kpos = k0 + lax.broadcasted_iota(jnp.int32, (tk, 1), 0)
        qpos = q0 + lax.broadcasted_iota(jnp.int32, (1, tq), 1)
        sc = jnp.where(kpos <= qpos, sc, -jnp.inf)
    bits = lax.bitcast_convert_type(sc, jnp.int32)
    return jnp.where(bits < 0, INT_MIN - bits, bits)


def _dsa_select_kernel(qi_ref, wit_ref, ki_ref, sel_ref, key_ref, *, tq, topk, nq):
    i = pl.program_id(1)
    nblk = i + 1
    q0 = i * tq
    qi_all = _stack_index_heads(qi_ref[...])
    wi_t = wit_ref[...]

    def fill_block(j):
        k0 = pl.multiple_of(j * tq, tq)
        key_ref[j] = _index_keys_t(ki_ref[pl.ds(k0, tq), :], qi_all, wi_t, q0, k0, tq, causal_mask=False)

    def fill_pair(t, carry):
        fill_block(2 * t)
        fill_block(2 * t + 1)
        return carry

    lax.fori_loop(0, i // 2, fill_pair, 0)

    @pl.when(i % 2 == 1)
    def _():
        fill_block(i - 1)

    key_ref[i] = _index_keys_t(ki_ref[pl.ds(pl.multiple_of(q0, tq), tq), :], qi_all, wi_t, q0, q0, tq)

    kr = COUNT_ROWS
    row = lax.broadcasted_iota(jnp.int32, (kr, 1), 0)

    def count(pred):
        def body(j, acc):
            for s in range(tq // kr):
                kk = key_ref[j, s * kr:(s + 1) * kr, :]
                acc = acc + jnp.where(pred(kk, j * tq + s * kr), 1, 0)
            return acc
        acc = lax.fori_loop(0, nblk, body, jnp.zeros((kr, tq), jnp.int32))
        return jnp.sum(acc.astype(jnp.float32), axis=0, keepdims=True).astype(jnp.int32)

    def any_true(mask):
        return jnp.max(jnp.where(mask, 1.0, 0.0)) > 0.5

    def bit_cond(st):
        return jnp.logical_and(st[0] < 32, st[4])

    def bit_step(st):
        p, thr, cnt, final, _ = st
        cand = thr ^ lax.shift_left(jnp.int32(1), 31 - p)
        c = count(lambda kk, base: kk >= cand)
        take = c >= topk
        thr = jnp.where(take, cand, thr)
        cnt = jnp.where(take, c, cnt)
        final = lax.cond(p == TIE_CHECK_PASS,
                         lambda: jnp.where(cnt - count(lambda kk, base: kk == thr) < topk, 1, 0),
                         lambda: final)
        return p + 1, thr, cnt, final, any_true(jnp.logical_and(cnt != topk, final == 0))

    ncols = nblk * tq
    init = (jnp.int32(0), jnp.full((1, tq), INT_MIN, jnp.int32), jnp.full((1, tq), ncols, jnp.int32),
            jnp.zeros((1, tq), jnp.int32), ncols != topk)
    _, thr, cnt, _, _ = lax.while_loop(bit_cond, bit_step, init)
    tie = jnp.logical_and(cnt > topk, thr > KEY_NEG_INF)

    def resolve_ties():
        need = (topk - count(lambda kk, base: kk > thr)).astype(jnp.float32)
        ri = lax.broadcasted_iota(jnp.int32, (kr, kr), 0)
        ci = lax.broadcasted_iota(jnp.int32, (kr, kr), 1)
        lower = jnp.where(ci <= ri, 1.0, 0.0).astype(jnp.bfloat16)
        rowf = row.astype(jnp.float32)

        def body(j, st):
            carry, cut_acc = st
            eqs = [key_ref[j, s * kr:(s + 1) * kr, :] == thr for s in range(tq // kr)]
            within = [jnp.dot(lower, jnp.where(eq, 1.0, 0.0).astype(jnp.bfloat16), preferred_element_type=jnp.float32)
                      for eq in eqs]
            for s, (eq, pref) in enumerate(zip(eqs, within)):
                end_pos = rowf + (j * tq + s * kr + 1).astype(jnp.float32)
                cut_acc = jnp.maximum(cut_acc, jnp.where(jnp.logical_and(eq, pref + carry == need), end_pos, 0.0))
                carry = carry + pref[kr - 1:kr, :]
            return carry, cut_acc

        _, cut_acc = lax.fori_loop(0, nblk, body, (jnp.zeros((1, tq), jnp.float32), jnp.zeros((kr, tq), jnp.float32)))
        return jnp.where(tie, jnp.max(cut_acc, axis=0, keepdims=True).astype(jnp.int32), INT_MAX)

    cut = lax.cond(any_true(tie), resolve_ties, lambda: jnp.full((1, tq), INT_MAX, jnp.int32))
    thr = jnp.maximum(thr, KEY_NEG_INF + 1)

    def emit(j, carry):
        kk = key_ref[j]
        kpos = j * tq + lax.broadcasted_iota(jnp.int32, (tq, 1), 0)
        sel = jnp.logical_or(kk > thr, jnp.logical_and(kk == thr, kpos < cut))
        sel_ref[0, j] = jnp.where(sel, 1, 0).astype(jnp.int8)
        return carry

    lax.fori_loop(0, nblk, emit, 0)

    def clear(j, carry):
        sel_ref[0, j] = jnp.zeros((tq, tq), jnp.int8)
        return carry

    lax.fori_loop(nblk, nq, clear, 0)


def _dsa_select(hb, wi_t, bsz, L, tq):
    T = bsz * L
    nq = L // tq
    topk = min(DSA_TOPK_MAX, L // 4)
    qmap = lambda b, i: (0, b * nq + i)
    return pl.pallas_call(
        partial(_dsa_select_kernel, tq=tq, topk=topk, nq=nq),
        grid=(bsz, nq),
        in_specs=[pl.BlockSpec((tq, IDX_EXP_WIDTH), lambda b, i: (b * nq + i, OD_QI // IDX_EXP_WIDTH)),
                  pl.BlockSpec((8, tq), qmap),
                  pl.BlockSpec((L, LANES), lambda b, i: (b, OD_KI // LANES))],
        out_specs=pl.BlockSpec((1, nq, tq, tq), lambda b, i: (b * nq + i, 0, 0, 0)),
        out_shape=jax.ShapeDtypeStruct((bsz * nq, nq, tq, tq), jnp.int8),
        scratch_shapes=[pltpu.VMEM((nq, tq, tq), jnp.int32)],
        compiler_params=_params("parallel", "arbitrary"),
    )(hb, wi_t, hb)


def _split_head_pair(x_pair):
    lane = lax.broadcasted_iota(jnp.int32, x_pair.shape, 1)
    zero = jnp.zeros_like(x_pair)
    return jnp.where(lane < 64, x_pair, zero), jnp.where(lane >= 64, x_pair, zero)


def _dsa_attn_kernel(qidx_ref, kidx_ref, q_ref, sel_ref, k_ref, vt_ref, o_ref, qm_ref, m_ref, acc_ref, *, tq, tk):
    i = qidx_ref[pl.program_id(1)]
    j = kidx_ref[pl.program_id(1)]
    npair = DSA_HEADS // 2

    @pl.when(j == 0)
    def _():
        for p in range(npair):
            qa, qb = _split_head_pair(q_ref[:, LANES * p:LANES * (p + 1)])
            qm_ref[2 * p] = qa
            qm_ref[2 * p + 1] = qb
        m_ref[...] = jnp.full(m_ref.shape, NEG_BIG, jnp.float32)
        acc_ref[...] = jnp.zeros(acc_ref.shape, jnp.float32)

    sel = jnp.concatenate([sel_ref[0, c] for c in range(tk // tq)], axis=0)
    bias = (sel.astype(jnp.float32) - 1.0) * (-NEG_BIG)
    vrow = lax.broadcasted_iota(jnp.int32, (LANES, tk), 0)
    ones = jnp.ones((LANES, tk), jnp.bfloat16)
    scores = []
    for h in range(DSA_HEADS):
        kp = k_ref[:, LANES * (h // 2):LANES * (h // 2 + 1)]
        s = lax.dot_general(kp, qm_ref[h], (((1,), (1,)), ((), ())), preferred_element_type=jnp.float32)
        scores.append(s + bias)
    for h in range(DSA_HEADS):
        vt = vt_ref[LANES * (h // 2):LANES * (h // 2 + 1), :]
        v_aug = jnp.where((vrow < 64) if h % 2 == 0 else (vrow >= 64), vt, ones)
        s = scores[h]
        m_prev = m_ref[h:h + 1, :]
        m_new = jnp.maximum(m_prev, jnp.max(s, axis=0, keepdims=True))
        alpha = jnp.exp(m_prev - m_new)
        e = jnp.exp(s - m_new).astype(jnp.bfloat16)
        acc_ref[h] = alpha * acc_ref[h] + jnp.dot(v_aug, e, preferred_element_type=jnp.float32)
        m_ref[h:h + 1, :] = m_new

    @pl.when(j == ((i + 1) * tq - 1) // tk)
    def _():
        rowi = lax.broadcasted_iota(jnp.int32, (LANES, tq), 0)
        for p in range(npair):
            a = acc_ref[2 * p]
            b = acc_ref[2 * p + 1]
            out_t = jnp.where(rowi < 64, a / a[64:65, :], b / b[0:1, :])
            o_ref[:, LANES * p:LANES * (p + 1)] = out_t.T.astype(o_ref.dtype)


def _dsa_attention(hb, v_t, sel, bsz, L, tq, tk):
    T = bsz * L
    nq, nk = L // tq, L // tk
    pairs = [(i, j) for i in range(nq) for j in range(((i + 1) * tq - 1) // tk + 1)]
    qidx = jnp.asarray(np.array([p[0] for p in pairs], np.int32))
    kidx = jnp.asarray(np.array([p[1] for p in pairs], np.int32))

    def qmap(col):
        return lambda b, s, qi, ki: (b * nq + qi[s], col)

    def kmap(col):
        return lambda b, s, qi, ki: (b * nk + ki[s], col)

    grid_spec = pltpu.PrefetchScalarGridSpec(
        num_scalar_prefetch=2,
        grid=(bsz, len(pairs)),
        in_specs=[pl.BlockSpec((tq, DSA_WIDTH), qmap(OD_Q // DSA_WIDTH)),
                  pl.BlockSpec((1, tk // tq, tq, tq), lambda b, s, qi, ki: (b * nq + qi[s], ki[s], 0, 0)),
                  pl.BlockSpec((tk, DSA_WIDTH), kmap(OD_K // DSA_WIDTH)),
                  pl.BlockSpec((DSA_WIDTH, tk), lambda b, s, qi, ki: (0, b * nk + ki[s]))],
        out_specs=pl.BlockSpec((tq, DSA_WIDTH), qmap(0)),
        scratch_shapes=[pltpu.VMEM((DSA_HEADS, tq, LANES), jnp.bfloat16),
                        pltpu.VMEM((DSA_HEADS, tq), jnp.float32),
                        pltpu.VMEM((DSA_HEADS, LANES, tq), jnp.float32)])
    return pl.pallas_call(
        partial(_dsa_attn_kernel, tq=tq, tk=tk),
        grid_spec=grid_spec,
        out_shape=jax.ShapeDtypeStruct((T, DSA_WIDTH), jnp.bfloat16),
        compiler_params=_params("parallel", "arbitrary"),
    )(qidx, kidx, hb, sel, hb, v_t)


def _dilated_kernel(q_ref, kp_ref, kc_ref, vp_ref, vc_ref, o_ref, lse_ref, *, tq):
    a = pl.program_id(2)
    row = lax.broadcasted_iota(jnp.int32, (tq, 2 * tq), 0)
    c = lax.broadcasted_iota(jnp.int32, (tq, 2 * tq), 1)
    first_col = jnp.where(a == 0, tq, 0)
    valid = jnp.logical_and(jnp.logical_and(c >= row, c <= row + tq), c >= first_col)
    lane = lax.broadcasted_iota(jnp.int32, (tq, LANES), 1)
    for p in range(DIL_HEADS // 2):
        sl = slice(LANES * p, LANES * (p + 1))
        kk = jnp.concatenate([kp_ref[:, sl], kc_ref[:, sl]], axis=0)
        vv = jnp.concatenate([vp_ref[:, sl], vc_ref[:, sl]], axis=0)
        outs, lses = [], []
        for qh in _split_head_pair(q_ref[:, sl]):
            s = lax.dot_general(qh, kk, (((1,), (1,)), ((), ())), preferred_element_type=jnp.float32)
            s = jnp.where(valid, s, NEG_BIG)
            m = jnp.max(s, axis=1, keepdims=True)
            e = jnp.exp(s - m)
            l = jnp.sum(e, axis=1, keepdims=True)
            outs.append(jnp.dot(e.astype(jnp.bfloat16), vv, preferred_element_type=jnp.float32) / l)
            lses.append(m + jnp.log(l))
        o_ref[:, sl] = jnp.where(lane < 64, outs[0], outs[1])
        lse_ref[:, sl] = jnp.where(lane < 64, lses[0], lses[1])


def _dil_proj_kernel(x_ref, w_ref, o_ref, *scratch, dil):
    res = jnp.dot(x_ref[...].astype(jnp.bfloat16), w_ref[...], preferred_element_type=jnp.float32)
    if dil == 1:
        o_ref[...] = res.astype(o_ref.dtype)
        return
    res_ref, = scratch
    rows, cols = res.shape[0] // dil, res.shape[1]
    for c in range(cols // LANES):
        res_ref[c] = res[:, LANES * c:LANES * (c + 1)]
    for r in range(dil):
        for c in range(cols // LANES):
            o_ref[:, cols * r + LANES * c:cols * r + LANES * (c + 1)] = (
                res_ref.at[c][pl.ds(r, rows, stride=dil), :].astype(o_ref.dtype))


def _dil_proj(x, w, dil, tm=1024):
    T, K = x.shape
    C = w.shape[1]
    tm = min(tm, T)
    return pl.pallas_call(
        partial(_dil_proj_kernel, dil=dil),
        grid=(T // tm,),
        in_specs=[pl.BlockSpec((tm, K), lambda i: (i, 0)), pl.BlockSpec((K, C), lambda i: (0, 0))],
        out_specs=pl.BlockSpec((tm // dil, dil * C), lambda i: (i, 0)),
        out_shape=jax.ShapeDtypeStruct((T // dil, dil * C), jnp.bfloat16),
        scratch_shapes=[] if dil == 1 else [pltpu.VMEM((C // LANES, tm, LANES), jnp.float32)],
        compiler_params=_params("parallel"),
    )(x, w)


def _dilated_group(pg, bsz, L, g, tq):
    window, dil = DIL_PATTERNS[g]
    assert window // dil == tq
    M = L // dil
    nb = M // tq
    ncol = DIL_COLS // DIL_WIDTH

    def cur(col):
        return lambda b, r, a: (b * nb + a, r * ncol + col)

    def prev(col):
        return lambda b, r, a: (b * nb + jnp.maximum(a - 1, 0), r * ncol + col)

    blk = (tq, DIL_WIDTH)
    out_map = lambda b, r, a: (b * nb + a, r)
    return pl.pallas_call(
        partial(_dilated_kernel, tq=tq),
        grid=(bsz, dil, nb),
        in_specs=[pl.BlockSpec(blk, cur(0)), pl.BlockSpec(blk, prev(1)), pl.BlockSpec(blk, cur(1)),
                  pl.BlockSpec(blk, prev(2)), pl.BlockSpec(blk, cur(2))],
        out_specs=[pl.BlockSpec(blk, out_map), pl.BlockSpec(blk, out_map)],
        out_shape=[jax.ShapeDtypeStruct((bsz * M, dil * DIL_WIDTH), jnp.float32)] * 2,
        compiler_params=_params("parallel", "parallel", "arbitrary"),
    )(pg, pg, pg, pg, pg)


def _dilated_combine_kernel(*refs, dils, tm):
    ng = len(dils)
    o_refs, l_refs, out_ref, scratch = refs[:ng], refs[ng:2 * ng], refs[2 * ng], refs[2 * ng + 1:]

    def token_major(ref, dil, buf):
        if dil == 1:
            return ref[...]
        nslab = DIL_WIDTH // LANES
        for r in range(dil):
            for c in range(nslab):
                buf.at[c][pl.ds(r, tm // dil, stride=dil), :] = ref[:, DIL_WIDTH * r + LANES * c:DIL_WIDTH * r + LANES * (c + 1)]
        return jnp.concatenate([buf[c] for c in range(nslab)], axis=1)

    bufs = iter(scratch)
    outs = [token_major(o_refs[g], d, None if d == 1 else next(bufs)) for g, d in enumerate(dils)]
    lses = [token_major(l_refs[g], d, None if d == 1 else next(bufs)) for g, d in enumerate(dils)]
    m = lses[0]
    for l in lses[1:]:
        m = jnp.maximum(m, l)
    ws = [jnp.exp(l - m) for l in lses]
    num, den = ws[0] * outs[0], ws[0]
    for w, o in zip(ws[1:], outs[1:]):
        num, den = num + w * o, den + w
    out_ref[...] = (num / den).astype(out_ref.dtype)


def _dilated_combine(outs, lses, dils, T, tm=512):
    tm = min(tm, T)
    specs = [pl.BlockSpec((tm // d, d * DIL_WIDTH), lambda i: (i, 0)) for d in dils]
    n_buf = 2 * sum(1 for d in dils if d != 1)
    return pl.pallas_call(
        partial(_dilated_combine_kernel, dils=tuple(dils), tm=tm),
        grid=(T // tm,),
        in_specs=specs + specs,
        out_specs=pl.BlockSpec((tm, DIL_WIDTH), lambda i: (i, 0)),
        out_shape=jax.ShapeDtypeStruct((T, DIL_WIDTH), jnp.bfloat16),
        scratch_shapes=[pltpu.VMEM((DIL_WIDTH // LANES, tm, LANES), jnp.float32)] * n_buf,
        compiler_params=_params("parallel"),
    )(*outs, *lses)


def _mm_nt_kernel(w_ref, x_ref, o_ref):
    o_ref[...] = lax.dot_general(w_ref[...], x_ref[...].astype(jnp.bfloat16), (((1,), (1,)), ((), ())),
                                 preferred_element_type=jnp.float32).astype(o_ref.dtype)


def _matmul_nt(w_t, x, out_dtype, tm=1024):
    N, K = w_t.shape
    T = x.shape[0]
    tm = min(tm, T)
    return pl.pallas_call(
        _mm_nt_kernel,
        grid=(T // tm,),
        in_specs=[pl.BlockSpec((N, K), lambda i: (0, 0)), pl.BlockSpec((tm, K), lambda i: (i, 0))],
        out_specs=pl.BlockSpec((N, tm), lambda i: (0, i)),
        out_shape=jax.ShapeDtypeStruct((N, T), out_dtype),
        compiler_params=_params("parallel"),
    )(w_t, x)


def _odd_weights(w_in):
    D = w_in.shape[0]
    q, k, v, qi, ki, wi, dil = _split(w_in, OD_SIZES)
    zeros = lambda n: jnp.zeros((D, n), w_in.dtype)
    qi_exp = jnp.concatenate(
        [jnp.concatenate([qi[:, IDX_DIM * h:IDX_DIM * (h + 1)] * IDX_DIM ** -0.5, zeros(LANES - IDX_DIM)], axis=1)
         for h in range(IDX_HEADS)], axis=1)
    dil = dil.reshape(D, 3, DIL_GROUPS, DIL_WIDTH)
    w_dil = [jnp.concatenate([dil[:, 0, g] * DIL_HEAD_DIM ** -0.5, dil[:, 1, g], dil[:, 2, g]], axis=1).astype(jnp.bfloat16)
             for g in range(DIL_GROUPS)]
    wb = jnp.concatenate([q * DSA_HEAD_DIM ** -0.5, k, qi_exp, ki, zeros(LANES - IDX_DIM),
                          zeros(OD_COLS_PADDED - OD_KI - LANES)], axis=1).astype(jnp.bfloat16)
    wv_t = v.T.astype(jnp.bfloat16)
    wwi_t = jnp.concatenate([wi * IDX_HEADS ** -0.5, zeros(8 - IDX_HEADS)], axis=1).T.astype(jnp.bfloat16)
    return wb, w_dil, wv_t, wwi_t


def _odd_layer_mix(x2, w_in, bsz, L):
    wb, w_dil, wv_t, wwi_t = _odd_weights(w_in)
    hb = _matmul(x2, wb, jnp.bfloat16, tm=1024, tn=1024)
    v_t = _matmul_nt(wv_t, x2, jnp.bfloat16)
    wi_t = _matmul_nt(wwi_t, x2, jnp.float32)
    tq, tk = min(DSA_TQ, L), min(DSA_TK, L)
    sel = _dsa_select(hb, wi_t, bsz, L, tq)
    o_c = _dsa_attention(hb, v_t, sel, bsz, L, tq, tk)
    dils = [d for _, d in DIL_PATTERNS]
    groups = [_dilated_group(_dil_proj(x2, w_dil[g], dils[g]), bsz, L, g, DIL_PATTERNS[g][0] // dils[g])
              for g in range(DIL_GROUPS)]
    o_d = _dilated_combine([o for o, _ in groups], [l for _, l in groups], dils, bsz * L)
    return o_c, o_d


EV_Q, EV_K, EV_V, EV_R, EV_G = 0, 256, 512, 1024, 1536
EV_COLS_PADDED = 2048
GLA_ROWS = 512
S5_CHUNK = 64


def _gla_kernel(q_ref, k_ref, v_ref, r_ref, g_ref, wg_ref, bg_ref, ng_ref, o_ref, st_ref, *, rows):
    C = GLA_CHUNK

    @pl.when(pl.program_id(1) == 0)
    def _():
        st_ref[...] = jnp.zeros(st_ref.shape, jnp.float32)

    ri = lax.broadcasted_iota(jnp.int32, (C, C), 0)
    ci = lax.broadcasted_iota(jnp.int32, (C, C), 1)
    causal = ci <= ri
    tril = jnp.where(causal, 1.0, 0.0).astype(jnp.float32)
    wg = wg_ref[...].astype(jnp.bfloat16)
    bg = bg_ref[...]
    ng = ng_ref[...]

    def chunk(c, carry):
        r0 = pl.multiple_of(c * C, C)
        rs = pl.ds(r0, C)
        logit = jnp.dot(g_ref[rs, :].astype(jnp.bfloat16), wg, preferred_element_type=jnp.float32) + bg
        log_a = jax.nn.log_sigmoid(logit) / GLA_GATE_TAU
        bcum = jnp.dot(tril, log_a, precision=lax.Precision.HIGHEST, preferred_element_type=jnp.float32)
        b_last = bcum[C - 1:C, :]
        q_t = (q_ref[rs, :] * jnp.exp(bcum)).astype(jnp.bfloat16)
        k_t = (k_ref[rs, :] * jnp.exp(-bcum)).astype(jnp.bfloat16)
        k_end = (k_ref[rs, :] * jnp.exp(b_last - bcum)).astype(jnp.bfloat16)
        dec = jnp.exp(b_last)
        for p in range(GLA_HEADS // 2):
            sl = slice(LANES * p, LANES * (p + 1))
            q_halves = _split_head_pair(q_t[:, sl])
            ke_halves = _split_head_pair(k_end[:, sl])
            for half in range(2):
                h = 2 * p + half
                hs = slice(GLA_DV * h, GLA_DV * (h + 1))
                qm = q_halves[half]
                att = lax.dot_general(qm, k_t[:, sl], (((1,), (1,)), ((), ())), preferred_element_type=jnp.float32)
                att = jnp.where(causal, att, 0.0).astype(jnp.bfloat16)
                v_h = v_ref[rs, hs].astype(jnp.bfloat16)
                st = st_ref[h]
                o = jnp.dot(att, v_h, preferred_element_type=jnp.float32)
                o = o + lax.dot_general(qm, st.astype(jnp.bfloat16), (((1,), (1,)), ((), ())),
                                        preferred_element_type=jnp.float32)
                kv_t = lax.dot_general(v_h, ke_halves[half], (((0,), (0,)), ((), ())),
                                       preferred_element_type=jnp.float32)
                st_ref[h] = st * dec[:, sl] + kv_t
                o = o * lax.rsqrt(jnp.mean(o * o, axis=-1, keepdims=True) + LN_EPS) * ng
                o = o * jax.nn.silu(r_ref[rs, hs])
                o_ref[rs, hs] = o.astype(o_ref.dtype)
        return carry

    lax.fori_loop(0, rows // C, chunk, 0)


def _gla(hf, w_gate2, b_gate2, norm_g, bsz, L):
    T = bsz * L
    rows = min(GLA_ROWS, L)
    nb = L // rows
    dkw = GLA_HEADS * GLA_DK
    dvw = GLA_HEADS * GLA_DV
    wg = jnp.pad(w_gate2, ((0, LANES - GLA_GATE_RANK), (0, 0)))

    def rmap(col):
        return lambda b, i: (b * nb + i, col)

    const = lambda b, i: (0, 0)
    return pl.pallas_call(
        partial(_gla_kernel, rows=rows),
        grid=(bsz, nb),
        in_specs=[pl.BlockSpec((rows, dkw), rmap(EV_Q // dkw)),
                  pl.BlockSpec((rows, dkw), rmap(EV_K // dkw)),
                  pl.BlockSpec((rows, dvw), rmap(EV_V // dvw)),
                  pl.BlockSpec((rows, dvw), rmap(EV_R // dvw)),
                  pl.BlockSpec((rows, LANES), rmap(EV_G // LANES)),
                  pl.BlockSpec((LANES, dkw), const),
                  pl.BlockSpec((1, dkw), const),
                  pl.BlockSpec((1, GLA_DV), const)],
        out_specs=pl.BlockSpec((rows, dvw), rmap(0)),
        out_shape=jax.ShapeDtypeStruct((T, dvw), jnp.bfloat16),
        scratch_shapes=[pltpu.VMEM((GLA_HEADS, GLA_DV, LANES), jnp.float32)],
        compiler_params=_params("parallel", "arbitrary"),
    )(hf, hf, hf, hf, hf, wg, b_gate2.reshape(1, dkw), norm_g.reshape(1, GLA_DV))


def _s5_tables(a_re, a_im, log_dt, b_re, b_im, c_re, c_im, d_skip):
    f32 = jnp.float32
    Cs, G, P, N = S5_CHUNK, S5_GROUPS, S5_STATE, S5_GROUP
    lam_re = jnp.minimum(a_re.astype(f32), S5_MAX_RE)
    lam_im = a_im.astype(f32)
    dt = jnp.exp(log_dt.astype(f32))[:, None]
    mag = jnp.exp(lam_re * dt)
    ab_re = mag * jnp.cos(lam_im * dt)
    ab_im = mag * jnp.sin(lam_im * dt)
    inv = 1.0 / (lam_re * lam_re + lam_im * lam_im)
    z_re = ((ab_re - 1.0) * lam_re + ab_im * lam_im) * inv
    z_im = (ab_im * lam_re - (ab_re - 1.0) * lam_im) * inv
    br, bi = b_re.astype(f32), b_im.astype(f32)
    bb_re = z_re[..., None] * br - z_im[..., None] * bi
    bb_im = z_re[..., None] * bi + z_im[..., None] * br
    kk = jnp.arange(Cs + 1, dtype=f32)[:, None, None]
    pmag = jnp.exp(kk * (lam_re * dt))
    pw_re = pmag * jnp.cos(kk * (lam_im * dt))
    pw_im = pmag * jnp.sin(kk * (lam_im * dt))
    cr, ci = c_re.astype(f32), c_im.astype(f32)
    ca_re = cr[None] * pw_re[:, :, None, :] - ci[None] * pw_im[:, :, None, :]
    ca_im = cr[None] * pw_im[:, :, None, :] + ci[None] * pw_re[:, :, None, :]
    hi = lax.Precision.HIGHEST
    kern = (jnp.einsum('kgnp,gpm->kgnm', ca_re[:Cs], bb_re, precision=hi)
            - jnp.einsum('kgnp,gpm->kgnm', ca_im[:Cs], bb_im, precision=hi))
    jj = jnp.arange(Cs)[:, None]
    ii = jnp.arange(Cs)[None, :]
    tz = jnp.where((ii >= jj)[:, :, None, None, None], kern[jnp.maximum(ii - jj, 0)], 0.0)
    tz = jnp.transpose(tz, (2, 4, 0, 3, 1)).reshape(G, N * Cs, N * Cs)
    rev_re, rev_im = pw_re[Cs - 1::-1][:Cs], pw_im[Cs - 1::-1][:Cs]
    ws_re = rev_re[..., None] * bb_re[None] - rev_im[..., None] * bb_im[None]
    ws_im = rev_re[..., None] * bb_im[None] + rev_im[..., None] * bb_re[None]
    to_ws = lambda w: jnp.pad(jnp.transpose(w, (1, 3, 0, 2)).reshape(G, N * Cs, P), ((0, 0), (0, 0), (0, LANES - P)))
    to_wo = lambda w: jnp.pad(jnp.transpose(w, (1, 3, 2, 0)).reshape(G, P, N * Cs), ((0, 0), (0, LANES - P), (0, 0)))
    a_cs = jnp.stack([jnp.pad(pw_re[Cs], ((0, 0), (0, LANES - P))), jnp.pad(pw_im[Cs], ((0, 0), (0, LANES - P)))], axis=1)
    d_exp = jnp.repeat(d_skip.astype(f32).reshape(G, 1, N), Cs, axis=2)
    bf = jnp.bfloat16
    return (tz.astype(bf), to_ws(ws_re).astype(bf), to_ws(ws_im).astype(bf),
            to_wo(ca_re[1:]).astype(bf), to_wo(-ca_im[1:]).astype(bf), a_cs, d_exp)


def _s5_kernel(u_ref, tz_ref, wsr_ref, wsi_ref, wor_ref, woi_ref, acs_ref, d_ref, y_ref, xr_ref, xi_ref,
               *, nchunk, nbatch):
    u32 = jnp.concatenate([u_ref[m] for m in range(S5_GROUP)], axis=1)
    u = u32.astype(jnp.bfloat16)
    xr_ref[...] = jnp.dot(u, wsr_ref[0], preferred_element_type=jnp.float32)
    xi_ref[...] = jnp.dot(u, wsi_ref[0], preferred_element_type=jnp.float32)
    ar = acs_ref[0, 0:1, :]
    ai = acs_ref[0, 1:2, :]

    def step(c, carry):
        new = []
        for b in range(nbatch):
            sr, si = carry[2 * b], carry[2 * b + 1]
            row = pl.ds(b * nchunk + c, 1)
            lr, li = xr_ref[row, :], xi_ref[row, :]
            xr_ref[row, :] = sr
            xi_ref[row, :] = si
            new += [ar * sr - ai * si + lr, ar * si + ai * sr + li]
        return tuple(new)

    zero = jnp.zeros((1, LANES), jnp.float32)
    lax.fori_loop(0, nchunk, step, (zero,) * (2 * nbatch))
    y = jnp.dot(u, tz_ref[0], preferred_element_type=jnp.float32)
    y = y + jnp.dot(xr_ref[...].astype(jnp.bfloat16), wor_ref[0], preferred_element_type=jnp.float32)
    y = y + jnp.dot(xi_ref[...].astype(jnp.bfloat16), woi_ref[0], preferred_element_type=jnp.float32)
    y = jax.nn.gelu(y + d_ref[0] * u32)
    for n in range(S5_GROUP):
        y_ref[n] = y[:, S5_CHUNK * n:S5_CHUNK * (n + 1)]


def _s5_scan(u_t, tables, bsz, nchunk):
    width, R, Cs = u_t.shape
    G, N = S5_GROUPS, S5_GROUP
    W = N * Cs
    tz, wsr, wsi, wor, woi, a_cs, d_exp = tables
    gmap = lambda g: (g, 0, 0)
    return pl.pallas_call(
        partial(_s5_kernel, nchunk=nchunk, nbatch=bsz),
        grid=(G,),
        in_specs=[pl.BlockSpec((N, R, Cs), gmap), pl.BlockSpec((1, W, W), gmap),
                  pl.BlockSpec((1, W, LANES), gmap), pl.BlockSpec((1, W, LANES), gmap),
                  pl.BlockSpec((1, LANES, W), gmap), pl.BlockSpec((1, LANES, W), gmap),
                  pl.BlockSpec((1, 2, LANES), gmap), pl.BlockSpec((1, 1, W), gmap)],
        out_specs=pl.BlockSpec((N, R, Cs), gmap),
        out_shape=jax.ShapeDtypeStruct((width, R, Cs), jnp.float32),
        scratch_shapes=[pltpu.VMEM((R, LANES), jnp.float32), pltpu.VMEM((R, LANES), jnp.float32)],
        compiler_params=_params("parallel"),
    )(u_t, tz, wsr, wsi, wor, woi, a_cs, d_exp)


def _glu_kernel(yt_ref, wt_ref, b_ref, o_ref):
    y = yt_ref[...]
    gate = jnp.dot(wt_ref[...], y.astype(jnp.bfloat16), preferred_element_type=jnp.float32) + b_ref[...]
    o_ref[...] = (y * jax.nn.sigmoid(gate)).T.astype(o_ref.dtype)


def _glu(y_t, w_glu, b_glu, tm=1024):
    W, T = y_t.shape
    tm = min(tm, T)
    return pl.pallas_call(
        _glu_kernel,
        grid=(T // tm,),
        in_specs=[pl.BlockSpec((W, tm), lambda i: (0, i)),
                  pl.BlockSpec((W, W), lambda i: (0, 0)),
                  pl.BlockSpec((W, 1), lambda i: (0, 0))],
        out_specs=pl.BlockSpec((tm, W), lambda i: (i, 0)),
        out_shape=jax.ShapeDtypeStruct((T, W), jnp.bfloat16),
        compiler_params=_params("parallel"),
    )(y_t, w_glu.T.astype(jnp.bfloat16), b_glu.reshape(W, 1))


def _s5(u_t, s5_params, w_glu, b_glu, bsz, L):
    T = bsz * L
    nchunk = L // S5_CHUNK
    y_t = _s5_scan(u_t.reshape(S5_WIDTH, bsz * nchunk, S5_CHUNK), _s5_tables(*s5_params), bsz, nchunk)
    return _glu(y_t.reshape(S5_WIDTH, T), w_glu, b_glu)


def _even_weights(w_in):
    D = w_in.shape[0]
    q, k, v, r, g_lr, u = _split(w_in, EV_SIZES)
    pad = jnp.zeros((D, EV_COLS_PADDED - EV_G - GLA_GATE_RANK), w_in.dtype)
    wb = jnp.concatenate([q * GLA_DK ** -0.5, k, v, r, g_lr, pad], axis=1).astype(jnp.bfloat16)
    return wb, u.T.astype(jnp.bfloat16)


def _even_layer_mix(x2, w_in, w_gate2, b_gate2, norm_g, s5_params, w_glu, b_glu, bsz, L):
    wb, wu_t = _even_weights(w_in)
    hf = _matmul(x2, wb, jnp.float32, tm=1024, tn=512)
    u_t = _matmul_nt(wu_t, x2, jnp.float32)
    o_a = _gla(hf, w_gate2, b_gate2, norm_g, bsz, L)
    o_b = _s5(u_t, s5_params, w_glu, b_glu, bsz, L)
    return o_a, o_b


MOE_TILE = 512
ROW_TILE = 8
DMA_UNROLL = 8
MOE_VMEM_LIMIT_BYTES = 56 * 1024 * 1024


def _router_kernel(x_ref, w_ref, b_ref, e_ref, g_ref):
    logits = jnp.dot(x_ref[...].astype(jnp.bfloat16), w_ref[...], preferred_element_type=jnp.float32) + b_ref[...]
    tm = logits.shape[0]
    lane = lax.broadcasted_iota(jnp.int32, (tm, LANES), 1)
    logits = jnp.where(lane < N_EXPERTS, logits, NEG_BIG)
    tops, idxs = [], []
    for _ in range(TOP_K):
        m = jnp.max(logits, axis=1, keepdims=True)
        idx = jnp.min(jnp.where(logits == m, lane, LANES), axis=1, keepdims=True)
        tops.append(m)
        idxs.append(idx)
        logits = jnp.where(lane == idx, NEG_BIG, logits)
    exps = [jnp.exp(t - tops[0]) for t in tops]
    denom = exps[0]
    for e in exps[1:]:
        denom = denom + e
    lane4 = lax.broadcasted_iota(jnp.int32, (tm, TOP_K), 1)
    e_out = jnp.zeros((tm, TOP_K), jnp.int32)
    g_out = jnp.zeros((tm, TOP_K), jnp.float32)
    for k in range(TOP_K):
        e_out = jnp.where(lane4 == k, idxs[k], e_out)
        g_out = jnp.where(lane4 == k, exps[k] / denom, g_out)
    e_ref[...] = e_out
    g_ref[...] = g_out


def _router(x, router_w, router_b, tm=512):
    T, D = x.shape
    tm = min(tm, T)
    w = jnp.pad(router_w, ((0, 0), (0, LANES - N_EXPERTS))).astype(jnp.bfloat16)
    b = jnp.pad(router_b, (0, LANES - N_EXPERTS)).reshape(1, LANES)
    return pl.pallas_call(
        _router_kernel,
        grid=(T // tm,),
        in_specs=[pl.BlockSpec((tm, D), lambda i: (i, 0)),
                  pl.BlockSpec((D, LANES), lambda i: (0, 0)),
                  pl.BlockSpec((1, LANES), lambda i: (0, 0))],
        out_specs=[pl.BlockSpec((tm, TOP_K), lambda i: (i, 0)), pl.BlockSpec((tm, TOP_K), lambda i: (i, 0))],
        out_shape=[jax.ShapeDtypeStruct((T, TOP_K), jnp.int32), jax.ShapeDtypeStruct((T, TOP_K), jnp.float32)],
        compiler_params=_params("parallel"),
    )(x, w, b)


def _moe_rank_kernel(e_ref, rank_ref, count_ref, carry_ref):
    @pl.when(pl.program_id(0) == 0)
    def _():
        carry_ref[...] = jnp.zeros(carry_ref.shape, jnp.float32)

    e = e_ref[...]
    tm = e.shape[0]
    lane = lax.broadcasted_iota(jnp.int32, (tm, LANES), 1)
    onehot = jnp.zeros((tm, LANES), jnp.float32)
    for k in range(TOP_K):
        onehot = onehot + jnp.where(lane == e[:, k:k + 1], 1.0, 0.0)
    ri = lax.broadcasted_iota(jnp.int32, (tm, tm), 0)
    ci = lax.broadcasted_iota(jnp.int32, (tm, tm), 1)
    strict_lower = jnp.where(ci < ri, 1.0, 0.0).astype(jnp.bfloat16)
    before = jnp.dot(strict_lower, onehot.astype(jnp.bfloat16), preferred_element_type=jnp.float32) + carry_ref[...]
    lane4 = lax.broadcasted_iota(jnp.int32, (tm, TOP_K), 1)
    rank = jnp.zeros((tm, TOP_K), jnp.int32)
    for k in range(TOP_K):
        r_k = jnp.sum(jnp.where(lane == e[:, k:k + 1], before, 0.0), axis=1, keepdims=True).astype(jnp.int32)
        rank = jnp.where(lane4 == k, r_k, rank)
    rank_ref[...] = rank
    carry_ref[...] = carry_ref[...] + jnp.sum(onehot, axis=0, keepdims=True)
    count_ref[...] = carry_ref[...]


def _moe_rank(top_e, tm=256):
    T = top_e.shape[0]
    tm = min(tm, T)
    return pl.pallas_call(
        _moe_rank_kernel,
        grid=(T // tm,),
        in_specs=[pl.BlockSpec((tm, TOP_K), lambda i: (i, 0))],
        out_specs=[pl.BlockSpec((tm, TOP_K), lambda i: (i, 0)), pl.BlockSpec((1, LANES), lambda i: (0, 0))],
        out_shape=[jax.ShapeDtypeStruct((T, TOP_K), jnp.int32), jax.ShapeDtypeStruct((1, LANES), jnp.float32)],
        scratch_shapes=[pltpu.VMEM((1, LANES), jnp.float32)],
        compiler_params=_params("arbitrary"),
    )(top_e)


def _to_token_tiles(x, dst_ref, rows):
    for c in range(ROW_TILE):
        dst_ref[pl.ds(c, rows, stride=ROW_TILE), :] = x[:, LANES * c:LANES * (c + 1)]


def _from_token_tiles(src_ref, rows):
    return jnp.concatenate([src_ref[pl.ds(c, rows, stride=ROW_TILE), :] for c in range(ROW_TILE)], axis=1)


def _moe_dispatch_kernel(dest_ref, x_ref, init_ref, xbuf_ref, xs_ref, sem, *, tm):
    del init_ref
    i = pl.program_id(0)
    n = pl.num_programs(0)
    slot = i % 2

    def row_copy(s, r, d):
        return pltpu.make_async_copy(xs_ref.at[s, pl.ds(pl.multiple_of(r * ROW_TILE, ROW_TILE), ROW_TILE), :],
                                     xbuf_ref.at[pl.ds(pl.multiple_of(d * ROW_TILE, ROW_TILE), ROW_TILE), :], sem.at[s])

    def drain(s):
        def body(a, c):
            row_copy(s, 0, 0).wait()
            return c
        lax.fori_loop(0, tm * TOP_K, body, 0, unroll=DMA_UNROLL)

    @pl.when(i >= 2)
    def _():
        drain(slot)

    _to_token_tiles(x_ref[...], xs_ref.at[slot], tm)

    def body(r, c):
        for k in range(TOP_K):
            row_copy(slot, r, dest_ref[r * TOP_K + k]).start()
        return c

    lax.fori_loop(0, tm, body, 0, unroll=DMA_UNROLL)

    @pl.when(i == n - 1)
    def _():
        drain(slot)

        @pl.when(n > 1)
        def _():
            drain(1 - slot)


def _moe_dispatch(x, dest_flat, n_rows, tm=256):
    T, D = x.shape
    tm = min(tm, T)
    assert D == ROW_TILE * LANES
    init = jnp.zeros((n_rows * ROW_TILE, LANES), jnp.float32)
    return pl.pallas_call(
        partial(_moe_dispatch_kernel, tm=tm),
        grid=(T // tm,),
        in_specs=[pl.BlockSpec((tm * TOP_K,), lambda i: (i,), memory_space=pltpu.SMEM),
                  pl.BlockSpec((tm, D), lambda i: (i, 0)),
                  pl.BlockSpec(memory_space=pl.ANY)],
        out_specs=pl.BlockSpec(memory_space=pl.ANY),
        out_shape=jax.ShapeDtypeStruct((n_rows * ROW_TILE, LANES), jnp.float32),
        scratch_shapes=[pltpu.VMEM((2, tm * ROW_TILE, LANES), jnp.float32), pltpu.SemaphoreType.DMA((2,))],
        input_output_aliases={2: 0},
        compiler_params=_params("arbitrary"),
    )(dest_flat, x, init)


def _moe_ffn_kernel(te_ref, nt_ref, x_ref, w1_ref, b1_ref, w2_ref, b2_ref, y_ref, w1b_ref, w2b_ref):
    i = pl.program_id(0)
    prev = te_ref[jnp.maximum(i - 1, 0)]

    @pl.when(jnp.logical_or(i == 0, te_ref[i] != prev))
    def _():
        w1b_ref[...] = w1_ref[0].astype(jnp.bfloat16)
        w2b_ref[...] = w2_ref[0].astype(jnp.bfloat16)

    @pl.when(i < nt_ref[0])
    def _():
        x = _from_token_tiles(x_ref, MOE_TILE).astype(jnp.bfloat16)
        h = jnp.dot(x, w1b_ref[...], preferred_element_type=jnp.float32) + b1_ref[0]
        glu = jnp.minimum(h[:, :D_FF], SWIGLU_LIMIT)
        lin = jnp.clip(h[:, D_FF:], -SWIGLU_LIMIT, SWIGLU_LIMIT)
        act = glu * jax.nn.sigmoid(SWIGLU_ALPHA * glu) * (lin + 1.0)
        y = jnp.dot(act.astype(jnp.bfloat16), w2b_ref[...], preferred_element_type=jnp.float32) + b2_ref[0]
        _to_token_tiles(y, y_ref, MOE_TILE)

    @pl.when(i >= nt_ref[0])
    def _():
        y_ref[...] = jnp.zeros(y_ref.shape, y_ref.dtype)


def _moe_expert_ffn(xbuf, tile_expert, n_used, w1, b1, w2, b2):
    D, F2 = w1.shape[1], w1.shape[2]
    P = xbuf.shape[0] // ROW_TILE
    n_tiles = P // MOE_TILE
    blk = (MOE_TILE * ROW_TILE, LANES)
    grid_spec = pltpu.PrefetchScalarGridSpec(
        num_scalar_prefetch=2,
        grid=(n_tiles,),
        in_specs=[pl.BlockSpec(blk, lambda i, te, nt: (jnp.minimum(i, nt[0] - 1), 0)),
                  pl.BlockSpec((1, D, F2), lambda i, te, nt: (te[i], 0, 0)),
                  pl.BlockSpec((1, 1, F2), lambda i, te, nt: (te[i], 0, 0)),
                  pl.BlockSpec((1, D_FF, D), lambda i, te, nt: (te[i], 0, 0)),
                  pl.BlockSpec((1, 1, D), lambda i, te, nt: (te[i], 0, 0))],
        out_specs=pl.BlockSpec(blk, lambda i, te, nt: (i, 0)),
        scratch_shapes=[pltpu.VMEM((D, F2), jnp.bfloat16), pltpu.VMEM((D_FF, D), jnp.bfloat16)],
    )
    return pl.pallas_call(
        _moe_ffn_kernel,
        grid_spec=grid_spec,
        out_shape=jax.ShapeDtypeStruct(xbuf.shape, jnp.float32),
        compiler_params=pltpu.CompilerParams(dimension_semantics=("arbitrary",),
                                             vmem_limit_bytes=MOE_VMEM_LIMIT_BYTES),
    )(tile_expert, n_used, xbuf, w1, b1.reshape(-1, 1, F2), w2, b2.reshape(-1, 1, D))


def _moe_combine_kernel(dest_ref, gate_ref, x_ref, g_ref, b_ref, ybuf_ref, o_ref, rows_ref, sem, *, tm):
    def row_copy(k, r, d):
        return pltpu.make_async_copy(ybuf_ref.at[pl.ds(pl.multiple_of(d * ROW_TILE, ROW_TILE), ROW_TILE), :],
                                     rows_ref.at[k, pl.ds(pl.multiple_of(r * ROW_TILE, ROW_TILE), ROW_TILE), :], sem.at[0])

    def start(r, c):
        for k in range(TOP_K):
            row_copy(k, r, dest_ref[r * TOP_K + k]).start()
        return c

    lax.fori_loop(0, tm, start, 0, unroll=DMA_UNROLL)

    def wait(a, c):
        row_copy(0, 0, 0).wait()
        return c

    lax.fori_loop(0, tm * TOP_K, wait, 0, unroll=DMA_UNROLL)
    gate = gate_ref[...]
    ffn = gate[:, 0:1] * _from_token_tiles(rows_ref.at[0], tm)
    for k in range(1, TOP_K):
        ffn = ffn + gate[:, k:k + 1] * _from_token_tiles(rows_ref.at[k], tm)
    o_ref[...] = _layer_norm_rows(DEEPNORM_ALPHA * x_ref[...] + ffn, g_ref[...], b_ref[...])


def _moe_combine(ybuf, dest_flat, gate, x, g, b, tm=256):
    T, D = x.shape
    tm = min(tm, T)
    return pl.pallas_call(
        partial(_moe_combine_kernel, tm=tm),
        grid=(T // tm,),
        in_specs=[pl.BlockSpec((tm * TOP_K,), lambda i: (i,), memory_space=pltpu.SMEM),
                  pl.BlockSpec((tm, TOP_K), lambda i: (i, 0)),
                  pl.BlockSpec((tm, D), lambda i: (i, 0)),
                  pl.BlockSpec((1, D), lambda i: (0, 0)),
                  pl.BlockSpec((1, D), lambda i: (0, 0)),
                  pl.BlockSpec(memory_space=pl.ANY)],
        out_specs=pl.BlockSpec((tm, D), lambda i: (i, 0)),
        out_shape=jax.ShapeDtypeStruct((T, D), jnp.float32),
        scratch_shapes=[pltpu.VMEM((TOP_K, tm * ROW_TILE, LANES), jnp.float32), pltpu.SemaphoreType.DMA((1,))],
        compiler_params=_params("arbitrary"),
    )(dest_flat, gate, x, g.reshape(1, D), b.reshape(1, D), ybuf)


def _moe_layer(x, router_w, router_b, w1, b1, w2, b2, first_expert, ln_g, ln_b):
    T, D = x.shape
    A = T * TOP_K
    n_tiles = -(-(A + N_EXPERTS * (MOE_TILE - 1)) // MOE_TILE)
    top_e, gate = _router(x, router_w, router_b)
    rank, counts = _moe_rank(top_e)
    counts = counts[0, :N_EXPERTS].astype(jnp.int32)
    padded = (counts + MOE_TILE - 1) // MOE_TILE * MOE_TILE
    pend = jnp.cumsum(padded)
    pstart = pend - padded
    dest = (pstart[top_e] + rank).reshape(A)
    tile_start = jnp.arange(n_tiles, dtype=jnp.int32) * MOE_TILE
    tile_expert = jnp.minimum(jnp.sum(pend[None, :] <= tile_start[:, None], axis=1), N_EXPERTS - 1).astype(jnp.int32)
    n_used = (pend[-1:] // MOE_TILE).astype(jnp.int32)
    xbuf = _moe_dispatch(x, dest, n_tiles * MOE_TILE)
    ybuf = _moe_expert_ffn(xbuf, tile_expert + first_expert, n_used, w1, b1, w2, b2)
    return _moe_combine(ybuf, dest, gate, x, ln_g, ln_b)


def kernel(x, ev_w_in, gla_w_gate2, gla_b_gate2, gla_norm_g, s5_a_re, s5_a_im, s5_log_dt, s5_b_re, s5_b_im,
           s5_c_re, s5_c_im, s5_d, s5_w_glu, s5_b_glu, ev_w_out, od_w_in, od_w_out, ln1_g, ln1_b, ln2_g, ln2_b,
           router_w, router_b, moe_w1, moe_b1, moe_w2, moe_b2):
    bsz, L, D = x.shape
    T = bsz * L
    x = x.reshape(T, D)
    w1_all, b1_all = moe_w1.reshape((-1,) + moe_w1.shape[2:]), moe_b1.reshape(-1, moe_b1.shape[-1])
    w2_all, b2_all = moe_w2.reshape((-1,) + moe_w2.shape[2:]), moe_b2.reshape(-1, moe_b2.shape[-1])
    for layer in range(DEPTH):
        j = layer // 2
        if layer % 2 == 0:
            s5_params = (s5_a_re[j], s5_a_im[j], s5_log_dt[j], s5_b_re[j], s5_b_im[j], s5_c_re[j], s5_c_im[j], s5_d[j])
            o_1, o_2 = _even_layer_mix(x, ev_w_in[j], gla_w_gate2[j], gla_b_gate2[j], gla_norm_g[j], s5_params,
                                       s5_w_glu[j], s5_b_glu[j], bsz, L)
            w_out = ev_w_out[j]
        else:
            o_1, o_2 = _odd_layer_mix(x, od_w_in[j], bsz, L)
            w_out = od_w_out[j]
        x = _matmul2_res_ln(o_1, o_2, w_out, x, ln1_g[layer], ln1_b[layer])
        x = _moe_layer(x, router_w[layer], router_b[layer], w1_all, b1_all, w2_all, b2_all, layer * N_EXPERTS,
                       ln2_g[layer], ln2_b[layer])
    return x.reshape(bsz, L, D)
```

```python
from functools import partial

import numpy as np
import jax
import jax.numpy as jnp
from jax import lax
from jax.experimental import pallas as pl
from jax.experimental.pallas import tpu as pltpu

D_MODEL = 1024
DEPTH = 4
DEEPNORM_ALPHA = (2.0 * DEPTH) ** 0.25
LN_EPS = 1e-5
MIX_WIDTH = D_MODEL

GLA_HEADS = 4
GLA_DV = MIX_WIDTH // 2 // GLA_HEADS
GLA_DK = GLA_DV // 2
GLA_GATE_RANK = 16
GLA_GATE_TAU = 16.0
GLA_CHUNK = 64

S5_WIDTH = MIX_WIDTH // 2
S5_GROUP = 16
S5_GROUPS = S5_WIDTH // S5_GROUP
S5_STATE = 64
S5_MAX_RE = -1e-4

EV_SIZES = (GLA_HEADS * GLA_DK, GLA_HEADS * GLA_DK, GLA_HEADS * GLA_DV, GLA_HEADS * GLA_DV, GLA_GATE_RANK, S5_WIDTH)

DSA_HEADS = 8
DSA_HEAD_DIM = 64
DSA_WIDTH = DSA_HEADS * DSA_HEAD_DIM
IDX_HEADS = 4
IDX_DIM = 64
DSA_TOPK_MAX = 256

DIL_PATTERNS = ((128, 1), (512, 4), (2048, 16))
DIL_GROUPS = len(DIL_PATTERNS)
DIL_HEADS = 8
DIL_HEAD_DIM = 64
DIL_WIDTH = DIL_HEADS * DIL_HEAD_DIM

OD_SIZES = (DSA_WIDTH, DSA_WIDTH, DSA_WIDTH, IDX_HEADS * IDX_DIM, IDX_DIM, IDX_HEADS, 3 * DIL_GROUPS * DIL_WIDTH)

N_EXPERTS = 32
TOP_K = 4
D_FF = D_MODEL
SWIGLU_ALPHA = 1.702
SWIGLU_LIMIT = 7.0

LANES = 128
VMEM_LIMIT_BYTES = 48 * 1024 * 1024
NEG_BIG = -1e30
INT_MIN = -2 ** 31
INT_MAX = 2 ** 31 - 1
KEY_NEG_INF = -0x7F800000

IDX_EXP_WIDTH = IDX_HEADS * LANES
OD_Q, OD_K, OD_QI, OD_KI = 0, DSA_WIDTH, 2 * DSA_WIDTH, 2 * DSA_WIDTH + IDX_EXP_WIDTH
OD_COLS_PADDED = 2048
DIL_COLS = 3 * DIL_WIDTH


def _split(h, sizes):
    return jnp.split(h, [int(i) for i in np.cumsum(sizes)[:-1]], axis=-1)


def _params(*sem):
    return pltpu.CompilerParams(dimension_semantics=sem, vmem_limit_bytes=VMEM_LIMIT_BYTES)


def _mm_kernel(x_ref, w_ref, o_ref):
    o_ref[...] = jnp.dot(x_ref[...].astype(jnp.bfloat16), w_ref[...].astype(jnp.bfloat16),
                         preferred_element_type=jnp.float32).astype(o_ref.dtype)


def _matmul(x, w, out_dtype=jnp.float32, tm=512, tn=512):
    T, K = x.shape
    N = w.shape[1]
    tm, tn = min(tm, T), min(tn, N)
    assert T % tm == 0 and N % tn == 0
    return pl.pallas_call(
        _mm_kernel,
        grid=(T // tm, N // tn),
        in_specs=[pl.BlockSpec((tm, K), lambda i, j: (i, 0)),
                  pl.BlockSpec((K, tn), lambda i, j: (0, j))],
        out_specs=pl.BlockSpec((tm, tn), lambda i, j: (i, j)),
        out_shape=jax.ShapeDtypeStruct((T, N), out_dtype),
        compiler_params=_params("parallel", "arbitrary"),
    )(x, w)


def _layer_norm_rows(z, g, b):
    mu = jnp.mean(z, axis=-1, keepdims=True)
    zc = z - mu
    var = jnp.mean(zc * zc, axis=-1, keepdims=True)
    return zc * lax.rsqrt(var + LN_EPS) * g + b


def _mm2_res_ln_kernel(a1_ref, a2_ref, w1_ref, w2_ref, x_ref, g_ref, b_ref, o_ref):
    mix = jnp.dot(a1_ref[...], w1_ref[...], preferred_element_type=jnp.float32)
    mix += jnp.dot(a2_ref[...], w2_ref[...], preferred_element_type=jnp.float32)
    o_ref[...] = _layer_norm_rows(DEEPNORM_ALPHA * x_ref[...] + mix, g_ref[...], b_ref[...])


def _matmul2_res_ln(a1, a2, w, x, g, b, tm=512):
    T, K1 = a1.shape
    D = w.shape[1]
    tm = min(tm, T)
    wb = w.astype(jnp.bfloat16)
    return pl.pallas_call(
        _mm2_res_ln_kernel,
        grid=(T // tm,),
        in_specs=[pl.BlockSpec((tm, K1), lambda i: (i, 0)),
                  pl.BlockSpec((tm, a2.shape[1]), lambda i: (i, 0)),
                  pl.BlockSpec((K1, D), lambda i: (0, 0)),
                  pl.BlockSpec((a2.shape[1], D), lambda i: (1, 0)),
                  pl.BlockSpec((tm, D), lambda i: (i, 0)),
                  pl.BlockSpec((1, D), lambda i: (0, 0)),
                  pl.BlockSpec((1, D), lambda i: (0, 0))],
        out_specs=pl.BlockSpec((tm, D), lambda i: (i, 0)),
        out_shape=jax.ShapeDtypeStruct((T, D), jnp.float32),
        compiler_params=_params("parallel"),
    )(a1, a2, wb, wb, x, g.reshape(1, D), b.reshape(1, D))


DSA_TQ = 256
DSA_ATT_TQ = 512
DSA_TK = 1024
COUNT_ROWS = 64
TIE_CHECK_PASS = 8


def _stack_index_heads(qi_blk):
    return jnp.concatenate([qi_blk[:, LANES * h:LANES * (h + 1)] for h in range(IDX_HEADS)], axis=0)


def _index_keys_t(ki_blk, qi_all, wi_t, q0, k0, tq, causal_mask=True):
    tk = ki_blk.shape[0]
    d = lax.dot_general(ki_blk, qi_all, (((1,), (1,)), ((), ())), preferred_element_type=jnp.float32)
    sc = wi_t[0:1, :] * jnp.maximum(d[:, 0:tq], 0.0)
    for h in range(1, IDX_HEADS):
        sc = sc + wi_t[h:h + 1, :] * jnp.maximum(d[:, h * tq:(h + 1) * tq], 0.0)
    if causal_mask:
        kpos = k0 + lax.broadcasted_iota(jnp.int32, (tk, 1), 0)
        qpos = q0 + lax.broadcasted_iota(jnp.int32, (1, tq), 1)
        sc = jnp.where(kpos <= qpos, sc, -jnp.inf)
    bits = lax.bitcast_convert_type(sc, jnp.int32)
    return jnp.where(bits < 0, INT_MIN - bits, bits)


def _dsa_select_kernel(qi_ref, wit_ref, ki_ref, sel_ref, key_ref, *, tq, topk, nq):
    i = pl.program_id(1)
    nblk = i + 1
    q0 = i * tq
    qi_all = _stack_index_heads(qi_ref[...])
    wi_t = wit_ref[...]

    def fill_block(j):
        k0 = pl.multiple_of(j * tq, tq)
        key_ref[j] = _index_keys_t(ki_ref[pl.ds(k0, tq), :], qi_all, wi_t, q0, k0, tq, causal_mask=False)

    def fill_pair(t, carry):
        fill_block(2 * t)
        fill_block(2 * t + 1)
        return carry

    lax.fori_loop(0, i // 2, fill_pair, 0)

    @pl.when(i % 2 == 1)
    def _():
        fill_block(i - 1)

    key_ref[i] = _index_keys_t(ki_ref[pl.ds(pl.multiple_of(q0, tq), tq), :], qi_all, wi_t, q0, q0, tq)

    kr = COUNT_ROWS
    row = lax.broadcasted_iota(jnp.int32, (kr, 1), 0)

    def count(pred):
        def body(j, acc):
            for s in range(tq // kr):
                kk = key_ref[j, s * kr:(s + 1) * kr, :]
                acc = acc + jnp.where(pred(kk, j * tq + s * kr), 1, 0)
            return acc
        acc = lax.fori_loop(0, nblk, body, jnp.zeros((kr, tq), jnp.int32))
        return jnp.sum(acc.astype(jnp.float32), axis=0, keepdims=True).astype(jnp.int32)

    def any_true(mask):
        return jnp.max(jnp.where(mask, 1.0, 0.0)) > 0.5

    def bit_cond(st):
        return jnp.logical_and(st[0] < 32, st[4])

    def bit_step(st):
        p, thr, cnt, final, _ = st
        cand = thr ^ lax.shift_left(jnp.int32(1), 31 - p)
        c = count(lambda kk, base: kk >= cand)
        take = c >= topk
        thr = jnp.where(take, cand, thr)
        cnt = jnp.where(take, c, cnt)
        final = lax.cond(p == TIE_CHECK_PASS,
                         lambda: jnp.where(cnt - count(lambda kk, base: kk == thr) < topk, 1, 0),
                         lambda: final)
        return p + 1, thr, cnt, final, any_true(jnp.logical_and(cnt != topk, final == 0))

    ncols = nblk * tq
    init = (jnp.int32(0), jnp.full((1, tq), INT_MIN, jnp.int32), jnp.full((1, tq), ncols, jnp.int32),
            jnp.zeros((1, tq), jnp.int32), ncols != topk)
    _, thr, cnt, _, _ = lax.while_loop(bit_cond, bit_step, init)
    tie = jnp.logical_and(cnt > topk, thr > KEY_NEG_INF)

    def resolve_ties():
        need = (topk - count(lambda kk, base: kk > thr)).astype(jnp.float32)
        ri = lax.broadcasted_iota(jnp.int32, (kr, kr), 0)
        ci = lax.broadcasted_iota(jnp.int32, (kr, kr), 1)
        lower = jnp.where(ci <= ri, 1.0, 0.0).astype(jnp.bfloat16)
        rowf = row.astype(jnp.float32)

        def body(j, st):
            carry, cut_acc = st
            eqs = [key_ref[j, s * kr:(s + 1) * kr, :] == thr for s in range(tq // kr)]
            within = [jnp.dot(lower, jnp.where(eq, 1.0, 0.0).astype(jnp.bfloat16), preferred_element_type=jnp.float32)
                      for eq in eqs]
            for s, (eq, pref) in enumerate(zip(eqs, within)):
                end_pos = rowf + (j * tq + s * kr + 1).astype(jnp.float32)
                cut_acc = jnp.maximum(cut_acc, jnp.where(jnp.logical_and(eq, pref + carry == need), end_pos, 0.0))
                carry = carry + pref[kr - 1:kr, :]
            return carry, cut_acc

        _, cut_acc = lax.fori_loop(0, nblk, body, (jnp.zeros((1, tq), jnp.float32), jnp.zeros((kr, tq), jnp.float32)))
        return jnp.where(tie, jnp.max(cut_acc, axis=0, keepdims=True).astype(jnp.int32), INT_MAX)

    cut = lax.cond(any_true(tie), resolve_ties, lambda: jnp.full((1, tq), INT_MAX, jnp.int32))
    thr = jnp.maximum(thr, KEY_NEG_INF + 1)

    def emit(j, carry):
        kk = key_ref[j]
        kpos = j * tq + lax.broadcasted_iota(jnp.int32, (tq, 1), 0)
        sel = jnp.logical_or(kk > thr, jnp.logical_and(kk == thr, kpos < cut))
        sel_ref[0, j] = jnp.where(sel, 1, 0).astype(jnp.int8)
        return carry

    lax.fori_loop(0, nblk, emit, 0)

    def clear(j, carry):
        sel_ref[0, j] = jnp.zeros((tq, tq), jnp.int8)
        return carry

    lax.fori_loop(nblk, nq, clear, 0)


def _dsa_select(hb, wi_t, bsz, L, tq):
    T = bsz * L
    nq = L // tq
    topk = min(DSA_TOPK_MAX, L // 4)
    qmap = lambda b, i: (0, b * nq + i)
    return pl.pallas_call(
        partial(_dsa_select_kernel, tq=tq, topk=topk, nq=nq),
        grid=(bsz, nq),
        in_specs=[pl.BlockSpec((tq, IDX_EXP_WIDTH), lambda b, i: (b * nq + i, OD_QI // IDX_EXP_WIDTH)),
                  pl.BlockSpec((8, tq), qmap),
                  pl.BlockSpec((L, LANES), lambda b, i: (b, OD_KI // LANES))],
        out_specs=pl.BlockSpec((1, nq, tq, tq), lambda b, i: (b * nq + i, 0, 0, 0)),
        out_shape=jax.ShapeDtypeStruct((bsz * nq, nq, tq, tq), jnp.int8),
        scratch_shapes=[pltpu.VMEM((nq, tq, tq), jnp.int32)],
        compiler_params=_params("parallel", "arbitrary"),
    )(hb, wi_t, hb)


def _split_head_pair(x_pair):
    lane = lax.broadcasted_iota(jnp.int32, x_pair.shape, 1)
    zero = jnp.zeros_like(x_pair)
    return jnp.where(lane < 64, x_pair, zero), jnp.where(lane >= 64, x_pair, zero)


def _dsa_attn_kernel(qidx_ref, kidx_ref, q_ref, sel_ref, k_ref, vt_ref, o_ref, qm_ref, m_ref, acc_ref, *, tq, tk, ts):
    i = qidx_ref[pl.program_id(1)]
    j = kidx_ref[pl.program_id(1)]
    npair = DSA_HEADS // 2

    @pl.when(j == 0)
    def _():
        for p in range(npair):
            qa, qb = _split_head_pair(q_ref[:, LANES * p:LANES * (p + 1)])
            qm_ref[2 * p] = qa
            qm_ref[2 * p + 1] = qb
        m_ref[...] = jnp.full(m_ref.shape, NEG_BIG, jnp.float32)
        acc_ref[...] = jnp.zeros(acc_ref.shape, jnp.float32)

    sel = jnp.concatenate([jnp.concatenate([sel_ref[a, c] for a in range(tq // ts)], axis=1)
                           for c in range(tk // ts)], axis=0)
    bias = (sel.astype(jnp.float32) - 1.0) * (-NEG_BIG)
    vrow = lax.broadcasted_iota(jnp.int32, (LANES, tk), 0)
    ones = jnp.ones((LANES, tk), jnp.bfloat16)
    scores = []
    for h in range(DSA_HEADS):
        kp = k_ref[:, LANES * (h // 2):LANES * (h // 2 + 1)]
        s = lax.dot_general(kp, qm_ref[h], (((1,), (1,)), ((), ())), preferred_element_type=jnp.float32)
        scores.append(s + bias)
    for h in range(DSA_HEADS):
        vt = vt_ref[LANES * (h // 2):LANES * (h // 2 + 1), :]
        v_aug = jnp.where((vrow < 64) if h % 2 == 0 else (vrow >= 64), vt, ones)
        s = scores[h]
        m_prev = m_ref[h:h + 1, :]
        m_new = jnp.maximum(m_prev, jnp.max(s, axis=0, keepdims=True))
        alpha = jnp.exp(m_prev - m_new)
        e = jnp.exp(s - m_new).astype(jnp.bfloat16)
        acc_ref[h] = alpha * acc_ref[h] + jnp.dot(v_aug, e, preferred_element_type=jnp.float32)
        m_ref[h:h + 1, :] = m_new

    @pl.when(j == ((i + 1) * tq - 1) // tk)
    def _():
        rowi = lax.broadcasted_iota(jnp.int32, (LANES, tq), 0)
        for p in range(npair):
            a = acc_ref[2 * p]
            b = acc_ref[2 * p + 1]
            out_t = jnp.where(rowi < 64, a / a[64:65, :], b / b[0:1, :])
            o_ref[:, LANES * p:LANES * (p + 1)] = out_t.T.astype(o_ref.dtype)


def _dsa_attention(hb, v_t, sel, bsz, L, tq, tk):
    T = bsz * L
    nq, nk = L // tq, L // tk
    ts = sel.shape[-1]
    assert tq % ts == 0 and tk % ts == 0 and (L // ts) % (tq // ts) == 0
    pairs = [(i, j) for i in range(nq) for j in range(((i + 1) * tq - 1) // tk + 1)]
    qidx = jnp.asarray(np.array([p[0] for p in pairs], np.int32))
    kidx = jnp.asarray(np.array([p[1] for p in pairs], np.int32))

    def qmap(col):
        return lambda b, s, qi, ki: (b * nq + qi[s], col)

    def kmap(col):
        return lambda b, s, qi, ki: (b * nk + ki[s], col)

    grid_spec = pltpu.PrefetchScalarGridSpec(
        num_scalar_prefetch=2,
        grid=(bsz, len(pairs)),
        in_specs=[pl.BlockSpec((tq, DSA_WIDTH), qmap(OD_Q // DSA_WIDTH)),
                  pl.BlockSpec((tq // ts, tk // ts, ts, ts), lambda b, s, qi, ki: (b * nq + qi[s], ki[s], 0, 0)),
                  pl.BlockSpec((tk, DSA_WIDTH), kmap(OD_K // DSA_WIDTH)),
                  pl.BlockSpec((DSA_WIDTH, tk), lambda b, s, qi, ki: (0, b * nk + ki[s]))],
        out_specs=pl.BlockSpec((tq, DSA_WIDTH), qmap(0)),
        scratch_shapes=[pltpu.VMEM((DSA_HEADS, tq, LANES), jnp.bfloat16),
                        pltpu.VMEM((DSA_HEADS, tq), jnp.float32),
                        pltpu.VMEM((DSA_HEADS, LANES, tq), jnp.float32)])
    return pl.pallas_call(
        partial(_dsa_attn_kernel, tq=tq, tk=tk, ts=ts),
        grid_spec=grid_spec,
        out_shape=jax.ShapeDtypeStruct((T, DSA_WIDTH), jnp.bfloat16),
        compiler_params=_params("parallel", "arbitrary"),
    )(qidx, kidx, hb, sel, hb, v_t)


def _dilated_kernel(q_ref, kp_ref, kc_ref, vp_ref, vc_ref, o_ref, lse_ref, *, tq):
    a = pl.program_id(2)
    row = lax.broadcasted_iota(jnp.int32, (tq, 2 * tq), 0)
    c = lax.broadcasted_iota(jnp.int32, (tq, 2 * tq), 1)
    first_col = jnp.where(a == 0, tq, 0)
    valid = jnp.logical_and(jnp.logical_and(c >= row, c <= row + tq), c >= first_col)
    lane = lax.broadcasted_iota(jnp.int32, (tq, LANES), 1)
    for p in range(DIL_HEADS // 2):
        sl = slice(LANES * p, LANES * (p + 1))
        kk = jnp.concatenate([kp_ref[:, sl], kc_ref[:, sl]], axis=0)
        vv = jnp.concatenate([vp_ref[:, sl], vc_ref[:, sl]], axis=0)
        outs, lses = [], []
        for qh in _split_head_pair(q_ref[:, sl]):
            s = lax.dot_general(qh, kk, (((1,), (1,)), ((), ())), preferred_element_type=jnp.float32)
            s = jnp.where(valid, s, NEG_BIG)
            m = jnp.max(s, axis=1, keepdims=True)
            e = jnp.exp(s - m)
            l = jnp.sum(e, axis=1, keepdims=True)
            outs.append(jnp.dot(e.astype(jnp.bfloat16), vv, preferred_element_type=jnp.float32) / l)
            lses.append(m + jnp.log(l))
        o_ref[:, sl] = jnp.where(lane < 64, outs[0], outs[1])
        lse_ref[:, sl] = jnp.where(lane < 64, lses[0], lses[1])


def _dil_proj_kernel(x_ref, w_ref, o_ref, *scratch, dil):
    res = jnp.dot(x_ref[...].astype(jnp.bfloat16), w_ref[...], preferred_element_type=jnp.float32)
    if dil == 1:
        o_ref[...] = res.astype(o_ref.dtype)
        return
    res_ref, = scratch
    rows, cols = res.shape[0] // dil, res.shape[1]
    for c in range(cols // LANES):
        res_ref[c] = res[:, LANES * c:LANES * (c + 1)]
    for r in range(dil):
        for c in range(cols // LANES):
            o_ref[:, cols * r + LANES * c:cols * r + LANES * (c + 1)] = (
                res_ref.at[c][pl.ds(r, rows, stride=dil), :].astype(o_ref.dtype))


def _dil_proj(x, w, dil, tm=1024):
    T, K = x.shape
    C = w.shape[1]
    tm = min(tm, T)
    return pl.pallas_call(
        partial(_dil_proj_kernel, dil=dil),
        grid=(T // tm,),
        in_specs=[pl.BlockSpec((tm, K), lambda i: (i, 0)), pl.BlockSpec((K, C), lambda i: (0, 0))],
        out_specs=pl.BlockSpec((tm // dil, dil * C), lambda i: (i, 0)),
        out_shape=jax.ShapeDtypeStruct((T // dil, dil * C), jnp.bfloat16),
        scratch_shapes=[] if dil == 1 else [pltpu.VMEM((C // LANES, tm, LANES), jnp.float32)],
        compiler_params=_params("parallel"),
    )(x, w)


def _dilated_group(pg, bsz, L, g, tq):
    window, dil = DIL_PATTERNS[g]
    assert window // dil == tq
    M = L // dil
    nb = M // tq
    ncol = DIL_COLS // DIL_WIDTH

    def cur(col):
        return lambda b, r, a: (b * nb + a, r * ncol + col)

    def prev(col):
        return lambda b, r, a: (b * nb + jnp.maximum(a - 1, 0), r * ncol + col)

    blk = (tq, DIL_WIDTH)
    out_map = lambda b, r, a: (b * nb + a, r)
    return pl.pallas_call(
        partial(_dilated_kernel, tq=tq),
        grid=(bsz, dil, nb),
        in_specs=[pl.BlockSpec(blk, cur(0)), pl.BlockSpec(blk, prev(1)), pl.BlockSpec(blk, cur(1)),
                  pl.BlockSpec(blk, prev(2)), pl.BlockSpec(blk, cur(2))],
        out_specs=[pl.BlockSpec(blk, out_map), pl.BlockSpec(blk, out_map)],
        out_shape=[jax.ShapeDtypeStruct((bsz * M, dil * DIL_WIDTH), jnp.float32)] * 2,
        compiler_params=_params("parallel", "parallel", "arbitrary"),
    )(pg, pg, pg, pg, pg)


def _dilated_combine_kernel(*refs, dils, tm):
    ng = len(dils)
    o_refs, l_refs, out_ref, scratch = refs[:ng], refs[ng:2 * ng], refs[2 * ng], refs[2 * ng + 1:]

    def token_major(ref, dil, buf):
        if dil == 1:
            return ref[...]
        nslab = DIL_WIDTH // LANES
        for r in range(dil):
            for c in range(nslab):
                buf.at[c][pl.ds(r, tm // dil, stride=dil), :] = ref[:, DIL_WIDTH * r + LANES * c:DIL_WIDTH * r + LANES * (c + 1)]
        return jnp.concatenate([buf[c] for c in range(nslab)], axis=1)

    bufs = iter(scratch)
    outs = [token_major(o_refs[g], d, None if d == 1 else next(bufs)) for g, d in enumerate(dils)]
    lses = [token_major(l_refs[g], d, None if d == 1 else next(bufs)) for g, d in enumerate(dils)]
    m = lses[0]
    for l in lses[1:]:
        m = jnp.maximum(m, l)
    ws = [jnp.exp(l - m) for l in lses]
    num, den = ws[0] * outs[0], ws[0]
    for w, o in zip(ws[1:], outs[1:]):
        num, den = num + w * o, den + w
    out_ref[...] = (num / den).astype(out_ref.dtype)


def _dilated_combine(outs, lses, dils, T, tm=512):
    tm = min(tm, T)
    specs = [pl.BlockSpec((tm // d, d * DIL_WIDTH), lambda i: (i, 0)) for d in dils]
    n_buf = 2 * sum(1 for d in dils if d != 1)
    return pl.pallas_call(
        partial(_dilated_combine_kernel, dils=tuple(dils), tm=tm),
        grid=(T // tm,),
        in_specs=specs + specs,
        out_specs=pl.BlockSpec((tm, DIL_WIDTH), lambda i: (i, 0)),
        out_shape=jax.ShapeDtypeStruct((T, DIL_WIDTH), jnp.bfloat16),
        scratch_shapes=[pltpu.VMEM((DIL_WIDTH // LANES, tm, LANES), jnp.float32)] * n_buf,
        compiler_params=_params("parallel"),
    )(*outs, *lses)


def _mm_nt_kernel(w_ref, x_ref, o_ref):
    o_ref[...] = lax.dot_general(w_ref[...], x_ref[...].astype(jnp.bfloat16), (((1,), (1,)), ((), ())),
                                 preferred_element_type=jnp.float32).astype(o_ref.dtype)


def _matmul_nt(w_t, x, out_dtype, tm=1024):
    N, K = w_t.shape
    T = x.shape[0]
    tm = min(tm, T)
    return pl.pallas_call(
        _mm_nt_kernel,
        grid=(T // tm,),
        in_specs=[pl.BlockSpec((N, K), lambda i: (0, 0)), pl.BlockSpec((tm, K), lambda i: (i, 0))],
        out_specs=pl.BlockSpec((N, tm), lambda i: (0, i)),
        out_shape=jax.ShapeDtypeStruct((N, T), out_dtype),
        compiler_params=_params("parallel"),
    )(w_t, x)


def _odd_weights(w_in):
    D = w_in.shape[0]
    q, k, v, qi, ki, wi, dil = _split(w_in, OD_SIZES)
    zeros = lambda n: jnp.zeros((D, n), w_in.dtype)
    qi_exp = jnp.concatenate(
        [jnp.concatenate([qi[:, IDX_DIM * h:IDX_DIM * (h + 1)] * IDX_DIM ** -0.5, zeros(LANES - IDX_DIM)], axis=1)
         for h in range(IDX_HEADS)], axis=1)
    dil = dil.reshape(D, 3, DIL_GROUPS, DIL_WIDTH)
    w_dil = [jnp.concatenate([dil[:, 0, g] * DIL_HEAD_DIM ** -0.5, dil[:, 1, g], dil[:, 2, g]], axis=1).astype(jnp.bfloat16)
             for g in range(DIL_GROUPS)]
    wb = jnp.concatenate([q * DSA_HEAD_DIM ** -0.5, k, qi_exp, ki, zeros(LANES - IDX_DIM),
                          zeros(OD_COLS_PADDED - OD_KI - LANES)], axis=1).astype(jnp.bfloat16)
    wv_t = v.T.astype(jnp.bfloat16)
    wwi_t = jnp.concatenate([wi * IDX_HEADS ** -0.5, zeros(8 - IDX_HEADS)], axis=1).T.astype(jnp.bfloat16)
    return wb, w_dil, wv_t, wwi_t


def _odd_layer_mix(x2, w_in, bsz, L):
    wb, w_dil, wv_t, wwi_t = _odd_weights(w_in)
    hb = _matmul(x2, wb, jnp.bfloat16, tm=1024, tn=1024)
    v_t = _matmul_nt(wv_t, x2, jnp.bfloat16)
    wi_t = _matmul_nt(wwi_t, x2, jnp.float32)
    sel = _dsa_select(hb, wi_t, bsz, L, min(DSA_TQ, L))
    o_c = _dsa_attention(hb, v_t, sel, bsz, L, min(DSA_ATT_TQ, L), min(DSA_TK, L))
    dils = [d for _, d in DIL_PATTERNS]
    groups = [_dilated_group(_dil_proj(x2, w_dil[g], dils[g]), bsz, L, g, DIL_PATTERNS[g][0] // dils[g])
              for g in range(DIL_GROUPS)]
    o_d = _dilated_combine([o for o, _ in groups], [l for _, l in groups], dils, bsz * L)
    return o_c, o_d


EV_Q, EV_K, EV_V, EV_R, EV_G = 0, 256, 512, 1024, 1536
EV_COLS_PADDED = 2048
GLA_ROWS = 512
S5_CHUNK = 64


def _gla_kernel(q_ref, k_ref, v_ref, r_ref, g_ref, wg_ref, bg_ref, ng_ref, o_ref, st_ref, *, rows):
    C = GLA_CHUNK

    @pl.when(pl.program_id(1) == 0)
    def _():
        st_ref[...] = jnp.zeros(st_ref.shape, jnp.float32)

    ri = lax.broadcasted_iota(jnp.int32, (C, C), 0)
    ci = lax.broadcasted_iota(jnp.int32, (C, C), 1)
    causal = ci <= ri
    tril = jnp.where(causal, 1.0, 0.0).astype(jnp.float32)
    wg = wg_ref[...].astype(jnp.bfloat16)
    bg = bg_ref[...]
    ng = ng_ref[...]

    def chunk(c, carry):
        r0 = pl.multiple_of(c * C, C)
        rs = pl.ds(r0, C)
        logit = jnp.dot(g_ref[rs, :].astype(jnp.bfloat16), wg, preferred_element_type=jnp.float32) + bg
        log_a = jax.nn.log_sigmoid(logit) / GLA_GATE_TAU
        bcum = jnp.dot(tril, log_a, precision=lax.Precision.HIGHEST, preferred_element_type=jnp.float32)
        b_last = bcum[C - 1:C, :]
        q_t = (q_ref[rs, :] * jnp.exp(bcum)).astype(jnp.bfloat16)
        k_t = (k_ref[rs, :] * jnp.exp(-bcum)).astype(jnp.bfloat16)
        k_end = (k_ref[rs, :] * jnp.exp(b_last - bcum)).astype(jnp.bfloat16)
        dec = jnp.exp(b_last)
        for p in range(GLA_HEADS // 2):
            sl = slice(LANES * p, LANES * (p + 1))
            q_halves = _split_head_pair(q_t[:, sl])
            ke_halves = _split_head_pair(k_end[:, sl])
            for half in range(2):
                h = 2 * p + half
                hs = slice(GLA_DV * h, GLA_DV * (h + 1))
                qm = q_halves[half]
                att = lax.dot_general(qm, k_t[:, sl], (((1,), (1,)), ((), ())), preferred_element_type=jnp.float32)
                att = jnp.where(causal, att, 0.0).astype(jnp.bfloat16)
                v_h = v_ref[rs, hs].astype(jnp.bfloat16)
                st = st_ref[h]
                o = jnp.dot(att, v_h, preferred_element_type=jnp.float32)
                o = o + lax.dot_general(qm, st.astype(jnp.bfloat16), (((1,), (1,)), ((), ())),
                                        preferred_element_type=jnp.float32)
                kv_t = lax.dot_general(v_h, ke_halves[half], (((0,), (0,)), ((), ())),
                                       preferred_element_type=jnp.float32)
                st_ref[h] = st * dec[:, sl] + kv_t
                o = o * lax.rsqrt(jnp.mean(o * o, axis=-1, keepdims=True) + LN_EPS) * ng
                o = o * jax.nn.silu(r_ref[rs, hs])
                o_ref[rs, hs] = o.astype(o_ref.dtype)
        return carry

    lax.fori_loop(0, rows // C, chunk, 0)


def _gla(hf, w_gate2, b_gate2, norm_g, bsz, L):
    T = bsz * L
    rows = min(GLA_ROWS, L)
    nb = L // rows
    dkw = GLA_HEADS * GLA_DK
    dvw = GLA_HEADS * GLA_DV
    wg = jnp.pad(w_gate2, ((0, LANES - GLA_GATE_RANK), (0, 0)))

    def rmap(col):
        return lambda b, i: (b * nb + i, col)

    const = lambda b, i: (0, 0)
    return pl.pallas_call(
        partial(_gla_kernel, rows=rows),
        grid=(bsz, nb),
        in_specs=[pl.BlockSpec((rows, dkw), rmap(EV_Q // dkw)),
                  pl.BlockSpec((rows, dkw), rmap(EV_K // dkw)),
                  pl.BlockSpec((rows, dvw), rmap(EV_V // dvw)),
                  pl.BlockSpec((rows, dvw), rmap(EV_R // dvw)),
                  pl.BlockSpec((rows, LANES), rmap(EV_G // LANES)),
                  pl.BlockSpec((LANES, dkw), const),
                  pl.BlockSpec((1, dkw), const),
                  pl.BlockSpec((1, GLA_DV), const)],
        out_specs=pl.BlockSpec((rows, dvw), rmap(0)),
        out_shape=jax.ShapeDtypeStruct((T, dvw), jnp.bfloat16),
        scratch_shapes=[pltpu.VMEM((GLA_HEADS, GLA_DV, LANES), jnp.float32)],
        compiler_params=_params("parallel", "arbitrary"),
    )(hf, hf, hf, hf, hf, wg, b_gate2.reshape(1, dkw), norm_g.reshape(1, GLA_DV))


def _s5_tables(a_re, a_im, log_dt, b_re, b_im, c_re, c_im, d_skip):
    f32 = jnp.float32
    Cs, G, P, N = S5_CHUNK, S5_GROUPS, S5_STATE, S5_GROUP
    lam_re = jnp.minimum(a_re.astype(f32), S5_MAX_RE)
    lam_im = a_im.astype(f32)
    dt = jnp.exp(log_dt.astype(f32))[:, None]
    mag = jnp.exp(lam_re * dt)
    ab_re = mag * jnp.cos(lam_im * dt)
    ab_im = mag * jnp.sin(lam_im * dt)
    inv = 1.0 / (lam_re * lam_re + lam_im * lam_im)
    z_re = ((ab_re - 1.0) * lam_re + ab_im * lam_im) * inv
    z_im = (ab_im * lam_re - (ab_re - 1.0) * lam_im) * inv
    br, bi = b_re.astype(f32), b_im.astype(f32)
    bb_re = z_re[..., None] * br - z_im[..., None] * bi
    bb_im = z_re[..., None] * bi + z_im[..., None] * br
    kk = jnp.arange(Cs + 1, dtype=f32)[:, None, None]
    pmag = jnp.exp(kk * (lam_re * dt))
    pw_re = pmag * jnp.cos(kk * (lam_im * dt))
    pw_im = pmag * jnp.sin(kk * (lam_im * dt))
    cr, ci = c_re.astype(f32), c_im.astype(f32)
    ca_re = cr[None] * pw_re[:, :, None, :] - ci[None] * pw_im[:, :, None, :]
    ca_im = cr[None] * pw_im[:, :, None, :] + ci[None] * pw_re[:, :, None, :]
    hi = lax.Precision.HIGHEST
    kern = (jnp.einsum('kgnp,gpm->kgnm', ca_re[:Cs], bb_re, precision=hi)
            - jnp.einsum('kgnp,gpm->kgnm', ca_im[:Cs], bb_im, precision=hi))
    jj = jnp.arange(Cs)[:, None]
    ii = jnp.arange(Cs)[None, :]
    tz = jnp.where((ii >= jj)[:, :, None, None, None], kern[jnp.maximum(ii - jj, 0)], 0.0)
    tz = jnp.transpose(tz, (2, 4, 0, 3, 1)).reshape(G, N * Cs, N * Cs)
    rev_re, rev_im = pw_re[Cs - 1::-1][:Cs], pw_im[Cs - 1::-1][:Cs]
    ws_re = rev_re[..., None] * bb_re[None] - rev_im[..., None] * bb_im[None]
    ws_im = rev_re[..., None] * bb_im[None] + rev_im[..., None] * bb_re[None]
    to_ws = lambda w: jnp.pad(jnp.transpose(w, (1, 3, 0, 2)).reshape(G, N * Cs, P), ((0, 0), (0, 0), (0, LANES - P)))
    to_wo = lambda w: jnp.pad(jnp.transpose(w, (1, 3, 2, 0)).reshape(G, P, N * Cs), ((0, 0), (0, LANES - P), (0, 0)))
    a_cs = jnp.stack([jnp.pad(pw_re[Cs], ((0, 0), (0, LANES - P))), jnp.pad(pw_im[Cs], ((0, 0), (0, LANES - P)))], axis=1)
    d_exp = jnp.repeat(d_skip.astype(f32).reshape(G, 1, N), Cs, axis=2)
    bf = jnp.bfloat16
    return (tz.astype(bf), to_ws(ws_re).astype(bf), to_ws(ws_im).astype(bf),
            to_wo(ca_re[1:]).astype(bf), to_wo(-ca_im[1:]).astype(bf), a_cs, d_exp)


def _s5_kernel(u_ref, tz_ref, wsr_ref, wsi_ref, wor_ref, woi_ref, acs_ref, d_ref, y_ref, xr_ref, xi_ref,
               *, nchunk, nbatch):
    u32 = jnp.concatenate([u_ref[m] for m in range(S5_GROUP)], axis=1)
    u = u32.astype(jnp.bfloat16)
    xr_ref[...] = jnp.dot(u, wsr_ref[0], preferred_element_type=jnp.float32)
    xi_ref[...] = jnp.dot(u, wsi_ref[0], preferred_element_type=jnp.float32)
    ar = acs_ref[0, 0:1, :]
    ai = acs_ref[0, 1:2, :]

    def step(c, carry):
        new = []
        for b in range(nbatch):
            sr, si = carry[2 * b], carry[2 * b + 1]
            row = pl.ds(b * nchunk + c, 1)
            lr, li = xr_ref[row, :], xi_ref[row, :]
            xr_ref[row, :] = sr
            xi_ref[row, :] = si
            new += [ar * sr - ai * si + lr, ar * si + ai * sr + li]
        return tuple(new)

    zero = jnp.zeros((1, LANES), jnp.float32)
    lax.fori_loop(0, nchunk, step, (zero,) * (2 * nbatch))
    y = jnp.dot(u, tz_ref[0], preferred_element_type=jnp.float32)
    y = y + jnp.dot(xr_ref[...].astype(jnp.bfloat16), wor_ref[0], preferred_element_type=jnp.float32)
    y = y + jnp.dot(xi_ref[...].astype(jnp.bfloat16), woi_ref[0], preferred_element_type=jnp.float32)
    y = jax.nn.gelu(y + d_ref[0] * u32)
    for n in range(S5_GROUP):
        y_ref[n] = y[:, S5_CHUNK * n:S5_CHUNK * (n + 1)]


def _s5_scan(u_t, tables, bsz, nchunk):
    width, R, Cs = u_t.shape
    G, N = S5_GROUPS, S5_GROUP
    W = N * Cs
    tz, wsr, wsi, wor, woi, a_cs, d_exp = tables
    gmap = lambda g: (g, 0, 0)
    return pl.pallas_call(
        partial(_s5_kernel, nchunk=nchunk, nbatch=bsz),
        grid=(G,),
        in_specs=[pl.BlockSpec((N, R, Cs), gmap), pl.BlockSpec((1, W, W), gmap),
                  pl.BlockSpec((1, W, LANES), gmap), pl.BlockSpec((1, W, LANES), gmap),
                  pl.BlockSpec((1, LANES, W), gmap), pl.BlockSpec((1, LANES, W), gmap),
                  pl.BlockSpec((1, 2, LANES), gmap), pl.BlockSpec((1, 1, W), gmap)],
        out_specs=pl.BlockSpec((N, R, Cs), gmap),
        out_shape=jax.ShapeDtypeStruct((width, R, Cs), jnp.float32),
        scratch_shapes=[pltpu.VMEM((R, LANES), jnp.float32), pltpu.VMEM((R, LANES), jnp.float32)],
        compiler_params=_params("parallel"),
    )(u_t, tz, wsr, wsi, wor, woi, a_cs, d_exp)


def _glu_kernel(yt_ref, wt_ref, b_ref, o_ref):
    y = yt_ref[...]
    gate = jnp.dot(wt_ref[...], y.astype(jnp.bfloat16), preferred_element_type=jnp.float32) + b_ref[...]
    o_ref[...] = (y * jax.nn.sigmoid(gate)).T.astype(o_ref.dtype)


def _glu(y_t, w_glu, b_glu, tm=1024):
    W, T = y_t.shape
    tm = min(tm, T)
    return pl.pallas_call(
        _glu_kernel,
        grid=(T // tm,),
        in_specs=[pl.BlockSpec((W, tm), lambda i: (0, i)),
                  pl.BlockSpec((W, W), lambda i: (0, 0)),
                  pl.BlockSpec((W, 1), lambda i: (0, 0))],
        out_specs=pl.BlockSpec((tm, W), lambda i: (i, 0)),
        out_shape=jax.ShapeDtypeStruct((T, W), jnp.bfloat16),
        compiler_params=_params("parallel"),
    )(y_t, w_glu.T.astype(jnp.bfloat16), b_glu.reshape(W, 1))


def _s5(u_t, s5_params, w_glu, b_glu, bsz, L):
    T = bsz * L
    nchunk = L // S5_CHUNK
    y_t = _s5_scan(u_t.reshape(S5_WIDTH, bsz * nchunk, S5_CHUNK), _s5_tables(*s5_params), bsz, nchunk)
    return _glu(y_t.reshape(S5_WIDTH, T), w_glu, b_glu)


def _even_weights(w_in):
    D = w_in.shape[0]
    q, k, v, r, g_lr, u = _split(w_in, EV_SIZES)
    pad = jnp.zeros((D, EV_COLS_PADDED - EV_G - GLA_GATE_RANK), w_in.dtype)
    wb = jnp.concatenate([q * GLA_DK ** -0.5, k, v, r, g_lr, pad], axis=1).astype(jnp.bfloat16)
    return wb, u.T.astype(jnp.bfloat16)


def _even_layer_mix(x2, w_in, w_gate2, b_gate2, norm_g, s5_params, w_glu, b_glu, bsz, L):
    wb, wu_t = _even_weights(w_in)
    hf = _matmul(x2, wb, jnp.float32, tm=1024, tn=512)
    u_t = _matmul_nt(wu_t, x2, jnp.float32)
    o_a = _gla(hf, w_gate2, b_gate2, norm_g, bsz, L)
    o_b = _s5(u_t, s5_params, w_glu, b_glu, bsz, L)
    return o_a, o_b


MOE_TILE = 512
ROW_TILE = 8
DMA_UNROLL = 8
MOE_VMEM_LIMIT_BYTES = 56 * 1024 * 1024


def _router_kernel(x_ref, w_ref, b_ref, e_ref, g_ref):
    logits = jnp.dot(x_ref[...].astype(jnp.bfloat16), w_ref[...], preferred_element_type=jnp.float32) + b_ref[...]
    tm = logits.shape[0]
    lane = lax.broadcasted_iota(jnp.int32, (tm, LANES), 1)
    logits = jnp.where(lane < N_EXPERTS, logits, NEG_BIG)
    tops, idxs = [], []
    for _ in range(TOP_K):
        m = jnp.max(logits, axis=1, keepdims=True)
        idx = jnp.min(jnp.where(logits == m, lane, LANES), axis=1, keepdims=True)
        tops.append(m)
        idxs.append(idx)
        logits = jnp.where(lane == idx, NEG_BIG, logits)
    exps = [jnp.exp(t - tops[0]) for t in tops]
    denom = exps[0]
    for e in exps[1:]:
        denom = denom + e
    lane4 = lax.broadcasted_iota(jnp.int32, (tm, TOP_K), 1)
    e_out = jnp.zeros((tm, TOP_K), jnp.int32)
    g_out = jnp.zeros((tm, TOP_K), jnp.float32)
    for k in range(TOP_K):
        e_out = jnp.where(lane4 == k, idxs[k], e_out)
        g_out = jnp.where(lane4 == k, exps[k] / denom, g_out)
    e_ref[...] = e_out
    g_ref[...] = g_out


def _router(x, router_w, router_b, tm=512):
    T, D = x.shape
    tm = min(tm, T)
    w = jnp.pad(router_w, ((0, 0), (0, LANES - N_EXPERTS))).astype(jnp.bfloat16)
    b = jnp.pad(router_b, (0, LANES - N_EXPERTS)).reshape(1, LANES)
    return pl.pallas_call(
        _router_kernel,
        grid=(T // tm,),
        in_specs=[pl.BlockSpec((tm, D), lambda i: (i, 0)),
                  pl.BlockSpec((D, LANES), lambda i: (0, 0)),
                  pl.BlockSpec((1, LANES), lambda i: (0, 0))],
        out_specs=[pl.BlockSpec((tm, TOP_K), lambda i: (i, 0)), pl.BlockSpec((tm, TOP_K), lambda i: (i, 0))],
        out_shape=[jax.ShapeDtypeStruct((T, TOP_K), jnp.int32), jax.ShapeDtypeStruct((T, TOP_K), jnp.float32)],
        compiler_params=_params("parallel"),
    )(x, w, b)


def _moe_rank_kernel(e_ref, rank_ref, count_ref, carry_ref):
    @pl.when(pl.program_id(0) == 0)
    def _():
        carry_ref[...] = jnp.zeros(carry_ref.shape, jnp.float32)

    e = e_ref[...]
    tm = e.shape[0]
    lane = lax.broadcasted_iota(jnp.int32, (tm, LANES), 1)
    onehot = jnp.zeros((tm, LANES), jnp.float32)
    for k in range(TOP_K):
        onehot = onehot + jnp.where(lane == e[:, k:k + 1], 1.0, 0.0)
    ri = lax.broadcasted_iota(jnp.int32, (tm, tm), 0)
    ci = lax.broadcasted_iota(jnp.int32, (tm, tm), 1)
    strict_lower = jnp.where(ci < ri, 1.0, 0.0).astype(jnp.bfloat16)
    before = jnp.dot(strict_lower, onehot.astype(jnp.bfloat16), preferred_element_type=jnp.float32) + carry_ref[...]
    lane4 = lax.broadcasted_iota(jnp.int32, (tm, TOP_K), 1)
    rank = jnp.zeros((tm, TOP_K), jnp.int32)
    for k in range(TOP_K):
        r_k = jnp.sum(jnp.where(lane == e[:, k:k + 1], before, 0.0), axis=1, keepdims=True).astype(jnp.int32)
        rank = jnp.where(lane4 == k, r_k, rank)
    rank_ref[...] = rank
    carry_ref[...] = carry_ref[...] + jnp.sum(onehot, axis=0, keepdims=True)
    count_ref[...] = carry_ref[...]


def _moe_rank(top_e, tm=256):
    T = top_e.shape[0]
    tm = min(tm, T)
    return pl.pallas_call(
        _moe_rank_kernel,
        grid=(T // tm,),
        in_specs=[pl.BlockSpec((tm, TOP_K), lambda i: (i, 0))],
        out_specs=[pl.BlockSpec((tm, TOP_K), lambda i: (i, 0)), pl.BlockSpec((1, LANES), lambda i: (0, 0))],
        out_shape=[jax.ShapeDtypeStruct((T, TOP_K), jnp.int32), jax.ShapeDtypeStruct((1, LANES), jnp.float32)],
        scratch_shapes=[pltpu.VMEM((1, LANES), jnp.float32)],
        compiler_params=_params("arbitrary"),
    )(top_e)


def _to_token_tiles(x, dst_ref, rows):
    for c in range(ROW_TILE):
        dst_ref[pl.ds(c, rows, stride=ROW_TILE), :] = x[:, LANES * c:LANES * (c + 1)]


def _from_token_tiles(src_ref, rows):
    return jnp.concatenate([src_ref[pl.ds(c, rows, stride=ROW_TILE), :] for c in range(ROW_TILE)], axis=1)


def _moe_dispatch_kernel(dest_ref, x_ref, init_ref, xbuf_ref, xs_ref, sem, *, tm):
    del init_ref
    i = pl.program_id(0)
    n = pl.num_programs(0)
    slot = i % 2

    def row_copy(s, r, d):
        return pltpu.make_async_copy(xs_ref.at[s, pl.ds(pl.multiple_of(r * ROW_TILE, ROW_TILE), ROW_TILE), :],
                                     xbuf_ref.at[pl.ds(pl.multiple_of(d * ROW_TILE, ROW_TILE), ROW_TILE), :], sem.at[s])

    def drain(s):
        def body(a, c):
            row_copy(s, 0, 0).wait()
            return c
        lax.fori_loop(0, tm * TOP_K, body, 0, unroll=DMA_UNROLL)

    @pl.when(i >= 2)
    def _():
        drain(slot)

    _to_token_tiles(x_ref[...], xs_ref.at[slot], tm)

    def body(r, c):
        for k in range(TOP_K):
            row_copy(slot, r, dest_ref[r * TOP_K + k]).start()
        return c

    lax.fori_loop(0, tm, body, 0, unroll=DMA_UNROLL)

    @pl.when(i == n - 1)
    def _():
        drain(slot)

        @pl.when(n > 1)
        def _():
            drain(1 - slot)


def _moe_dispatch(x, dest_flat, n_rows, tm=256):
    T, D = x.shape
    tm = min(tm, T)
    assert D == ROW_TILE * LANES
    init = jnp.zeros((n_rows * ROW_TILE, LANES), jnp.float32)
    return pl.pallas_call(
        partial(_moe_dispatch_kernel, tm=tm),
        grid=(T // tm,),
        in_specs=[pl.BlockSpec((tm * TOP_K,), lambda i: (i,), memory_space=pltpu.SMEM),
                  pl.BlockSpec((tm, D), lambda i: (i, 0)),
                  pl.BlockSpec(memory_space=pl.ANY)],
        out_specs=pl.BlockSpec(memory_space=pl.ANY),
        out_shape=jax.ShapeDtypeStruct((n_rows * ROW_TILE, LANES), jnp.float32),
        scratch_shapes=[pltpu.VMEM((2, tm * ROW_TILE, LANES), jnp.float32), pltpu.SemaphoreType.DMA((2,))],
        input_output_aliases={2: 0},
        compiler_params=_params("arbitrary"),
    )(dest_flat, x, init)


def _moe_ffn_kernel(te_ref, nt_ref, x_ref, w1_ref, b1_ref, w2_ref, b2_ref, y_ref, w1b_ref, w2b_ref):
    i = pl.program_id(0)
    prev = te_ref[jnp.maximum(i - 1, 0)]

    @pl.when(jnp.logical_or(i == 0, te_ref[i] != prev))
    def _():
        w1b_ref[...] = w1_ref[0].astype(jnp.bfloat16)
        w2b_ref[...] = w2_ref[0].astype(jnp.bfloat16)

    @pl.when(i < nt_ref[0])
    def _():
        x = _from_token_tiles(x_ref, MOE_TILE).astype(jnp.bfloat16)
        h = jnp.dot(x, w1b_ref[...], preferred_element_type=jnp.float32) + b1_ref[0]
        glu = jnp.minimum(h[:, :D_FF], SWIGLU_LIMIT)
        lin = jnp.clip(h[:, D_FF:], -SWIGLU_LIMIT, SWIGLU_LIMIT)
        act = glu * jax.nn.sigmoid(SWIGLU_ALPHA * glu) * (lin + 1.0)
        y = jnp.dot(act.astype(jnp.bfloat16), w2b_ref[...], preferred_element_type=jnp.float32) + b2_ref[0]
        _to_token_tiles(y, y_ref, MOE_TILE)

    @pl.when(i >= nt_ref[0])
    def _():
        y_ref[...] = jnp.zeros(y_ref.shape, y_ref.dtype)


def _moe_expert_ffn(xbuf, tile_expert, n_used, w1, b1, w2, b2):
    D, F2 = w1.shape[1], w1.shape[2]
    P = xbuf.shape[0] // ROW_TILE
    n_tiles = P // MOE_TILE
    blk = (MOE_TILE * ROW_TILE, LANES)
    grid_spec = pltpu.PrefetchScalarGridSpec(
        num_scalar_prefetch=2,
        grid=(n_tiles,),
        in_specs=[pl.BlockSpec(blk, lambda i, te, nt: (jnp.minimum(i, nt[0] - 1), 0)),
                  pl.BlockSpec((1, D, F2), lambda i, te, nt: (te[i], 0, 0)),
                  pl.BlockSpec((1, 1, F2), lambda i, te, nt: (te[i], 0, 0)),
                  pl.BlockSpec((1, D_FF, D), lambda i, te, nt: (te[i], 0, 0)),
                  pl.BlockSpec((1, 1, D), lambda i, te, nt: (te[i], 0, 0))],
        out_specs=pl.BlockSpec(blk, lambda i, te, nt: (i, 0)),
        scratch_shapes=[pltpu.VMEM((D, F2), jnp.bfloat16), pltpu.VMEM((D_FF, D), jnp.bfloat16)],
    )
    return pl.pallas_call(
        _moe_ffn_kernel,
        grid_spec=grid_spec,
        out_shape=jax.ShapeDtypeStruct(xbuf.shape, jnp.float32),
        compiler_params=pltpu.CompilerParams(dimension_semantics=("arbitrary",),
                                             vmem_limit_bytes=MOE_VMEM_LIMIT_BYTES),
    )(tile_expert, n_used, xbuf, w1, b1.reshape(-1, 1, F2), w2, b2.reshape(-1, 1, D))


def _moe_combine_kernel(dest_ref, gate_ref, x_ref, g_ref, b_ref, ybuf_ref, o_ref, rows_ref, sem, *, tm):
    def row_copy(k, r, d):
        return pltpu.make_async_copy(ybuf_ref.at[pl.ds(pl.multiple_of(d * ROW_TILE, ROW_TILE), ROW_TILE), :],
                                     rows_ref.at[k, pl.ds(pl.multiple_of(r * ROW_TILE, ROW_TILE), ROW_TILE), :], sem.at[0])

    def start(r, c):
        for k in range(TOP_K):
            row_copy(k, r, dest_ref[r * TOP_K + k]).start()
        return c

    lax.fori_loop(0, tm, start, 0, unroll=DMA_UNROLL)

    def wait(a, c):
        row_copy(0, 0, 0).wait()
        return c

    lax.fori_loop(0, tm * TOP_K, wait, 0, unroll=DMA_UNROLL)
    gate = gate_ref[...]
    ffn = gate[:, 0:1] * _from_token_tiles(rows_ref.at[0], tm)
    for k in range(1, TOP_K):
        ffn = ffn + gate[:, k:k + 1] * _from_token_tiles(rows_ref.at[k], tm)
    o_ref[...] = _layer_norm_rows(DEEPNORM_ALPHA * x_ref[...] + ffn, g_ref[...], b_ref[...])


def _moe_combine(ybuf, dest_flat, gate, x, g, b, tm=256):
    T, D = x.shape
    tm = min(tm, T)
    return pl.pallas_call(
        partial(_moe_combine_kernel, tm=tm),
        grid=(T // tm,),
        in_specs=[pl.BlockSpec((tm * TOP_K,), lambda i: (i,), memory_space=pltpu.SMEM),
                  pl.BlockSpec((tm, TOP_K), lambda i: (i, 0)),
                  pl.BlockSpec((tm, D), lambda i: (i, 0)),
                  pl.BlockSpec((1, D), lambda i: (0, 0)),
                  pl.BlockSpec((1, D), lambda i: (0, 0)),
                  pl.BlockSpec(memory_space=pl.ANY)],
        out_specs=pl.BlockSpec((tm, D), lambda i: (i, 0)),
        out_shape=jax.ShapeDtypeStruct((T, D), jnp.float32),
        scratch_shapes=[pltpu.VMEM((TOP_K, tm * ROW_TILE, LANES), jnp.float32), pltpu.SemaphoreType.DMA((1,))],
        compiler_params=_params("arbitrary"),
    )(dest_flat, gate, x, g.reshape(1, D), b.reshape(1, D), ybuf)


def _moe_layer(x, router_w, router_b, w1, b1, w2, b2, first_expert, ln_g, ln_b):
    T, D = x.shape
    A = T * TOP_K
    n_tiles = -(-(A + N_EXPERTS * (MOE_TILE - 1)) // MOE_TILE)
    top_e, gate = _router(x, router_w, router_b)
    rank, counts = _moe_rank(top_e)
    counts = counts[0, :N_EXPERTS].astype(jnp.int32)
    padded = (counts + MOE_TILE - 1) // MOE_TILE * MOE_TILE
    pend = jnp.cumsum(padded)
    pstart = pend - padded
    dest = (pstart[top_e] + rank).reshape(A)
    tile_start = jnp.arange(n_tiles, dtype=jnp.int32) * MOE_TILE
    tile_expert = jnp.minimum(jnp.sum(pend[None, :] <= tile_start[:, None], axis=1), N_EXPERTS - 1).astype(jnp.int32)
    n_used = (pend[-1:] // MOE_TILE).astype(jnp.int32)
    xbuf = _moe_dispatch(x, dest, n_tiles * MOE_TILE)
    ybuf = _moe_expert_ffn(xbuf, tile_expert + first_expert, n_used, w1, b1, w2, b2)
    return _moe_combine(ybuf, dest, gate, x, ln_g, ln_b)


def kernel(x, ev_w_in, gla_w_gate2, gla_b_gate2, gla_norm_g, s5_a_re, s5_a_im, s5_log_dt, s5_b_re, s5_b_im,
           s5_c_re, s5_c_im, s5_d, s5_w_glu, s5_b_glu, ev_w_out, od_w_in, od_w_out, ln1_g, ln1_b, ln2_g, ln2_b,
           router_w, router_b, moe_w1, moe_b1, moe_w2, moe_b2):
    bsz, L, D = x.shape
    T = bsz * L
    x = x.reshape(T, D)
    w1_all, b1_all = moe_w1.reshape((-1,) + moe_w1.shape[2:]), moe_b1.reshape(-1, moe_b1.shape[-1])
    w2_all, b2_all = moe_w2.reshape((-1,) + moe_w2.shape[2:]), moe_b2.reshape(-1, moe_b2.shape[-1])
    for layer in range(DEPTH):
        j = layer // 2
        if layer % 2 == 0:
            s5_params = (s5_a_re[j], s5_a_im[j], s5_log_dt[j], s5_b_re[j], s5_b_im[j], s5_c_re[j], s5_c_im[j], s5_d[j])
            o_1, o_2 = _even_layer_mix(x, ev_w_in[j], gla_w_gate2[j], gla_b_gate2[j], gla_norm_g[j], s5_params,
                                       s5_w_glu[j], s5_b_glu[j], bsz, L)
            w_out = ev_w_out[j]
        else:
            o_1, o_2 = _odd_layer_mix(x, od_w_in[j], bsz, L)
            w_out = od_w_out[j]
        x = _matmul2_res_ln(o_1, o_2, w_out, x, ln1_g[layer], ln1_b[layer])
        x = _moe_layer(x, router_w[layer], router_b[layer], w1_all, b1_all, w2_all, b2_all, layer * N_EXPERTS,
                       ln2_g[layer], ln2_b[layer])
    return x.reshape(bsz, L, D)
```

```python
from functools import partial

import numpy as np
import jax
import jax.numpy as jnp
from jax import lax
from jax.experimental import pallas as pl
from jax.experimental.pallas import tpu as pltpu

D_MODEL = 1024
DEPTH = 4
DEEPNORM_ALPHA = (2.0 * DEPTH) ** 0.25
LN_EPS = 1e-5
MIX_WIDTH = D_MODEL

GLA_HEADS = 4
GLA_DV = MIX_WIDTH // 2 // GLA_HEADS
GLA_DK = GLA_DV // 2
GLA_GATE_RANK = 16
GLA_GATE_TAU = 16.0
GLA_CHUNK = 64

S5_WIDTH = MIX_WIDTH // 2
S5_GROUP = 16
S5_GROUPS = S5_WIDTH // S5_GROUP
S5_STATE = 64
S5_MAX_RE = -1e-4

EV_SIZES = (GLA_HEADS * GLA_DK, GLA_HEADS * GLA_DK, GLA_HEADS * GLA_DV, GLA_HEADS * GLA_DV, GLA_GATE_RANK, S5_WIDTH)

DSA_HEADS = 8
DSA_HEAD_DIM = 64
DSA_WIDTH = DSA_HEADS * DSA_HEAD_DIM
IDX_HEADS = 4
IDX_DIM = 64
DSA_TOPK_MAX = 256

DIL_PATTERNS = ((128, 1), (512, 4), (2048, 16))
DIL_GROUPS = len(DIL_PATTERNS)
DIL_HEADS = 8
DIL_HEAD_DIM = 64
DIL_WIDTH = DIL_HEADS * DIL_HEAD_DIM

OD_SIZES = (DSA_WIDTH, DSA_WIDTH, DSA_WIDTH, IDX_HEADS * IDX_DIM, IDX_DIM, IDX_HEADS, 3 * DIL_GROUPS * DIL_WIDTH)

N_EXPERTS = 32
TOP_K = 4
D_FF = D_MODEL
SWIGLU_ALPHA = 1.702
SWIGLU_LIMIT = 7.0

LANES = 128
VMEM_LIMIT_BYTES = 48 * 1024 * 1024
NEG_BIG = -1e30
INT_MIN = -2 ** 31
INT_MAX = 2 ** 31 - 1
KEY_NEG_INF = -0x7F800000

IDX_EXP_WIDTH = IDX_HEADS * LANES
OD_Q, OD_K, OD_QI, OD_KI = 0, DSA_WIDTH, 2 * DSA_WIDTH, 2 * DSA_WIDTH + IDX_EXP_WIDTH
OD_COLS_PADDED = 2048
DIL_COLS = 3 * DIL_WIDTH


def _split(h, sizes):
    return jnp.split(h, [int(i) for i in np.cumsum(sizes)[:-1]], axis=-1)


def _params(*sem):
    return pltpu.CompilerParams(dimension_semantics=sem, vmem_limit_bytes=VMEM_LIMIT_BYTES)


def _mm_kernel(x_ref, w_ref, o_ref):
    o_ref[...] = jnp.dot(x_ref[...].astype(jnp.bfloat16), w_ref[...].astype(jnp.bfloat16),
                         preferred_element_type=jnp.float32).astype(o_ref.dtype)


def _matmul(x, w, out_dtype=jnp.float32, tm=512, tn=512):
    T, K = x.shape
    N = w.shape[1]
    tm, tn = min(tm, T), min(tn, N)
    assert T % tm == 0 and N % tn == 0
    return pl.pallas_call(
        _mm_kernel,
        grid=(T // tm, N // tn),
        in_specs=[pl.BlockSpec((tm, K), lambda i, j: (i, 0)),
                  pl.BlockSpec((K, tn), lambda i, j: (0, j))],
        out_specs=pl.BlockSpec((tm, tn), lambda i, j: (i, j)),
        out_shape=jax.ShapeDtypeStruct((T, N), out_dtype),
        compiler_params=_params("parallel", "arbitrary"),
    )(x, w)


def _layer_norm_rows(z, g, b):
    mu = jnp.mean(z, axis=-1, keepdims=True)
    zc = z - mu
    var = jnp.mean(zc * zc, axis=-1, keepdims=True)
    return zc * lax.rsqrt(var + LN_EPS) * g + b


def _mm2_res_ln_kernel(a1_ref, a2_ref, w1_ref, w2_ref, x_ref, g_ref, b_ref, o_ref):
    mix = jnp.dot(a1_ref[...], w1_ref[...], preferred_element_type=jnp.float32)
    mix += jnp.dot(a2_ref[...], w2_ref[...], preferred_element_type=jnp.float32)
    o_ref[...] = _layer_norm_rows(DEEPNORM_ALPHA * x_ref[...] + mix, g_ref[...], b_ref[...])


def _matmul2_res_ln(a1, a2, w, x, g, b, tm=512):
    T, K1 = a1.shape
    D = w.shape[1]
    tm = min(tm, T)
    wb = w.astype(jnp.bfloat16)
    return pl.pallas_call(
        _mm2_res_ln_kernel,
        grid=(T // tm,),
        in_specs=[pl.BlockSpec((tm, K1), lambda i: (i, 0)),
                  pl.BlockSpec((tm, a2.shape[1]), lambda i: (i, 0)),
                  pl.BlockSpec((K1, D), lambda i: (0, 0)),
                  pl.BlockSpec((a2.shape[1], D), lambda i: (1, 0)),
                  pl.BlockSpec((tm, D), lambda i: (i, 0)),
                  pl.BlockSpec((1, D), lambda i: (0, 0)),
                  pl.BlockSpec((1, D), lambda i: (0, 0))],
        out_specs=pl.BlockSpec((tm, D), lambda i: (i, 0)),
        out_shape=jax.ShapeDtypeStruct((T, D), jnp.float32),
        compiler_params=_params("parallel"),
    )(a1, a2, wb, wb, x, g.reshape(1, D), b.reshape(1, D))


DSA_TQ = 256
DSA_ATT_TQ = 512
DSA_TK = 1024
COUNT_ROWS = 64
TIE_CHECK_PASS = 8


def _stack_index_heads(qi_blk):
    return jnp.concatenate([qi_blk[:, LANES * h:LANES * (h + 1)] for h in range(IDX_HEADS)], axis=0)


def _index_keys_t(ki_blk, qi_all, wi_t, q0, k0, tq, causal_mask=True):
    tk = ki_blk.shape[0]
    d = lax.dot_general(ki_blk, qi_all, (((1,), (1,)), ((), ())), preferred_element_type=jnp.float32)
    sc = wi_t[0:1, :] * jnp.maximum(d[:, 0:tq], 0.0)
    for h in range(1, IDX_HEADS):
        sc = sc + wi_t[h:h + 1, :] * jnp.maximum(d[:, h * tq:(h + 1) * tq], 0.0)
    if causal_mask:
        kpos = k0 + lax.broadcasted_iota(jnp.int32, (tk, 1), 0)
        qpos = q0 + lax.broadcasted_iota(jnp.int32, (1, tq), 1)
        sc = jnp.where(kpos <= qpos, sc, -jnp.inf)
    bits = lax.bitcast_convert_type(sc, jnp.int32)
    return jnp.where(bits < 0, INT_MIN - bits, bits)


def _dsa_select_kernel(qi_ref, wit_ref, ki_ref, sel_ref, key_ref, *, tq, topk, nq):
    i = pl.program_id(1)
    nblk = i + 1
    q0 = i * tq
    qi_all = _stack_index_heads(qi_ref[...])
    wi_t = wit_ref[...]

    def fill_block(j):
        k0 = pl.multiple_of(j * tq, tq)
        key_ref[j] = _index_keys_t(ki_ref[pl.ds(k0, tq), :], qi_all, wi_t, q0, k0, tq, causal_mask=False)

    def fill_pair(t, carry):
        fill_block(2 * t)
        fill_block(2 * t + 1)
        return carry

    lax.fori_loop(0, i // 2, fill_pair, 0)

    @pl.when(i % 2 == 1)
    def _():
        fill_block(i - 1)

    key_ref[i] = _index_keys_t(ki_ref[pl.ds(pl.multiple_of(q0, tq), tq), :], qi_all, wi_t, q0, q0, tq)

    kr = COUNT_ROWS
    row = lax.broadcasted_iota(jnp.int32, (kr, 1), 0)

    def count(pred):
        def block(j, acc):
            for s in range(tq // kr):
                kk = key_ref[j, s * kr:(s + 1) * kr, :]
                acc = acc + jnp.where(pred(kk, j * tq + s * kr), 1, 0)
            return acc

        acc = lax.fori_loop(0, nblk // 2, lambda t, a: block(2 * t + 1, block(2 * t, a)), jnp.zeros((kr, tq), jnp.int32))
        acc = lax.cond(nblk % 2 == 1, lambda a: block(nblk - 1, a), lambda a: a, acc)
        return jnp.sum(acc.astype(jnp.float32), axis=0, keepdims=True).astype(jnp.int32)

    def any_true(mask):
        return jnp.max(jnp.where(mask, 1.0, 0.0)) > 0.5

    def bit_cond(st):
        return jnp.logical_and(st[0] < 32, st[4])

    def bit_step(st):
        p, thr, cnt, final, _ = st
        cand = thr ^ lax.shift_left(jnp.int32(1), 31 - p)
        c = count(lambda kk, base: kk >= cand)
        take = c >= topk
        thr = jnp.where(take, cand, thr)
        cnt = jnp.where(take, c, cnt)
        final = lax.cond(p == TIE_CHECK_PASS,
                         lambda: jnp.where(cnt - count(lambda kk, base: kk == thr) < topk, 1, 0),
                         lambda: final)
        return p + 1, thr, cnt, final, any_true(jnp.logical_and(cnt != topk, final == 0))

    ncols = nblk * tq
    init = (jnp.int32(0), jnp.full((1, tq), INT_MIN, jnp.int32), jnp.full((1, tq), ncols, jnp.int32),
            jnp.zeros((1, tq), jnp.int32), ncols != topk)
    _, thr, cnt, _, _ = lax.while_loop(bit_cond, bit_step, init)
    tie = jnp.logical_and(cnt > topk, thr > KEY_NEG_INF)

    def resolve_ties():
        need = (topk - count(lambda kk, base: kk > thr)).astype(jnp.float32)
        ri = lax.broadcasted_iota(jnp.int32, (kr, kr), 0)
        ci = lax.broadcasted_iota(jnp.int32, (kr, kr), 1)
        lower = jnp.where(ci <= ri, 1.0, 0.0).astype(jnp.bfloat16)
        rowf = row.astype(jnp.float32)

        def body(j, st):
            carry, cut_acc = st
            eqs = [key_ref[j, s * kr:(s + 1) * kr, :] == thr for s in range(tq // kr)]
            within = [jnp.dot(lower, jnp.where(eq, 1.0, 0.0).astype(jnp.bfloat16), preferred_element_type=jnp.float32)
                      for eq in eqs]
            for s, (eq, pref) in enumerate(zip(eqs, within)):
                end_pos = rowf + (j * tq + s * kr + 1).astype(jnp.float32)
                cut_acc = jnp.maximum(cut_acc, jnp.where(jnp.logical_and(eq, pref + carry == need), end_pos, 0.0))
                carry = carry + pref[kr - 1:kr, :]
            return carry, cut_acc

        _, cut_acc = lax.fori_loop(0, nblk, body, (jnp.zeros((1, tq), jnp.float32), jnp.zeros((kr, tq), jnp.float32)))
        return jnp.where(tie, jnp.max(cut_acc, axis=0, keepdims=True).astype(jnp.int32), INT_MAX)

    cut = lax.cond(any_true(tie), resolve_ties, lambda: jnp.full((1, tq), INT_MAX, jnp.int32))
    thr = jnp.maximum(thr, KEY_NEG_INF + 1)

    def emit(j, carry):
        kk = key_ref[j]
        kpos = j * tq + lax.broadcasted_iota(jnp.int32, (tq, 1), 0)
        sel = jnp.logical_or(kk > thr, jnp.logical_and(kk == thr, kpos < cut))
        sel_ref[0, j] = jnp.where(sel, 1, 0).astype(jnp.int8)
        return carry

    lax.fori_loop(0, nblk, emit, 0)

    def clear(j, carry):
        sel_ref[0, j] = jnp.zeros((tq, tq), jnp.int8)
        return carry

    lax.fori_loop(nblk, nq, clear, 0)


def _dsa_select(hb, wi_t, bsz, L, tq):
    T = bsz * L
    nq = L // tq
    topk = min(DSA_TOPK_MAX, L // 4)
    qmap = lambda b, i: (0, b * nq + i)
    return pl.pallas_call(
        partial(_dsa_select_kernel, tq=tq, topk=topk, nq=nq),
        grid=(bsz, nq),
        in_specs=[pl.BlockSpec((tq, IDX_EXP_WIDTH), lambda b, i: (b * nq + i, OD_QI // IDX_EXP_WIDTH)),
                  pl.BlockSpec((8, tq), qmap),
                  pl.BlockSpec((L, LANES), lambda b, i: (b, OD_KI // LANES))],
        out_specs=pl.BlockSpec((1, nq, tq, tq), lambda b, i: (b * nq + i, 0, 0, 0)),
        out_shape=jax.ShapeDtypeStruct((bsz * nq, nq, tq, tq), jnp.int8),
        scratch_shapes=[pltpu.VMEM((nq, tq, tq), jnp.int32)],
        compiler_params=_params("parallel", "arbitrary"),
    )(hb, wi_t, hb)


def _split_head_pair(x_pair):
    lane = lax.broadcasted_iota(jnp.int32, x_pair.shape, 1)
    zero = jnp.zeros_like(x_pair)
    return jnp.where(lane < 64, x_pair, zero), jnp.where(lane >= 64, x_pair, zero)


def _dsa_attn_kernel(qidx_ref, kidx_ref, q_ref, sel_ref, k_ref, vt_ref, o_ref, qm_ref, m_ref, acc_ref, *, tq, tk, ts):
    i = qidx_ref[pl.program_id(1)]
    j = kidx_ref[pl.program_id(1)]
    npair = DSA_HEADS // 2

    @pl.when(j == 0)
    def _():
        for p in range(npair):
            qa, qb = _split_head_pair(q_ref[:, LANES * p:LANES * (p + 1)])
            qm_ref[2 * p] = qa
            qm_ref[2 * p + 1] = qb
        m_ref[...] = jnp.full(m_ref.shape, NEG_BIG, jnp.float32)
        acc_ref[...] = jnp.zeros(acc_ref.shape, jnp.float32)

    sel = jnp.concatenate([jnp.concatenate([sel_ref[a, c] for a in range(tq // ts)], axis=1)
                           for c in range(tk // ts)], axis=0)
    bias = (sel.astype(jnp.float32) - 1.0) * (-NEG_BIG)
    vrow = lax.broadcasted_iota(jnp.int32, (LANES, tk), 0)
    ones = jnp.ones((LANES, tk), jnp.bfloat16)
    scores = []
    for h in range(DSA_HEADS):
        kp = k_ref[:, LANES * (h // 2):LANES * (h // 2 + 1)]
        s = lax.dot_general(kp, qm_ref[h], (((1,), (1,)), ((), ())), preferred_element_type=jnp.float32)
        scores.append(s + bias)
    for h in range(DSA_HEADS):
        vt = vt_ref[LANES * (h // 2):LANES * (h // 2 + 1), :]
        v_aug = jnp.where((vrow < 64) if h % 2 == 0 else (vrow >= 64), vt, ones)
        s = scores[h]
        m_prev = m_ref[h:h + 1, :]
        m_new = jnp.maximum(m_prev, jnp.max(s, axis=0, keepdims=True))
        alpha = jnp.exp(m_prev - m_new)
        e = jnp.exp(s - m_new).astype(jnp.bfloat16)
        acc_ref[h] = alpha * acc_ref[h] + jnp.dot(v_aug, e, preferred_element_type=jnp.float32)
        m_ref[h:h + 1, :] = m_new

    @pl.when(j == ((i + 1) * tq - 1) // tk)
    def _():
        rowi = lax.broadcasted_iota(jnp.int32, (LANES, tq), 0)
        for p in range(npair):
            a = acc_ref[2 * p]
            b = acc_ref[2 * p + 1]
            out_t = jnp.where(rowi < 64, a / a[64:65, :], b / b[0:1, :])
            o_ref[:, LANES * p:LANES * (p + 1)] = out_t.T.astype(o_ref.dtype)


def _dsa_attention(hb, v_t, sel, bsz, L, tq, tk):
    T = bsz * L
    nq, nk = L // tq, L // tk
    ts = sel.shape[-1]
    assert tq % ts == 0 and tk % ts == 0 and (L // ts) % (tq // ts) == 0
    pairs = [(i, j) for i in range(nq) for j in range(((i + 1) * tq - 1) // tk + 1)]
    qidx = jnp.asarray(np.array([p[0] for p in pairs], np.int32))
    kidx = jnp.asarray(np.array([p[1] for p in pairs], np.int32))

    def qmap(col):
        return lambda b, s, qi, ki: (b * nq + qi[s], col)

    def kmap(col):
        return lambda b, s, qi, ki: (b * nk + ki[s], col)

    grid_spec = pltpu.PrefetchScalarGridSpec(
        num_scalar_prefetch=2,
        grid=(bsz, len(pairs)),
        in_specs=[pl.BlockSpec((tq, DSA_WIDTH), qmap(OD_Q // DSA_WIDTH)),
                  pl.BlockSpec((tq // ts, tk // ts, ts, ts), lambda b, s, qi, ki: (b * nq + qi[s], ki[s], 0, 0)),
                  pl.BlockSpec((tk, DSA_WIDTH), kmap(OD_K // DSA_WIDTH)),
                  pl.BlockSpec((DSA_WIDTH, tk), lambda b, s, qi, ki: (0, b * nk + ki[s]))],
        out_specs=pl.BlockSpec((tq, DSA_WIDTH), qmap(0)),
        scratch_shapes=[pltpu.VMEM((DSA_HEADS, tq, LANES), jnp.bfloat16),
                        pltpu.VMEM((DSA_HEADS, tq), jnp.float32),
                        pltpu.VMEM((DSA_HEADS, LANES, tq), jnp.float32)])
    return pl.pallas_call(
        partial(_dsa_attn_kernel, tq=tq, tk=tk, ts=ts),
        grid_spec=grid_spec,
        out_shape=jax.ShapeDtypeStruct((T, DSA_WIDTH), jnp.bfloat16),
        compiler_params=_params("parallel", "arbitrary"),
    )(qidx, kidx, hb, sel, hb, v_t)


def _dilated_kernel(q_ref, kp_ref, kc_ref, vp_ref, vc_ref, o_ref, lse_ref, *, tq):
    a = pl.program_id(2)
    row = lax.broadcasted_iota(jnp.int32, (tq, 2 * tq), 0)
    c = lax.broadcasted_iota(jnp.int32, (tq, 2 * tq), 1)
    first_col = jnp.where(a == 0, tq, 0)
    valid = jnp.logical_and(jnp.logical_and(c >= row, c <= row + tq), c >= first_col)
    lane = lax.broadcasted_iota(jnp.int32, (tq, LANES), 1)
    for p in range(DIL_HEADS // 2):
        sl = slice(LANES * p, LANES * (p + 1))
        kk = jnp.concatenate([kp_ref[:, sl], kc_ref[:, sl]], axis=0)
        vv = jnp.concatenate([vp_ref[:, sl], vc_ref[:, sl]], axis=0)
        outs, lses = [], []
        for qh in _split_head_pair(q_ref[:, sl]):
            s = lax.dot_general(qh, kk, (((1,), (1,)), ((), ())), preferred_element_type=jnp.float32)
            s = jnp.where(valid, s, NEG_BIG)
            m = jnp.max(s, axis=1, keepdims=True)
            e = jnp.exp(s - m)
            l = jnp.sum(e, axis=1, keepdims=True)
            outs.append(jnp.dot(e.astype(jnp.bfloat16), vv, preferred_element_type=jnp.float32) / l)
            lses.append(m + jnp.log(l))
        o_ref[:, sl] = jnp.where(lane < 64, outs[0], outs[1])
        lse_ref[:, sl] = jnp.where(lane < 64, lses[0], lses[1])


def _dil_proj_kernel(x_ref, w_ref, o_ref, *scratch, dil):
    res = jnp.dot(x_ref[...].astype(jnp.bfloat16), w_ref[...], preferred_element_type=jnp.float32)
    if dil == 1:
        o_ref[...] = res.astype(o_ref.dtype)
        return
    res_ref, = scratch
    rows, cols = res.shape[0] // dil, res.shape[1]
    for c in range(cols // LANES):
        res_ref[c] = res[:, LANES * c:LANES * (c + 1)]
    for r in range(dil):
        for c in range(cols // LANES):
            o_ref[:, cols * r + LANES * c:cols * r + LANES * (c + 1)] = (
                res_ref.at[c][pl.ds(r, rows, stride=dil), :].astype(o_ref.dtype))


def _dil_proj(x, w, dil, tm=1024):
    T, K = x.shape
    C = w.shape[1]
    tm = min(tm, T)
    return pl.pallas_call(
        partial(_dil_proj_kernel, dil=dil),
        grid=(T // tm,),
        in_specs=[pl.BlockSpec((tm, K), lambda i: (i, 0)), pl.BlockSpec((K, C), lambda i: (0, 0))],
        out_specs=pl.BlockSpec((tm // dil, dil * C), lambda i: (i, 0)),
        out_shape=jax.ShapeDtypeStruct((T // dil, dil * C), jnp.bfloat16),
        scratch_shapes=[] if dil == 1 else [pltpu.VMEM((C // LANES, tm, LANES), jnp.float32)],
        compiler_params=_params("parallel"),
    )(x, w)


def _dilated_group(pg, bsz, L, g, tq):
    window, dil = DIL_PATTERNS[g]
    assert window // dil == tq
    M = L // dil
    nb = M // tq
    ncol = DIL_COLS // DIL_WIDTH

    def cur(col):
        return lambda b, r, a: (b * nb + a, r * ncol + col)

    def prev(col):
        return lambda b, r, a: (b * nb + jnp.maximum(a - 1, 0), r * ncol + col)

    blk = (tq, DIL_WIDTH)
    out_map = lambda b, r, a: (b * nb + a, r)
    return pl.pallas_call(
        partial(_dilated_kernel, tq=tq),
        grid=(bsz, dil, nb),
        in_specs=[pl.BlockSpec(blk, cur(0)), pl.BlockSpec(blk, prev(1)), pl.BlockSpec(blk, cur(1)),
                  pl.BlockSpec(blk, prev(2)), pl.BlockSpec(blk, cur(2))],
        out_specs=[pl.BlockSpec(blk, out_map), pl.BlockSpec(blk, out_map)],
        out_shape=[jax.ShapeDtypeStruct((bsz * M, dil * DIL_WIDTH), jnp.float32)] * 2,
        compiler_params=_params("parallel", "parallel", "arbitrary"),
    )(pg, pg, pg, pg, pg)


def _dilated_combine_kernel(*refs, dils, tm):
    ng = len(dils)
    o_refs, l_refs, out_ref, scratch = refs[:ng], refs[ng:2 * ng], refs[2 * ng], refs[2 * ng + 1:]

    def token_major(ref, dil, buf):
        if dil == 1:
            return ref[...]
        nslab = DIL_WIDTH // LANES
        for r in range(dil):
            for c in range(nslab):
                buf.at[c][pl.ds(r, tm // dil, stride=dil), :] = ref[:, DIL_WIDTH * r + LANES * c:DIL_WIDTH * r + LANES * (c + 1)]
        return jnp.concatenate([buf[c] for c in range(nslab)], axis=1)

    bufs = iter(scratch)
    outs = [token_major(o_refs[g], d, None if d == 1 else next(bufs)) for g, d in enumerate(dils)]
    lses = [token_major(l_refs[g], d, None if d == 1 else next(bufs)) for g, d in enumerate(dils)]
    m = lses[0]
    for l in lses[1:]:
        m = jnp.maximum(m, l)
    ws = [jnp.exp(l - m) for l in lses]
    num, den = ws[0] * outs[0], ws[0]
    for w, o in zip(ws[1:], outs[1:]):
        num, den = num + w * o, den + w
    out_ref[...] = (num / den).astype(out_ref.dtype)


def _dilated_combine(outs, lses, dils, T, tm=512):
    tm = min(tm, T)
    specs = [pl.BlockSpec((tm // d, d * DIL_WIDTH), lambda i: (i, 0)) for d in dils]
    n_buf = 2 * sum(1 for d in dils if d != 1)
    return pl.pallas_call(
        partial(_dilated_combine_kernel, dils=tuple(dils), tm=tm),
        grid=(T // tm,),
        in_specs=specs + specs,
        out_specs=pl.BlockSpec((tm, DIL_WIDTH), lambda i: (i, 0)),
        out_shape=jax.ShapeDtypeStruct((T, DIL_WIDTH), jnp.bfloat16),
        scratch_shapes=[pltpu.VMEM((DIL_WIDTH // LANES, tm, LANES), jnp.float32)] * n_buf,
        compiler_params=_params("parallel"),
    )(*outs, *lses)


def _mm_nt_kernel(w_ref, x_ref, o_ref):
    o_ref[...] = lax.dot_general(w_ref[...], x_ref[...].astype(jnp.bfloat16), (((1,), (1,)), ((), ())),
                                 preferred_element_type=jnp.float32).astype(o_ref.dtype)


def _matmul_nt(w_t, x, out_dtype, tm=1024):
    N, K = w_t.shape
    T = x.shape[0]
    tm = min(tm, T)
    return pl.pallas_call(
        _mm_nt_kernel,
        grid=(T // tm,),
        in_specs=[pl.BlockSpec((N, K), lambda i: (0, 0)), pl.BlockSpec((tm, K), lambda i: (i, 0))],
        out_specs=pl.BlockSpec((N, tm), lambda i: (0, i)),
        out_shape=jax.ShapeDtypeStruct((N, T), out_dtype),
        compiler_params=_params("parallel"),
    )(w_t, x)


def _odd_weights(w_in):
    D = w_in.shape[0]
    q, k, v, qi, ki, wi, dil = _split(w_in, OD_SIZES)
    zeros = lambda n: jnp.zeros((D, n), w_in.dtype)
    qi_exp = jnp.concatenate(
        [jnp.concatenate([qi[:, IDX_DIM * h:IDX_DIM * (h + 1)] * IDX_DIM ** -0.5, zeros(LANES - IDX_DIM)], axis=1)
         for h in range(IDX_HEADS)], axis=1)
    dil = dil.reshape(D, 3, DIL_GROUPS, DIL_WIDTH)
    w_dil = [jnp.concatenate([dil[:, 0, g] * DIL_HEAD_DIM ** -0.5, dil[:, 1, g], dil[:, 2, g]], axis=1).astype(jnp.bfloat16)
             for g in range(DIL_GROUPS)]
    wb = jnp.concatenate([q * DSA_HEAD_DIM ** -0.5, k, qi_exp, ki, zeros(LANES - IDX_DIM),
                          zeros(OD_COLS_PADDED - OD_KI - LANES)], axis=1).astype(jnp.bfloat16)
    wv_t = v.T.astype(jnp.bfloat16)
    wwi_t = jnp.concatenate([wi * IDX_HEADS ** -0.5, zeros(8 - IDX_HEADS)], axis=1).T.astype(jnp.bfloat16)
    return wb, w_dil, wv_t, wwi_t


def _odd_layer_mix(x2, w_in, bsz, L):
    wb, w_dil, wv_t, wwi_t = _odd_weights(w_in)
    hb = _matmul(x2, wb, jnp.bfloat16, tm=1024, tn=1024)
    v_t = _matmul_nt(wv_t, x2, jnp.bfloat16)
    wi_t = _matmul_nt(wwi_t, x2, jnp.float32)
    sel = _dsa_select(hb, wi_t, bsz, L, min(DSA_TQ, L))
    o_c = _dsa_attention(hb, v_t, sel, bsz, L, min(DSA_ATT_TQ, L), min(DSA_TK, L))
    dils = [d for _, d in DIL_PATTERNS]
    groups = [_dilated_group(_dil_proj(x2, w_dil[g], dils[g]), bsz, L, g, DIL_PATTERNS[g][0] // dils[g])
              for g in range(DIL_GROUPS)]
    o_d = _dilated_combine([o for o, _ in groups], [l for _, l in groups], dils, bsz * L)
    return o_c, o_d


EV_Q, EV_K, EV_V, EV_R, EV_G = 0, 256, 512, 1024, 1536
EV_COLS_PADDED = 2048
GLA_ROWS = 512
S5_CHUNK = 64


def _gla_kernel(q_ref, k_ref, v_ref, r_ref, g_ref, wg_ref, bg_ref, ng_ref, o_ref, st_ref, *, rows):
    C = GLA_CHUNK

    @pl.when(pl.program_id(1) == 0)
    def _():
        st_ref[...] = jnp.zeros(st_ref.shape, jnp.float32)

    ri = lax.broadcasted_iota(jnp.int32, (C, C), 0)
    ci = lax.broadcasted_iota(jnp.int32, (C, C), 1)
    causal = ci <= ri
    tril = jnp.where(causal, 1.0, 0.0).astype(jnp.float32)
    wg = wg_ref[...].astype(jnp.bfloat16)
    bg = bg_ref[...]
    ng = ng_ref[...]

    def chunk(c, carry):
        r0 = pl.multiple_of(c * C, C)
        rs = pl.ds(r0, C)
        logit = jnp.dot(g_ref[rs, :].astype(jnp.bfloat16), wg, preferred_element_type=jnp.float32) + bg
        log_a = jax.nn.log_sigmoid(logit) / GLA_GATE_TAU
        bcum = jnp.dot(tril, log_a, precision=lax.Precision.HIGHEST, preferred_element_type=jnp.float32)
        b_last = bcum[C - 1:C, :]
        q_t = (q_ref[rs, :] * jnp.exp(bcum)).astype(jnp.bfloat16)
        k_t = (k_ref[rs, :] * jnp.exp(-bcum)).astype(jnp.bfloat16)
        k_end = (k_ref[rs, :] * jnp.exp(b_last - bcum)).astype(jnp.bfloat16)
        dec = jnp.exp(b_last)
        for p in range(GLA_HEADS // 2):
            sl = slice(LANES * p, LANES * (p + 1))
            q_halves = _split_head_pair(q_t[:, sl])
            ke_halves = _split_head_pair(k_end[:, sl])
            for half in range(2):
                h = 2 * p + half
                hs = slice(GLA_DV * h, GLA_DV * (h + 1))
                qm = q_halves[half]
                att = lax.dot_general(qm, k_t[:, sl], (((1,), (1,)), ((), ())), preferred_element_type=jnp.float32)
                att = jnp.where(causal, att, 0.0).astype(jnp.bfloat16)
                v_h = v_ref[rs, hs].astype(jnp.bfloat16)
                st = st_ref[h]
                o = jnp.dot(att, v_h, preferred_element_type=jnp.float32)
                o = o + lax.dot_general(qm, st.astype(jnp.bfloat16), (((1,), (1,)), ((), ())),
                                        preferred_element_type=jnp.float32)
                kv_t = lax.dot_general(v_h, ke_halves[half], (((0,), (0,)), ((), ())),
                                       preferred_element_type=jnp.float32)
                st_ref[h] = st * dec[:, sl] + kv_t
                o = o * lax.rsqrt(jnp.mean(o * o, axis=-1, keepdims=True) + LN_EPS) * ng
                o = o * jax.nn.silu(r_ref[rs, hs])
                o_ref[rs, hs] = o.astype(o_ref.dtype)
        return carry

    lax.fori_loop(0, rows // C, chunk, 0)


def _gla(hf, w_gate2, b_gate2, norm_g, bsz, L):
    T = bsz * L
    rows = min(GLA_ROWS, L)
    nb = L // rows
    dkw = GLA_HEADS * GLA_DK
    dvw = GLA_HEADS * GLA_DV
    wg = jnp.pad(w_gate2, ((0, LANES - GLA_GATE_RANK), (0, 0)))

    def rmap(col):
        return lambda b, i: (b * nb + i, col)

    const = lambda b, i: (0, 0)
    return pl.pallas_call(
        partial(_gla_kernel, rows=rows),
        grid=(bsz, nb),
        in_specs=[pl.BlockSpec((rows, dkw), rmap(EV_Q // dkw)),
                  pl.BlockSpec((rows, dkw), rmap(EV_K // dkw)),
                  pl.BlockSpec((rows, dvw), rmap(EV_V // dvw)),
                  pl.BlockSpec((rows, dvw), rmap(EV_R // dvw)),
                  pl.BlockSpec((rows, LANES), rmap(EV_G // LANES)),
                  pl.BlockSpec((LANES, dkw), const),
                  pl.BlockSpec((1, dkw), const),
                  pl.BlockSpec((1, GLA_DV), const)],
        out_specs=pl.BlockSpec((rows, dvw), rmap(0)),
        out_shape=jax.ShapeDtypeStruct((T, dvw), jnp.bfloat16),
        scratch_shapes=[pltpu.VMEM((GLA_HEADS, GLA_DV, LANES), jnp.float32)],
        compiler_params=_params("parallel", "arbitrary"),
    )(hf, hf, hf, hf, hf, wg, b_gate2.reshape(1, dkw), norm_g.reshape(1, GLA_DV))


def _s5_tables(a_re, a_im, log_dt, b_re, b_im, c_re, c_im, d_skip):
    f32 = jnp.float32
    Cs, G, P, N = S5_CHUNK, S5_GROUPS, S5_STATE, S5_GROUP
    lam_re = jnp.minimum(a_re.astype(f32), S5_MAX_RE)
    lam_im = a_im.astype(f32)
    dt = jnp.exp(log_dt.astype(f32))[:, None]
    mag = jnp.exp(lam_re * dt)
    ab_re = mag * jnp.cos(lam_im * dt)
    ab_im = mag * jnp.sin(lam_im * dt)
    inv = 1.0 / (lam_re * lam_re + lam_im * lam_im)
    z_re = ((ab_re - 1.0) * lam_re + ab_im * lam_im) * inv
    z_im = (ab_im * lam_re - (ab_re - 1.0) * lam_im) * inv
    br, bi = b_re.astype(f32), b_im.astype(f32)
    bb_re = z_re[..., None] * br - z_im[..., None] * bi
    bb_im = z_re[..., None] * bi + z_im[..., None] * br
    kk = jnp.arange(Cs + 1, dtype=f32)[:, None, None]
    pmag = jnp.exp(kk * (lam_re * dt))
    pw_re = pmag * jnp.cos(kk * (lam_im * dt))
    pw_im = pmag * jnp.sin(kk * (lam_im * dt))
    cr, ci = c_re.astype(f32), c_im.astype(f32)
    ca_re = cr[None] * pw_re[:, :, None, :] - ci[None] * pw_im[:, :, None, :]
    ca_im = cr[None] * pw_im[:, :, None, :] + ci[None] * pw_re[:, :, None, :]
    hi = lax.Precision.HIGHEST
    kern = (jnp.einsum('kgnp,gpm->kgnm', ca_re[:Cs], bb_re, precision=hi)
            - jnp.einsum('kgnp,gpm->kgnm', ca_im[:Cs], bb_im, precision=hi))
    jj = jnp.arange(Cs)[:, None]
    ii = jnp.arange(Cs)[None, :]
    tz = jnp.where((ii >= jj)[:, :, None, None, None], kern[jnp.maximum(ii - jj, 0)], 0.0)
    tz = jnp.transpose(tz, (2, 4, 0, 3, 1)).reshape(G, N * Cs, N * Cs)
    rev_re, rev_im = pw_re[Cs - 1::-1][:Cs], pw_im[Cs - 1::-1][:Cs]
    ws_re = rev_re[..., None] * bb_re[None] - rev_im[..., None] * bb_im[None]
    ws_im = rev_re[..., None] * bb_im[None] + rev_im[..., None] * bb_re[None]
    to_ws = lambda w: jnp.pad(jnp.transpose(w, (1, 3, 0, 2)).reshape(G, N * Cs, P), ((0, 0), (0, 0), (0, LANES - P)))
    to_wo = lambda w: jnp.pad(jnp.transpose(w, (1, 3, 2, 0)).reshape(G, P, N * Cs), ((0, 0), (0, LANES - P), (0, 0)))
    a_cs = jnp.stack([jnp.pad(pw_re[Cs], ((0, 0), (0, LANES - P))), jnp.pad(pw_im[Cs], ((0, 0), (0, LANES - P)))], axis=1)
    d_exp = jnp.repeat(d_skip.astype(f32).reshape(G, 1, N), Cs, axis=2)
    bf = jnp.bfloat16
    return (tz.astype(bf), to_ws(ws_re).astype(bf), to_ws(ws_im).astype(bf),
            to_wo(ca_re[1:]).astype(bf), to_wo(-ca_im[1:]).astype(bf), a_cs, d_exp)


def _s5_kernel(u_ref, tz_ref, wsr_ref, wsi_ref, wor_ref, woi_ref, acs_ref, d_ref, y_ref, xr_ref, xi_ref,
               *, nchunk, nbatch):
    u32 = jnp.concatenate([u_ref[m] for m in range(S5_GROUP)], axis=1)
    u = u32.astype(jnp.bfloat16)
    xr_ref[...] = jnp.dot(u, wsr_ref[0], preferred_element_type=jnp.float32)
    xi_ref[...] = jnp.dot(u, wsi_ref[0], preferred_element_type=jnp.float32)
    ar = acs_ref[0, 0:1, :]
    ai = acs_ref[0, 1:2, :]

    def step(c, carry):
        new = []
        for b in range(nbatch):
            sr, si = carry[2 * b], carry[2 * b + 1]
            row = pl.ds(b * nchunk + c, 1)
            lr, li = xr_ref[row, :], xi_ref[row, :]
            xr_ref[row, :] = sr
            xi_ref[row, :] = si
            new += [ar * sr - ai * si + lr, ar * si + ai * sr + li]
        return tuple(new)

    zero = jnp.zeros((1, LANES), jnp.float32)
    lax.fori_loop(0, nchunk, step, (zero,) * (2 * nbatch))
    y = jnp.dot(u, tz_ref[0], preferred_element_type=jnp.float32)
    y = y + jnp.dot(xr_ref[...].astype(jnp.bfloat16), wor_ref[0], preferred_element_type=jnp.float32)
    y = y + jnp.dot(xi_ref[...].astype(jnp.bfloat16), woi_ref[0], preferred_element_type=jnp.float32)
    y = jax.nn.gelu(y + d_ref[0] * u32)
    for n in range(S5_GROUP):
        y_ref[n] = y[:, S5_CHUNK * n:S5_CHUNK * (n + 1)]


def _s5_scan(u_t, tables, bsz, nchunk):
    width, R, Cs = u_t.shape
    G, N = S5_GROUPS, S5_GROUP
    W = N * Cs
    tz, wsr, wsi, wor, woi, a_cs, d_exp = tables
    gmap = lambda g: (g, 0, 0)
    return pl.pallas_call(
        partial(_s5_kernel, nchunk=nchunk, nbatch=bsz),
        grid=(G,),
        in_specs=[pl.BlockSpec((N, R, Cs), gmap), pl.BlockSpec((1, W, W), gmap),
                  pl.BlockSpec((1, W, LANES), gmap), pl.BlockSpec((1, W, LANES), gmap),
                  pl.BlockSpec((1, LANES, W), gmap), pl.BlockSpec((1, LANES, W), gmap),
                  pl.BlockSpec((1, 2, LANES), gmap), pl.BlockSpec((1, 1, W), gmap)],
        out_specs=pl.BlockSpec((N, R, Cs), gmap),
        out_shape=jax.ShapeDtypeStruct((width, R, Cs), jnp.float32),
        scratch_shapes=[pltpu.VMEM((R, LANES), jnp.float32), pltpu.VMEM((R, LANES), jnp.float32)],
        compiler_params=_params("parallel"),
    )(u_t, tz, wsr, wsi, wor, woi, a_cs, d_exp)


def _glu_kernel(yt_ref, wt_ref, b_ref, o_ref):
    y = yt_ref[...]
    gate = jnp.dot(wt_ref[...], y.astype(jnp.bfloat16), preferred_element_type=jnp.float32) + b_ref[...]
    o_ref[...] = (y * jax.nn.sigmoid(gate)).T.astype(o_ref.dtype)


def _glu(y_t, w_glu, b_glu, tm=1024):
    W, T = y_t.shape
    tm = min(tm, T)
    return pl.pallas_call(
        _glu_kernel,
        grid=(T // tm,),
        in_specs=[pl.BlockSpec((W, tm), lambda i: (0, i)),
                  pl.BlockSpec((W, W), lambda i: (0, 0)),
                  pl.BlockSpec((W, 1), lambda i: (0, 0))],
        out_specs=pl.BlockSpec((tm, W), lambda i: (i, 0)),
        out_shape=jax.ShapeDtypeStruct((T, W), jnp.bfloat16),
        compiler_params=_params("parallel"),
    )(y_t, w_glu.T.astype(jnp.bfloat16), b_glu.reshape(W, 1))


def _s5(u_t, s5_params, w_glu, b_glu, bsz, L):
    T = bsz * L
    nchunk = L // S5_CHUNK
    y_t = _s5_scan(u_t.reshape(S5_WIDTH, bsz * nchunk, S5_CHUNK), _s5_tables(*s5_params), bsz, nchunk)
    return _glu(y_t.reshape(S5_WIDTH, T), w_glu, b_glu)


def _even_weights(w_in):
    D = w_in.shape[0]
    q, k, v, r, g_lr, u = _split(w_in, EV_SIZES)
    pad = jnp.zeros((D, EV_COLS_PADDED - EV_G - GLA_GATE_RANK), w_in.dtype)
    wb = jnp.concatenate([q * GLA_DK ** -0.5, k, v, r, g_lr, pad], axis=1).astype(jnp.bfloat16)
    return wb, u.T.astype(jnp.bfloat16)


def _even_layer_mix(x2, w_in, w_gate2, b_gate2, norm_g, s5_params, w_glu, b_glu, bsz, L):
    wb, wu_t = _even_weights(w_in)
    hf = _matmul(x2, wb, jnp.float32, tm=1024, tn=512)
    u_t = _matmul_nt(wu_t, x2, jnp.float32)
    o_a = _gla(hf, w_gate2, b_gate2, norm_g, bsz, L)
    o_b = _s5(u_t, s5_params, w_glu, b_glu, bsz, L)
    return o_a, o_b


MOE_TILE = 512
ROW_TILE = 8
DMA_UNROLL = 8
MOE_VMEM_LIMIT_BYTES = 56 * 1024 * 1024


def _router_kernel(x_ref, w_ref, b_ref, e_ref, g_ref):
    logits = jnp.dot(x_ref[...].astype(jnp.bfloat16), w_ref[...], preferred_element_type=jnp.float32) + b_ref[...]
    tm = logits.shape[0]
    lane = lax.broadcasted_iota(jnp.int32, (tm, LANES), 1)
    logits = jnp.where(lane < N_EXPERTS, logits, NEG_BIG)
    tops, idxs = [], []
    for _ in range(TOP_K):
        m = jnp.max(logits, axis=1, keepdims=True)
        idx = jnp.min(jnp.where(logits == m, lane, LANES), axis=1, keepdims=True)
        tops.append(m)
        idxs.append(idx)
        logits = jnp.where(lane == idx, NEG_BIG, logits)
    exps = [jnp.exp(t - tops[0]) for t in tops]
    denom = exps[0]
    for e in exps[1:]:
        denom = denom + e
    lane4 = lax.broadcasted_iota(jnp.int32, (tm, TOP_K), 1)
    e_out = jnp.zeros((tm, TOP_K), jnp.int32)
    g_out = jnp.zeros((tm, TOP_K), jnp.float32)
    for k in range(TOP_K):
        e_out = jnp.where(lane4 == k, idxs[k], e_out)
        g_out = jnp.where(lane4 == k, exps[k] / denom, g_out)
    e_ref[...] = e_out
    g_ref[...] = g_out


def _router(x, router_w, router_b, tm=512):
    T, D = x.shape
    tm = min(tm, T)
    w = jnp.pad(router_w, ((0, 0), (0, LANES - N_EXPERTS))).astype(jnp.bfloat16)
    b = jnp.pad(router_b, (0, LANES - N_EXPERTS)).reshape(1, LANES)
    return pl.pallas_call(
        _router_kernel,
        grid=(T // tm,),
        in_specs=[pl.BlockSpec((tm, D), lambda i: (i, 0)),
                  pl.BlockSpec((D, LANES), lambda i: (0, 0)),
                  pl.BlockSpec((1, LANES), lambda i: (0, 0))],
        out_specs=[pl.BlockSpec((tm, TOP_K), lambda i: (i, 0)), pl.BlockSpec((tm, TOP_K), lambda i: (i, 0))],
        out_shape=[jax.ShapeDtypeStruct((T, TOP_K), jnp.int32), jax.ShapeDtypeStruct((T, TOP_K), jnp.float32)],
        compiler_params=_params("parallel"),
    )(x, w, b)


def _moe_rank_kernel(e_ref, rank_ref, count_ref, carry_ref):
    @pl.when(pl.program_id(0) == 0)
    def _():
        carry_ref[...] = jnp.zeros(carry_ref.shape, jnp.float32)

    e = e_ref[...]
    tm = e.shape[0]
    lane = lax.broadcasted_iota(jnp.int32, (tm, LANES), 1)
    onehot = jnp.zeros((tm, LANES), jnp.float32)
    for k in range(TOP_K):
        onehot = onehot + jnp.where(lane == e[:, k:k + 1], 1.0, 0.0)
    ri = lax.broadcasted_iota(jnp.int32, (tm, tm), 0)
    ci = lax.broadcasted_iota(jnp.int32, (tm, tm), 1)
    strict_lower = jnp.where(ci < ri, 1.0, 0.0).astype(jnp.bfloat16)
    before = jnp.dot(strict_lower, onehot.astype(jnp.bfloat16), preferred_element_type=jnp.float32) + carry_ref[...]
    lane4 = lax.broadcasted_iota(jnp.int32, (tm, TOP_K), 1)
    rank = jnp.zeros((tm, TOP_K), jnp.int32)
    for k in range(TOP_K):
        r_k = jnp.sum(jnp.where(lane == e[:, k:k + 1], before, 0.0), axis=1, keepdims=True).astype(jnp.int32)
        rank = jnp.where(lane4 == k, r_k, rank)
    rank_ref[...] = rank
    carry_ref[...] = carry_ref[...] + jnp.sum(onehot, axis=0, keepdims=True)
    count_ref[...] = carry_ref[...]


def _moe_rank(top_e, tm=256):
    T = top_e.shape[0]
    tm = min(tm, T)
    return pl.pallas_call(
        _moe_rank_kernel,
        grid=(T // tm,),
        in_specs=[pl.BlockSpec((tm, TOP_K), lambda i: (i, 0))],
        out_specs=[pl.BlockSpec((tm, TOP_K), lambda i: (i, 0)), pl.BlockSpec((1, LANES), lambda i: (0, 0))],
        out_shape=[jax.ShapeDtypeStruct((T, TOP_K), jnp.int32), jax.ShapeDtypeStruct((1, LANES), jnp.float32)],
        scratch_shapes=[pltpu.VMEM((1, LANES), jnp.float32)],
        compiler_params=_params("arbitrary"),
    )(top_e)


def _to_token_tiles(x, dst_ref, rows):
    for c in range(ROW_TILE):
        dst_ref[pl.ds(c, rows, stride=ROW_TILE), :] = x[:, LANES * c:LANES * (c + 1)]


def _from_token_tiles(src_ref, rows):
    return jnp.concatenate([src_ref[pl.ds(c, rows, stride=ROW_TILE), :] for c in range(ROW_TILE)], axis=1)


def _moe_dispatch_kernel(dest_ref, x_ref, init_ref, xbuf_ref, xs_ref, sem, *, tm):
    del init_ref
    i = pl.program_id(0)
    n = pl.num_programs(0)
    slot = i % 2

    def row_copy(s, r, d):
        return pltpu.make_async_copy(xs_ref.at[s, pl.ds(pl.multiple_of(r * ROW_TILE, ROW_TILE), ROW_TILE), :],
                                     xbuf_ref.at[pl.ds(pl.multiple_of(d * ROW_TILE, ROW_TILE), ROW_TILE), :], sem.at[s])

    def drain(s):
        def body(a, c):
            row_copy(s, 0, 0).wait()
            return c
        lax.fori_loop(0, tm * TOP_K, body, 0, unroll=DMA_UNROLL)

    @pl.when(i >= 2)
    def _():
        drain(slot)

    _to_token_tiles(x_ref[...], xs_ref.at[slot], tm)

    def body(r, c):
        for k in range(TOP_K):
            row_copy(slot, r, dest_ref[r * TOP_K + k]).start()
        return c

    lax.fori_loop(0, tm, body, 0, unroll=DMA_UNROLL)

    @pl.when(i == n - 1)
    def _():
        drain(slot)

        @pl.when(n > 1)
        def _():
            drain(1 - slot)


def _moe_dispatch(x, dest_flat, n_rows, tm=256):
    T, D = x.shape
    tm = min(tm, T)
    assert D == ROW_TILE * LANES
    init = jnp.zeros((n_rows * ROW_TILE, LANES), jnp.float32)
    return pl.pallas_call(
        partial(_moe_dispatch_kernel, tm=tm),
        grid=(T // tm,),
        in_specs=[pl.BlockSpec((tm * TOP_K,), lambda i: (i,), memory_space=pltpu.SMEM),
                  pl.BlockSpec((tm, D), lambda i: (i, 0)),
                  pl.BlockSpec(memory_space=pl.ANY)],
        out_specs=pl.BlockSpec(memory_space=pl.ANY),
        out_shape=jax.ShapeDtypeStruct((n_rows * ROW_TILE, LANES), jnp.float32),
        scratch_shapes=[pltpu.VMEM((2, tm * ROW_TILE, LANES), jnp.float32), pltpu.SemaphoreType.DMA((2,))],
        input_output_aliases={2: 0},
        compiler_params=_params("arbitrary"),
    )(dest_flat, x, init)


def _moe_ffn_kernel(te_ref, nt_ref, x_ref, w1_ref, b1_ref, w2_ref, b2_ref, y_ref, w1b_ref, w2b_ref):
    i = pl.program_id(0)
    prev = te_ref[jnp.maximum(i - 1, 0)]

    @pl.when(jnp.logical_or(i == 0, te_ref[i] != prev))
    def _():
        w1b_ref[...] = w1_ref[0].astype(jnp.bfloat16)
        w2b_ref[...] = w2_ref[0].astype(jnp.bfloat16)

    @pl.when(i < nt_ref[0])
    def _():
        x = _from_token_tiles(x_ref, MOE_TILE).astype(jnp.bfloat16)
        h = jnp.dot(x, w1b_ref[...], preferred_element_type=jnp.float32) + b1_ref[0]
        glu = jnp.minimum(h[:, :D_FF], SWIGLU_LIMIT)
        lin = jnp.clip(h[:, D_FF:], -SWIGLU_LIMIT, SWIGLU_LIMIT)
        act = glu * jax.nn.sigmoid(SWIGLU_ALPHA * glu) * (lin + 1.0)
        y = jnp.dot(act.astype(jnp.bfloat16), w2b_ref[...], preferred_element_type=jnp.float32) + b2_ref[0]
        _to_token_tiles(y, y_ref, MOE_TILE)

    @pl.when(i >= nt_ref[0])
    def _():
        y_ref[...] = jnp.zeros(y_ref.shape, y_ref.dtype)


def _moe_expert_ffn(xbuf, tile_expert, n_used, w1, b1, w2, b2):
    D, F2 = w1.shape[1], w1.shape[2]
    P = xbuf.shape[0] // ROW_TILE
    n_tiles = P // MOE_TILE
    blk = (MOE_TILE * ROW_TILE, LANES)
    grid_spec = pltpu.PrefetchScalarGridSpec(
        num_scalar_prefetch=2,
        grid=(n_tiles,),
        in_specs=[pl.BlockSpec(blk, lambda i, te, nt: (jnp.minimum(i, nt[0] - 1), 0)),
                  pl.BlockSpec((1, D, F2), lambda i, te, nt: (te[i], 0, 0)),
                  pl.BlockSpec((1, 1, F2), lambda i, te, nt: (te[i], 0, 0)),
                  pl.BlockSpec((1, D_FF, D), lambda i, te, nt: (te[i], 0, 0)),
                  pl.BlockSpec((1, 1, D), lambda i, te, nt: (te[i], 0, 0))],
        out_specs=pl.BlockSpec(blk, lambda i, te, nt: (i, 0)),
        scratch_shapes=[pltpu.VMEM((D, F2), jnp.bfloat16), pltpu.VMEM((D_FF, D), jnp.bfloat16)],
    )
    return pl.pallas_call(
        _moe_ffn_kernel,
        grid_spec=grid_spec,
        out_shape=jax.ShapeDtypeStruct(xbuf.shape, jnp.float32),
        compiler_params=pltpu.CompilerParams(dimension_semantics=("arbitrary",),
                                             vmem_limit_bytes=MOE_VMEM_LIMIT_BYTES),
    )(tile_expert, n_used, xbuf, w1, b1.reshape(-1, 1, F2), w2, b2.reshape(-1, 1, D))


def _moe_combine_kernel(dest_ref, gate_ref, x_ref, g_ref, b_ref, ybuf_ref, o_ref, rows_ref, sem, *, tm):
    def row_copy(k, r, d):
        return pltpu.make_async_copy(ybuf_ref.at[pl.ds(pl.multiple_of(d * ROW_TILE, ROW_TILE), ROW_TILE), :],
                                     rows_ref.at[k, pl.ds(pl.multiple_of(r * ROW_TILE, ROW_TILE), ROW_TILE), :], sem.at[0])

    def start(r, c):
        for k in range(TOP_K):
            row_copy(k, r, dest_ref[r * TOP_K + k]).start()
        return c

    lax.fori_loop(0, tm, start, 0, unroll=DMA_UNROLL)

    def wait(a, c):
        row_copy(0, 0, 0).wait()
        return c

    lax.fori_loop(0, tm * TOP_K, wait, 0, unroll=DMA_UNROLL)
    gate = gate_ref[...]
    ffn = gate[:, 0:1] * _from_token_tiles(rows_ref.at[0], tm)
    for k in range(1, TOP_K):
        ffn = ffn + gate[:, k:k + 1] * _from_token_tiles(rows_ref.at[k], tm)
    o_ref[...] = _layer_norm_rows(DEEPNORM_ALPHA * x_ref[...] + ffn, g_ref[...], b_ref[...])


def _moe_combine(ybuf, dest_flat, gate, x, g, b, tm=256):
    T, D = x.shape
    tm = min(tm, T)
    return pl.pallas_call(
        partial(_moe_combine_kernel, tm=tm),
        grid=(T // tm,),
        in_specs=[pl.BlockSpec((tm * TOP_K,), lambda i: (i,), memory_space=pltpu.SMEM),
                  pl.BlockSpec((tm, TOP_K), lambda i: (i, 0)),
                  pl.BlockSpec((tm, D), lambda i: (i, 0)),
                  pl.BlockSpec((1, D), lambda i: (0, 0)),
                  pl.BlockSpec((1, D), lambda i: (0, 0)),
                  pl.BlockSpec(memory_space=pl.ANY)],
        out_specs=pl.BlockSpec((tm, D), lambda i: (i, 0)),
        out_shape=jax.ShapeDtypeStruct((T, D), jnp.float32),
        scratch_shapes=[pltpu.VMEM((TOP_K, tm * ROW_TILE, LANES), jnp.float32), pltpu.SemaphoreType.DMA((1,))],
        compiler_params=_params("arbitrary"),
    )(dest_flat, gate, x, g.reshape(1, D), b.reshape(1, D), ybuf)


def _moe_layer(x, router_w, router_b, w1, b1, w2, b2, first_expert, ln_g, ln_b):
    T, D = x.shape
    A = T * TOP_K
    n_tiles = -(-(A + N_EXPERTS * (MOE_TILE - 1)) // MOE_TILE)
    top_e, gate = _router(x, router_w, router_b)
    rank, counts = _moe_rank(top_e)
    counts = counts[0, :N_EXPERTS].astype(jnp.int32)
    padded = (counts + MOE_TILE - 1) // MOE_TILE * MOE_TILE
    pend = jnp.cumsum(padded)
    pstart = pend - padded
    dest = (pstart[top_e] + rank).reshape(A)
    tile_start = jnp.arange(n_tiles, dtype=jnp.int32) * MOE_TILE
    tile_expert = jnp.minimum(jnp.sum(pend[None, :] <= tile_start[:, None], axis=1), N_EXPERTS - 1).astype(jnp.int32)
    n_used = (pend[-1:] // MOE_TILE).astype(jnp.int32)
    xbuf = _moe_dispatch(x, dest, n_tiles * MOE_TILE)
    ybuf = _moe_expert_ffn(xbuf, tile_expert + first_expert, n_used, w1, b1, w2, b2)
    return _moe_combine(ybuf, dest, gate, x, ln_g, ln_b)


def kernel(x, ev_w_in, gla_w_gate2, gla_b_gate2, gla_norm_g, s5_a_re, s5_a_im, s5_log_dt, s5_b_re, s5_b_im,
           s5_c_re, s5_c_im, s5_d, s5_w_glu, s5_b_glu, ev_w_out, od_w_in, od_w_out, ln1_g, ln1_b, ln2_g, ln2_b,
           router_w, router_b, moe_w1, moe_b1, moe_w2, moe_b2):
    bsz, L, D = x.shape
    T = bsz * L
    x = x.reshape(T, D)
    w1_all, b1_all = moe_w1.reshape((-1,) + moe_w1.shape[2:]), moe_b1.reshape(-1, moe_b1.shape[-1])
    w2_all, b2_all = moe_w2.reshape((-1,) + moe_w2.shape[2:]), moe_b2.reshape(-1, moe_b2.shape[-1])
    for layer in range(DEPTH):
        j = layer // 2
        if layer % 2 == 0:
            s5_params = (s5_a_re[j], s5_a_im[j], s5_log_dt[j], s5_b_re[j], s5_b_im[j], s5_c_re[j], s5_c_im[j], s5_d[j])
            o_1, o_2 = _even_layer_mix(x, ev_w_in[j], gla_w_gate2[j], gla_b_gate2[j], gla_norm_g[j], s5_params,
                                       s5_w_glu[j], s5_b_glu[j], bsz, L)
            w_out = ev_w_out[j]
        else:
            o_1, o_2 = _odd_layer_mix(x, od_w_in[j], bsz, L)
            w_out = od_w_out[j]
        x = _matmul2_res_ln(o_1, o_2, w_out, x, ln1_g[layer], ln1_b[layer])
        x = _moe_layer(x, router_w[layer], router_b[layer], w1_all, b1_all, w2_all, b2_all, layer * N_EXPERTS,
                       ln2_g[layer], ln2_b[layer])
    return x.reshape(bsz, L, D)
```

```python
from functools import partial

import numpy as np
import jax
import jax.numpy as jnp
from jax import lax
from jax.experimental import pallas as pl
from jax.experimental.pallas import tpu as pltpu

D_MODEL = 1024
DEPTH = 4
DEEPNORM_ALPHA = (2.0 * DEPTH) ** 0.25
LN_EPS = 1e-5
MIX_WIDTH = D_MODEL

GLA_HEADS = 4
GLA_DV = MIX_WIDTH // 2 // GLA_HEADS
GLA_DK = GLA_DV // 2
GLA_GATE_RANK = 16
GLA_GATE_TAU = 16.0
GLA_CHUNK = 64

S5_WIDTH = MIX_WIDTH // 2
S5_GROUP = 16
S5_GROUPS = S5_WIDTH // S5_GROUP
S5_STATE = 64
S5_MAX_RE = -1e-4

EV_SIZES = (GLA_HEADS * GLA_DK, GLA_HEADS * GLA_DK, GLA_HEADS * GLA_DV, GLA_HEADS * GLA_DV, GLA_GATE_RANK, S5_WIDTH)

DSA_HEADS = 8
DSA_HEAD_DIM = 64
DSA_WIDTH = DSA_HEADS * DSA_HEAD_DIM
IDX_HEADS = 4
IDX_DIM = 64
DSA_TOPK_MAX = 256

DIL_PATTERNS = ((128, 1), (512, 4), (2048, 16))
DIL_GROUPS = len(DIL_PATTERNS)
DIL_HEADS = 8
DIL_HEAD_DIM = 64
DIL_WIDTH = DIL_HEADS * DIL_HEAD_DIM

OD_SIZES = (DSA_WIDTH, DSA_WIDTH, DSA_WIDTH, IDX_HEADS * IDX_DIM, IDX_DIM, IDX_HEADS, 3 * DIL_GROUPS * DIL_WIDTH)

N_EXPERTS = 32
TOP_K = 4
D_FF = D_MODEL
SWIGLU_ALPHA = 1.702
SWIGLU_LIMIT = 7.0

LANES = 128
VMEM_LIMIT_BYTES = 48 * 1024 * 1024
NEG_BIG = -1e30
INT_MIN = -2 ** 31
INT_MAX = 2 ** 31 - 1
KEY_NEG_INF = -0x7F800000

IDX_EXP_WIDTH = IDX_HEADS * LANES
OD_Q, OD_K, OD_QI, OD_KI = 0, DSA_WIDTH, 2 * DSA_WIDTH, 2 * DSA_WIDTH + IDX_EXP_WIDTH
OD_COLS_PADDED = 2048
DIL_COLS = 3 * DIL_WIDTH


def _split(h, sizes):
    return jnp.split(h, [int(i) for i in np.cumsum(sizes)[:-1]], axis=-1)


def _params(*sem):
    return pltpu.CompilerParams(dimension_semantics=sem, vmem_limit_bytes=VMEM_LIMIT_BYTES)


def _mm_kernel(x_ref, w_ref, o_ref):
    o_ref[...] = jnp.dot(x_ref[...].astype(jnp.bfloat16), w_ref[...].astype(jnp.bfloat16),
                         preferred_element_type=jnp.float32).astype(o_ref.dtype)


def _matmul(x, w, out_dtype=jnp.float32, tm=512, tn=512):
    T, K = x.shape
    N = w.shape[1]
    tm, tn = min(tm, T), min(tn, N)
    assert T % tm == 0 and N % tn == 0
    return pl.pallas_call(
        _mm_kernel,
        grid=(T // tm, N // tn),
        in_specs=[pl.BlockSpec((tm, K), lambda i, j: (i, 0)),
                  pl.BlockSpec((K, tn), lambda i, j: (0, j))],
        out_specs=pl.BlockSpec((tm, tn), lambda i, j: (i, j)),
        out_shape=jax.ShapeDtypeStruct((T, N), out_dtype),
        compiler_params=_params("parallel", "arbitrary"),
    )(x, w)


def _layer_norm_rows(z, g, b):
    mu = jnp.mean(z, axis=-1, keepdims=True)
    zc = z - mu
    var = jnp.mean(zc * zc, axis=-1, keepdims=True)
    return zc * lax.rsqrt(var + LN_EPS) * g + b


def _mm2_res_ln_kernel(a1_ref, a2_ref, w1_ref, w2_ref, x_ref, g_ref, b_ref, o_ref):
    mix = jnp.dot(a1_ref[...], w1_ref[...], preferred_element_type=jnp.float32)
    mix += jnp.dot(a2_ref[...], w2_ref[...], preferred_element_type=jnp.float32)
    o_ref[...] = _layer_norm_rows(DEEPNORM_ALPHA * x_ref[...] + mix, g_ref[...], b_ref[...])


def _matmul2_res_ln(a1, a2, w, x, g, b, tm=512):
    T, K1 = a1.shape
    D = w.shape[1]
    tm = min(tm, T)
    wb = w.astype(jnp.bfloat16)
    return pl.pallas_call(
        _mm2_res_ln_kernel,
        grid=(T // tm,),
        in_specs=[pl.BlockSpec((tm, K1), lambda i: (i, 0)),
                  pl.BlockSpec((tm, a2.shape[1]), lambda i: (i, 0)),
                  pl.BlockSpec((K1, D), lambda i: (0, 0)),
                  pl.BlockSpec((a2.shape[1], D), lambda i: (1, 0)),
                  pl.BlockSpec((tm, D), lambda i: (i, 0)),
                  pl.BlockSpec((1, D), lambda i: (0, 0)),
                  pl.BlockSpec((1, D), lambda i: (0, 0))],
        out_specs=pl.BlockSpec((tm, D), lambda i: (i, 0)),
        out_shape=jax.ShapeDtypeStruct((T, D), jnp.float32),
        compiler_params=_params("parallel"),
    )(a1, a2, wb, wb, x, g.reshape(1, D), b.reshape(1, D))


DSA_TQ = 256
DSA_ATT_TQ = 512
DSA_TK = 1024
COUNT_ROWS = 64
TIE_CHECK_PASS = 8


def _stack_index_heads(qi_blk):
    return jnp.concatenate([qi_blk[:, LANES * h:LANES * (h + 1)] for h in range(IDX_HEADS)], axis=0)


def _index_keys_t(ki_blk, qi_all, wi_t, q0, k0, tq, causal_mask=True):
    tk = ki_blk.shape[0]
    d = lax.dot_general(ki_blk, qi_all, (((1,), (1,)), ((), ())), preferred_element_type=jnp.float32)
    sc = wi_t[0:1, :] * jnp.maximum(d[:, 0:tq], 0.0)
    for h in range(1, IDX_HEADS):
        sc = sc + wi_t[h:h + 1, :] * jnp.maximum(d[:, h * tq:(h + 1) * tq], 0.0)
    if causal_mask:
        kpos = k0 + lax.broadcasted_iota(jnp.int32, (tk, 1), 0)
        qpos = q0 + lax.broadcasted_iota(jnp.int32, (1, tq), 1)
        sc = jnp.where(kpos <= qpos, sc, -jnp.inf)
    bits = lax.bitcast_convert_type(sc, jnp.int32)
    return jnp.where(bits < 0, INT_MIN - bits, bits)


def _dsa_select_kernel(qi_ref, wit_ref, ki_ref, sel_ref, key_ref, *, tq, topk, nq):
    i = pl.program_id(1)
    nblk = i + 1
    q0 = i * tq
    qi_all = _stack_index_heads(qi_ref[...])
    wi_t = wit_ref[...]

    def fill_block(j):
        k0 = pl.multiple_of(j * tq, tq)
        key_ref[j] = _index_keys_t(ki_ref[pl.ds(k0, tq), :], qi_all, wi_t, q0, k0, tq, causal_mask=False)

    def fill_pair(t, carry):
        fill_block(2 * t)
        fill_block(2 * t + 1)
        return carry

    lax.fori_loop(0, i // 2, fill_pair, 0)

    @pl.when(i % 2 == 1)
    def _():
        fill_block(i - 1)

    key_ref[i] = _index_keys_t(ki_ref[pl.ds(pl.multiple_of(q0, tq), tq), :], qi_all, wi_t, q0, q0, tq)

    kr = COUNT_ROWS
    row = lax.broadcasted_iota(jnp.int32, (kr, 1), 0)

    def count(pred):
        def block(j, acc):
            for s in range(tq // kr):
                kk = key_ref[j, s * kr:(s + 1) * kr, :]
                acc = acc + jnp.where(pred(kk, j * tq + s * kr), 1, 0)
            return acc

        acc = lax.fori_loop(0, nblk // 2, lambda t, a: block(2 * t + 1, block(2 * t, a)), jnp.zeros((kr, tq), jnp.int32))
        acc = lax.cond(nblk % 2 == 1, lambda a: block(nblk - 1, a), lambda a: a, acc)
        return jnp.sum(acc.astype(jnp.float32), axis=0, keepdims=True).astype(jnp.int32)

    def any_true(mask):
        return jnp.max(jnp.where(mask, 1.0, 0.0)) > 0.5

    def bit_cond(st):
        return jnp.logical_and(st[0] < 32, st[4])

    def bit_step(st):
        p, thr, cnt, final, _ = st
        cand = thr ^ lax.shift_left(jnp.int32(1), 31 - p)
        c = count(lambda kk, base: kk >= cand)
        take = c >= topk
        thr = jnp.where(take, cand, thr)
        cnt = jnp.where(take, c, cnt)
        final = lax.cond(p == TIE_CHECK_PASS,
                         lambda: jnp.where(cnt - count(lambda kk, base: kk == thr) < topk, 1, 0),
                         lambda: final)
        return p + 1, thr, cnt, final, any_true(jnp.logical_and(cnt != topk, final == 0))

    ncols = nblk * tq
    init = (jnp.int32(0), jnp.full((1, tq), INT_MIN, jnp.int32), jnp.full((1, tq), ncols, jnp.int32),
            jnp.zeros((1, tq), jnp.int32), ncols != topk)
    _, thr, cnt, _, _ = lax.while_loop(bit_cond, bit_step, init)
    tie = jnp.logical_and(cnt > topk, thr > KEY_NEG_INF)

    def resolve_ties():
        need = (topk - count(lambda kk, base: kk > thr)).astype(jnp.float32)
        ri = lax.broadcasted_iota(jnp.int32, (kr, kr), 0)
        ci = lax.broadcasted_iota(jnp.int32, (kr, kr), 1)
        lower = jnp.where(ci <= ri, 1.0, 0.0).astype(jnp.bfloat16)
        rowf = row.astype(jnp.float32)

        def blocks(js, st):
            carry, cut_acc = st
            eqs = [(j, s, key_ref[j, s * kr:(s + 1) * kr, :] == thr) for j in js for s in range(tq // kr)]
            within = [jnp.dot(lower, jnp.where(eq, 1.0, 0.0).astype(jnp.bfloat16), preferred_element_type=jnp.float32)
                      for _, _, eq in eqs]
            for (j, s, eq), pref in zip(eqs, within):
                end_pos = rowf + (j * tq + s * kr + 1).astype(jnp.float32)
                cut_acc = jnp.maximum(cut_acc, jnp.where(jnp.logical_and(eq, pref + carry == need), end_pos, 0.0))
                carry = carry + pref[kr - 1:kr, :]
            return carry, cut_acc

        st = lax.fori_loop(0, nblk // 2, lambda t, st: blocks((2 * t, 2 * t + 1), st),
                           (jnp.zeros((1, tq), jnp.float32), jnp.zeros((kr, tq), jnp.float32)))
        _, cut_acc = lax.cond(nblk % 2 == 1, lambda st: blocks((nblk - 1,), st), lambda st: st, st)
        return jnp.where(tie, jnp.max(cut_acc, axis=0, keepdims=True).astype(jnp.int32), INT_MAX)

    cut = lax.cond(any_true(tie), resolve_ties, lambda: jnp.full((1, tq), INT_MAX, jnp.int32))
    thr = jnp.maximum(thr, KEY_NEG_INF + 1)

    def emit(j, carry):
        kk = key_ref[j]
        kpos = j * tq + lax.broadcasted_iota(jnp.int32, (tq, 1), 0)
        sel = jnp.logical_or(kk > thr, jnp.logical_and(kk == thr, kpos < cut))
        sel_ref[0, j] = jnp.where(sel, 1, 0).astype(jnp.int8)
        return carry

    lax.fori_loop(0, nblk, emit, 0)

    def clear(j, carry):
        sel_ref[0, j] = jnp.zeros((tq, tq), jnp.int8)
        return carry

    lax.fori_loop(nblk, nq, clear, 0)


def _dsa_select(hb, wi_t, bsz, L, tq):
    T = bsz * L
    nq = L // tq
    topk = min(DSA_TOPK_MAX, L // 4)
    qmap = lambda b, i: (0, b * nq + i)
    return pl.pallas_call(
        partial(_dsa_select_kernel, tq=tq, topk=topk, nq=nq),
        grid=(bsz, nq),
        in_specs=[pl.BlockSpec((tq, IDX_EXP_WIDTH), lambda b, i: (b * nq + i, OD_QI // IDX_EXP_WIDTH)),
                  pl.BlockSpec((8, tq), qmap),
                  pl.BlockSpec((L, LANES), lambda b, i: (b, OD_KI // LANES))],
        out_specs=pl.BlockSpec((1, nq, tq, tq), lambda b, i: (b * nq + i, 0, 0, 0)),
        out_shape=jax.ShapeDtypeStruct((bsz * nq, nq, tq, tq), jnp.int8),
        scratch_shapes=[pltpu.VMEM((nq, tq, tq), jnp.int32)],
        compiler_params=_params("parallel", "arbitrary"),
    )(hb, wi_t, hb)


def _split_head_pair(x_pair):
    lane = lax.broadcasted_iota(jnp.int32, x_pair.shape, 1)
    zero = jnp.zeros_like(x_pair)
    return jnp.where(lane < 64, x_pair, zero), jnp.where(lane >= 64, x_pair, zero)


def _dsa_attn_kernel(qidx_ref, kidx_ref, q_ref, sel_ref, k_ref, vt_ref, o_ref, qm_ref, m_ref, acc_ref, *, tq, tk, ts):
    i = qidx_ref[pl.program_id(1)]
    j = kidx_ref[pl.program_id(1)]
    npair = DSA_HEADS // 2

    @pl.when(j == 0)
    def _():
        for p in range(npair):
            qa, qb = _split_head_pair(q_ref[:, LANES * p:LANES * (p + 1)])
            qm_ref[2 * p] = qa
            qm_ref[2 * p + 1] = qb
        m_ref[...] = jnp.full(m_ref.shape, NEG_BIG, jnp.float32)
        acc_ref[...] = jnp.zeros(acc_ref.shape, jnp.float32)

    sel = jnp.concatenate([jnp.concatenate([sel_ref[a, c] for a in range(tq // ts)], axis=1)
                           for c in range(tk // ts)], axis=0)
    bias = (sel.astype(jnp.float32) - 1.0) * (-NEG_BIG)
    vrow = lax.broadcasted_iota(jnp.int32, (LANES, tk), 0)
    ones = jnp.ones((LANES, tk), jnp.bfloat16)
    scores = []
    for h in range(DSA_HEADS):
        kp = k_ref[:, LANES * (h // 2):LANES * (h // 2 + 1)]
        s = lax.dot_general(kp, qm_ref[h], (((1,), (1,)), ((), ())), preferred_element_type=jnp.float32)
        scores.append(s + bias)
    for h in range(DSA_HEADS):
        vt = vt_ref[LANES * (h // 2):LANES * (h // 2 + 1), :]
        v_aug = jnp.where((vrow < 64) if h % 2 == 0 else (vrow >= 64), vt, ones)
        s = scores[h]
        m_prev = m_ref[h:h + 1, :]
        m_new = jnp.maximum(m_prev, jnp.max(s, axis=0, keepdims=True))
        alpha = jnp.exp(m_prev - m_new)
        e = jnp.exp(s - m_new).astype(jnp.bfloat16)
        acc_ref[h] = alpha * acc_ref[h] + jnp.dot(v_aug, e, preferred_element_type=jnp.float32)
        m_ref[h:h + 1, :] = m_new

    @pl.when(j == ((i + 1) * tq - 1) // tk)
    def _():
        rowi = lax.broadcasted_iota(jnp.int32, (LANES, tq), 0)
        for p in range(npair):
            a = acc_ref[2 * p]
            b = acc_ref[2 * p + 1]
            out_t = jnp.where(rowi < 64, a / a[64:65, :], b / b[0:1, :])
            o_ref[:, LANES * p:LANES * (p + 1)] = out_t.T.astype(o_ref.dtype)


def _dsa_attention(hb, v_t, sel, bsz, L, tq, tk):
    T = bsz * L
    nq, nk = L // tq, L // tk
    ts = sel.shape[-1]
    assert tq % ts == 0 and tk % ts == 0 and (L // ts) % (tq // ts) == 0
    pairs = [(i, j) for i in range(nq) for j in range(((i + 1) * tq - 1) // tk + 1)]
    qidx = jnp.asarray(np.array([p[0] for p in pairs], np.int32))
    kidx = jnp.asarray(np.array([p[1] for p in pairs], np.int32))

    def qmap(col):
        return lambda b, s, qi, ki: (b * nq + qi[s], col)

    def kmap(col):
        return lambda b, s, qi, ki: (b * nk + ki[s], col)

    grid_spec = pltpu.PrefetchScalarGridSpec(
        num_scalar_prefetch=2,
        grid=(bsz, len(pairs)),
        in_specs=[pl.BlockSpec((tq, DSA_WIDTH), qmap(OD_Q // DSA_WIDTH)),
                  pl.BlockSpec((tq // ts, tk // ts, ts, ts), lambda b, s, qi, ki: (b * nq + qi[s], ki[s], 0, 0)),
                  pl.BlockSpec((tk, DSA_WIDTH), kmap(OD_K // DSA_WIDTH)),
                  pl.BlockSpec((DSA_WIDTH, tk), lambda b, s, qi, ki: (0, b * nk + ki[s]))],
        out_specs=pl.BlockSpec((tq, DSA_WIDTH), qmap(0)),
        scratch_shapes=[pltpu.VMEM((DSA_HEADS, tq, LANES), jnp.bfloat16),
                        pltpu.VMEM((DSA_HEADS, tq), jnp.float32),
                        pltpu.VMEM((DSA_HEADS, LANES, tq), jnp.float32)])
    return pl.pallas_call(
        partial(_dsa_attn_kernel, tq=tq, tk=tk, ts=ts),
        grid_spec=grid_spec,
        out_shape=jax.ShapeDtypeStruct((T, DSA_WIDTH), jnp.bfloat16),
        compiler_params=_params("parallel", "arbitrary"),
    )(qidx, kidx, hb, sel, hb, v_t)


def _dilated_kernel(q_ref, kp_ref, kc_ref, vp_ref, vc_ref, o_ref, lse_ref, *, tq):
    a = pl.program_id(2)
    row = lax.broadcasted_iota(jnp.int32, (tq, 2 * tq), 0)
    c = lax.broadcasted_iota(jnp.int32, (tq, 2 * tq), 1)
    first_col = jnp.where(a == 0, tq, 0)
    valid = jnp.logical_and(jnp.logical_and(c >= row, c <= row + tq), c >= first_col)
    lane = lax.broadcasted_iota(jnp.int32, (tq, LANES), 1)
    for p in range(DIL_HEADS // 2):
        sl = slice(LANES * p, LANES * (p + 1))
        kk = jnp.concatenate([kp_ref[:, sl], kc_ref[:, sl]], axis=0)
        vv = jnp.concatenate([vp_ref[:, sl], vc_ref[:, sl]], axis=0)
        outs, lses = [], []
        for qh in _split_head_pair(q_ref[:, sl]):
            s = lax.dot_general(qh, kk, (((1,), (1,)), ((), ())), preferred_element_type=jnp.float32)
            s = jnp.where(valid, s, NEG_BIG)
            m = jnp.max(s, axis=1, keepdims=True)
            e = jnp.exp(s - m)
            l = jnp.sum(e, axis=1, keepdims=True)
            outs.append(jnp.dot(e.astype(jnp.bfloat16), vv, preferred_element_type=jnp.float32) / l)
            lses.append(m + jnp.log(l))
        o_ref[:, sl] = jnp.where(lane < 64, outs[0], outs[1])
        lse_ref[:, sl] = jnp.where(lane < 64, lses[0], lses[1])


def _dil_proj_kernel(x_ref, w_ref, o_ref, *scratch, dil):
    res = jnp.dot(x_ref[...].astype(jnp.bfloat16), w_ref[...], preferred_element_type=jnp.float32)
    if dil == 1:
        o_ref[...] = res.astype(o_ref.dtype)
        return
    res_ref, = scratch
    rows, cols = res.shape[0] // dil, res.shape[1]
    for c in range(cols // LANES):
        res_ref[c] = res[:, LANES * c:LANES * (c + 1)]
    for r in range(dil):
        for c in range(cols // LANES):
            o_ref[:, cols * r + LANES * c:cols * r + LANES * (c + 1)] = (
                res_ref.at[c][pl.ds(r, rows, stride=dil), :].astype(o_ref.dtype))


def _dil_proj(x, w, dil, tm=1024):
    T, K = x.shape
    C = w.shape[1]
    tm = min(tm, T)
    return pl.pallas_call(
        partial(_dil_proj_kernel, dil=dil),
        grid=(T // tm,),
        in_specs=[pl.BlockSpec((tm, K), lambda i: (i, 0)), pl.BlockSpec((K, C), lambda i: (0, 0))],
        out_specs=pl.BlockSpec((tm // dil, dil * C), lambda i: (i, 0)),
        out_shape=jax.ShapeDtypeStruct((T // dil, dil * C), jnp.bfloat16),
        scratch_shapes=[] if dil == 1 else [pltpu.VMEM((C // LANES, tm, LANES), jnp.float32)],
        compiler_params=_params("parallel"),
    )(x, w)


def _dilated_group(pg, bsz, L, g, tq):
    window, dil = DIL_PATTERNS[g]
    assert window // dil == tq
    M = L // dil
    nb = M // tq
    ncol = DIL_COLS // DIL_WIDTH

    def cur(col):
        return lambda b, r, a: (b * nb + a, r * ncol + col)

    def prev(col):
        return lambda b, r, a: (b * nb + jnp.maximum(a - 1, 0), r * ncol + col)

    blk = (tq, DIL_WIDTH)
    out_map = lambda b, r, a: (b * nb + a, r)
    return pl.pallas_call(
        partial(_dilated_kernel, tq=tq),
        grid=(bsz, dil, nb),
        in_specs=[pl.BlockSpec(blk, cur(0)), pl.BlockSpec(blk, prev(1)), pl.BlockSpec(blk, cur(1)),
                  pl.BlockSpec(blk, prev(2)), pl.BlockSpec(blk, cur(2))],
        out_specs=[pl.BlockSpec(blk, out_map), pl.BlockSpec(blk, out_map)],
        out_shape=[jax.ShapeDtypeStruct((bsz * M, dil * DIL_WIDTH), jnp.float32)] * 2,
        compiler_params=_params("parallel", "parallel", "arbitrary"),
    )(pg, pg, pg, pg, pg)


def _dilated_combine_kernel(*refs, dils, tm):
    ng = len(dils)
    o_refs, l_refs, out_ref, scratch = refs[:ng], refs[ng:2 * ng], refs[2 * ng], refs[2 * ng + 1:]

    def token_major(ref, dil, buf):
        if dil == 1:
            return ref[...]
        nslab = DIL_WIDTH // LANES
        for r in range(dil):
            for c in range(nslab):
                buf.at[c][pl.ds(r, tm // dil, stride=dil), :] = ref[:, DIL_WIDTH * r + LANES * c:DIL_WIDTH * r + LANES * (c + 1)]
        return jnp.concatenate([buf[c] for c in range(nslab)], axis=1)

    bufs = iter(scratch)
    outs = [token_major(o_refs[g], d, None if d == 1 else next(bufs)) for g, d in enumerate(dils)]
    lses = [token_major(l_refs[g], d, None if d == 1 else next(bufs)) for g, d in enumerate(dils)]
    m = lses[0]
    for l in lses[1:]:
        m = jnp.maximum(m, l)
    ws = [jnp.exp(l - m) for l in lses]
    num, den = ws[0] * outs[0], ws[0]
    for w, o in zip(ws[1:], outs[1:]):
        num, den = num + w * o, den + w
    out_ref[...] = (num / den).astype(out_ref.dtype)


def _dilated_combine(outs, lses, dils, T, tm=512):
    tm = min(tm, T)
    specs = [pl.BlockSpec((tm // d, d * DIL_WIDTH), lambda i: (i, 0)) for d in dils]
    n_buf = 2 * sum(1 for d in dils if d != 1)
    return pl.pallas_call(
        partial(_dilated_combine_kernel, dils=tuple(dils), tm=tm),
        grid=(T // tm,),
        in_specs=specs + specs,
        out_specs=pl.BlockSpec((tm, DIL_WIDTH), lambda i: (i, 0)),
        out_shape=jax.ShapeDtypeStruct((T, DIL_WIDTH), jnp.bfloat16),
        scratch_shapes=[pltpu.VMEM((DIL_WIDTH // LANES, tm, LANES), jnp.float32)] * n_buf,
        compiler_params=_params("parallel"),
    )(*outs, *lses)


def _mm_nt_kernel(w_ref, x_ref, o_ref):
    o_ref[...] = lax.dot_general(w_ref[...], x_ref[...].astype(jnp.bfloat16), (((1,), (1,)), ((), ())),
                                 preferred_element_type=jnp.float32).astype(o_ref.dtype)


def _matmul_nt(w_t, x, out_dtype, tm=1024):
    N, K = w_t.shape
    T = x.shape[0]
    tm = min(tm, T)
    return pl.pallas_call(
        _mm_nt_kernel,
        grid=(T // tm,),
        in_specs=[pl.BlockSpec((N, K), lambda i: (0, 0)), pl.BlockSpec((tm, K), lambda i: (i, 0))],
        out_specs=pl.BlockSpec((N, tm), lambda i: (0, i)),
        out_shape=jax.ShapeDtypeStruct((N, T), out_dtype),
        compiler_params=_params("parallel"),
    )(w_t, x)


def _odd_weights(w_in):
    D = w_in.shape[0]
    q, k, v, qi, ki, wi, dil = _split(w_in, OD_SIZES)
    zeros = lambda n: jnp.zeros((D, n), w_in.dtype)
    qi_exp = jnp.concatenate(
        [jnp.concatenate([qi[:, IDX_DIM * h:IDX_DIM * (h + 1)] * IDX_DIM ** -0.5, zeros(LANES - IDX_DIM)], axis=1)
         for h in range(IDX_HEADS)], axis=1)
    dil = dil.reshape(D, 3, DIL_GROUPS, DIL_WIDTH)
    w_dil = [jnp.concatenate([dil[:, 0, g] * DIL_HEAD_DIM ** -0.5, dil[:, 1, g], dil[:, 2, g]], axis=1).astype(jnp.bfloat16)
             for g in range(DIL_GROUPS)]
    wb = jnp.concatenate([q * DSA_HEAD_DIM ** -0.5, k, qi_exp, ki, zeros(LANES - IDX_DIM),
                          zeros(OD_COLS_PADDED - OD_KI - LANES)], axis=1).astype(jnp.bfloat16)
    wv_t = v.T.astype(jnp.bfloat16)
    wwi_t = jnp.concatenate([wi * IDX_HEADS ** -0.5, zeros(8 - IDX_HEADS)], axis=1).T.astype(jnp.bfloat16)
    return wb, w_dil, wv_t, wwi_t


def _odd_layer_mix(x2, w_in, bsz, L):
    wb, w_dil, wv_t, wwi_t = _odd_weights(w_in)
    hb = _matmul(x2, wb, jnp.bfloat16, tm=1024, tn=1024)
    v_t = _matmul_nt(wv_t, x2, jnp.bfloat16)
    wi_t = _matmul_nt(wwi_t, x2, jnp.float32)
    sel = _dsa_select(hb, wi_t, bsz, L, min(DSA_TQ, L))
    o_c = _dsa_attention(hb, v_t, sel, bsz, L, min(DSA_ATT_TQ, L), min(DSA_TK, L))
    dils = [d for _, d in DIL_PATTERNS]
    groups = [_dilated_group(_dil_proj(x2, w_dil[g], dils[g]), bsz, L, g, DIL_PATTERNS[g][0] // dils[g])
              for g in range(DIL_GROUPS)]
    o_d = _dilated_combine([o for o, _ in groups], [l for _, l in groups], dils, bsz * L)
    return o_c, o_d


EV_Q, EV_K, EV_V, EV_R, EV_G = 0, 256, 512, 1024, 1536
EV_COLS_PADDED = 2048
GLA_ROWS = 512
S5_CHUNK = 64


def _gla_kernel(q_ref, k_ref, v_ref, r_ref, g_ref, wg_ref, bg_ref, ng_ref, o_ref, st_ref, *, rows):
    C = GLA_CHUNK

    @pl.when(pl.program_id(1) == 0)
    def _():
        st_ref[...] = jnp.zeros(st_ref.shape, jnp.float32)

    ri = lax.broadcasted_iota(jnp.int32, (C, C), 0)
    ci = lax.broadcasted_iota(jnp.int32, (C, C), 1)
    causal = ci <= ri
    tril = jnp.where(causal, 1.0, 0.0).astype(jnp.float32)
    wg = wg_ref[...].astype(jnp.bfloat16)
    bg = bg_ref[...]
    ng = ng_ref[...]

    def chunk(c, carry):
        r0 = pl.multiple_of(c * C, C)
        rs = pl.ds(r0, C)
        logit = jnp.dot(g_ref[rs, :].astype(jnp.bfloat16), wg, preferred_element_type=jnp.float32) + bg
        log_a = jax.nn.log_sigmoid(logit) / GLA_GATE_TAU
        bcum = jnp.dot(tril, log_a, precision=lax.Precision.HIGHEST, preferred_element_type=jnp.float32)
        b_last = bcum[C - 1:C, :]
        q_t = (q_ref[rs, :] * jnp.exp(bcum)).astype(jnp.bfloat16)
        k_t = (k_ref[rs, :] * jnp.exp(-bcum)).astype(jnp.bfloat16)
        k_end = (k_ref[rs, :] * jnp.exp(b_last - bcum)).astype(jnp.bfloat16)
        dec = jnp.exp(b_last)
        for p in range(GLA_HEADS // 2):
            sl = slice(LANES * p, LANES * (p + 1))
            q_halves = _split_head_pair(q_t[:, sl])
            ke_halves = _split_head_pair(k_end[:, sl])
            for half in range(2):
                h = 2 * p + half
                hs = slice(GLA_DV * h, GLA_DV * (h + 1))
                qm = q_halves[half]
                att = lax.dot_general(qm, k_t[:, sl], (((1,), (1,)), ((), ())), preferred_element_type=jnp.float32)
                att = jnp.where(causal, att, 0.0).astype(jnp.bfloat16)
                v_h = v_ref[rs, hs].astype(jnp.bfloat16)
                st = st_ref[h]
                o = jnp.dot(att, v_h, preferred_element_type=jnp.float32)
                o = o + lax.dot_general(qm, st.astype(jnp.bfloat16), (((1,), (1,)), ((), ())),
                                        preferred_element_type=jnp.float32)
                kv_t = lax.dot_general(v_h, ke_halves[half], (((0,), (0,)), ((), ())),
                                       preferred_element_type=jnp.float32)
                st_ref[h] = st * dec[:, sl] + kv_t
                o = o * lax.rsqrt(jnp.mean(o * o, axis=-1, keepdims=True) + LN_EPS) * ng
                o = o * jax.nn.silu(r_ref[rs, hs])
                o_ref[rs, hs] = o.astype(o_ref.dtype)
        return carry

    lax.fori_loop(0, rows // C, chunk, 0)


def _gla(hf, w_gate2, b_gate2, norm_g, bsz, L):
    T = bsz * L
    rows = min(GLA_ROWS, L)
    nb = L // rows
    dkw = GLA_HEADS * GLA_DK
    dvw = GLA_HEADS * GLA_DV
    wg = jnp.pad(w_gate2, ((0, LANES - GLA_GATE_RANK), (0, 0)))

    def rmap(col):
        return lambda b, i: (b * nb + i, col)

    const = lambda b, i: (0, 0)
    return pl.pallas_call(
        partial(_gla_kernel, rows=rows),
        grid=(bsz, nb),
        in_specs=[pl.BlockSpec((rows, dkw), rmap(EV_Q // dkw)),
                  pl.BlockSpec((rows, dkw), rmap(EV_K // dkw)),
                  pl.BlockSpec((rows, dvw), rmap(EV_V // dvw)),
                  pl.BlockSpec((rows, dvw), rmap(EV_R // dvw)),
                  pl.BlockSpec((rows, LANES), rmap(EV_G // LANES)),
                  pl.BlockSpec((LANES, dkw), const),
                  pl.BlockSpec((1, dkw), const),
                  pl.BlockSpec((1, GLA_DV), const)],
        out_specs=pl.BlockSpec((rows, dvw), rmap(0)),
        out_shape=jax.ShapeDtypeStruct((T, dvw), jnp.bfloat16),
        scratch_shapes=[pltpu.VMEM((GLA_HEADS, GLA_DV, LANES), jnp.float32)],
        compiler_params=_params("parallel", "arbitrary"),
    )(hf, hf, hf, hf, hf, wg, b_gate2.reshape(1, dkw), norm_g.reshape(1, GLA_DV))


def _s5_tables(a_re, a_im, log_dt, b_re, b_im, c_re, c_im, d_skip):
    f32 = jnp.float32
    Cs, G, P, N = S5_CHUNK, S5_GROUPS, S5_STATE, S5_GROUP
    lam_re = jnp.minimum(a_re.astype(f32), S5_MAX_RE)
    lam_im = a_im.astype(f32)
    dt = jnp.exp(log_dt.astype(f32))[:, None]
    mag = jnp.exp(lam_re * dt)
    ab_re = mag * jnp.cos(lam_im * dt)
    ab_im = mag * jnp.sin(lam_im * dt)
    inv = 1.0 / (lam_re * lam_re + lam_im * lam_im)
    z_re = ((ab_re - 1.0) * lam_re + ab_im * lam_im) * inv
    z_im = (ab_im * lam_re - (ab_re - 1.0) * lam_im) * inv
    br, bi = b_re.astype(f32), b_im.astype(f32)
    bb_re = z_re[..., None] * br - z_im[..., None] * bi
    bb_im = z_re[..., None] * bi + z_im[..., None] * br
    kk = jnp.arange(Cs + 1, dtype=f32)[:, None, None]
    pmag = jnp.exp(kk * (lam_re * dt))
    pw_re = pmag * jnp.cos(kk * (lam_im * dt))
    pw_im = pmag * jnp.sin(kk * (lam_im * dt))
    cr, ci = c_re.astype(f32), c_im.astype(f32)
    ca_re = cr[None] * pw_re[:, :, None, :] - ci[None] * pw_im[:, :, None, :]
    ca_im = cr[None] * pw_im[:, :, None, :] + ci[None] * pw_re[:, :, None, :]
    hi = lax.Precision.HIGHEST
    kern = (jnp.einsum('kgnp,gpm->kgnm', ca_re[:Cs], bb_re, precision=hi)
            - jnp.einsum('kgnp,gpm->kgnm', ca_im[:Cs], bb_im, precision=hi))
    jj = jnp.arange(Cs)[:, None]
    ii = jnp.arange(Cs)[None, :]
    tz = jnp.where((ii >= jj)[:, :, None, None, None], kern[jnp.maximum(ii - jj, 0)], 0.0)
    tz = jnp.transpose(tz, (2, 4, 0, 3, 1)).reshape(G, N * Cs, N * Cs)
    rev_re, rev_im = pw_re[Cs - 1::-1][:Cs], pw_im[Cs - 1::-1][:Cs]
    ws_re = rev_re[..., None] * bb_re[None] - rev_im[..., None] * bb_im[None]
    ws_im = rev_re[..., None] * bb_im[None] + rev_im[..., None] * bb_re[None]
    to_ws = lambda w: jnp.pad(jnp.transpose(w, (1, 3, 0, 2)).reshape(G, N * Cs, P), ((0, 0), (0, 0), (0, LANES - P)))
    to_wo = lambda w: jnp.pad(jnp.transpose(w, (1, 3, 2, 0)).reshape(G, P, N * Cs), ((0, 0), (0, LANES - P), (0, 0)))
    a_cs = jnp.stack([jnp.pad(pw_re[Cs], ((0, 0), (0, LANES - P))), jnp.pad(pw_im[Cs], ((0, 0), (0, LANES - P)))], axis=1)
    d_exp = jnp.repeat(d_skip.astype(f32).reshape(G, 1, N), Cs, axis=2)
    bf = jnp.bfloat16
    return (tz.astype(bf), to_ws(ws_re).astype(bf), to_ws(ws_im).astype(bf),
            to_wo(ca_re[1:]).astype(bf), to_wo(-ca_im[1:]).astype(bf), a_cs, d_exp)


def _s5_kernel(u_ref, tz_ref, wsr_ref, wsi_ref, wor_ref, woi_ref, acs_ref, d_ref, y_ref, xr_ref, xi_ref,
               *, nchunk, nbatch):
    u32 = jnp.concatenate([u_ref[m] for m in range(S5_GROUP)], axis=1)
    u = u32.astype(jnp.bfloat16)
    xr_ref[...] = jnp.dot(u, wsr_ref[0], preferred_element_type=jnp.float32)
    xi_ref[...] = jnp.dot(u, wsi_ref[0], preferred_element_type=jnp.float32)
    ar = acs_ref[0, 0:1, :]
    ai = acs_ref[0, 1:2, :]

    def step(c, carry):
        new = []
        for b in range(nbatch):
            sr, si = carry[2 * b], carry[2 * b + 1]
            row = pl.ds(b * nchunk + c, 1)
            lr, li = xr_ref[row, :], xi_ref[row, :]
            xr_ref[row, :] = sr
            xi_ref[row, :] = si
            new += [ar * sr - ai * si + lr, ar * si + ai * sr + li]
        return tuple(new)

    zero = jnp.zeros((1, LANES), jnp.float32)
    lax.fori_loop(0, nchunk, step, (zero,) * (2 * nbatch))
    y = jnp.dot(u, tz_ref[0], preferred_element_type=jnp.float32)
    y = y + jnp.dot(xr_ref[...].astype(jnp.bfloat16), wor_ref[0], preferred_element_type=jnp.float32)
    y = y + jnp.dot(xi_ref[...].astype(jnp.bfloat16), woi_ref[0], preferred_element_type=jnp.float32)
    y = jax.nn.gelu(y + d_ref[0] * u32)
    for n in range(S5_GROUP):
        y_ref[n] = y[:, S5_CHUNK * n:S5_CHUNK * (n + 1)]


def _s5_scan(u_t, tables, bsz, nchunk):
    width, R, Cs = u_t.shape
    G, N = S5_GROUPS, S5_GROUP
    W = N * Cs
    tz, wsr, wsi, wor, woi, a_cs, d_exp = tables
    gmap = lambda g: (g, 0, 0)
    return pl.pallas_call(
        partial(_s5_kernel, nchunk=nchunk, nbatch=bsz),
        grid=(G,),
        in_specs=[pl.BlockSpec((N, R, Cs), gmap), pl.BlockSpec((1, W, W), gmap),
                  pl.BlockSpec((1, W, LANES), gmap), pl.BlockSpec((1, W, LANES), gmap),
                  pl.BlockSpec((1, LANES, W), gmap), pl.BlockSpec((1, LANES, W), gmap),
                  pl.BlockSpec((1, 2, LANES), gmap), pl.BlockSpec((1, 1, W), gmap)],
        out_specs=pl.BlockSpec((N, R, Cs), gmap),
        out_shape=jax.ShapeDtypeStruct((width, R, Cs), jnp.float32),
        scratch_shapes=[pltpu.VMEM((R, LANES), jnp.float32), pltpu.VMEM((R, LANES), jnp.float32)],
        compiler_params=_params("parallel"),
    )(u_t, tz, wsr, wsi, wor, woi, a_cs, d_exp)


def _glu_kernel(yt_ref, wt_ref, b_ref, o_ref):
    y = yt_ref[...]
    gate = jnp.dot(wt_ref[...], y.astype(jnp.bfloat16), preferred_element_type=jnp.float32) + b_ref[...]
    o_ref[...] = (y * jax.nn.sigmoid(gate)).T.astype(o_ref.dtype)


def _glu(y_t, w_glu, b_glu, tm=1024):
    W, T = y_t.shape
    tm = min(tm, T)
    return pl.pallas_call(
        _glu_kernel,
        grid=(T // tm,),
        in_specs=[pl.BlockSpec((W, tm), lambda i: (0, i)),
                  pl.BlockSpec((W, W), lambda i: (0, 0)),
                  pl.BlockSpec((W, 1), lambda i: (0, 0))],
        out_specs=pl.BlockSpec((tm, W), lambda i: (i, 0)),
        out_shape=jax.ShapeDtypeStruct((T, W), jnp.bfloat16),
        compiler_params=_params("parallel"),
    )(y_t, w_glu.T.astype(jnp.bfloat16), b_glu.reshape(W, 1))


def _s5(u_t, s5_params, w_glu, b_glu, bsz, L):
    T = bsz * L
    nchunk = L // S5_CHUNK
    y_t = _s5_scan(u_t.reshape(S5_WIDTH, bsz * nchunk, S5_CHUNK), _s5_tables(*s5_params), bsz, nchunk)
    return _glu(y_t.reshape(S5_WIDTH, T), w_glu, b_glu)


def _even_weights(w_in):
    D = w_in.shape[0]
    q, k, v, r, g_lr, u = _split(w_in, EV_SIZES)
    pad = jnp.zeros((D, EV_COLS_PADDED - EV_G - GLA_GATE_RANK), w_in.dtype)
    wb = jnp.concatenate([q * GLA_DK ** -0.5, k, v, r, g_lr, pad], axis=1).astype(jnp.bfloat16)
    return wb, u.T.astype(jnp.bfloat16)


def _even_layer_mix(x2, w_in, w_gate2, b_gate2, norm_g, s5_params, w_glu, b_glu, bsz, L):
    wb, wu_t = _even_weights(w_in)
    hf = _matmul(x2, wb, jnp.float32, tm=1024, tn=512)
    u_t = _matmul_nt(wu_t, x2, jnp.float32)
    o_a = _gla(hf, w_gate2, b_gate2, norm_g, bsz, L)
    o_b = _s5(u_t, s5_params, w_glu, b_glu, bsz, L)
    return o_a, o_b


MOE_TILE = 512
ROW_TILE = 8
DMA_UNROLL = 8
MOE_VMEM_LIMIT_BYTES = 56 * 1024 * 1024


def _router_kernel(x_ref, w_ref, b_ref, e_ref, g_ref):
    logits = jnp.dot(x_ref[...].astype(jnp.bfloat16), w_ref[...], preferred_element_type=jnp.float32) + b_ref[...]
    tm = logits.shape[0]
    lane = lax.broadcasted_iota(jnp.int32, (tm, LANES), 1)
    logits = jnp.where(lane < N_EXPERTS, logits, NEG_BIG)
    tops, idxs = [], []
    for _ in range(TOP_K):
        m = jnp.max(logits, axis=1, keepdims=True)
        idx = jnp.min(jnp.where(logits == m, lane, LANES), axis=1, keepdims=True)
        tops.append(m)
        idxs.append(idx)
        logits = jnp.where(lane == idx, NEG_BIG, logits)
    exps = [jnp.exp(t - tops[0]) for t in tops]
    denom = exps[0]
    for e in exps[1:]:
        denom = denom + e
    lane4 = lax.broadcasted_iota(jnp.int32, (tm, TOP_K), 1)
    e_out = jnp.zeros((tm, TOP_K), jnp.int32)
    g_out = jnp.zeros((tm, TOP_K), jnp.float32)
    for k in range(TOP_K):
        e_out = jnp.where(lane4 == k, idxs[k], e_out)
        g_out = jnp.where(lane4 == k, exps[k] / denom, g_out)
    e_ref[...] = e_out
    g_ref[...] = g_out


def _router(x, router_w, router_b, tm=512):
    T, D = x.shape
    tm = min(tm, T)
    w = jnp.pad(router_w, ((0, 0), (0, LANES - N_EXPERTS))).astype(jnp.bfloat16)
    b = jnp.pad(router_b, (0, LANES - N_EXPERTS)).reshape(1, LANES)
    return pl.pallas_call(
        _router_kernel,
        grid=(T // tm,),
        in_specs=[pl.BlockSpec((tm, D), lambda i: (i, 0)),
                  pl.BlockSpec((D, LANES), lambda i: (0, 0)),
                  pl.BlockSpec((1, LANES), lambda i: (0, 0))],
        out_specs=[pl.BlockSpec((tm, TOP_K), lambda i: (i, 0)), pl.BlockSpec((tm, TOP_K), lambda i: (i, 0))],
        out_shape=[jax.ShapeDtypeStruct((T, TOP_K), jnp.int32), jax.ShapeDtypeStruct((T, TOP_K), jnp.float32)],
        compiler_params=_params("parallel"),
    )(x, w, b)


def _moe_rank_kernel(e_ref, rank_ref, count_ref, carry_ref):
    @pl.when(pl.program_id(0) == 0)
    def _():
        carry_ref[...] = jnp.zeros(carry_ref.shape, jnp.float32)

    e = e_ref[...]
    tm = e.shape[0]
    lane = lax.broadcasted_iota(jnp.int32, (tm, LANES), 1)
    onehot = jnp.zeros((tm, LANES), jnp.float32)
    for k in range(TOP_K):
        onehot = onehot + jnp.where(lane == e[:, k:k + 1], 1.0, 0.0)
    ri = lax.broadcasted_iota(jnp.int32, (tm, tm), 0)
    ci = lax.broadcasted_iota(jnp.int32, (tm, tm), 1)
    strict_lower = jnp.where(ci < ri, 1.0, 0.0).astype(jnp.bfloat16)
    before = jnp.dot(strict_lower, onehot.astype(jnp.bfloat16), preferred_element_type=jnp.float32) + carry_ref[...]
    lane4 = lax.broadcasted_iota(jnp.int32, (tm, TOP_K), 1)
    rank = jnp.zeros((tm, TOP_K), jnp.int32)
    for k in range(TOP_K):
        r_k = jnp.sum(jnp.where(lane == e[:, k:k + 1], before, 0.0), axis=1, keepdims=True).astype(jnp.int32)
        rank = jnp.where(lane4 == k, r_k, rank)
    rank_ref[...] = rank
    carry_ref[...] = carry_ref[...] + jnp.sum(onehot, axis=0, keepdims=True)
    count_ref[...] = carry_ref[...]


def _moe_rank(top_e, tm=256):
    T = top_e.shape[0]
    tm = min(tm, T)
    return pl.pallas_call(
        _moe_rank_kernel,
        grid=(T // tm,),
        in_specs=[pl.BlockSpec((tm, TOP_K), lambda i: (i, 0))],
        out_specs=[pl.BlockSpec((tm, TOP_K), lambda i: (i, 0)), pl.BlockSpec((1, LANES), lambda i: (0, 0))],
        out_shape=[jax.ShapeDtypeStruct((T, TOP_K), jnp.int32), jax.ShapeDtypeStruct((1, LANES), jnp.float32)],
        scratch_shapes=[pltpu.VMEM((1, LANES), jnp.float32)],
        compiler_params=_params("arbitrary"),
    )(top_e)


def _to_token_tiles(x, dst_ref, rows):
    for c in range(ROW_TILE):
        dst_ref[pl.ds(c, rows, stride=ROW_TILE), :] = x[:, LANES * c:LANES * (c + 1)]


def _from_token_tiles(src_ref, rows):
    return jnp.concatenate([src_ref[pl.ds(c, rows, stride=ROW_TILE), :] for c in range(ROW_TILE)], axis=1)


def _moe_dispatch_kernel(dest_ref, x_ref, init_ref, xbuf_ref, xs_ref, sem, *, tm):
    del init_ref
    i = pl.program_id(0)
    n = pl.num_programs(0)
    slot = i % 2

    def row_copy(s, r, d):
        return pltpu.make_async_copy(xs_ref.at[s, pl.ds(pl.multiple_of(r * ROW_TILE, ROW_TILE), ROW_TILE), :],
                                     xbuf_ref.at[pl.ds(pl.multiple_of(d * ROW_TILE, ROW_TILE), ROW_TILE), :], sem.at[s])

    def drain(s):
        def body(a, c):
            row_copy(s, 0, 0).wait()
            return c
        lax.fori_loop(0, tm * TOP_K, body, 0, unroll=DMA_UNROLL)

    @pl.when(i >= 2)
    def _():
        drain(slot)

    _to_token_tiles(x_ref[...], xs_ref.at[slot], tm)

    def body(r, c):
        for k in range(TOP_K):
            row_copy(slot, r, dest_ref[r * TOP_K + k]).start()
        return c

    lax.fori_loop(0, tm, body, 0, unroll=DMA_UNROLL)

    @pl.when(i == n - 1)
    def _():
        drain(slot)

        @pl.when(n > 1)
        def _():
            drain(1 - slot)


def _moe_dispatch(x, dest_flat, n_rows, tm=256):
    T, D = x.shape
    tm = min(tm, T)
    assert D == ROW_TILE * LANES
    init = jnp.zeros((n_rows * ROW_TILE, LANES), jnp.float32)
    return pl.pallas_call(
        partial(_moe_dispatch_kernel, tm=tm),
        grid=(T // tm,),
        in_specs=[pl.BlockSpec((tm * TOP_K,), lambda i: (i,), memory_space=pltpu.SMEM),
                  pl.BlockSpec((tm, D), lambda i: (i, 0)),
                  pl.BlockSpec(memory_space=pl.ANY)],
        out_specs=pl.BlockSpec(memory_space=pl.ANY),
        out_shape=jax.ShapeDtypeStruct((n_rows * ROW_TILE, LANES), jnp.float32),
        scratch_shapes=[pltpu.VMEM((2, tm * ROW_TILE, LANES), jnp.float32), pltpu.SemaphoreType.DMA((2,))],
        input_output_aliases={2: 0},
        compiler_params=_params("arbitrary"),
    )(dest_flat, x, init)


def _moe_ffn_kernel(te_ref, nt_ref, x_ref, w1_ref, b1_ref, w2_ref, b2_ref, y_ref, w1b_ref, w2b_ref):
    i = pl.program_id(0)
    prev = te_ref[jnp.maximum(i - 1, 0)]

    @pl.when(jnp.logical_or(i == 0, te_ref[i] != prev))
    def _():
        w1b_ref[...] = w1_ref[0].astype(jnp.bfloat16)
        w2b_ref[...] = w2_ref[0].astype(jnp.bfloat16)

    @pl.when(i < nt_ref[0])
    def _():
        x = _from_token_tiles(x_ref, MOE_TILE).astype(jnp.bfloat16)
        h = jnp.dot(x, w1b_ref[...], preferred_element_type=jnp.float32) + b1_ref[0]
        glu = jnp.minimum(h[:, :D_FF], SWIGLU_LIMIT)
        lin = jnp.clip(h[:, D_FF:], -SWIGLU_LIMIT, SWIGLU_LIMIT)
        act = glu * jax.nn.sigmoid(SWIGLU_ALPHA * glu) * (lin + 1.0)
        y = jnp.dot(act.astype(jnp.bfloat16), w2b_ref[...], preferred_element_type=jnp.float32) + b2_ref[0]
        _to_token_tiles(y, y_ref, MOE_TILE)

    @pl.when(i >= nt_ref[0])
    def _():
        y_ref[...] = jnp.zeros(y_ref.shape, y_ref.dtype)


def _moe_expert_ffn(xbuf, tile_expert, n_used, w1, b1, w2, b2):
    D, F2 = w1.shape[1], w1.shape[2]
    P = xbuf.shape[0] // ROW_TILE
    n_tiles = P // MOE_TILE
    blk = (MOE_TILE * ROW_TILE, LANES)
    grid_spec = pltpu.PrefetchScalarGridSpec(
        num_scalar_prefetch=2,
        grid=(n_tiles,),
        in_specs=[pl.BlockSpec(blk, lambda i, te, nt: (jnp.minimum(i, nt[0] - 1), 0)),
                  pl.BlockSpec((1, D, F2), lambda i, te, nt: (te[i], 0, 0)),
                  pl.BlockSpec((1, 1, F2), lambda i, te, nt: (te[i], 0, 0)),
                  pl.BlockSpec((1, D_FF, D), lambda i, te, nt: (te[i], 0, 0)),
                  pl.BlockSpec((1, 1, D), lambda i, te, nt: (te[i], 0, 0))],
        out_specs=pl.BlockSpec(blk, lambda i, te, nt: (i, 0)),
        scratch_shapes=[pltpu.VMEM((D, F2), jnp.bfloat16), pltpu.VMEM((D_FF, D), jnp.bfloat16)],
    )
    return pl.pallas_call(
        _moe_ffn_kernel,
        grid_spec=grid_spec,
        out_shape=jax.ShapeDtypeStruct(xbuf.shape, jnp.float32),
        compiler_params=pltpu.CompilerParams(dimension_semantics=("arbitrary",),
                                             vmem_limit_bytes=MOE_VMEM_LIMIT_BYTES),
    )(tile_expert, n_used, xbuf, w1, b1.reshape(-1, 1, F2), w2, b2.reshape(-1, 1, D))


def _moe_combine_kernel(dest_ref, gate_ref, x_ref, g_ref, b_ref, ybuf_ref, o_ref, rows_ref, sem, *, tm):
    def row_copy(k, r, d):
        return pltpu.make_async_copy(ybuf_ref.at[pl.ds(pl.multiple_of(d * ROW_TILE, ROW_TILE), ROW_TILE), :],
                                     rows_ref.at[k, pl.ds(pl.multiple_of(r * ROW_TILE, ROW_TILE), ROW_TILE), :], sem.at[0])

    def start(r, c):
        for k in range(TOP_K):
            row_copy(k, r, dest_ref[r * TOP_K + k]).start()
        return c

    lax.fori_loop(0, tm, start, 0, unroll=DMA_UNROLL)

    def wait(a, c):
        row_copy(0, 0, 0).wait()
        return c

    lax.fori_loop(0, tm * TOP_K, wait, 0, unroll=DMA_UNROLL)
    gate = gate_ref[...]
    ffn = gate[:, 0:1] * _from_token_tiles(rows_ref.at[0], tm)
    for k in range(1, TOP_K):
        ffn = ffn + gate[:, k:k + 1] * _from_token_tiles(rows_ref.at[k], tm)
    o_ref[...] = _layer_norm_rows(DEEPNORM_ALPHA * x_ref[...] + ffn, g_ref[...], b_ref[...])


def _moe_combine(ybuf, dest_flat, gate, x, g, b, tm=256):
    T, D = x.shape
    tm = min(tm, T)
    return pl.pallas_call(
        partial(_moe_combine_kernel, tm=tm),
        grid=(T // tm,),
        in_specs=[pl.BlockSpec((tm * TOP_K,), lambda i: (i,), memory_space=pltpu.SMEM),
                  pl.BlockSpec((tm, TOP_K), lambda i: (i, 0)),
                  pl.BlockSpec((tm, D), lambda i: (i, 0)),
                  pl.BlockSpec((1, D), lambda i: (0, 0)),
                  pl.BlockSpec((1, D), lambda i: (0, 0)),
                  pl.BlockSpec(memory_space=pl.ANY)],
        out_specs=pl.BlockSpec((tm, D), lambda i: (i, 0)),
        out_shape=jax.ShapeDtypeStruct((T, D), jnp.float32),
        scratch_shapes=[pltpu.VMEM((TOP_K, tm * ROW_TILE, LANES), jnp.float32), pltpu.SemaphoreType.DMA((1,))],
        compiler_params=_params("arbitrary"),
    )(dest_flat, gate, x, g.reshape(1, D), b.reshape(1, D), ybuf)


def _moe_layer(x, router_w, router_b, w1, b1, w2, b2, first_expert, ln_g, ln_b):
    T, D = x.shape
    A = T * TOP_K
    n_tiles = -(-(A + N_EXPERTS * (MOE_TILE - 1)) // MOE_TILE)
    top_e, gate = _router(x, router_w, router_b)
    rank, counts = _moe_rank(top_e)
    counts = counts[0, :N_EXPERTS].astype(jnp.int32)
    padded = (counts + MOE_TILE - 1) // MOE_TILE * MOE_TILE
    pend = jnp.cumsum(padded)
    pstart = pend - padded
    dest = (pstart[top_e] + rank).reshape(A)
    tile_start = jnp.arange(n_tiles, dtype=jnp.int32) * MOE_TILE
    tile_expert = jnp.minimum(jnp.sum(pend[None, :] <= tile_start[:, None], axis=1), N_EXPERTS - 1).astype(jnp.int32)
    n_used = (pend[-1:] // MOE_TILE).astype(jnp.int32)
    xbuf = _moe_dispatch(x, dest, n_tiles * MOE_TILE)
    ybuf = _moe_expert_ffn(xbuf, tile_expert + first_expert, n_used, w1, b1, w2, b2)
    return _moe_combine(ybuf, dest, gate, x, ln_g, ln_b)


def kernel(x, ev_w_in, gla_w_gate2, gla_b_gate2, gla_norm_g, s5_a_re, s5_a_im, s5_log_dt, s5_b_re, s5_b_im,
           s5_c_re, s5_c_im, s5_d, s5_w_glu, s5_b_glu, ev_w_out, od_w_in, od_w_out, ln1_g, ln1_b, ln2_g, ln2_b,
           router_w, router_b, moe_w1, moe_b1, moe_w2, moe_b2):
    bsz, L, D = x.shape
    T = bsz * L
    x = x.reshape(T, D)
    w1_all, b1_all = moe_w1.reshape((-1,) + moe_w1.shape[2:]), moe_b1.reshape(-1, moe_b1.shape[-1])
    w2_all, b2_all = moe_w2.reshape((-1,) + moe_w2.shape[2:]), moe_b2.reshape(-1, moe_b2.shape[-1])
    for layer in range(DEPTH):
        j = layer // 2
        if layer % 2 == 0:
            s5_params = (s5_a_re[j], s5_a_im[j], s5_log_dt[j], s5_b_re[j], s5_b_im[j], s5_c_re[j], s5_c_im[j], s5_d[j])
            o_1, o_2 = _even_layer_mix(x, ev_w_in[j], gla_w_gate2[j], gla_b_gate2[j], gla_norm_g[j], s5_params,
                                       s5_w_glu[j], s5_b_glu[j], bsz, L)
            w_out = ev_w_out[j]
        else:
            o_1, o_2 = _odd_layer_mix(x, od_w_in[j], bsz, L)
            w_out = od_w_out[j]
        x = _matmul2_res_ln(o_1, o_2, w_out, x, ln1_g[layer], ln1_b[layer])
        x = _moe_layer(x, router_w[layer], router_b[layer], w1_all, b1_all, w2_all, b2_all, layer * N_EXPERTS,
                       ln2_g[layer], ln2_b[layer])
    return x.reshape(bsz, L, D)
```

```python
from functools import partial

import numpy as np
import jax
import jax.numpy as jnp
from jax import lax
from jax.experimental import pallas as pl
from jax.experimental.pallas import tpu as pltpu

D_MODEL = 1024
DEPTH = 4
DEEPNORM_ALPHA = (2.0 * DEPTH) ** 0.25
LN_EPS = 1e-5
MIX_WIDTH = D_MODEL

GLA_HEADS = 4
GLA_DV = MIX_WIDTH // 2 // GLA_HEADS
GLA_DK = GLA_DV // 2
GLA_GATE_RANK = 16
GLA_GATE_TAU = 16.0
GLA_CHUNK = 64

S5_WIDTH = MIX_WIDTH // 2
S5_GROUP = 16
S5_GROUPS = S5_WIDTH // S5_GROUP
S5_STATE = 64
S5_MAX_RE = -1e-4

EV_SIZES = (GLA_HEADS * GLA_DK, GLA_HEADS * GLA_DK, GLA_HEADS * GLA_DV, GLA_HEADS * GLA_DV, GLA_GATE_RANK, S5_WIDTH)

DSA_HEADS = 8
DSA_HEAD_DIM = 64
DSA_WIDTH = DSA_HEADS * DSA_HEAD_DIM
IDX_HEADS = 4
IDX_DIM = 64
DSA_TOPK_MAX = 256

DIL_PATTERNS = ((128, 1), (512, 4), (2048, 16))
DIL_GROUPS = len(DIL_PATTERNS)
DIL_HEADS = 8
DIL_HEAD_DIM = 64
DIL_WIDTH = DIL_HEADS * DIL_HEAD_DIM

OD_SIZES = (DSA_WIDTH, DSA_WIDTH, DSA_WIDTH, IDX_HEADS * IDX_DIM, IDX_DIM, IDX_HEADS, 3 * DIL_GROUPS * DIL_WIDTH)

N_EXPERTS = 32
TOP_K = 4
D_FF = D_MODEL
SWIGLU_ALPHA = 1.702
SWIGLU_LIMIT = 7.0

LANES = 128
VMEM_LIMIT_BYTES = 48 * 1024 * 1024
NEG_BIG = -1e30
INT_MIN = -2 ** 31
INT_MAX = 2 ** 31 - 1
KEY_NEG_INF = -0x7F800000

IDX_EXP_WIDTH = IDX_HEADS * LANES
OD_Q, OD_K, OD_QI, OD_KI = 0, DSA_WIDTH, 2 * DSA_WIDTH, 2 * DSA_WIDTH + IDX_EXP_WIDTH
OD_COLS_PADDED = 2048
DIL_COLS = 3 * DIL_WIDTH


def _split(h, sizes):
    return jnp.split(h, [int(i) for i in np.cumsum(sizes)[:-1]], axis=-1)


def _params(*sem):
    return pltpu.CompilerParams(dimension_semantics=sem, vmem_limit_bytes=VMEM_LIMIT_BYTES)


def _mm_kernel(x_ref, w_ref, o_ref):
    o_ref[...] = jnp.dot(x_ref[...].astype(jnp.bfloat16), w_ref[...].astype(jnp.bfloat16),
                         preferred_element_type=jnp.float32).astype(o_ref.dtype)


def _matmul(x, w, out_dtype=jnp.float32, tm=512, tn=512):
    T, K = x.shape
    N = w.shape[1]
    tm, tn = min(tm, T), min(tn, N)
    assert T % tm == 0 and N % tn == 0
    return pl.pallas_call(
        _mm_kernel,
        grid=(T // tm, N // tn),
        in_specs=[pl.BlockSpec((tm, K), lambda i, j: (i, 0)),
                  pl.BlockSpec((K, tn), lambda i, j: (0, j))],
        out_specs=pl.BlockSpec((tm, tn), lambda i, j: (i, j)),
        out_shape=jax.ShapeDtypeStruct((T, N), out_dtype),
        compiler_params=_params("parallel", "arbitrary"),
    )(x, w)


def _layer_norm_rows(z, g, b):
    mu = jnp.mean(z, axis=-1, keepdims=True)
    zc = z - mu
    var = jnp.mean(zc * zc, axis=-1, keepdims=True)
    return zc * lax.rsqrt(var + LN_EPS) * g + b


def _mm2_res_ln_kernel(a1_ref, a2_ref, w1_ref, w2_ref, x_ref, g_ref, b_ref, o_ref):
    mix = jnp.dot(a1_ref[...], w1_ref[...], preferred_element_type=jnp.float32)
    mix += jnp.dot(a2_ref[...], w2_ref[...], preferred_element_type=jnp.float32)
    o_ref[...] = _layer_norm_rows(DEEPNORM_ALPHA * x_ref[...] + mix, g_ref[...], b_ref[...])


def _matmul2_res_ln(a1, a2, w, x, g, b, tm=512):
    T, K1 = a1.shape
    D = w.shape[1]
    tm = min(tm, T)
    wb = w.astype(jnp.bfloat16)
    return pl.pallas_call(
        _mm2_res_ln_kernel,
        grid=(T // tm,),
        in_specs=[pl.BlockSpec((tm, K1), lambda i: (i, 0)),
                  pl.BlockSpec((tm, a2.shape[1]), lambda i: (i, 0)),
                  pl.BlockSpec((K1, D), lambda i: (0, 0)),
                  pl.BlockSpec((a2.shape[1], D), lambda i: (1, 0)),
                  pl.BlockSpec((tm, D), lambda i: (i, 0)),
                  pl.BlockSpec((1, D), lambda i: (0, 0)),
                  pl.BlockSpec((1, D), lambda i: (0, 0))],
        out_specs=pl.BlockSpec((tm, D), lambda i: (i, 0)),
        out_shape=jax.ShapeDtypeStruct((T, D), jnp.float32),
        compiler_params=_params("parallel"),
    )(a1, a2, wb, wb, x, g.reshape(1, D), b.reshape(1, D))


DSA_TQ = 256
DSA_ATT_TQ = 512
DSA_TK = 1024
COUNT_ROWS = 64
TIE_CHECK_PASS = 8


def _stack_index_heads(qi_blk):
    return jnp.concatenate([qi_blk[:, LANES * h:LANES * (h + 1)] for h in range(IDX_HEADS)], axis=0)


def _index_keys_t(ki_blk, qi_all, wi_t, q0, k0, tq, causal_mask=True):
    tk = ki_blk.shape[0]
    d = lax.dot_general(ki_blk, qi_all, (((1,), (1,)), ((), ())), preferred_element_type=jnp.float32)
    sc = wi_t[0:1, :] * jnp.maximum(d[:, 0:tq], 0.0)
    for h in range(1, IDX_HEADS):
        sc = sc + wi_t[h:h + 1, :] * jnp.maximum(d[:, h * tq:(h + 1) * tq], 0.0)
    if causal_mask:
        kpos = k0 + lax.broadcasted_iota(jnp.int32, (tk, 1), 0)
        qpos = q0 + lax.broadcasted_iota(jnp.int32, (1, tq), 1)
        sc = jnp.where(kpos <= qpos, sc, -jnp.inf)
    bits = lax.bitcast_convert_type(sc, jnp.int32)
    return jnp.where(bits < 0, INT_MIN - bits, bits)


def _dsa_select_kernel(qi_ref, wit_ref, ki_ref, sel_ref, key_ref, *, tq, topk, nq):
    i = pl.program_id(1)
    nblk = i + 1
    q0 = i * tq
    qi_all = _stack_index_heads(qi_ref[...])
    wi_t = wit_ref[...]

    def fill_block(j):
        k0 = pl.multiple_of(j * tq, tq)
        key_ref[j] = _index_keys_t(ki_ref[pl.ds(k0, tq), :], qi_all, wi_t, q0, k0, tq, causal_mask=False)

    def fill_pair(t, carry):
        fill_block(2 * t)
        fill_block(2 * t + 1)
        return carry

    lax.fori_loop(0, i // 2, fill_pair, 0)

    @pl.when(i % 2 == 1)
    def _():
        fill_block(i - 1)

    key_ref[i] = _index_keys_t(ki_ref[pl.ds(pl.multiple_of(q0, tq), tq), :], qi_all, wi_t, q0, q0, tq)

    kr = COUNT_ROWS
    row = lax.broadcasted_iota(jnp.int32, (kr, 1), 0)

    def count(pred):
        def block(j, acc):
            for s in range(tq // kr):
                kk = key_ref[j, s * kr:(s + 1) * kr, :]
                acc = acc + jnp.where(pred(kk, j * tq + s * kr), 1, 0)
            return acc

        acc = lax.fori_loop(0, nblk // 2, lambda t, a: block(2 * t + 1, block(2 * t, a)), jnp.zeros((kr, tq), jnp.int32))
        acc = lax.cond(nblk % 2 == 1, lambda a: block(nblk - 1, a), lambda a: a, acc)
        return jnp.sum(acc.astype(jnp.float32), axis=0, keepdims=True).astype(jnp.int32)

    def any_true(mask):
        return jnp.max(jnp.where(mask, 1.0, 0.0)) > 0.5

    def bit_cond(st):
        return jnp.logical_and(st[0] < 32, st[4])

    def bit_step(st):
        p, thr, cnt, final, _ = st
        cand = thr ^ lax.shift_left(jnp.int32(1), 31 - p)
        c = count(lambda kk, base: kk >= cand)
        take = c >= topk
        thr = jnp.where(take, cand, thr)
        cnt = jnp.where(take, c, cnt)
        final = lax.cond(p == TIE_CHECK_PASS,
                         lambda: jnp.where(cnt - count(lambda kk, base: kk == thr) < topk, 1, 0),
                         lambda: final)
        return p + 1, thr, cnt, final, any_true(jnp.logical_and(cnt != topk, final == 0))

    ncols = nblk * tq
    init = (jnp.int32(0), jnp.full((1, tq), INT_MIN, jnp.int32), jnp.full((1, tq), ncols, jnp.int32),
            jnp.zeros((1, tq), jnp.int32), ncols != topk)
    _, thr, cnt, _, _ = lax.while_loop(bit_cond, bit_step, init)
    tie = jnp.logical_and(cnt > topk, thr > KEY_NEG_INF)

    def resolve_ties():
        need = (topk - count(lambda kk, base: kk > thr)).astype(jnp.float32)
        ri = lax.broadcasted_iota(jnp.int32, (kr, kr), 0)
        ci = lax.broadcasted_iota(jnp.int32, (kr, kr), 1)
        lower = jnp.where(ci <= ri, 1.0, 0.0).astype(jnp.bfloat16)
        rowf = row.astype(jnp.float32)

        def blocks(js, st):
            carry, cut_acc = st
            eqs = [(j, s, key_ref[j, s * kr:(s + 1) * kr, :] == thr) for j in js for s in range(tq // kr)]
            within = [jnp.dot(lower, jnp.where(eq, 1.0, 0.0).astype(jnp.bfloat16), preferred_element_type=jnp.float32)
                      for _, _, eq in eqs]
            for (j, s, eq), pref in zip(eqs, within):
                end_pos = rowf + (j * tq + s * kr + 1).astype(jnp.float32)
                cut_acc = jnp.maximum(cut_acc, jnp.where(jnp.logical_and(eq, pref + carry == need), end_pos, 0.0))
                carry = carry + pref[kr - 1:kr, :]
            return carry, cut_acc

        st = lax.fori_loop(0, nblk // 2, lambda t, st: blocks((2 * t, 2 * t + 1), st),
                           (jnp.zeros((1, tq), jnp.float32), jnp.zeros((kr, tq), jnp.float32)))
        _, cut_acc = lax.cond(nblk % 2 == 1, lambda st: blocks((nblk - 1,), st), lambda st: st, st)
        return jnp.where(tie, jnp.max(cut_acc, axis=0, keepdims=True).astype(jnp.int32), INT_MAX)

    cut = lax.cond(any_true(tie), resolve_ties, lambda: jnp.full((1, tq), INT_MAX, jnp.int32))
    thr = jnp.maximum(thr, KEY_NEG_INF + 1)

    def emit(j, carry):
        kk = key_ref[j]
        kpos = j * tq + lax.broadcasted_iota(jnp.int32, (tq, 1), 0)
        sel = jnp.logical_or(kk > thr, jnp.logical_and(kk == thr, kpos < cut))
        sel_ref[0, j] = jnp.where(sel, 1, 0).astype(jnp.int8)
        return carry

    lax.fori_loop(0, nblk, emit, 0)

    def clear(j, carry):
        sel_ref[0, j] = jnp.zeros((tq, tq), jnp.int8)
        return carry

    lax.fori_loop(nblk, nq, clear, 0)


def _dsa_select(hb, wi_t, bsz, L, tq):
    T = bsz * L
    nq = L // tq
    topk = min(DSA_TOPK_MAX, L // 4)
    qmap = lambda b, i: (0, b * nq + i)
    return pl.pallas_call(
        partial(_dsa_select_kernel, tq=tq, topk=topk, nq=nq),
        grid=(bsz, nq),
        in_specs=[pl.BlockSpec((tq, IDX_EXP_WIDTH), lambda b, i: (b * nq + i, OD_QI // IDX_EXP_WIDTH)),
                  pl.BlockSpec((8, tq), qmap),
                  pl.BlockSpec((L, LANES), lambda b, i: (b, OD_KI // LANES))],
        out_specs=pl.BlockSpec((1, nq, tq, tq), lambda b, i: (b * nq + i, 0, 0, 0)),
        out_shape=jax.ShapeDtypeStruct((bsz * nq, nq, tq, tq), jnp.int8),
        scratch_shapes=[pltpu.VMEM((nq, tq, tq), jnp.int32)],
        compiler_params=_params("parallel", "arbitrary"),
    )(hb, wi_t, hb)


def _split_head_pair(x_pair):
    lane = lax.broadcasted_iota(jnp.int32, x_pair.shape, 1)
    zero = jnp.zeros_like(x_pair)
    return jnp.where(lane < 64, x_pair, zero), jnp.where(lane >= 64, x_pair, zero)


def _dsa_attn_kernel(qidx_ref, kidx_ref, q_ref, sel_ref, k_ref, vt_ref, o_ref, qm_ref, m_ref, acc_ref, *, tq, tk, ts):
    i = qidx_ref[pl.program_id(1)]
    j = kidx_ref[pl.program_id(1)]
    npair = DSA_HEADS // 2

    @pl.when(j == 0)
    def _():
        for p in range(npair):
            qa, qb = _split_head_pair(q_ref[:, LANES * p:LANES * (p + 1)])
            qm_ref[2 * p] = qa
            qm_ref[2 * p + 1] = qb
        m_ref[...] = jnp.full(m_ref.shape, NEG_BIG, jnp.float32)
        acc_ref[...] = jnp.zeros(acc_ref.shape, jnp.float32)

    sel = jnp.concatenate([jnp.concatenate([sel_ref[a, c] for a in range(tq // ts)], axis=1)
                           for c in range(tk // ts)], axis=0)
    bias = (sel.astype(jnp.float32) - 1.0) * (-NEG_BIG)
    vrow = lax.broadcasted_iota(jnp.int32, (LANES, tk), 0)
    ones = jnp.ones((LANES, tk), jnp.bfloat16)
    scores = []
    for h in range(DSA_HEADS):
        kp = k_ref[:, LANES * (h // 2):LANES * (h // 2 + 1)]
        s = lax.dot_general(kp, qm_ref[h], (((1,), (1,)), ((), ())), preferred_element_type=jnp.float32)
        scores.append(s + bias)
    for h in range(DSA_HEADS):
        vt = vt_ref[LANES * (h // 2):LANES * (h // 2 + 1), :]
        v_aug = jnp.where((vrow < 64) if h % 2 == 0 else (vrow >= 64), vt, ones)
        s = scores[h]
        m_prev = m_ref[h:h + 1, :]
        m_new = jnp.maximum(m_prev, jnp.max(s, axis=0, keepdims=True))
        alpha = jnp.exp(m_prev - m_new)
        e = jnp.exp(s - m_new).astype(jnp.bfloat16)
        acc_ref[h] = alpha * acc_ref[h] + jnp.dot(v_aug, e, preferred_element_type=jnp.float32)
        m_ref[h:h + 1, :] = m_new

    @pl.when(j == ((i + 1) * tq - 1) // tk)
    def _():
        rowi = lax.broadcasted_iota(jnp.int32, (LANES, tq), 0)
        for p in range(npair):
            a = acc_ref[2 * p]
            b = acc_ref[2 * p + 1]
            out_t = jnp.where(rowi < 64, a / a[64:65, :], b / b[0:1, :])
            o_ref[:, LANES * p:LANES * (p + 1)] = out_t.T.astype(o_ref.dtype)


def _dsa_attention(hb, v_t, sel, bsz, L, tq, tk):
    T = bsz * L
    nq, nk = L // tq, L // tk
    ts = sel.shape[-1]
    assert tq % ts == 0 and tk % ts == 0 and (L // ts) % (tq // ts) == 0
    pairs = [(i, j) for i in range(nq) for j in range(((i + 1) * tq - 1) // tk + 1)]
    qidx = jnp.asarray(np.array([p[0] for p in pairs], np.int32))
    kidx = jnp.asarray(np.array([p[1] for p in pairs], np.int32))

    def qmap(col):
        return lambda b, s, qi, ki: (b * nq + qi[s], col)

    def kmap(col):
        return lambda b, s, qi, ki: (b * nk + ki[s], col)

    grid_spec = pltpu.PrefetchScalarGridSpec(
        num_scalar_prefetch=2,
        grid=(bsz, len(pairs)),
        in_specs=[pl.BlockSpec((tq, DSA_WIDTH), qmap(OD_Q // DSA_WIDTH)),
                  pl.BlockSpec((tq // ts, tk // ts, ts, ts), lambda b, s, qi, ki: (b * nq + qi[s], ki[s], 0, 0)),
                  pl.BlockSpec((tk, DSA_WIDTH), kmap(OD_K // DSA_WIDTH)),
                  pl.BlockSpec((DSA_WIDTH, tk), lambda b, s, qi, ki: (0, b * nk + ki[s]))],
        out_specs=pl.BlockSpec((tq, DSA_WIDTH), qmap(0)),
        scratch_shapes=[pltpu.VMEM((DSA_HEADS, tq, LANES), jnp.bfloat16),
                        pltpu.VMEM((DSA_HEADS, tq), jnp.float32),
                        pltpu.VMEM((DSA_HEADS, LANES, tq), jnp.float32)])
    return pl.pallas_call(
        partial(_dsa_attn_kernel, tq=tq, tk=tk, ts=ts),
        grid_spec=grid_spec,
        out_shape=jax.ShapeDtypeStruct((T, DSA_WIDTH), jnp.bfloat16),
        compiler_params=_params("parallel", "arbitrary"),
    )(qidx, kidx, hb, sel, hb, v_t)


def _dilated_kernel(q_ref, kp_ref, kc_ref, vp_ref, vc_ref, o_ref, lse_ref, *, tq):
    a = pl.program_id(2)
    row = lax.broadcasted_iota(jnp.int32, (tq, 2 * tq), 0)
    c = lax.broadcasted_iota(jnp.int32, (tq, 2 * tq), 1)
    first_col = jnp.where(a == 0, tq, 0)
    valid = jnp.logical_and(jnp.logical_and(c >= row, c <= row + tq), c >= first_col)
    lane = lax.broadcasted_iota(jnp.int32, (tq, LANES), 1)
    for p in range(DIL_HEADS // 2):
        sl = slice(LANES * p, LANES * (p + 1))
        kk = jnp.concatenate([kp_ref[:, sl], kc_ref[:, sl]], axis=0)
        vv = jnp.concatenate([vp_ref[:, sl], vc_ref[:, sl]], axis=0)
        outs, lses = [], []
        for qh in _split_head_pair(q_ref[:, sl]):
            s = lax.dot_general(qh, kk, (((1,), (1,)), ((), ())), preferred_element_type=jnp.float32)
            s = jnp.where(valid, s, NEG_BIG)
            m = jnp.max(s, axis=1, keepdims=True)
            e = jnp.exp(s - m)
            l = jnp.sum(e, axis=1, keepdims=True)
            outs.append(jnp.dot(e.astype(jnp.bfloat16), vv, preferred_element_type=jnp.float32) / l)
            lses.append(m + jnp.log(l))
        o_ref[:, sl] = jnp.where(lane < 64, outs[0], outs[1])
        lse_ref[:, sl] = jnp.where(lane < 64, lses[0], lses[1])


def _dil_proj_kernel(x_ref, w_ref, o_ref, *scratch, dil):
    res = jnp.dot(x_ref[...].astype(jnp.bfloat16), w_ref[...], preferred_element_type=jnp.float32)
    if dil == 1:
        o_ref[...] = res.astype(o_ref.dtype)
        return
    res_ref, = scratch
    rows, cols = res.shape[0] // dil, res.shape[1]
    for c in range(cols // LANES):
        res_ref[c] = res[:, LANES * c:LANES * (c + 1)]
    for r in range(dil):
        for c in range(cols // LANES):
            o_ref[:, cols * r + LANES * c:cols * r + LANES * (c + 1)] = (
                res_ref.at[c][pl.ds(r, rows, stride=dil), :].astype(o_ref.dtype))


def _dil_proj(x, w, dil, tm=1024):
    T, K = x.shape
    C = w.shape[1]
    tm = min(tm, T)
    return pl.pallas_call(
        partial(_dil_proj_kernel, dil=dil),
        grid=(T // tm,),
        in_specs=[pl.BlockSpec((tm, K), lambda i: (i, 0)), pl.BlockSpec((K, C), lambda i: (0, 0))],
        out_specs=pl.BlockSpec((tm // dil, dil * C), lambda i: (i, 0)),
        out_shape=jax.ShapeDtypeStruct((T // dil, dil * C), jnp.bfloat16),
        scratch_shapes=[] if dil == 1 else [pltpu.VMEM((C // LANES, tm, LANES), jnp.float32)],
        compiler_params=_params("parallel"),
    )(x, w)


def _dilated_group(pg, bsz, L, g, tq):
    window, dil = DIL_PATTERNS[g]
    assert window // dil == tq
    M = L // dil
    nb = M // tq
    ncol = DIL_COLS // DIL_WIDTH

    def cur(col):
        return lambda b, r, a: (b * nb + a, r * ncol + col)

    def prev(col):
        return lambda b, r, a: (b * nb + jnp.maximum(a - 1, 0), r * ncol + col)

    blk = (tq, DIL_WIDTH)
    out_map = lambda b, r, a: (b * nb + a, r)
    return pl.pallas_call(
        partial(_dilated_kernel, tq=tq),
        grid=(bsz, dil, nb),
        in_specs=[pl.BlockSpec(blk, cur(0)), pl.BlockSpec(blk, prev(1)), pl.BlockSpec(blk, cur(1)),
                  pl.BlockSpec(blk, prev(2)), pl.BlockSpec(blk, cur(2))],
        out_specs=[pl.BlockSpec(blk, out_map), pl.BlockSpec(blk, out_map)],
        out_shape=[jax.ShapeDtypeStruct((bsz * M, dil * DIL_WIDTH), jnp.float32)] * 2,
        compiler_params=_params("parallel", "parallel", "arbitrary"),
    )(pg, pg, pg, pg, pg)


def _dilated_combine_kernel(*refs, dils, tm):
    ng = len(dils)
    o_refs, l_refs, out_ref, scratch = refs[:ng], refs[ng:2 * ng], refs[2 * ng], refs[2 * ng + 1:]

    def token_major(ref, dil, buf):
        if dil == 1:
            return ref[...]
        nslab = DIL_WIDTH // LANES
        for r in range(dil):
            for c in range(nslab):
                buf.at[c][pl.ds(r, tm // dil, stride=dil), :] = ref[:, DIL_WIDTH * r + LANES * c:DIL_WIDTH * r + LANES * (c + 1)]
        return jnp.concatenate([buf[c] for c in range(nslab)], axis=1)

    bufs = iter(scratch)
    outs = [token_major(o_refs[g], d, None if d == 1 else next(bufs)) for g, d in enumerate(dils)]
    lses = [token_major(l_refs[g], d, None if d == 1 else next(bufs)) for g, d in enumerate(dils)]
    m = lses[0]
    for l in lses[1:]:
        m = jnp.maximum(m, l)
    ws = [jnp.exp(l - m) for l in lses]
    num, den = ws[0] * outs[0], ws[0]
    for w, o in zip(ws[1:], outs[1:]):
        num, den = num + w * o, den + w
    out_ref[...] = (num / den).astype(out_ref.dtype)


def _dilated_combine(outs, lses, dils, T, tm=512):
    tm = min(tm, T)
    specs = [pl.BlockSpec((tm // d, d * DIL_WIDTH), lambda i: (i, 0)) for d in dils]
    n_buf = 2 * sum(1 for d in dils if d != 1)
    return pl.pallas_call(
        partial(_dilated_combine_kernel, dils=tuple(dils), tm=tm),
        grid=(T // tm,),
        in_specs=specs + specs,
        out_specs=pl.BlockSpec((tm, DIL_WIDTH), lambda i: (i, 0)),
        out_shape=jax.ShapeDtypeStruct((T, DIL_WIDTH), jnp.bfloat16),
        scratch_shapes=[pltpu.VMEM((DIL_WIDTH // LANES, tm, LANES), jnp.float32)] * n_buf,
        compiler_params=_params("parallel"),
    )(*outs, *lses)


def _mm_nt_kernel(w_ref, x_ref, o_ref):
    o_ref[...] = lax.dot_general(w_ref[...], x_ref[...].astype(jnp.bfloat16), (((1,), (1,)), ((), ())),
                                 preferred_element_type=jnp.float32).astype(o_ref.dtype)


def _matmul_nt(w_t, x, out_dtype, tm=1024):
    N, K = w_t.shape
    T = x.shape[0]
    tm = min(tm, T)
    return pl.pallas_call(
        _mm_nt_kernel,
        grid=(T // tm,),
        in_specs=[pl.BlockSpec((N, K), lambda i: (0, 0)), pl.BlockSpec((tm, K), lambda i: (i, 0))],
        out_specs=pl.BlockSpec((N, tm), lambda i: (0, i)),
        out_shape=jax.ShapeDtypeStruct((N, T), out_dtype),
        compiler_params=_params("parallel"),
    )(w_t, x)


def _odd_weights(w_in):
    D = w_in.shape[0]
    q, k, v, qi, ki, wi, dil = _split(w_in, OD_SIZES)
    zeros = lambda n: jnp.zeros((D, n), w_in.dtype)
    qi_exp = jnp.concatenate(
        [jnp.concatenate([qi[:, IDX_DIM * h:IDX_DIM * (h + 1)] * IDX_DIM ** -0.5, zeros(LANES - IDX_DIM)], axis=1)
         for h in range(IDX_HEADS)], axis=1)
    dil = dil.reshape(D, 3, DIL_GROUPS, DIL_WIDTH)
    w_dil = [jnp.concatenate([dil[:, 0, g] * DIL_HEAD_DIM ** -0.5, dil[:, 1, g], dil[:, 2, g]], axis=1).astype(jnp.bfloat16)
             for g in range(DIL_GROUPS)]
    wb = jnp.concatenate([q * DSA_HEAD_DIM ** -0.5, k, qi_exp, ki, zeros(LANES - IDX_DIM),
                          zeros(OD_COLS_PADDED - OD_KI - LANES)], axis=1).astype(jnp.bfloat16)
    wv_t = v.T.astype(jnp.bfloat16)
    wwi_t = jnp.concatenate([wi * IDX_HEADS ** -0.5, zeros(8 - IDX_HEADS)], axis=1).T.astype(jnp.bfloat16)
    return wb, w_dil, wv_t, wwi_t


def _odd_layer_mix(x2, w_in, bsz, L):
    wb, w_dil, wv_t, wwi_t = _odd_weights(w_in)
    hb = _matmul(x2, wb, jnp.bfloat16, tm=1024, tn=1024)
    v_t = _matmul_nt(wv_t, x2, jnp.bfloat16)
    wi_t = _matmul_nt(wwi_t, x2, jnp.float32)
    sel = _dsa_select(hb, wi_t, bsz, L, min(DSA_TQ, L))
    o_c = _dsa_attention(hb, v_t, sel, bsz, L, min(DSA_ATT_TQ, L), min(DSA_TK, L))
    dils = [d for _, d in DIL_PATTERNS]
    groups = [_dilated_group(_dil_proj(x2, w_dil[g], dils[g]), bsz, L, g, DIL_PATTERNS[g][0] // dils[g])
              for g in range(DIL_GROUPS)]
    o_d = _dilated_combine([o for o, _ in groups], [l for _, l in groups], dils, bsz * L)
    return o_c, o_d


EV_Q, EV_K, EV_V, EV_R, EV_G = 0, 256, 512, 1024, 1536
EV_COLS_PADDED = 2048
GLA_ROWS = 512
S5_CHUNK = 64


def _gla_kernel(q_ref, k_ref, v_ref, r_ref, g_ref, wg_ref, bg_ref, ng_ref, o_ref, st_ref, *, rows):
    C = GLA_CHUNK

    @pl.when(pl.program_id(1) == 0)
    def _():
        st_ref[...] = jnp.zeros(st_ref.shape, jnp.float32)

    ri = lax.broadcasted_iota(jnp.int32, (C, C), 0)
    ci = lax.broadcasted_iota(jnp.int32, (C, C), 1)
    causal = ci <= ri
    tril = jnp.where(causal, 1.0, 0.0).astype(jnp.float32)
    wg = wg_ref[...].astype(jnp.bfloat16)
    bg = bg_ref[...]
    ng = ng_ref[...]

    def chunk(c, carry):
        r0 = pl.multiple_of(c * C, C)
        rs = pl.ds(r0, C)
        logit = jnp.dot(g_ref[rs, :].astype(jnp.bfloat16), wg, preferred_element_type=jnp.float32) + bg
        log_a = jax.nn.log_sigmoid(logit) / GLA_GATE_TAU
        bcum = jnp.dot(tril, log_a, precision=lax.Precision.HIGHEST, preferred_element_type=jnp.float32)
        b_last = bcum[C - 1:C, :]
        q_t = (q_ref[rs, :] * jnp.exp(bcum)).astype(jnp.bfloat16)
        k_t = (k_ref[rs, :] * jnp.exp(-bcum)).astype(jnp.bfloat16)
        k_end = (k_ref[rs, :] * jnp.exp(b_last - bcum)).astype(jnp.bfloat16)
        dec = jnp.exp(b_last)
        for p in range(GLA_HEADS // 2):
            sl = slice(LANES * p, LANES * (p + 1))
            q_halves = _split_head_pair(q_t[:, sl])
            ke_halves = _split_head_pair(k_end[:, sl])
            for half in range(2):
                h = 2 * p + half
                hs = slice(GLA_DV * h, GLA_DV * (h + 1))
                qm = q_halves[half]
                att = lax.dot_general(qm, k_t[:, sl], (((1,), (1,)), ((), ())), preferred_element_type=jnp.float32)
                att = jnp.where(causal, att, 0.0).astype(jnp.bfloat16)
                v_h = v_ref[rs, hs].astype(jnp.bfloat16)
                st = st_ref[h]
                o = jnp.dot(att, v_h, preferred_element_type=jnp.float32)
                o = o + lax.dot_general(qm, st.astype(jnp.bfloat16), (((1,), (1,)), ((), ())),
                                        preferred_element_type=jnp.float32)
                kv_t = lax.dot_general(v_h, ke_halves[half], (((0,), (0,)), ((), ())),
                                       preferred_element_type=jnp.float32)
                st_ref[h] = st * dec[:, sl] + kv_t
                o = o * lax.rsqrt(jnp.mean(o * o, axis=-1, keepdims=True) + LN_EPS) * ng
                o = o * jax.nn.silu(r_ref[rs, hs])
                o_ref[rs, hs] = o.astype(o_ref.dtype)
        return carry

    lax.fori_loop(0, rows // C, chunk, 0)


def _gla(hf, w_gate2, b_gate2, norm_g, bsz, L):
    T = bsz * L
    rows = min(GLA_ROWS, L)
    nb = L // rows
    dkw = GLA_HEADS * GLA_DK
    dvw = GLA_HEADS * GLA_DV
    wg = jnp.pad(w_gate2, ((0, LANES - GLA_GATE_RANK), (0, 0)))

    def rmap(col):
        return lambda b, i: (b * nb + i, col)

    const = lambda b, i: (0, 0)
    return pl.pallas_call(
        partial(_gla_kernel, rows=rows),
        grid=(bsz, nb),
        in_specs=[pl.BlockSpec((rows, dkw), rmap(EV_Q // dkw)),
                  pl.BlockSpec((rows, dkw), rmap(EV_K // dkw)),
                  pl.BlockSpec((rows, dvw), rmap(EV_V // dvw)),
                  pl.BlockSpec((rows, dvw), rmap(EV_R // dvw)),
                  pl.BlockSpec((rows, LANES), rmap(EV_G // LANES)),
                  pl.BlockSpec((LANES, dkw), const),
                  pl.BlockSpec((1, dkw), const),
                  pl.BlockSpec((1, GLA_DV), const)],
        out_specs=pl.BlockSpec((rows, dvw), rmap(0)),
        out_shape=jax.ShapeDtypeStruct((T, dvw), jnp.bfloat16),
        scratch_shapes=[pltpu.VMEM((GLA_HEADS, GLA_DV, LANES), jnp.float32)],
        compiler_params=_params("parallel", "arbitrary"),
    )(hf, hf, hf, hf, hf, wg, b_gate2.reshape(1, dkw), norm_g.reshape(1, GLA_DV))


def _s5_tables(a_re, a_im, log_dt, b_re, b_im, c_re, c_im, d_skip):
    f32 = jnp.float32
    Cs, G, P, N = S5_CHUNK, S5_GROUPS, S5_STATE, S5_GROUP
    lam_re = jnp.minimum(a_re.astype(f32), S5_MAX_RE)
    lam_im = a_im.astype(f32)
    dt = jnp.exp(log_dt.astype(f32))[:, None]
    mag = jnp.exp(lam_re * dt)
    ab_re = mag * jnp.cos(lam_im * dt)
    ab_im = mag * jnp.sin(lam_im * dt)
    inv = 1.0 / (lam_re * lam_re + lam_im * lam_im)
    z_re = ((ab_re - 1.0) * lam_re + ab_im * lam_im) * inv
    z_im = (ab_im * lam_re - (ab_re - 1.0) * lam_im) * inv
    br, bi = b_re.astype(f32), b_im.astype(f32)
    bb_re = z_re[..., None] * br - z_im[..., None] * bi
    bb_im = z_re[..., None] * bi + z_im[..., None] * br
    kk = jnp.arange(Cs + 1, dtype=f32)[:, None, None]
    pmag = jnp.exp(kk * (lam_re * dt))
    pw_re = pmag * jnp.cos(kk * (lam_im * dt))
    pw_im = pmag * jnp.sin(kk * (lam_im * dt))
    cr, ci = c_re.astype(f32), c_im.astype(f32)
    ca_re = cr[None] * pw_re[:, :, None, :] - ci[None] * pw_im[:, :, None, :]
    ca_im = cr[None] * pw_im[:, :, None, :] + ci[None] * pw_re[:, :, None, :]
    hi = lax.Precision.HIGHEST
    kern = (jnp.einsum('kgnp,gpm->kgnm', ca_re[:Cs], bb_re, precision=hi)
            - jnp.einsum('kgnp,gpm->kgnm', ca_im[:Cs], bb_im, precision=hi))
    jj = jnp.arange(Cs)[:, None]
    ii = jnp.arange(Cs)[None, :]
    tz = jnp.where((ii >= jj)[:, :, None, None, None], kern[jnp.maximum(ii - jj, 0)], 0.0)
    tz = jnp.transpose(tz, (2, 4, 0, 3, 1)).reshape(G, N * Cs, N * Cs)
    rev_re, rev_im = pw_re[Cs - 1::-1][:Cs], pw_im[Cs - 1::-1][:Cs]
    ws_re = rev_re[..., None] * bb_re[None] - rev_im[..., None] * bb_im[None]
    ws_im = rev_re[..., None] * bb_im[None] + rev_im[..., None] * bb_re[None]
    to_ws = lambda w: jnp.pad(jnp.transpose(w, (1, 3, 0, 2)).reshape(G, N * Cs, P), ((0, 0), (0, 0), (0, LANES - P)))
    to_wo = lambda w: jnp.pad(jnp.transpose(w, (1, 3, 2, 0)).reshape(G, P, N * Cs), ((0, 0), (0, LANES - P), (0, 0)))
    a_cs = jnp.stack([jnp.pad(pw_re[Cs], ((0, 0), (0, LANES - P))), jnp.pad(pw_im[Cs], ((0, 0), (0, LANES - P)))], axis=1)
    d_exp = jnp.repeat(d_skip.astype(f32).reshape(G, 1, N), Cs, axis=2)
    bf = jnp.bfloat16
    return (tz.astype(bf), to_ws(ws_re).astype(bf), to_ws(ws_im).astype(bf),
            to_wo(ca_re[1:]).astype(bf), to_wo(-ca_im[1:]).astype(bf), a_cs, d_exp)


def _s5_kernel(u_ref, tz_ref, wsr_ref, wsi_ref, wor_ref, woi_ref, acs_ref, d_ref, y_ref, xr_ref, xi_ref,
               *, nchunk, nbatch):
    u32 = jnp.concatenate([u_ref[m] for m in range(S5_GROUP)], axis=1)
    u = u32.astype(jnp.bfloat16)
    xr_ref[...] = jnp.dot(u, wsr_ref[0], preferred_element_type=jnp.float32)
    xi_ref[...] = jnp.dot(u, wsi_ref[0], preferred_element_type=jnp.float32)
    ar = acs_ref[0, 0:1, :]
    ai = acs_ref[0, 1:2, :]

    def step(c, carry):
        new = []
        for b in range(nbatch):
            sr, si = carry[2 * b], carry[2 * b + 1]
            row = pl.ds(b * nchunk + c, 1)
            lr, li = xr_ref[row, :], xi_ref[row, :]
            xr_ref[row, :] = sr
            xi_ref[row, :] = si
            new += [ar * sr - ai * si + lr, ar * si + ai * sr + li]
        return tuple(new)

    zero = jnp.zeros((1, LANES), jnp.float32)
    lax.fori_loop(0, nchunk, step, (zero,) * (2 * nbatch))
    y = jnp.dot(u, tz_ref[0], preferred_element_type=jnp.float32)
    y = y + jnp.dot(xr_ref[...].astype(jnp.bfloat16), wor_ref[0], preferred_element_type=jnp.float32)
    y = y + jnp.dot(xi_ref[...].astype(jnp.bfloat16), woi_ref[0], preferred_element_type=jnp.float32)
    y = jax.nn.gelu(y + d_ref[0] * u32)
    for n in range(S5_GROUP):
        y_ref[n] = y[:, S5_CHUNK * n:S5_CHUNK * (n + 1)]


def _s5_scan(u_t, tables, bsz, nchunk):
    width, R, Cs = u_t.shape
    G, N = S5_GROUPS, S5_GROUP
    W = N * Cs
    tz, wsr, wsi, wor, woi, a_cs, d_exp = tables
    gmap = lambda g: (g, 0, 0)
    return pl.pallas_call(
        partial(_s5_kernel, nchunk=nchunk, nbatch=bsz),
        grid=(G,),
        in_specs=[pl.BlockSpec((N, R, Cs), gmap), pl.BlockSpec((1, W, W), gmap),
                  pl.BlockSpec((1, W, LANES), gmap), pl.BlockSpec((1, W, LANES), gmap),
                  pl.BlockSpec((1, LANES, W), gmap), pl.BlockSpec((1, LANES, W), gmap),
                  pl.BlockSpec((1, 2, LANES), gmap), pl.BlockSpec((1, 1, W), gmap)],
        out_specs=pl.BlockSpec((N, R, Cs), gmap),
        out_shape=jax.ShapeDtypeStruct((width, R, Cs), jnp.float32),
        scratch_shapes=[pltpu.VMEM((R, LANES), jnp.float32), pltpu.VMEM((R, LANES), jnp.float32)],
        compiler_params=_params("parallel"),
    )(u_t, tz, wsr, wsi, wor, woi, a_cs, d_exp)


def _glu_kernel(yt_ref, wt_ref, b_ref, o_ref):
    y = yt_ref[...]
    gate = jnp.dot(wt_ref[...], y.astype(jnp.bfloat16), preferred_element_type=jnp.float32) + b_ref[...]
    o_ref[...] = (y * jax.nn.sigmoid(gate)).T.astype(o_ref.dtype)


def _glu(y_t, w_glu, b_glu, tm=1024):
    W, T = y_t.shape
    tm = min(tm, T)
    return pl.pallas_call(
        _glu_kernel,
        grid=(T // tm,),
        in_specs=[pl.BlockSpec((W, tm), lambda i: (0, i)),
                  pl.BlockSpec((W, W), lambda i: (0, 0)),
                  pl.BlockSpec((W, 1), lambda i: (0, 0))],
        out_specs=pl.BlockSpec((tm, W), lambda i: (i, 0)),
        out_shape=jax.ShapeDtypeStruct((T, W), jnp.bfloat16),
        compiler_params=_params("parallel"),
    )(y_t, w_glu.T.astype(jnp.bfloat16), b_glu.reshape(W, 1))


def _s5(u_t, s5_params, w_glu, b_glu, bsz, L):
    T = bsz * L
    nchunk = L // S5_CHUNK
    y_t = _s5_scan(u_t.reshape(S5_WIDTH, bsz * nchunk, S5_CHUNK), _s5_tables(*s5_params), bsz, nchunk)
    return _glu(y_t.reshape(S5_WIDTH, T), w_glu, b_glu)


def _even_weights(w_in):
    D = w_in.shape[0]
    q, k, v, r, g_lr, u = _split(w_in, EV_SIZES)
    pad = jnp.zeros((D, EV_COLS_PADDED - EV_G - GLA_GATE_RANK), w_in.dtype)
    wb = jnp.concatenate([q * GLA_DK ** -0.5, k, v, r, g_lr, pad], axis=1).astype(jnp.bfloat16)
    return wb, u.T.astype(jnp.bfloat16)


def _even_layer_mix(x2, w_in, w_gate2, b_gate2, norm_g, s5_params, w_glu, b_glu, bsz, L):
    wb, wu_t = _even_weights(w_in)
    hf = _matmul(x2, wb, jnp.float32, tm=1024, tn=512)
    u_t = _matmul_nt(wu_t, x2, jnp.float32)
    o_a = _gla(hf, w_gate2, b_gate2, norm_g, bsz, L)
    o_b = _s5(u_t, s5_params, w_glu, b_glu, bsz, L)
    return o_a, o_b


MOE_TILE = 512
ROW_TILE = 8
DMA_UNROLL = 8
MOE_VMEM_LIMIT_BYTES = 56 * 1024 * 1024


def _router_kernel(x_ref, w_ref, b_ref, e_ref, g_ref):
    logits = jnp.dot(x_ref[...].astype(jnp.bfloat16), w_ref[...], preferred_element_type=jnp.float32) + b_ref[...]
    tm = logits.shape[0]
    lane = lax.broadcasted_iota(jnp.int32, (tm, LANES), 1)
    logits = jnp.where(lane < N_EXPERTS, logits, NEG_BIG)
    tops, idxs = [], []
    for _ in range(TOP_K):
        m = jnp.max(logits, axis=1, keepdims=True)
        idx = jnp.min(jnp.where(logits == m, lane, LANES), axis=1, keepdims=True)
        tops.append(m)
        idxs.append(idx)
        logits = jnp.where(lane == idx, NEG_BIG, logits)
    exps = [jnp.exp(t - tops[0]) for t in tops]
    denom = exps[0]
    for e in exps[1:]:
        denom = denom + e
    lane4 = lax.broadcasted_iota(jnp.int32, (tm, TOP_K), 1)
    e_out = jnp.zeros((tm, TOP_K), jnp.int32)
    g_out = jnp.zeros((tm, TOP_K), jnp.float32)
    for k in range(TOP_K):
        e_out = jnp.where(lane4 == k, idxs[k], e_out)
        g_out = jnp.where(lane4 == k, exps[k] / denom, g_out)
    e_ref[...] = e_out
    g_ref[...] = g_out


def _router(x, router_w, router_b, tm=512):
    T, D = x.shape
    tm = min(tm, T)
    w = jnp.pad(router_w, ((0, 0), (0, LANES - N_EXPERTS))).astype(jnp.bfloat16)
    b = jnp.pad(router_b, (0, LANES - N_EXPERTS)).reshape(1, LANES)
    return pl.pallas_call(
        _router_kernel,
        grid=(T // tm,),
        in_specs=[pl.BlockSpec((tm, D), lambda i: (i, 0)),
                  pl.BlockSpec((D, LANES), lambda i: (0, 0)),
                  pl.BlockSpec((1, LANES), lambda i: (0, 0))],
        out_specs=[pl.BlockSpec((tm, TOP_K), lambda i: (i, 0)), pl.BlockSpec((tm, TOP_K), lambda i: (i, 0))],
        out_shape=[jax.ShapeDtypeStruct((T, TOP_K), jnp.int32), jax.ShapeDtypeStruct((T, TOP_K), jnp.float32)],
        compiler_params=_params("parallel"),
    )(x, w, b)


def _moe_rank_kernel(e_ref, rank_ref, count_ref, carry_ref):
    @pl.when(pl.program_id(0) == 0)
    def _():
        carry_ref[...] = jnp.zeros(carry_ref.shape, jnp.float32)

    e = e_ref[...]
    tm = e.shape[0]
    lane = lax.broadcasted_iota(jnp.int32, (tm, LANES), 1)
    onehot = jnp.zeros((tm, LANES), jnp.float32)
    for k in range(TOP_K):
        onehot = onehot + jnp.where(lane == e[:, k:k + 1], 1.0, 0.0)
    ri = lax.broadcasted_iota(jnp.int32, (tm, tm), 0)
    ci = lax.broadcasted_iota(jnp.int32, (tm, tm), 1)
    strict_lower = jnp.where(ci < ri, 1.0, 0.0).astype(jnp.bfloat16)
    before = jnp.dot(strict_lower, onehot.astype(jnp.bfloat16), preferred_element_type=jnp.float32) + carry_ref[...]
    lane4 = lax.broadcasted_iota(jnp.int32, (tm, TOP_K), 1)
    rank = jnp.zeros((tm, TOP_K), jnp.int32)
    for k in range(TOP_K):
        r_k = jnp.sum(jnp.where(lane == e[:, k:k + 1], before, 0.0), axis=1, keepdims=True).astype(jnp.int32)
        rank = jnp.where(lane4 == k, r_k, rank)
    rank_ref[...] = rank
    carry_ref[...] = carry_ref[...] + jnp.sum(onehot, axis=0, keepdims=True)
    count_ref[...] = carry_ref[...]


def _moe_rank(top_e, tm=256):
    T = top_e.shape[0]
    tm = min(tm, T)
    return pl.pallas_call(
        _moe_rank_kernel,
        grid=(T // tm,),
        in_specs=[pl.BlockSpec((tm, TOP_K), lambda i: (i, 0))],
        out_specs=[pl.BlockSpec((tm, TOP_K), lambda i: (i, 0)), pl.BlockSpec((1, LANES), lambda i: (0, 0))],
        out_shape=[jax.ShapeDtypeStruct((T, TOP_K), jnp.int32), jax.ShapeDtypeStruct((1, LANES), jnp.float32)],
        scratch_shapes=[pltpu.VMEM((1, LANES), jnp.float32)],
        compiler_params=_params("arbitrary"),
    )(top_e)


def _to_token_tiles(x, dst_ref, rows):
    for c in range(ROW_TILE):
        dst_ref[pl.ds(c, rows, stride=ROW_TILE), :] = x[:, LANES * c:LANES * (c + 1)]


def _from_token_tiles(src_ref, rows):
    return jnp.concatenate([src_ref[pl.ds(c, rows, stride=ROW_TILE), :] for c in range(ROW_TILE)], axis=1)


def _moe_dispatch_kernel(dest_ref, x_ref, init_ref, xbuf_ref, xs_ref, sem, *, tm):
    del init_ref
    i = pl.program_id(0)
    n = pl.num_programs(0)
    slot = i % 2

    def row_copy(s, r, d):
        return pltpu.make_async_copy(xs_ref.at[s, pl.ds(pl.multiple_of(r * ROW_TILE, ROW_TILE), ROW_TILE), :],
                                     xbuf_ref.at[pl.ds(pl.multiple_of(d * ROW_TILE, ROW_TILE), ROW_TILE), :], sem.at[s])

    def drain(s):
        def body(a, c):
            row_copy(s, 0, 0).wait()
            return c
        lax.fori_loop(0, tm * TOP_K, body, 0, unroll=DMA_UNROLL)

    @pl.when(i >= 2)
    def _():
        drain(slot)

    _to_token_tiles(x_ref[...], xs_ref.at[slot], tm)

    def body(r, c):
        for k in range(TOP_K):
            row_copy(slot, r, dest_ref[r * TOP_K + k]).start(priority=k % 2)
        return c

    lax.fori_loop(0, tm, body, 0, unroll=DMA_UNROLL)

    @pl.when(i == n - 1)
    def _():
        drain(slot)

        @pl.when(n > 1)
        def _():
            drain(1 - slot)


def _moe_dispatch(x, dest_flat, n_rows, tm=256):
    T, D = x.shape
    tm = min(tm, T)
    assert D == ROW_TILE * LANES
    init = jnp.zeros((n_rows * ROW_TILE, LANES), jnp.float32)
    return pl.pallas_call(
        partial(_moe_dispatch_kernel, tm=tm),
        grid=(T // tm,),
        in_specs=[pl.BlockSpec((tm * TOP_K,), lambda i: (i,), memory_space=pltpu.SMEM),
                  pl.BlockSpec((tm, D), lambda i: (i, 0)),
                  pl.BlockSpec(memory_space=pl.ANY)],
        out_specs=pl.BlockSpec(memory_space=pl.ANY),
        out_shape=jax.ShapeDtypeStruct((n_rows * ROW_TILE, LANES), jnp.float32),
        scratch_shapes=[pltpu.VMEM((2, tm * ROW_TILE, LANES), jnp.float32), pltpu.SemaphoreType.DMA((2,))],
        input_output_aliases={2: 0},
        compiler_params=_params("arbitrary"),
    )(dest_flat, x, init)


def _moe_ffn_kernel(te_ref, nt_ref, x_ref, w1_ref, b1_ref, w2_ref, b2_ref, y_ref, w1b_ref, w2b_ref):
    i = pl.program_id(0)
    prev = te_ref[jnp.maximum(i - 1, 0)]

    @pl.when(jnp.logical_or(i == 0, te_ref[i] != prev))
    def _():
        w1b_ref[...] = w1_ref[0].astype(jnp.bfloat16)
        w2b_ref[...] = w2_ref[0].astype(jnp.bfloat16)

    @pl.when(i < nt_ref[0])
    def _():
        x = _from_token_tiles(x_ref, MOE_TILE).astype(jnp.bfloat16)
        h = jnp.dot(x, w1b_ref[...], preferred_element_type=jnp.float32) + b1_ref[0]
        glu = jnp.minimum(h[:, :D_FF], SWIGLU_LIMIT)
        lin = jnp.clip(h[:, D_FF:], -SWIGLU_LIMIT, SWIGLU_LIMIT)
        act = glu * jax.nn.sigmoid(SWIGLU_ALPHA * glu) * (lin + 1.0)
        y = jnp.dot(act.astype(jnp.bfloat16), w2b_ref[...], preferred_element_type=jnp.float32) + b2_ref[0]
        _to_token_tiles(y, y_ref, MOE_TILE)

    @pl.when(i >= nt_ref[0])
    def _():
        y_ref[...] = jnp.zeros(y_ref.shape, y_ref.dtype)


def _moe_expert_ffn(xbuf, tile_expert, n_used, w1, b1, w2, b2):
    D, F2 = w1.shape[1], w1.shape[2]
    P = xbuf.shape[0] // ROW_TILE
    n_tiles = P // MOE_TILE
    blk = (MOE_TILE * ROW_TILE, LANES)
    grid_spec = pltpu.PrefetchScalarGridSpec(
        num_scalar_prefetch=2,
        grid=(n_tiles,),
        in_specs=[pl.BlockSpec(blk, lambda i, te, nt: (jnp.minimum(i, nt[0] - 1), 0)),
                  pl.BlockSpec((1, D, F2), lambda i, te, nt: (te[i], 0, 0)),
                  pl.BlockSpec((1, 1, F2), lambda i, te, nt: (te[i], 0, 0)),
                  pl.BlockSpec((1, D_FF, D), lambda i, te, nt: (te[i], 0, 0)),
                  pl.BlockSpec((1, 1, D), lambda i, te, nt: (te[i], 0, 0))],
        out_specs=pl.BlockSpec(blk, lambda i, te, nt: (i, 0)),
        scratch_shapes=[pltpu.VMEM((D, F2), jnp.bfloat16), pltpu.VMEM((D_FF, D), jnp.bfloat16)],
    )
    return pl.pallas_call(
        _moe_ffn_kernel,
        grid_spec=grid_spec,
        out_shape=jax.ShapeDtypeStruct(xbuf.shape, jnp.float32),
        compiler_params=pltpu.CompilerParams(dimension_semantics=("arbitrary",),
                                             vmem_limit_bytes=MOE_VMEM_LIMIT_BYTES),
    )(tile_expert, n_used, xbuf, w1, b1.reshape(-1, 1, F2), w2, b2.reshape(-1, 1, D))


def _moe_combine_kernel(dest_ref, gate_ref, x_ref, g_ref, b_ref, ybuf_ref, o_ref, rows_ref, sem, *, tm):
    def row_copy(k, r, d):
        return pltpu.make_async_copy(ybuf_ref.at[pl.ds(pl.multiple_of(d * ROW_TILE, ROW_TILE), ROW_TILE), :],
                                     rows_ref.at[k, pl.ds(pl.multiple_of(r * ROW_TILE, ROW_TILE), ROW_TILE), :], sem.at[0])

    def start(r, c):
        for k in range(TOP_K):
            row_copy(k, r, dest_ref[r * TOP_K + k]).start(priority=k % 2)
        return c

    lax.fori_loop(0, tm, start, 0, unroll=DMA_UNROLL)

    def wait(a, c):
        row_copy(0, 0, 0).wait()
        return c

    lax.fori_loop(0, tm * TOP_K, wait, 0, unroll=DMA_UNROLL)
    gate = gate_ref[...]
    ffn = gate[:, 0:1] * _from_token_tiles(rows_ref.at[0], tm)
    for k in range(1, TOP_K):
        ffn = ffn + gate[:, k:k + 1] * _from_token_tiles(rows_ref.at[k], tm)
    o_ref[...] = _layer_norm_rows(DEEPNORM_ALPHA * x_ref[...] + ffn, g_ref[...], b_ref[...])


def _moe_combine(ybuf, dest_flat, gate, x, g, b, tm=256):
    T, D = x.shape
    tm = min(tm, T)
    return pl.pallas_call(
        partial(_moe_combine_kernel, tm=tm),
        grid=(T // tm,),
        in_specs=[pl.BlockSpec((tm * TOP_K,), lambda i: (i,), memory_space=pltpu.SMEM),
                  pl.BlockSpec((tm, TOP_K), lambda i: (i, 0)),
                  pl.BlockSpec((tm, D), lambda i: (i, 0)),
                  pl.BlockSpec((1, D), lambda i: (0, 0)),
                  pl.BlockSpec((1, D), lambda i: (0, 0)),
                  pl.BlockSpec(memory_space=pl.ANY)],
        out_specs=pl.BlockSpec((tm, D), lambda i: (i, 0)),
        out_shape=jax.ShapeDtypeStruct((T, D), jnp.float32),
        scratch_shapes=[pltpu.VMEM((TOP_K, tm * ROW_TILE, LANES), jnp.float32), pltpu.SemaphoreType.DMA((1,))],
        compiler_params=_params("arbitrary"),
    )(dest_flat, gate, x, g.reshape(1, D), b.reshape(1, D), ybuf)


def _moe_layer(x, router_w, router_b, w1, b1, w2, b2, first_expert, ln_g, ln_b):
    T, D = x.shape
    A = T * TOP_K
    n_tiles = -(-(A + N_EXPERTS * (MOE_TILE - 1)) // MOE_TILE)
    top_e, gate = _router(x, router_w, router_b)
    rank, counts = _moe_rank(top_e)
    counts = counts[0, :N_EXPERTS].astype(jnp.int32)
    padded = (counts + MOE_TILE - 1) // MOE_TILE * MOE_TILE
    pend = jnp.cumsum(padded)
    pstart = pend - padded
    dest = (pstart[top_e] + rank).reshape(A)
    tile_start = jnp.arange(n_tiles, dtype=jnp.int32) * MOE_TILE
    tile_expert = jnp.minimum(jnp.sum(pend[None, :] <= tile_start[:, None], axis=1), N_EXPERTS - 1).astype(jnp.int32)
    n_used = (pend[-1:] // MOE_TILE).astype(jnp.int32)
    xbuf = _moe_dispatch(x, dest, n_tiles * MOE_TILE)
    ybuf = _moe_expert_ffn(xbuf, tile_expert + first_expert, n_used, w1, b1, w2, b2)
    return _moe_combine(ybuf, dest, gate, x, ln_g, ln_b)


def kernel(x, ev_w_in, gla_w_gate2, gla_b_gate2, gla_norm_g, s5_a_re, s5_a_im, s5_log_dt, s5_b_re, s5_b_im,
           s5_c_re, s5_c_im, s5_d, s5_w_glu, s5_b_glu, ev_w_out, od_w_in, od_w_out, ln1_g, ln1_b, ln2_g, ln2_b,
           router_w, router_b, moe_w1, moe_b1, moe_w2, moe_b2):
    bsz, L, D = x.shape
    T = bsz * L
    x = x.reshape(T, D)
    w1_all, b1_all = moe_w1.reshape((-1,) + moe_w1.shape[2:]), moe_b1.reshape(-1, moe_b1.shape[-1])
    w2_all, b2_all = moe_w2.reshape((-1,) + moe_w2.shape[2:]), moe_b2.reshape(-1, moe_b2.shape[-1])
    for layer in range(DEPTH):
        j = layer // 2
        if layer % 2 == 0:
            s5_params = (s5_a_re[j], s5_a_im[j], s5_log_dt[j], s5_b_re[j], s5_b_im[j], s5_c_re[j], s5_c_im[j], s5_d[j])
            o_1, o_2 = _even_layer_mix(x, ev_w_in[j], gla_w_gate2[j], gla_b_gate2[j], gla_norm_g[j], s5_params,
                                       s5_w_glu[j], s5_b_glu[j], bsz, L)
            w_out = ev_w_out[j]
        else:
            o_1, o_2 = _odd_layer_mix(x, od_w_in[j], bsz, L)
            w_out = od_w_out[j]
        x = _matmul2_res_ln(o_1, o_2, w_out, x, ln1_g[layer], ln1_b[layer])
        x = _moe_layer(x, router_w[layer], router_b[layer], w1_all, b1_all, w2_all, b2_all, layer * N_EXPERTS,
                       ln2_g[layer], ln2_b[layer])
    return x.reshape(bsz, L, D)
```
